```python
import math
import jax
import jax.numpy as jnp
from jax import lax
import numpy as np

D_MODEL = 1024
BATCH = 16
SEQ = 256
DEPTH = 1
DEC_BATCH = 2
DEC_SEQ = 2048
PAST_LEN = 256

GRID_W = 64
HEAD_DIM = 64
NA_HEADS = D_MODEL // (2 * HEAD_DIM)
DIFF_HEADS = D_MODEL // (4 * HEAD_DIM)
NA_WIDTH = NA_HEADS * HEAD_DIM
DIFF_WIDTH = DIFF_HEADS * 2 * HEAD_DIM
MIX_WIDTH = NA_WIDTH + DIFF_WIDTH
QKV_WIDTH = 3 * NA_WIDTH + 3 * DIFF_WIDTH
NA_WIN_H = 8
NA_WIN_W = 16
N_EXPERTS = 16
EC_CAPACITY_FACTOR = 2
D_FF = int(math.ceil(8 * D_MODEL / 3 / 128)) * 128
ROPE_THETA = 10000.0
Q_BLOCK = 128
NORM_EPS = 1e-5
NEG_INF = -1e30
DEEPNORM_ALPHA = (2.0 * DEPTH) ** 0.25
DEEPNORM_BETA = (8.0 * DEPTH) ** -0.25

kernel_name = "hybrid_na_diffattn_ec_moe_dit_step"


def _layer_norm(x, g, b):
    xf = x.astype(jnp.float32)
    mu = jnp.mean(xf, axis=-1, keepdims=True)
    var = jnp.mean(jnp.square(xf - mu), axis=-1, keepdims=True)
    return ((xf - mu) * lax.rsqrt(var + NORM_EPS)).astype(x.dtype) * g + b


def _rms_norm(x, g):
    xf = x.astype(jnp.float32)
    return (xf * lax.rsqrt(jnp.mean(xf * xf, axis=-1, keepdims=True) + NORM_EPS)).astype(x.dtype) * g


def _modulation(cvec, w_ada, b_ada):
    m = jax.nn.silu(cvec) @ w_ada + b_ada
    return jnp.split(m[..., None, :], 6, axis=-1)


def _axial_rope(n_tokens):
    t = jnp.arange(n_tokens)
    row = (t // GRID_W).astype(jnp.float32)
    col = (t % GRID_W).astype(jnp.float32)
    half = HEAD_DIM // 2
    inv = ROPE_THETA ** (-jnp.arange(0, half, 2, dtype=jnp.float32) / half)
    ang_r = row[:, None] * inv[None, :]
    ang_c = col[:, None] * inv[None, :]
    ang = jnp.concatenate([ang_r, ang_r, ang_c, ang_c], axis=-1)
    return jnp.cos(ang), jnp.sin(ang)


def _rot_half(x):
    x1, x2 = jnp.split(x, 2, axis=-1)
    return jnp.concatenate([-x2, x1], axis=-1)


def _apply_rope(x, cos, sin):
    xr, xc = jnp.split(x, 2, axis=-1)
    rot = jnp.concatenate([_rot_half(xr), _rot_half(xc)], axis=-1)
    return x * cos.astype(x.dtype) + rot * sin.astype(x.dtype)


def _project(h, w_in):
    b, s, _ = h.shape
    qkv = h @ w_in
    cuts = [NA_WIDTH, 2 * NA_WIDTH, 3 * NA_WIDTH,
            3 * NA_WIDTH + DIFF_WIDTH, 3 * NA_WIDTH + 2 * DIFF_WIDTH]
    na_q, na_k, na_v, df_q, df_k, df_v = jnp.split(qkv, cuts, axis=-1)

    def na_heads(t):
        return t.reshape(b, s, NA_HEADS, HEAD_DIM).transpose(0, 2, 1, 3)

    def diff_qk(t):
        return t.reshape(b, s, DIFF_HEADS, 2, HEAD_DIM).transpose(0, 2, 3, 1, 4)

    df_v = df_v.reshape(b, s, DIFF_HEADS, 2 * HEAD_DIM).transpose(0, 2, 1, 3)
    return na_heads(na_q), na_heads(na_k), na_heads(na_v), diff_qk(df_q), diff_qk(df_k), df_v


def _dense_attention(q, k, v):
    b, h, sq, d = q.shape
    nblk = sq // Q_BLOCK
    qb = q.reshape(b, h, nblk, Q_BLOCK, d).transpose(2, 0, 1, 3, 4)

    def block(qblk):
        s = jnp.einsum("bhqd,bhkd->bhqk", qblk, k, preferred_element_type=jnp.float32) * (d ** -0.5)
        p = jax.nn.softmax(s, axis=-1).astype(v.dtype)
        return jnp.einsum("bhqk,bhkd->bhqd", p, v)

    o = lax.map(block, qb)
    return o.transpose(1, 2, 0, 3, 4).reshape(b, h, sq, d)


def _diff_attention(q, k, v, lam):
    b, h, _, sq, d = q.shape
    nblk = sq // Q_BLOCK
    qb = q.reshape(b, h, 2, nblk, Q_BLOCK, d).transpose(3, 0, 1, 2, 4, 5)

    def block(qblk):
        s = jnp.einsum("bhmqd,bhmkd->bhmqk", qblk, k, preferred_element_type=jnp.float32) * (d ** -0.5)
        p = jax.nn.softmax(s, axis=-1)
        a = (p[:, :, 0] - lam * p[:, :, 1]).astype(v.dtype)
        return jnp.einsum("bhqk,bhke->bhqe", a, v)

    o = lax.map(block, qb)
    return o.transpose(1, 2, 0, 3, 4).reshape(b, h, sq, 2 * d)


def _neighbourhood_attention(q, k, v, k_ctx, v_ctx, rpb):
    b, h, s, d = q.shape
    rows = s // GRID_W
    kh = min(NA_WIN_H, rows)
    kw = NA_WIN_W
    r = jnp.arange(rows)
    w = jnp.arange(GRID_W)
    r0 = jnp.clip(r - kh // 2, 0, rows - kh)
    band_rows = r0[:, None] + jnp.arange(kh)[None, :]
    c0 = jnp.clip(w - kw // 2, 0, GRID_W - kw)
    in_win = (w[None, :] >= c0[:, None]) & (w[None, :] < c0[:, None] + kw)
    row_off = band_rows - r[:, None] + (NA_WIN_H - 1)
    col_off = jnp.clip(w[None, :] - w[:, None], -(kw - 1), kw - 1) + (NA_WIN_W - 1)
    bias = rpb[:, row_off[:, None, :, None], col_off[None, :, None, :]]

    qg = q.reshape(b, h, rows, GRID_W, d)
    kg = k.reshape(b, h, rows, GRID_W, d)[:, :, band_rows]
    vg = v.reshape(b, h, rows, GRID_W, d)[:, :, band_rows]
    scale = d ** -0.5
    s_loc = jnp.einsum("bhrwd,bhrijd->bhrwij", qg, kg, preferred_element_type=jnp.float32) * scale
    s_loc = s_loc + bias[None].astype(jnp.float32)
    s_loc = jnp.where(in_win[None, None, None, :, None, :], s_loc, NEG_INF)
    s_ctx = jnp.einsum("bhrwd,bhld->bhrwl", qg, k_ctx, preferred_element_type=jnp.float32) * scale
    n_loc = kh * GRID_W
    s_all = jnp.concatenate([s_loc.reshape(b, h, rows, GRID_W, n_loc), s_ctx], axis=-1)
    p = jax.nn.softmax(s_all, axis=-1).astype(v.dtype)
    p_loc = p[..., :n_loc].reshape(b, h, rows, GRID_W, kh, GRID_W)
    p_ctx = p[..., n_loc:]
    o = (jnp.einsum("bhrwij,bhrijd->bhrwd", p_loc, vg)
         + jnp.einsum("bhrwl,bhld->bhrwd", p_ctx, v_ctx))
    return o.reshape(b, h, s, d)


def _merge_heads(na_o, df_o, subln_g, lambda_init, w_out):
    b, _, s, _ = na_o.shape
    df_o = _rms_norm(df_o, subln_g) * (1.0 - lambda_init)
    o = jnp.concatenate([na_o.transpose(0, 2, 1, 3).reshape(b, s, NA_WIDTH),
                         df_o.transpose(0, 2, 1, 3).reshape(b, s, DIFF_WIDTH)], axis=-1)
    return o @ w_out


def _context_attention(h, w_in, w_out, lam, lambda_init, subln_g):
    na_q, na_k, na_v, df_q, df_k, df_v = _project(h, w_in)
    na_o = _dense_attention(na_q, na_k, na_v)
    df_o = _diff_attention(df_q, df_k, df_v, lam)
    out = _merge_heads(na_o, df_o, subln_g, lambda_init, w_out)
    return out, na_k, na_v, df_k, df_v


def _latent_attention(h, ck_na, cv_na, ck_df, cv_df, w_in, w_out, rpb, lam, lambda_init, subln_g):
    s = h.shape[1]
    na_q, na_k, na_v, df_q, df_k, df_v = _project(h, w_in)
    cos, sin = _axial_rope(s)
    df_q = _apply_rope(df_q, cos, sin)
    df_k = _apply_rope(df_k, cos, sin)
    na_o = _neighbourhood_attention(na_q, na_k, na_v, ck_na, cv_na, rpb)
    df_o = _diff_attention(df_q,
                           jnp.concatenate([ck_df, df_k], axis=3),
                           jnp.concatenate([cv_df, df_v], axis=2), lam)
    return _merge_heads(na_o, df_o, subln_g, lambda_init, w_out)


def _expert_choice_ffn(h, w_router, w_gate, w_up, w_down):
    b, n, d = h.shape
    cap = EC_CAPACITY_FACTOR * n // N_EXPERTS
    aff = jax.nn.softmax(jnp.einsum("bnd,de->bne", h, w_router,
                                    preferred_element_type=jnp.float32), axis=-1)
    g, idx = lax.top_k(jnp.swapaxes(aff, 1, 2), cap)
    xs = jax.vmap(lambda hb, ib: hb[ib])(h, idx)
    a = jnp.einsum("becd,edf->becf", xs, w_gate)
    u = jnp.einsum("becd,edf->becf", xs, w_up)
    y = jnp.einsum("becf,efd->becd", jax.nn.silu(a) * u, w_down) * g[..., None].astype(h.dtype)
    return jax.vmap(lambda yb, ib: jnp.zeros((n, d), h.dtype).at[ib.reshape(-1)].add(yb.reshape(-1, d)))(y, idx)


def setup_inputs(seed: int = 0) -> dict:
    key = jax.random.key(seed)
    ks = jax.random.split(key, 26)
    f32 = jnp.float32
    nrm = lambda k, shape: jax.random.normal(k, shape, f32)
    nrel_h, nrel_w = 2 * NA_WIN_H - 1, 2 * NA_WIN_W - 1
    return {
        "x_prompt": nrm(ks[0], (BATCH, SEQ, D_MODEL)),
        "x_sample": nrm(ks[1], (DEC_BATCH, DEC_SEQ, D_MODEL)),
        "cache_na_k": nrm(ks[2], (DEC_BATCH, DEPTH, NA_HEADS, PAST_LEN, HEAD_DIM)),
        "cache_na_v": nrm(ks[3], (DEC_BATCH, DEPTH, NA_HEADS, PAST_LEN, HEAD_DIM)),
        "cache_diff_k": nrm(ks[4], (DEC_BATCH, DEPTH, DIFF_HEADS, 2, PAST_LEN, HEAD_DIM)),
        "cache_diff_v": nrm(ks[5], (DEC_BATCH, DEPTH, DIFF_HEADS, PAST_LEN, 2 * HEAD_DIM)),
        "c": nrm(ks[6], (DEC_BATCH, D_MODEL)),
        "c_ctx": nrm(ks[7], (D_MODEL,)),
        "w_ada": nrm(ks[8], (DEPTH, D_MODEL, 6 * D_MODEL)) * (0.5 * D_MODEL ** -0.5),
        "b_ada": nrm(ks[9], (DEPTH, 6 * D_MODEL)) * 0.01,
        "w_in": nrm(ks[10], (DEPTH, D_MODEL, QKV_WIDTH)) * D_MODEL ** -0.5,
        "w_out": nrm(ks[11], (DEPTH, MIX_WIDTH, D_MODEL)) * (MIX_WIDTH ** -0.5 * DEEPNORM_BETA),
        "na_rel_bias": nrm(ks[12], (DEPTH, NA_HEADS, nrel_h, nrel_w)) * 0.1,
        "lambda_q1": nrm(ks[13], (DEPTH, HEAD_DIM)) * 0.1,
        "lambda_k1": nrm(ks[14], (DEPTH, HEAD_DIM)) * 0.1,
        "lambda_q2": nrm(ks[15], (DEPTH, HEAD_DIM)) * 0.1,
        "lambda_k2": nrm(ks[16], (DEPTH, HEAD_DIM)) * 0.1,
        "subln_g": 1.0 + 0.01 * nrm(ks[17], (DEPTH, 2 * HEAD_DIM)),
        "ln1_g": 1.0 + 0.01 * nrm(ks[18], (DEPTH, D_MODEL)),
        "ln1_b": 0.01 * nrm(ks[19], (DEPTH, D_MODEL)),
        "ln2_g": 1.0 + 0.01 * nrm(ks[20], (DEPTH, D_MODEL)),
        "ln2_b": 0.01 * nrm(ks[21], (DEPTH, D_MODEL)),
        "w_router": nrm(ks[22], (DEPTH, D_MODEL, N_EXPERTS)) * D_MODEL ** -0.5,
        "w_gate": nrm(ks[23], (DEPTH, N_EXPERTS, D_MODEL, D_FF)) * D_MODEL ** -0.5,
        "w_up": nrm(ks[24], (DEPTH, N_EXPERTS, D_MODEL, D_FF)) * D_MODEL ** -0.5,
        "w_down": nrm(ks[25], (DEPTH, N_EXPERTS, D_FF, D_MODEL)) * (D_FF ** -0.5 * DEEPNORM_BETA),
    }


def reference(x_prompt, x_sample, cache_na_k, cache_na_v, cache_diff_k, cache_diff_v, c, c_ctx,
              w_ada, b_ada, w_in, w_out, na_rel_bias, lambda_q1, lambda_k1, lambda_q2, lambda_k2,
              subln_g, ln1_g, ln1_b, ln2_g, ln2_b, w_router, w_gate, w_up, w_down):
    xp = x_prompt
    xs = x_sample
    na_k_list, na_v_list, df_k_list, df_v_list = [], [], [], []
    for l in range(DEPTH):
        lambda_init = 0.8 - 0.6 * math.exp(-0.3 * l)
        lam = (jnp.exp(jnp.sum((lambda_q1[l] * lambda_k1[l]).astype(jnp.float32)))
               - jnp.exp(jnp.sum((lambda_q2[l] * lambda_k2[l]).astype(jnp.float32)))
               + lambda_init)

        sh1, sc1, g1, sh2, sc2, g2 = _modulation(c_ctx, w_ada[l], b_ada[l])
        a, nk, nv, dk, dv = _context_attention(xp * (1.0 + sc1) + sh1, w_in[l], w_out[l],
                                               lam, lambda_init, subln_g[l])
        xp = _layer_norm(DEEPNORM_ALPHA * xp + g1 * a, ln1_g[l], ln1_b[l])
        f = _expert_choice_ffn(xp * (1.0 + sc2) + sh2, w_router[l], w_gate[l], w_up[l], w_down[l])
        xp = _layer_norm(DEEPNORM_ALPHA * xp + g2 * f, ln2_g[l], ln2_b[l])
        na_k_list.append(nk)
        na_v_list.append(nv)
        df_k_list.append(dk)
        df_v_list.append(dv)

        sh1, sc1, g1, sh2, sc2, g2 = _modulation(c, w_ada[l], b_ada[l])
        a = _latent_attention(xs * (1.0 + sc1) + sh1, cache_na_k[:, l], cache_na_v[:, l],
                              cache_diff_k[:, l], cache_diff_v[:, l], w_in[l], w_out[l],
                              na_rel_bias[l], lam, lambda_init, subln_g[l])
        xs = _layer_norm(DEEPNORM_ALPHA * xs + g1 * a, ln1_g[l], ln1_b[l])
        f = _expert_choice_ffn(xs * (1.0 + sc2) + sh2, w_router[l], w_gate[l], w_up[l], w_down[l])
        xs = _layer_norm(DEEPNORM_ALPHA * xs + g2 * f, ln2_g[l], ln2_b[l])

    new_na_k = jnp.stack(na_k_list, axis=1)
    new_na_v = jnp.stack(na_v_list, axis=1)
    new_diff_k = jnp.stack(df_k_list, axis=1)
    new_diff_v = jnp.stack(df_v_list, axis=1)
    return (xp, xs, new_na_k, new_na_v, new_diff_k, new_diff_v)
```

```python
import functools
import math

import jax
import jax.numpy as jnp
from jax import lax
from jax.experimental import pallas as pl
from jax.experimental.pallas import tpu as pltpu

F32 = jnp.float32
BF16 = jnp.bfloat16
I32 = jnp.int32

D_MODEL = 1024
BATCH = 16
SEQ = 256
DEC_BATCH = 2
DEC_SEQ = 2048
PAST_LEN = 256
GRID_W = 64
GRID_ROWS = DEC_SEQ // GRID_W
HEAD_DIM = 64
NA_HEADS = 8
DIFF_HEADS = 4
NA_WIDTH = NA_HEADS * HEAD_DIM
DIFF_WIDTH = DIFF_HEADS * 2 * HEAD_DIM
QKV_WIDTH = 3 * NA_WIDTH + 3 * DIFF_WIDTH
NA_WIN_H = 8
NA_WIN_W = 16
N_REL_H = 2 * NA_WIN_H - 1
N_REL_W = 2 * NA_WIN_W - 1
N_EXPERTS = 16
EC_CAPACITY_FACTOR = 2
D_FF = 2816
ROPE_THETA = 10000.0
NORM_EPS = 1e-5
NEG_INF = -1e30
DEPTH = 1
DEEPNORM_ALPHA = (2.0 * DEPTH) ** 0.25
LAMBDA_INIT = 0.8 - 0.6 * math.exp(-0.3 * 0)
SCALE = HEAD_DIM ** -0.5

OFF_NAQ = 0
OFF_NAK = NA_WIDTH
OFF_NAV = 2 * NA_WIDTH
OFF_DFQ = 3 * NA_WIDTH
OFF_DFK = 3 * NA_WIDTH + DIFF_WIDTH
OFF_DFV = 3 * NA_WIDTH + 2 * DIFF_WIDTH

CAP_CTX = EC_CAPACITY_FACTOR * SEQ // N_EXPERTS
CAP_LAT = EC_CAPACITY_FACTOR * DEC_SEQ // N_EXPERTS
SLOTS_CTX = BATCH * CAP_CTX
SLOTS_LAT = DEC_BATCH * CAP_LAT

ROW_TILE = 256
NA_QUAD = 4
NA_WIN_ROWS = 12
FF_TILE = 256
LANES = 128
CUMSUM_CHUNK = 256

VMEM_LIMIT = 48 * 1024 * 1024


def _params(sem, vmem=VMEM_LIMIT):
    return pltpu.CompilerParams(dimension_semantics=sem, vmem_limit_bytes=vmem)


def _mm(a, b):
    return jnp.dot(a, b, preferred_element_type=F32)


def _nt(a, b):
    return lax.dot_general(a, b, (((1,), (1,)), ((), ())), preferred_element_type=F32)


def _split(a):
    hi = a.astype(BF16)
    lo = (a - hi.astype(F32)).astype(BF16)
    return hi, lo


def _mm3(a, b):
    ah, al = _split(a)
    bh, bl = _split(b)
    return _mm(ah, bh) + _mm(al, bh) + _mm(ah, bl)


def _nt3(a, b):
    ah, al = _split(a)
    bh, bl = _split(b)
    return _nt(ah, bh) + _nt(al, bh) + _nt(ah, bl)


def _layer_norm(y, g, b):
    mu = jnp.mean(y, axis=-1, keepdims=True)
    d = y - mu
    var = jnp.mean(d * d, axis=-1, keepdims=True)
    return d * lax.rsqrt(var + NORM_EPS) * g + b


def _lam(l_ref):
    l = l_ref[...]
    a = jnp.sum(l[0:1] * l[1:2], axis=-1, keepdims=True)
    b = jnp.sum(l[2:3] * l[3:4], axis=-1, keepdims=True)
    return jnp.exp(a) - jnp.exp(b) + LAMBDA_INIT


def _mod_rows(mod_ref, row):
    return [mod_ref[pl.ds(row, 1), i * D_MODEL:(i + 1) * D_MODEL] for i in range(6)]


def _sub_norm(o, sub):
    ms = jnp.mean(o * o, axis=-1, keepdims=True)
    return o * lax.rsqrt(ms + NORM_EPS) * sub * (1.0 - LAMBDA_INIT)


def _router_aff_t(h2, wrt):
    lg = _nt3(wrt, h2)
    m = jnp.max(lg, axis=0, keepdims=True)
    e = jnp.exp(lg - m)
    return e / jnp.sum(e, axis=0, keepdims=True)


def _mod_kernel(c_ref, w_ref, b_ref, o_ref):
    c = c_ref[...]
    s = c * (1.0 / (1.0 + jnp.exp(-c)))
    o_ref[...] = _mm3(s, w_ref[...]) + b_ref[...]


def _modulation(cvec, w_ada, b_ada):
    ncol = 6 * D_MODEL
    return pl.pallas_call(
        _mod_kernel,
        grid=(6,),
        in_specs=[pl.BlockSpec((8, D_MODEL), lambda j: (0, 0)),
                  pl.BlockSpec((D_MODEL, D_MODEL), lambda j: (0, j)),
                  pl.BlockSpec((1, D_MODEL), lambda j: (0, j))],
        out_specs=pl.BlockSpec((8, D_MODEL), lambda j: (0, j)),
        out_shape=jax.ShapeDtypeStruct((8, ncol), F32),
        compiler_params=_params(("arbitrary",)),
        name="modulation",
    )(cvec, w_ada, b_ada)


def _ctx_kernel(x_ref, mod_ref, lam_ref, win_ref, wout_ref, sub_ref, lng_ref, lnb_ref, wrt_ref,
                x1_ref, h2_ref, nak_ref, nav_ref, dfk_ref, dfv_ref, aff_ref, qkv_scr, o_scr):
    x = x_ref[0]
    sh1, sc1, g1, sh2, sc2, g2 = _mod_rows(mod_ref, 0)
    h = (x * (1.0 + sc1) + sh1).astype(BF16)
    qkv_scr[...] = _mm(h, win_ref[...])

    for hh in range(NA_HEADS):
        nak_ref[0, 0, hh] = qkv_scr[:, OFF_NAK + HEAD_DIM * hh:OFF_NAK + HEAD_DIM * (hh + 1)]
        nav_ref[0, 0, hh] = qkv_scr[:, OFF_NAV + HEAD_DIM * hh:OFF_NAV + HEAD_DIM * (hh + 1)]
    for hh in range(DIFF_HEADS):
        for m in range(2):
            c0 = OFF_DFK + HEAD_DIM * (2 * hh + m)
            dfk_ref[0, 0, hh, m] = qkv_scr[:, c0:c0 + HEAD_DIM]
        c0 = OFF_DFV + 2 * HEAD_DIM * hh
        dfv_ref[0, 0, hh] = qkv_scr[:, c0:c0 + 2 * HEAD_DIM]

    lam = _lam(lam_ref)

    def probs(c_q, c_k):
        q = (qkv_scr[:, c_q:c_q + HEAD_DIM] * SCALE).astype(BF16)
        k = qkv_scr[:, c_k:c_k + HEAD_DIM].astype(BF16)
        s = _nt(q, k)
        e = jnp.exp(s - jnp.max(s, axis=-1, keepdims=True))
        return e, jnp.sum(e, axis=-1, keepdims=True)

    for hh in range(NA_HEADS):
        e, l = probs(OFF_NAQ + HEAD_DIM * hh, OFF_NAK + HEAD_DIM * hh)
        v = qkv_scr[:, OFF_NAV + HEAD_DIM * hh:OFF_NAV + HEAD_DIM * (hh + 1)].astype(BF16)
        o = _mm(e.astype(BF16), v) / l
        o_scr[:, HEAD_DIM * hh:HEAD_DIM * (hh + 1)] = o.astype(BF16)

    sub = sub_ref[...]
    for hh in range(DIFF_HEADS):
        e1, l1 = probs(OFF_DFQ + HEAD_DIM * (2 * hh), OFF_DFK + HEAD_DIM * (2 * hh))
        e2, l2 = probs(OFF_DFQ + HEAD_DIM * (2 * hh + 1), OFF_DFK + HEAD_DIM * (2 * hh + 1))
        a = e1 * (1.0 / l1) - e2 * (lam / l2)
        c0 = OFF_DFV + 2 * HEAD_DIM * hh
        v = qkv_scr[:, c0:c0 + 2 * HEAD_DIM].astype(BF16)
        o = _sub_norm(_mm(a.astype(BF16), v), sub)
        c1 = NA_WIDTH + 2 * HEAD_DIM * hh
        o_scr[:, c1:c1 + 2 * HEAD_DIM] = o.astype(BF16)

    a = _mm(o_scr[...], wout_ref[...])
    x1 = _layer_norm(DEEPNORM_ALPHA * x + g1 * a, lng_ref[...], lnb_ref[...])
    x1_ref[0] = x1
    h2 = x1 * (1.0 + sc2) + sh2
    h2_ref[0] = h2.astype(BF16)
    aff_ref[0] = _router_aff_t(h2, wrt_ref[...])


def _context_block(x, mod, lamv, w_in, w_out, sub, ln_g, ln_b, wrt):
    full = lambda shape: pl.BlockSpec(shape, lambda b: (0,) * len(shape))
    return pl.pallas_call(
        _ctx_kernel,
        grid=(BATCH,),
        in_specs=[pl.BlockSpec((1, SEQ, D_MODEL), lambda b: (b, 0, 0)),
                  full((8, 6 * D_MODEL)), full((4, HEAD_DIM)),
                  full((D_MODEL, QKV_WIDTH)), full((D_MODEL, D_MODEL)),
                  full((1, 2 * HEAD_DIM)), full((1, D_MODEL)), full((1, D_MODEL)),
                  full((N_EXPERTS, D_MODEL))],
        out_specs=[pl.BlockSpec((1, SEQ, D_MODEL), lambda b: (b, 0, 0)),
                   pl.BlockSpec((1, SEQ, D_MODEL), lambda b: (b, 0, 0)),
                   pl.BlockSpec((1, 1, NA_HEADS, SEQ, HEAD_DIM), lambda b: (b, 0, 0, 0, 0)),
                   pl.BlockSpec((1, 1, NA_HEADS, SEQ, HEAD_DIM), lambda b: (b, 0, 0, 0, 0)),
                   pl.BlockSpec((1, 1, DIFF_HEADS, 2, SEQ, HEAD_DIM), lambda b: (b, 0, 0, 0, 0, 0)),
                   pl.BlockSpec((1, 1, DIFF_HEADS, SEQ, 2 * HEAD_DIM), lambda b: (b, 0, 0, 0, 0)),
                   pl.BlockSpec((1, N_EXPERTS, SEQ), lambda b: (b, 0, 0))],
        out_shape=[jax.ShapeDtypeStruct((BATCH, SEQ, D_MODEL), F32),
                   jax.ShapeDtypeStruct((BATCH, SEQ, D_MODEL), BF16),
                   jax.ShapeDtypeStruct((BATCH, 1, NA_HEADS, SEQ, HEAD_DIM), F32),
                   jax.ShapeDtypeStruct((BATCH, 1, NA_HEADS, SEQ, HEAD_DIM), F32),
                   jax.ShapeDtypeStruct((BATCH, 1, DIFF_HEADS, 2, SEQ, HEAD_DIM), F32),
                   jax.ShapeDtypeStruct((BATCH, 1, DIFF_HEADS, SEQ, 2 * HEAD_DIM), F32),
                   jax.ShapeDtypeStruct((BATCH, N_EXPERTS, SEQ), F32)],
        scratch_shapes=[pltpu.VMEM((SEQ, QKV_WIDTH), F32), pltpu.VMEM((SEQ, D_MODEL), BF16)],
        compiler_params=_params(("arbitrary",)),
        name="context_block",
    )(x, mod, lamv, w_in, w_out, sub, ln_g, ln_b, wrt)


def _lat_qkv_kernel(x_ref, mod_ref, win_ref, cos_ref, sin_ref,
                    naq_ref, nak_ref, nav_ref, dfq_ref, dfk_ref, dfv_ref, qkv_scr):
    b = pl.program_id(0)
    x = x_ref[0]
    sh1, sc1 = _mod_rows(mod_ref, 1 + b)[:2]
    h = (x * (1.0 + sc1) + sh1).astype(BF16)
    qkv_scr[...] = _mm(h, win_ref[...])

    for hh in range(NA_HEADS):
        lo, hi = HEAD_DIM * hh, HEAD_DIM * (hh + 1)
        naq_ref[0, hh] = (qkv_scr[:, OFF_NAQ + lo:OFF_NAQ + hi] * SCALE).astype(BF16)
        nak_ref[0, hh] = qkv_scr[:, OFF_NAK + lo:OFF_NAK + hi].astype(BF16)
        nav_ref[0, hh] = qkv_scr[:, OFF_NAV + lo:OFF_NAV + hi].astype(BF16)

    cos = cos_ref[...]
    sin = sin_ref[...]
    lane = lax.broadcasted_iota(I32, (ROW_TILE, LANES), 1)
    first = (lane & 31) < 16

    def rope(t):
        rot = jnp.where(first, pltpu.roll(t, LANES - 16, 1), pltpu.roll(t, 16, 1))
        return t * cos + rot * sin

    for hh in range(DIFF_HEADS):
        lo = 2 * HEAD_DIM * hh
        q = rope(qkv_scr[:, OFF_DFQ + lo:OFF_DFQ + lo + LANES]) * SCALE
        k = rope(qkv_scr[:, OFF_DFK + lo:OFF_DFK + lo + LANES])
        for m in range(2):
            dfq_ref[0, hh, m] = q[:, HEAD_DIM * m:HEAD_DIM * (m + 1)].astype(BF16)
            dfk_ref[0, hh, m] = k[:, HEAD_DIM * m:HEAD_DIM * (m + 1)].astype(BF16)
        dfv_ref[0, hh] = qkv_scr[:, OFF_DFV + lo:OFF_DFV + lo + LANES].astype(BF16)


def _latent_qkv(x, mod, w_in, cos, sin):
    nt = DEC_SEQ // ROW_TILE
    full = lambda shape: pl.BlockSpec(shape, lambda b, t: (0,) * len(shape))
    hs = lambda nh: pl.BlockSpec((1, nh, ROW_TILE, HEAD_DIM), lambda b, t: (b, 0, t, 0))
    return pl.pallas_call(
        _lat_qkv_kernel,
        grid=(DEC_BATCH, nt),
        in_specs=[pl.BlockSpec((1, ROW_TILE, D_MODEL), lambda b, t: (b, t, 0)),
                  full((8, 6 * D_MODEL)), full((D_MODEL, QKV_WIDTH)),
                  pl.BlockSpec((ROW_TILE, LANES), lambda b, t: (t, 0)),
                  pl.BlockSpec((ROW_TILE, LANES), lambda b, t: (t, 0))],
        out_specs=[hs(NA_HEADS), hs(NA_HEADS), hs(NA_HEADS),
                   pl.BlockSpec((1, DIFF_HEADS, 2, ROW_TILE, HEAD_DIM), lambda b, t: (b, 0, 0, t, 0)),
                   pl.BlockSpec((1, DIFF_HEADS, 2, ROW_TILE, HEAD_DIM), lambda b, t: (b, 0, 0, t, 0)),
                   pl.BlockSpec((1, DIFF_HEADS, ROW_TILE, 2 * HEAD_DIM), lambda b, t: (b, 0, t, 0))],
        out_shape=[jax.ShapeDtypeStruct((DEC_BATCH, NA_HEADS, DEC_SEQ, HEAD_DIM), BF16)] * 3
        + [jax.ShapeDtypeStruct((DEC_BATCH, DIFF_HEADS, 2, DEC_SEQ, HEAD_DIM), BF16)] * 2
        + [jax.ShapeDtypeStruct((DEC_BATCH, DIFF_HEADS, DEC_SEQ, 2 * HEAD_DIM), BF16)],
        scratch_shapes=[pltpu.VMEM((ROW_TILE, QKV_WIDTH), F32)],
        compiler_params=_params(("arbitrary", "arbitrary")),
        name="latent_qkv",
    )(x, mod, w_in, cos, sin)


def _na_row_offset(kind, qr, kr):
    if kind == 0:
        return kr - qr + 7 if kr < NA_WIN_H else None
    if kind == 1:
        return kr - qr + 3 if qr <= kr < qr + NA_WIN_H else None
    return kr - qr - 1 if NA_WIN_ROWS - NA_WIN_H <= kr else None


def _bias_kernel(rpb_ref, o_ref):
    h = pl.program_id(0)
    base = h * (N_REL_H * N_REL_W)
    wq = lax.broadcasted_iota(I32, (GRID_W, LANES), 0)
    c = lax.broadcasted_iota(I32, (GRID_W, LANES), 1)
    wk = c & (GRID_W - 1)
    right = c >= GRID_W
    coff = jnp.clip(wk - wq, -(NA_WIN_W - 1), NA_WIN_W - 1) + (NA_WIN_W - 1)
    c0 = jnp.clip(wq - NA_WIN_W // 2, 0, GRID_W - NA_WIN_W)
    in_win = (wk >= c0) & (wk < c0 + NA_WIN_W)
    cache = {}

    def tile(rl, rr):
        if (rl, rr) in cache:
            return cache[(rl, rr)]
        if rl is None and rr is None:
            t = jnp.full((GRID_W, LANES), NEG_INF, F32)
        else:
            acc = jnp.zeros((GRID_W, LANES), F32)
            for co in range(N_REL_W):
                vl = rpb_ref[base + rl * N_REL_W + co] if rl is not None else 0.0
                vr = rpb_ref[base + rr * N_REL_W + co] if rr is not None else 0.0
                acc = jnp.where(coff == co, jnp.where(right, vr, vl), acc)
            ok = in_win
            if rl is None:
                ok = ok & right
            if rr is None:
                ok = ok & jnp.logical_not(right)
            t = jnp.where(ok, acc, NEG_INF)
        cache[(rl, rr)] = t
        return t

    for kind in range(3):
        for qr in range(NA_QUAD):
            row = [tile(_na_row_offset(kind, qr, 2 * p), _na_row_offset(kind, qr, 2 * p + 1))
                   for p in range(NA_WIN_ROWS // 2)]
            o_ref[0, kind, GRID_W * qr:GRID_W * (qr + 1), :] = jnp.concatenate(row, axis=1)


def _na_bias_tables(rpb_flat):
    return pl.pallas_call(
        _bias_kernel,
        grid=(NA_HEADS,),
        in_specs=[pl.BlockSpec(memory_space=pltpu.SMEM)],
        out_specs=pl.BlockSpec((1, 3, NA_QUAD * GRID_W, NA_WIN_ROWS * GRID_W), lambda h: (h, 0, 0, 0)),
        out_shape=jax.ShapeDtypeStruct((NA_HEADS, 3, NA_QUAD * GRID_W, NA_WIN_ROWS * GRID_W), F32),
        compiler_params=_params(("arbitrary",)),
        name="na_bias_tables",
    )(rpb_flat)


def _na_kernel(q_ref, k_ref, v_ref, ck_ref, cv_ref, tbl_ref, o_ref):
    g = pl.program_id(2)
    u0 = jnp.clip(NA_QUAD * g - NA_QUAD, 0, GRID_ROWS - NA_WIN_ROWS)
    start = pl.multiple_of(u0 * GRID_W, GRID_W)
    nwin = NA_WIN_ROWS * GRID_W
    for j in range(2):
        q = q_ref[0, j]
        kw = k_ref[0, j, pl.ds(start, nwin), :]
        vw = v_ref[0, j, pl.ds(start, nwin), :]
        kc = ck_ref[0, 0, j].astype(BF16)
        vc = cv_ref[0, 0, j].astype(BF16)
        sl = _nt(q, kw) + tbl_ref[j, 0]
        sc = _nt(q, kc)
        m = jnp.maximum(jnp.max(sl, axis=-1, keepdims=True), jnp.max(sc, axis=-1, keepdims=True))
        el = jnp.exp(sl - m)
        ec = jnp.exp(sc - m)
        l = jnp.sum(el, axis=-1, keepdims=True) + jnp.sum(ec, axis=-1, keepdims=True)
        o = (_mm(el.astype(BF16), vw) + _mm(ec.astype(BF16), vc)) / l
        o_ref[0, :, HEAD_DIM * j:HEAD_DIM * (j + 1)] = o.astype(BF16)


def _latent_na(naq, nak, nav, cache_k, cache_v, tbl):
    nq = GRID_ROWS // NA_QUAD
    rows = NA_QUAD * GRID_W
    kind = lambda g: jnp.minimum(g, 1) + g // (nq - 1)
    return pl.pallas_call(
        _na_kernel,
        grid=(DEC_BATCH, NA_HEADS // 2, nq),
        in_specs=[pl.BlockSpec((1, 2, rows, HEAD_DIM), lambda b, p, g: (b, p, g, 0)),
                  pl.BlockSpec((1, 2, DEC_SEQ, HEAD_DIM), lambda b, p, g: (b, p, 0, 0)),
                  pl.BlockSpec((1, 2, DEC_SEQ, HEAD_DIM), lambda b, p, g: (b, p, 0, 0)),
                  pl.BlockSpec((1, 1, 2, PAST_LEN, HEAD_DIM), lambda b, p, g: (b, 0, p, 0, 0)),
                  pl.BlockSpec((1, 1, 2, PAST_LEN, HEAD_DIM), lambda b, p, g: (b, 0, p, 0, 0)),
                  pl.BlockSpec((2, 1, rows, NA_WIN_ROWS * GRID_W), lambda b, p, g: (p, kind(g), 0, 0))],
        out_specs=pl.BlockSpec((1, rows, 2 * HEAD_DIM), lambda b, p, g: (b, g, p)),
        out_shape=jax.ShapeDtypeStruct((DEC_BATCH, DEC_SEQ, NA_WIDTH), BF16),
        compiler_params=_params(("arbitrary",) * 3),
        name="latent_na",
    )(naq, nak, nav, cache_k, cache_v, tbl)


def _diff_kernel(q_ref, k_ref, v_ref, ck_ref, cv_ref, lam_ref, sub_ref, o_ref):
    lam = _lam(lam_ref)
    parts = []
    for m in range(2):
        q = q_ref[0, 0, m]
        sc = _nt(q, ck_ref[0, 0, 0, m].astype(BF16))
        sl = _nt(q, k_ref[0, 0, m])
        mx = jnp.maximum(jnp.max(sc, axis=-1, keepdims=True), jnp.max(sl, axis=-1, keepdims=True))
        ec = jnp.exp(sc - mx)
        el = jnp.exp(sl - mx)
        l = jnp.sum(ec, axis=-1, keepdims=True) + jnp.sum(el, axis=-1, keepdims=True)
        parts.append((ec, el, l))
    (ec1, el1, l1), (ec2, el2, l2) = parts
    r1 = 1.0 / l1
    r2 = lam / l2
    ac = (ec1 * r1 - ec2 * r2).astype(BF16)
    al = (el1 * r1 - el2 * r2).astype(BF16)
    o = _mm(ac, cv_ref[0, 0, 0].astype(BF16)) + _mm(al, v_ref[0, 0])
    o_ref[0] = _sub_norm(o, sub_ref[...]).astype(BF16)


def _latent_diff(dfq, dfk, dfv, cache_k, cache_v, lamv, sub):
    nt = DEC_SEQ // ROW_TILE
    full = lambda shape: pl.BlockSpec(shape, lambda b, h, t: (0,) * len(shape))
    return pl.pallas_call(
        _diff_kernel,
        grid=(DEC_BATCH, DIFF_HEADS, nt),
        in_specs=[pl.BlockSpec((1, 1, 2, ROW_TILE, HEAD_DIM), lambda b, h, t: (b, h, 0, t, 0)),
                  pl.BlockSpec((1, 1, 2, DEC_SEQ, HEAD_DIM), lambda b, h, t: (b, h, 0, 0, 0)),
                  pl.BlockSpec((1, 1, DEC_SEQ, 2 * HEAD_DIM), lambda b, h, t: (b, h, 0, 0)),
                  pl.BlockSpec((1, 1, 1, 2, PAST_LEN, HEAD_DIM), lambda b, h, t: (b, 0, h, 0, 0, 0)),
                  pl.BlockSpec((1, 1, 1, PAST_LEN, 2 * HEAD_DIM), lambda b, h, t: (b, 0, h, 0, 0)),
                  full((4, HEAD_DIM)), full((1, 2 * HEAD_DIM))],
        out_specs=pl.BlockSpec((1, ROW_TILE, 2 * HEAD_DIM), lambda b, h, t: (b, t, h)),
        out_shape=jax.ShapeDtypeStruct((DEC_BATCH, DEC_SEQ, DIFF_WIDTH), BF16),
        compiler_params=_params(("arbitrary",) * 3),
        name="latent_diff",
    )(dfq, dfk, dfv, cache_k, cache_v, lamv, sub)


def _lat_out_kernel(x_ref, na_ref, df_ref, mod_ref, wout_ref, lng_ref, lnb_ref, wrt_ref,
                    x1_ref, h2_ref, aff_ref):
    b = pl.program_id(0)
    x = x_ref[0]
    _, _, g1, sh2, sc2, _ = _mod_rows(mod_ref, 1 + b)
    a = _mm(na_ref[0], wout_ref[0:NA_WIDTH, :]) + _mm(df_ref[0], wout_ref[NA_WIDTH:, :])
    x1 = _layer_norm(DEEPNORM_ALPHA * x + g1 * a, lng_ref[...], lnb_ref[...])
    x1_ref[0] = x1
    h2 = x1 * (1.0 + sc2) + sh2
    h2_ref[0] = h2.astype(BF16)
    aff_ref[0] = _router_aff_t(h2, wrt_ref[...])


def _latent_out(x, na_o, df_o, mod, w_out, ln_g, ln_b, wrt):
    nt = DEC_SEQ // ROW_TILE
    full = lambda shape: pl.BlockSpec(shape, lambda b, t: (0,) * len(shape))
    tok = lambda w: pl.BlockSpec((1, ROW_TILE, w), lambda b, t: (b, t, 0))
    return pl.pallas_call(
        _lat_out_kernel,
        grid=(DEC_BATCH, nt),
        in_specs=[tok(D_MODEL), tok(NA_WIDTH), tok(DIFF_WIDTH), full((8, 6 * D_MODEL)),
                  full((D_MODEL, D_MODEL)), full((1, D_MODEL)), full((1, D_MODEL)),
                  full((N_EXPERTS, D_MODEL))],
        out_specs=[tok(D_MODEL), tok(D_MODEL),
                   pl.BlockSpec((1, N_EXPERTS, ROW_TILE), lambda b, t: (b, 0, t))],
        out_shape=[jax.ShapeDtypeStruct((DEC_BATCH, DEC_SEQ, D_MODEL), F32),
                   jax.ShapeDtypeStruct((DEC_BATCH, DEC_SEQ, D_MODEL), BF16),
                   jax.ShapeDtypeStruct((DEC_BATCH, N_EXPERTS, DEC_SEQ), F32)],
        compiler_params=_params(("arbitrary", "arbitrary")),
        name="latent_out",
    )(x, na_o, df_o, mod, w_out, ln_g, ln_b, wrt)


def _ranks_first(va, ia, vb, ib):
    return (va > vb) | ((va == vb) & (ia < ib))


def _route_kernel(aff_ref, idx_ref, gate_ref, *, cap, n):
    rows = aff_ref.shape[0]
    nb = n // LANES
    lane = lax.broadcasted_iota(I32, (rows, LANES), 1)
    v = [aff_ref[:, LANES * b:LANES * (b + 1)] for b in range(nb)]
    ix = [lane + LANES * b for b in range(nb)]
    k = 2
    while k <= n:
        j = k // 2
        while j >= 1:
            if j >= LANES:
                sb = j // LANES
                for lo in range(nb):
                    if lo & sb:
                        continue
                    hi = lo | sb
                    f = _ranks_first(v[lo], ix[lo], v[hi], ix[hi])
                    if k < n and (LANES * lo) & k:
                        f = jnp.logical_not(f)
                    v[lo], v[hi] = jnp.where(f, v[lo], v[hi]), jnp.where(f, v[hi], v[lo])
                    ix[lo], ix[hi] = jnp.where(f, ix[lo], ix[hi]), jnp.where(f, ix[hi], ix[lo])
            else:
                upper = (lane & j) != 0
                if k < LANES:
                    flip_lanes = jnp.logical_xor(upper, (lane & k) != 0)
                for b in range(nb):
                    pv = jnp.where(upper, pltpu.roll(v[b], j, 1), pltpu.roll(v[b], LANES - j, 1))
                    pi = jnp.where(upper, pltpu.roll(ix[b], j, 1), pltpu.roll(ix[b], LANES - j, 1))
                    f = _ranks_first(v[b], ix[b], pv, pi)
                    if k < LANES:
                        flip = flip_lanes
                    elif k < n and (LANES * b) & k:
                        flip = jnp.logical_not(upper)
                    else:
                        flip = upper
                    keep = jnp.logical_xor(f, flip)
                    v[b] = jnp.where(keep, v[b], pv)
                    ix[b] = jnp.where(keep, ix[b], pi)
            j //= 2
        k *= 2
    if cap < LANES:
        idx_ref[...] = ix[0][:, :cap]
        gate_ref[...] = v[0][:, :cap]
    else:
        idx_ref[...] = jnp.concatenate(ix[:cap // LANES], axis=1)
        gate_ref[...] = jnp.concatenate(v[:cap // LANES], axis=1)


def _route(aff_t, cap):
    rows, n = aff_t.shape
    return pl.pallas_call(
        functools.partial(_route_kernel, cap=cap, n=n),
        out_shape=[jax.ShapeDtypeStruct((rows, cap), I32), jax.ShapeDtypeStruct((rows, cap), F32)],
        compiler_params=_params(None),
        name=f"route_{n}",
    )(aff_t)


def _gather_ctx_kernel(idx_ref, h_ref, x_ref):
    nslot = N_EXPERTS * CAP_CTX
    hit = lax.broadcasted_iota(I32, (nslot, SEQ), 1) == idx_ref[0]
    xs = _mm(jnp.where(hit, 1.0, 0.0).astype(BF16), h_ref[0])
    x_ref[...] = xs.astype(BF16).reshape(N_EXPERTS, CAP_CTX, D_MODEL)


def _gather_ctx(idx_col, h2):
    nslot = N_EXPERTS * CAP_CTX
    return pl.pallas_call(
        _gather_ctx_kernel,
        grid=(BATCH,),
        in_specs=[pl.BlockSpec((1, nslot, 1), lambda s: (s, 0, 0)),
                  pl.BlockSpec((1, SEQ, D_MODEL), lambda s: (s, 0, 0))],
        out_specs=pl.BlockSpec((N_EXPERTS, CAP_CTX, D_MODEL), lambda s: (0, s, 0)),
        out_shape=jax.ShapeDtypeStruct((N_EXPERTS, SLOTS_CTX, D_MODEL), BF16),
        compiler_params=_params(("arbitrary",)),
        name="gather_ctx",
    )(idx_col, h2)


def _gather_lat_kernel(idx_ref, h_ref, x_ref):
    hit = lax.broadcasted_iota(I32, (CAP_LAT, DEC_SEQ), 1) == idx_ref[0, 0]
    x_ref[0] = _mm(jnp.where(hit, 1.0, 0.0).astype(BF16), h_ref[0]).astype(BF16)


def _gather_lat(idx_col, h2):
    return pl.pallas_call(
        _gather_lat_kernel,
        grid=(DEC_BATCH, N_EXPERTS),
        in_specs=[pl.BlockSpec((1, 1, CAP_LAT, 1), lambda s, e: (s, e, 0, 0)),
                  pl.BlockSpec((1, DEC_SEQ, D_MODEL), lambda s, e: (s, 0, 0))],
        out_specs=pl.BlockSpec((1, CAP_LAT, D_MODEL), lambda s, e: (e, s, 0)),
        out_shape=jax.ShapeDtypeStruct((N_EXPERTS, SLOTS_LAT, D_MODEL), BF16),
        compiler_params=_params(("arbitrary", "arbitrary")),
        name="gather_lat",
    )(idx_col, h2)


def _ffn_kernel(xc_ref, xl_ref, gc_ref, gl_ref, wg_ref, wu_ref, wd_ref, yc_ref, yl_ref, acc_ref):
    j = pl.program_id(1)
    wg = wg_ref[0].astype(BF16)
    wu = wu_ref[0].astype(BF16)
    wd = wd_ref[0].astype(BF16)

    @pl.when(j == 0)
    def _():
        acc_ref[...] = jnp.zeros_like(acc_ref)

    for i, x_ref in enumerate((xc_ref, xl_ref)):
        x = x_ref[0]
        a = _mm(x, wg)
        u = _mm(x, wu)
        hmid = (a * (1.0 / (1.0 + jnp.exp(-a))) * u).astype(BF16)
        acc_ref[i] += _mm(hmid, wd)

    @pl.when(j == pl.num_programs(1) - 1)
    def _():
        yc_ref[0] = (acc_ref[0] * gc_ref[0]).astype(BF16)
        yl_ref[0] = (acc_ref[1] * gl_ref[0]).astype(BF16)


def _expert_ffn(xc, xl, gc, gl, w_gate, w_up, w_down):
    nj = D_FF // FF_TILE
    xs = lambda n: pl.BlockSpec((1, n, D_MODEL), lambda e, j: (e, 0, 0))
    gs = lambda n: pl.BlockSpec((1, n, 1), lambda e, j: (e, 0, 0))
    return pl.pallas_call(
        _ffn_kernel,
        grid=(N_EXPERTS, nj),
        in_specs=[xs(SLOTS_CTX), xs(SLOTS_LAT), gs(SLOTS_CTX), gs(SLOTS_LAT),
                  pl.BlockSpec((1, D_MODEL, FF_TILE), lambda e, j: (e, 0, j)),
                  pl.BlockSpec((1, D_MODEL, FF_TILE), lambda e, j: (e, 0, j)),
                  pl.BlockSpec((1, FF_TILE, D_MODEL), lambda e, j: (e, j, 0))],
        out_specs=[xs(SLOTS_CTX), xs(SLOTS_LAT)],
        out_shape=[jax.ShapeDtypeStruct((N_EXPERTS, SLOTS_CTX, D_MODEL), BF16),
                   jax.ShapeDtypeStruct((N_EXPERTS, SLOTS_LAT, D_MODEL), BF16)],
        scratch_shapes=[pltpu.VMEM((2, SLOTS_CTX, D_MODEL), F32)],
        compiler_params=_params(("arbitrary", "arbitrary")),
        name="expert_ffn",
    )(xc, xl, gc, gl, w_gate, w_up, w_down)


def _combine_ctx_kernel(y_ref, idx_ref, x1_ref, mod_ref, lng_ref, lnb_ref, o_ref):
    g2 = _mod_rows(mod_ref, 0)[5]
    nslot = N_EXPERTS * CAP_CTX
    hit = lax.broadcasted_iota(I32, (SEQ, nslot), 0) == idx_ref[0]
    y = y_ref[...].reshape(nslot, D_MODEL)
    f = _mm(jnp.where(hit, 1.0, 0.0).astype(BF16), y)
    o_ref[0] = _layer_norm(DEEPNORM_ALPHA * x1_ref[0] + g2 * f, lng_ref[...], lnb_ref[...])


def _combine_ctx(yc, idx_row, x1, mod, ln_g, ln_b):
    full = lambda shape: pl.BlockSpec(shape, lambda s: (0,) * len(shape))
    return pl.pallas_call(
        _combine_ctx_kernel,
        grid=(BATCH,),
        in_specs=[pl.BlockSpec((N_EXPERTS, CAP_CTX, D_MODEL), lambda s: (0, s, 0)),
                  pl.BlockSpec((1, 1, N_EXPERTS * CAP_CTX), lambda s: (s, 0, 0)),
                  pl.BlockSpec((1, SEQ, D_MODEL), lambda s: (s, 0, 0)),
                  full((8, 6 * D_MODEL)), full((1, D_MODEL)), full((1, D_MODEL))],
        out_specs=pl.BlockSpec((1, SEQ, D_MODEL), lambda s: (s, 0, 0)),
        out_shape=jax.ShapeDtypeStruct((BATCH, SEQ, D_MODEL), F32),
        compiler_params=_params(("arbitrary",)),
        name="combine_ctx",
    )(yc, idx_row, x1, mod, ln_g, ln_b)


def _combine_lat_kernel(y_ref, idx_ref, x1_ref, mod_ref, lng_ref, lnb_ref, o_ref):
    s = pl.program_id(0)
    g2 = _mod_rows(mod_ref, 1 + s)[5]
    tok = lax.broadcasted_iota(I32, (ROW_TILE, CAP_LAT), 0) + ROW_TILE * pl.program_id(1)
    f = jnp.zeros((ROW_TILE, D_MODEL), F32)
    for e in range(N_EXPERTS):
        hit = tok == idx_ref[0, e:e + 1, :]
        f = f + _mm(jnp.where(hit, 1.0, 0.0).astype(BF16), y_ref[e])
    o_ref[0] = _layer_norm(DEEPNORM_ALPHA * x1_ref[0] + g2 * f, lng_ref[...], lnb_ref[...])


def _combine_lat(yl, idx, x1, mod, ln_g, ln_b):
    nt = DEC_SEQ // ROW_TILE
    full = lambda shape: pl.BlockSpec(shape, lambda s, t: (0,) * len(shape))
    return pl.pallas_call(
        _combine_lat_kernel,
        grid=(DEC_BATCH, nt),
        in_specs=[pl.BlockSpec((N_EXPERTS, CAP_LAT, D_MODEL), lambda s, t: (0, s, 0)),
                  pl.BlockSpec((1, N_EXPERTS, CAP_LAT), lambda s, t: (s, 0, 0)),
                  pl.BlockSpec((1, ROW_TILE, D_MODEL), lambda s, t: (s, t, 0)),
                  full((8, 6 * D_MODEL)), full((1, D_MODEL)), full((1, D_MODEL))],
        out_specs=pl.BlockSpec((1, ROW_TILE, D_MODEL), lambda s, t: (s, t, 0)),
        out_shape=jax.ShapeDtypeStruct((DEC_BATCH, DEC_SEQ, D_MODEL), F32),
        compiler_params=_params(("arbitrary", "arbitrary")),
        name="combine_lat",
    )(yl, idx, x1, mod, ln_g, ln_b)


def _rope_tables():
    t = jnp.arange(DEC_SEQ)
    row = (t // GRID_W).astype(F32)
    col = (t % GRID_W).astype(F32)
    half = HEAD_DIM // 2
    inv = ROPE_THETA ** (-jnp.arange(0, half, 2, dtype=F32) / half)
    ang_r = row[:, None] * inv[None, :]
    ang_c = col[:, None] * inv[None, :]
    ang = jnp.concatenate([ang_r, ang_r, ang_c, ang_c], axis=-1)
    cos = jnp.tile(jnp.cos(ang), (1, 2))
    sin = jnp.tile(jnp.sin(ang), (1, 2))
    first = (jnp.arange(LANES) % 32) < 16
    return cos, jnp.where(first[None, :], -sin, sin)


def kernel(x_prompt, x_sample, cache_na_k, cache_na_v, cache_diff_k, cache_diff_v, c, c_ctx, w_ada, b_ada, w_in, w_out, na_rel_bias, lambda_q1, lambda_k1, lambda_q2, lambda_k2, subln_g, ln1_g, ln1_b, ln2_g, ln2_b, w_router, w_gate, w_up, w_down):
    l = 0
    cvec = jnp.concatenate([c_ctx[None, :], c, jnp.zeros((8 - 1 - DEC_BATCH, D_MODEL), F32)], axis=0)
    mod = _modulation(cvec, w_ada[l], b_ada[l][None, :])
    lamv = jnp.stack([lambda_q1[l], lambda_k1[l], lambda_q2[l], lambda_k2[l]], axis=0)
    w_in_b = w_in[l].astype(BF16)
    w_out_b = w_out[l].astype(BF16)
    sub = subln_g[l][None, :]
    wrt = w_router[l].T

    x1c, h2c, new_na_k, new_na_v, new_diff_k, new_diff_v, affc = _context_block(
        x_prompt, mod, lamv, w_in_b, w_out_b, sub, ln1_g[l][None, :], ln1_b[l][None, :], wrt)

    cos, sin = _rope_tables()
    naq, nak, nav, dfq, dfk, dfv = _latent_qkv(x_sample, mod, w_in_b, cos, sin)
    tbl = _na_bias_tables(na_rel_bias[l].reshape(-1))
    na_o = _latent_na(naq, nak, nav, cache_na_k, cache_na_v, tbl)
    df_o = _latent_diff(dfq, dfk, dfv, cache_diff_k, cache_diff_v, lamv, sub)
    x1l, h2l, affl = _latent_out(x_sample, na_o, df_o, mod, w_out_b,
                                 ln1_g[l][None, :], ln1_b[l][None, :], wrt)

    idxc, gatec = _route(affc.reshape(BATCH * N_EXPERTS, SEQ), CAP_CTX)
    idxl, gatel = _route(affl.reshape(DEC_BATCH * N_EXPERTS, DEC_SEQ), CAP_LAT)
    xc = _gather_ctx(idxc.reshape(BATCH, N_EXPERTS * CAP_CTX, 1), h2c)
    xl = _gather_lat(idxl.reshape(DEC_BATCH, N_EXPERTS, CAP_LAT, 1), h2l)
    slot_major = lambda g, sets, cap: jnp.swapaxes(g.reshape(sets, N_EXPERTS, cap), 0, 1).reshape(
        N_EXPERTS, sets * cap, 1)
    yc, yl = _expert_ffn(xc, xl, slot_major(gatec, BATCH, CAP_CTX), slot_major(gatel, DEC_BATCH, CAP_LAT),
                         w_gate[l], w_up[l], w_down[l])
    y_prompt = _combine_ctx(yc, idxc.reshape(BATCH, 1, N_EXPERTS * CAP_CTX), x1c, mod,
                            ln2_g[l][None, :], ln2_b[l][None, :])
    y_sample = _combine_lat(yl, idxl.reshape(DEC_BATCH, N_EXPERTS, CAP_LAT), x1l, mod,
                            ln2_g[l][None, :], ln2_b[l][None, :])
    return (y_prompt, y_sample, new_na_k, new_na_v, new_diff_k, new_diff_v)
```

```python
import functools
import math

import jax
import jax.numpy as jnp
from jax import lax
from jax.experimental import pallas as pl
from jax.experimental.pallas import tpu as pltpu

F32 = jnp.float32
BF16 = jnp.bfloat16
I32 = jnp.int32

D_MODEL = 1024
BATCH = 16
SEQ = 256
DEC_BATCH = 2
DEC_SEQ = 2048
PAST_LEN = 256
GRID_W = 64
GRID_ROWS = DEC_SEQ // GRID_W
HEAD_DIM = 64
NA_HEADS = 8
DIFF_HEADS = 4
NA_WIDTH = NA_HEADS * HEAD_DIM
DIFF_WIDTH = DIFF_HEADS * 2 * HEAD_DIM
QKV_WIDTH = 3 * NA_WIDTH + 3 * DIFF_WIDTH
NA_WIN_H = 8
NA_WIN_W = 16
N_REL_H = 2 * NA_WIN_H - 1
N_REL_W = 2 * NA_WIN_W - 1
N_EXPERTS = 16
EC_CAPACITY_FACTOR = 2
D_FF = 2816
ROPE_THETA = 10000.0
NORM_EPS = 1e-5
NEG_INF = -1e30
DEPTH = 1
DEEPNORM_ALPHA = (2.0 * DEPTH) ** 0.25
LAMBDA_INIT = 0.8 - 0.6 * math.exp(-0.3 * 0)
SCALE = HEAD_DIM ** -0.5

OFF_NAQ = 0
OFF_NAK = NA_WIDTH
OFF_NAV = 2 * NA_WIDTH
OFF_DFQ = 3 * NA_WIDTH
OFF_DFK = 3 * NA_WIDTH + DIFF_WIDTH
OFF_DFV = 3 * NA_WIDTH + 2 * DIFF_WIDTH

CAP_CTX = EC_CAPACITY_FACTOR * SEQ // N_EXPERTS
CAP_LAT = EC_CAPACITY_FACTOR * DEC_SEQ // N_EXPERTS
SLOTS_CTX = BATCH * CAP_CTX
SLOTS_LAT = DEC_BATCH * CAP_LAT

ROW_TILE = 256
NA_QUAD = 4
NA_WIN_ROWS = 12
FF_TILE = 256
FF_HALF = D_FF // 2
OUT_TILE = 512
LANES = 128

VMEM_LIMIT = 48 * 1024 * 1024
FFN_VMEM_LIMIT = 58 * 1024 * 1024


def _params(sem, vmem=VMEM_LIMIT):
    return pltpu.CompilerParams(dimension_semantics=sem, vmem_limit_bytes=vmem)


def _mm(a, b):
    return jnp.dot(a, b, preferred_element_type=F32)


def _nt(a, b):
    return lax.dot_general(a, b, (((1,), (1,)), ((), ())), preferred_element_type=F32)


def _split(a):
    hi = a.astype(BF16)
    lo = (a - hi.astype(F32)).astype(BF16)
    return hi, lo


def _mm3(a, b):
    ah, al = _split(a)
    bh, bl = _split(b)
    return _mm(ah, bh) + _mm(al, bh) + _mm(ah, bl)


def _nt3(a, b):
    ah, al = _split(a)
    bh, bl = _split(b)
    return _nt(ah, bh) + _nt(al, bh) + _nt(ah, bl)


def _layer_norm(y, g, b):
    mu = jnp.mean(y, axis=-1, keepdims=True)
    d = y - mu
    var = jnp.mean(d * d, axis=-1, keepdims=True)
    return d * lax.rsqrt(var + NORM_EPS) * g + b


def _lam(l_ref):
    l = l_ref[...]
    a = jnp.sum(l[0:1] * l[1:2], axis=-1, keepdims=True)
    b = jnp.sum(l[2:3] * l[3:4], axis=-1, keepdims=True)
    return jnp.exp(a) - jnp.exp(b) + LAMBDA_INIT


def _mod_rows(mod_ref, row):
    return [mod_ref[pl.ds(row, 1), i * D_MODEL:(i + 1) * D_MODEL] for i in range(6)]


def _sub_norm(o, sub):
    ms = jnp.mean(o * o, axis=-1, keepdims=True)
    return o * lax.rsqrt(ms + NORM_EPS) * sub * (1.0 - LAMBDA_INIT)


def _router_aff_t(h2, wrt):
    lg = _nt3(wrt, h2)
    m = jnp.max(lg, axis=0, keepdims=True)
    e = jnp.exp(lg - m)
    return e / jnp.sum(e, axis=0, keepdims=True)


def _mod_kernel(c_ref, w_ref, b_ref, o_ref):
    c = c_ref[...]
    s = c * (1.0 / (1.0 + jnp.exp(-c)))
    o_ref[...] = _mm3(s, w_ref[...]) + b_ref[...]


def _modulation(cvec, w_ada, b_ada):
    ncol = 6 * D_MODEL
    return pl.pallas_call(
        _mod_kernel,
        grid=(6,),
        in_specs=[pl.BlockSpec((8, D_MODEL), lambda j: (0, 0)),
                  pl.BlockSpec((D_MODEL, D_MODEL), lambda j: (0, j)),
                  pl.BlockSpec((1, D_MODEL), lambda j: (0, j))],
        out_specs=pl.BlockSpec((8, D_MODEL), lambda j: (0, j)),
        out_shape=jax.ShapeDtypeStruct((8, ncol), F32),
        compiler_params=_params(("arbitrary",)),
        name="modulation",
    )(cvec, w_ada, b_ada)


def _ctx_kernel(x_ref, mod_ref, lam_ref, win_ref, wout_ref, sub_ref, lng_ref, lnb_ref, wrt_ref,
                x1_ref, h2_ref, nak_ref, nav_ref, dfk_ref, dfv_ref, aff_ref, qkv_scr, o_scr):
    x = x_ref[0]
    sh1, sc1, g1, sh2, sc2, g2 = _mod_rows(mod_ref, 0)
    h = (x * (1.0 + sc1) + sh1).astype(BF16)
    qkv_scr[...] = _mm(h, win_ref[...])

    for hh in range(NA_HEADS):
        nak_ref[0, 0, hh] = qkv_scr[:, OFF_NAK + HEAD_DIM * hh:OFF_NAK + HEAD_DIM * (hh + 1)]
        nav_ref[0, 0, hh] = qkv_scr[:, OFF_NAV + HEAD_DIM * hh:OFF_NAV + HEAD_DIM * (hh + 1)]
    for hh in range(DIFF_HEADS):
        for m in range(2):
            c0 = OFF_DFK + HEAD_DIM * (2 * hh + m)
            dfk_ref[0, 0, hh, m] = qkv_scr[:, c0:c0 + HEAD_DIM]
        c0 = OFF_DFV + 2 * HEAD_DIM * hh
        dfv_ref[0, 0, hh] = qkv_scr[:, c0:c0 + 2 * HEAD_DIM]

    lam = _lam(lam_ref)

    def probs(c_q, c_k):
        q = (qkv_scr[:, c_q:c_q + HEAD_DIM] * SCALE).astype(BF16)
        k = qkv_scr[:, c_k:c_k + HEAD_DIM].astype(BF16)
        s = _nt(q, k)
        e = jnp.exp(s - jnp.max(s, axis=-1, keepdims=True))
        return e, jnp.sum(e, axis=-1, keepdims=True)

    for hh in range(NA_HEADS):
        e, l = probs(OFF_NAQ + HEAD_DIM * hh, OFF_NAK + HEAD_DIM * hh)
        v = qkv_scr[:, OFF_NAV + HEAD_DIM * hh:OFF_NAV + HEAD_DIM * (hh + 1)].astype(BF16)
        o = _mm(e.astype(BF16), v) / l
        o_scr[:, HEAD_DIM * hh:HEAD_DIM * (hh + 1)] = o.astype(BF16)

    sub = sub_ref[...]
    for hh in range(DIFF_HEADS):
        e1, l1 = probs(OFF_DFQ + HEAD_DIM * (2 * hh), OFF_DFK + HEAD_DIM * (2 * hh))
        e2, l2 = probs(OFF_DFQ + HEAD_DIM * (2 * hh + 1), OFF_DFK + HEAD_DIM * (2 * hh + 1))
        a = e1 * (1.0 / l1) - e2 * (lam / l2)
        c0 = OFF_DFV + 2 * HEAD_DIM * hh
        v = qkv_scr[:, c0:c0 + 2 * HEAD_DIM].astype(BF16)
        o = _sub_norm(_mm(a.astype(BF16), v), sub)
        c1 = NA_WIDTH + 2 * HEAD_DIM * hh
        o_scr[:, c1:c1 + 2 * HEAD_DIM] = o.astype(BF16)

    a = _mm(o_scr[...], wout_ref[...])
    x1 = _layer_norm(DEEPNORM_ALPHA * x + g1 * a, lng_ref[...], lnb_ref[...])
    x1_ref[0] = x1
    h2 = x1 * (1.0 + sc2) + sh2
    h2_ref[0] = h2.astype(BF16)
    aff_ref[0] = _router_aff_t(h2, wrt_ref[...])


def _context_block(x, mod, lamv, w_in, w_out, sub, ln_g, ln_b, wrt):
    full = lambda shape: pl.BlockSpec(shape, lambda b: (0,) * len(shape))
    return pl.pallas_call(
        _ctx_kernel,
        grid=(BATCH,),
        in_specs=[pl.BlockSpec((1, SEQ, D_MODEL), lambda b: (b, 0, 0)),
                  full((8, 6 * D_MODEL)), full((4, HEAD_DIM)),
                  full((D_MODEL, QKV_WIDTH)), full((D_MODEL, D_MODEL)),
                  full((1, 2 * HEAD_DIM)), full((1, D_MODEL)), full((1, D_MODEL)),
                  full((N_EXPERTS, D_MODEL))],
        out_specs=[pl.BlockSpec((1, SEQ, D_MODEL), lambda b: (b, 0, 0)),
                   pl.BlockSpec((1, SEQ, D_MODEL), lambda b: (b, 0, 0)),
                   pl.BlockSpec((1, 1, NA_HEADS, SEQ, HEAD_DIM), lambda b: (b, 0, 0, 0, 0)),
                   pl.BlockSpec((1, 1, NA_HEADS, SEQ, HEAD_DIM), lambda b: (b, 0, 0, 0, 0)),
                   pl.BlockSpec((1, 1, DIFF_HEADS, 2, SEQ, HEAD_DIM), lambda b: (b, 0, 0, 0, 0, 0)),
                   pl.BlockSpec((1, 1, DIFF_HEADS, SEQ, 2 * HEAD_DIM), lambda b: (b, 0, 0, 0, 0)),
                   pl.BlockSpec((1, N_EXPERTS, SEQ), lambda b: (b, 0, 0))],
        out_shape=[jax.ShapeDtypeStruct((BATCH, SEQ, D_MODEL), F32),
                   jax.ShapeDtypeStruct((BATCH, SEQ, D_MODEL), BF16),
                   jax.ShapeDtypeStruct((BATCH, 1, NA_HEADS, SEQ, HEAD_DIM), F32),
                   jax.ShapeDtypeStruct((BATCH, 1, NA_HEADS, SEQ, HEAD_DIM), F32),
                   jax.ShapeDtypeStruct((BATCH, 1, DIFF_HEADS, 2, SEQ, HEAD_DIM), F32),
                   jax.ShapeDtypeStruct((BATCH, 1, DIFF_HEADS, SEQ, 2 * HEAD_DIM), F32),
                   jax.ShapeDtypeStruct((BATCH, N_EXPERTS, SEQ), F32)],
        scratch_shapes=[pltpu.VMEM((SEQ, QKV_WIDTH), F32), pltpu.VMEM((SEQ, D_MODEL), BF16)],
        compiler_params=_params(("arbitrary",)),
        name="context_block",
    )(x, mod, lamv, w_in, w_out, sub, ln_g, ln_b, wrt)


def _lat_qkv_kernel(x_ref, mod_ref, win_ref, cos_ref, sin_ref,
                    naq_ref, nak_ref, nav_ref, dfq_ref, dfk_ref, dfv_ref, qkv_scr):
    b = pl.program_id(0)
    x = x_ref[0]
    sh1, sc1 = _mod_rows(mod_ref, 1 + b)[:2]
    h = (x * (1.0 + sc1) + sh1).astype(BF16)
    qkv_scr[...] = _mm(h, win_ref[...])

    for hh in range(NA_HEADS):
        lo, hi = HEAD_DIM * hh, HEAD_DIM * (hh + 1)
        naq_ref[0, hh] = (qkv_scr[:, OFF_NAQ + lo:OFF_NAQ + hi] * SCALE).astype(BF16)
        nak_ref[0, hh] = qkv_scr[:, OFF_NAK + lo:OFF_NAK + hi].astype(BF16)
        nav_ref[0, hh] = qkv_scr[:, OFF_NAV + lo:OFF_NAV + hi].astype(BF16)

    cos = cos_ref[...]
    sin = sin_ref[...]
    lane = lax.broadcasted_iota(I32, (ROW_TILE, LANES), 1)
    first = (lane & 31) < 16

    def rope(t):
        rot = jnp.where(first, pltpu.roll(t, LANES - 16, 1), pltpu.roll(t, 16, 1))
        return t * cos + rot * sin

    for hh in range(DIFF_HEADS):
        lo = 2 * HEAD_DIM * hh
        q = rope(qkv_scr[:, OFF_DFQ + lo:OFF_DFQ + lo + LANES]) * SCALE
        k = rope(qkv_scr[:, OFF_DFK + lo:OFF_DFK + lo + LANES])
        for m in range(2):
            dfq_ref[0, hh, m] = q[:, HEAD_DIM * m:HEAD_DIM * (m + 1)].astype(BF16)
            dfk_ref[0, hh, m] = k[:, HEAD_DIM * m:HEAD_DIM * (m + 1)].astype(BF16)
        dfv_ref[0, hh] = qkv_scr[:, OFF_DFV + lo:OFF_DFV + lo + LANES].astype(BF16)


def _latent_qkv(x, mod, w_in, cos, sin):
    nt = DEC_SEQ // ROW_TILE
    full = lambda shape: pl.BlockSpec(shape, lambda b, t: (0,) * len(shape))
    hs = lambda nh: pl.BlockSpec((1, nh, ROW_TILE, HEAD_DIM), lambda b, t: (b, 0, t, 0))
    return pl.pallas_call(
        _lat_qkv_kernel,
        grid=(DEC_BATCH, nt),
        in_specs=[pl.BlockSpec((1, ROW_TILE, D_MODEL), lambda b, t: (b, t, 0)),
                  full((8, 6 * D_MODEL)), full((D_MODEL, QKV_WIDTH)),
                  pl.BlockSpec((ROW_TILE, LANES), lambda b, t: (t, 0)),
                  pl.BlockSpec((ROW_TILE, LANES), lambda b, t: (t, 0))],
        out_specs=[hs(NA_HEADS), hs(NA_HEADS), hs(NA_HEADS),
                   pl.BlockSpec((1, DIFF_HEADS, 2, ROW_TILE, HEAD_DIM), lambda b, t: (b, 0, 0, t, 0)),
                   pl.BlockSpec((1, DIFF_HEADS, 2, ROW_TILE, HEAD_DIM), lambda b, t: (b, 0, 0, t, 0)),
                   pl.BlockSpec((1, DIFF_HEADS, ROW_TILE, 2 * HEAD_DIM), lambda b, t: (b, 0, t, 0))],
        out_shape=[jax.ShapeDtypeStruct((DEC_BATCH, NA_HEADS, DEC_SEQ, HEAD_DIM), BF16)] * 3
        + [jax.ShapeDtypeStruct((DEC_BATCH, DIFF_HEADS, 2, DEC_SEQ, HEAD_DIM), BF16)] * 2
        + [jax.ShapeDtypeStruct((DEC_BATCH, DIFF_HEADS, DEC_SEQ, 2 * HEAD_DIM), BF16)],
        scratch_shapes=[pltpu.VMEM((ROW_TILE, QKV_WIDTH), F32)],
        compiler_params=_params(("arbitrary", "arbitrary")),
        name="latent_qkv",
    )(x, mod, w_in, cos, sin)


def _na_row_offset(kind, qr, kr):
    if kind == 0:
        return kr - qr + 7 if kr < NA_WIN_H else None
    if kind == 1:
        return kr - qr + 3 if qr <= kr < qr + NA_WIN_H else None
    return kr - qr - 1 if NA_WIN_ROWS - NA_WIN_H <= kr else None


def _bias_kernel(rpb_ref, o_ref):
    h = pl.program_id(0)
    base = h * (N_REL_H * N_REL_W)
    wq = lax.broadcasted_iota(I32, (GRID_W, LANES), 0)
    c = lax.broadcasted_iota(I32, (GRID_W, LANES), 1)
    wk = c & (GRID_W - 1)
    right = c >= GRID_W
    coff = jnp.clip(wk - wq, -(NA_WIN_W - 1), NA_WIN_W - 1) + (NA_WIN_W - 1)
    c0 = jnp.clip(wq - NA_WIN_W // 2, 0, GRID_W - NA_WIN_W)
    in_win = (wk >= c0) & (wk < c0 + NA_WIN_W)
    cache = {}

    def tile(rl, rr):
        if (rl, rr) in cache:
            return cache[(rl, rr)]
        if rl is None and rr is None:
            t = jnp.full((GRID_W, LANES), NEG_INF, F32)
        else:
            acc = jnp.zeros((GRID_W, LANES), F32)
            for co in range(N_REL_W):
                vl = rpb_ref[base + rl * N_REL_W + co] if rl is not None else 0.0
                vr = rpb_ref[base + rr * N_REL_W + co] if rr is not None else 0.0
                acc = jnp.where(coff == co, jnp.where(right, vr, vl), acc)
            ok = in_win
            if rl is None:
                ok = ok & right
            if rr is None:
                ok = ok & jnp.logical_not(right)
            t = jnp.where(ok, acc, NEG_INF)
        cache[(rl, rr)] = t
        return t

    for kind in range(3):
        for qr in range(NA_QUAD):
            row = [tile(_na_row_offset(kind, qr, 2 * p), _na_row_offset(kind, qr, 2 * p + 1))
                   for p in range(NA_WIN_ROWS // 2)]
            o_ref[0, kind, GRID_W * qr:GRID_W * (qr + 1), :] = jnp.concatenate(row, axis=1)


def _na_bias_tables(rpb_flat):
    return pl.pallas_call(
        _bias_kernel,
        grid=(NA_HEADS,),
        in_specs=[pl.BlockSpec(memory_space=pltpu.SMEM)],
        out_specs=pl.BlockSpec((1, 3, NA_QUAD * GRID_W, NA_WIN_ROWS * GRID_W), lambda h: (h, 0, 0, 0)),
        out_shape=jax.ShapeDtypeStruct((NA_HEADS, 3, NA_QUAD * GRID_W, NA_WIN_ROWS * GRID_W), F32),
        compiler_params=_params(("arbitrary",)),
        name="na_bias_tables",
    )(rpb_flat)


def _na_kernel(q_ref, k_ref, v_ref, ck_ref, cv_ref, tbl_ref, o_ref):
    g = pl.program_id(2)
    u0 = jnp.clip(NA_QUAD * g - NA_QUAD, 0, GRID_ROWS - NA_WIN_ROWS)
    start = pl.multiple_of(u0 * GRID_W, GRID_W)
    nwin = NA_WIN_ROWS * GRID_W
    for j in range(2):
        q = q_ref[0, j]
        kw = k_ref[0, j, pl.ds(start, nwin), :]
        vw = v_ref[0, j, pl.ds(start, nwin), :]
        kc = ck_ref[0, 0, j].astype(BF16)
        vc = cv_ref[0, 0, j].astype(BF16)
        sl = _nt(q, kw) + tbl_ref[j, 0]
        sc = _nt(q, kc)
        m = jnp.maximum(jnp.max(sl, axis=-1, keepdims=True), jnp.max(sc, axis=-1, keepdims=True))
        el = jnp.exp(sl - m)
        ec = jnp.exp(sc - m)
        l = jnp.sum(el, axis=-1, keepdims=True) + jnp.sum(ec, axis=-1, keepdims=True)
        o = (_mm(el.astype(BF16), vw) + _mm(ec.astype(BF16), vc)) / l
        o_ref[0, :, HEAD_DIM * j:HEAD_DIM * (j + 1)] = o.astype(BF16)


def _latent_na(naq, nak, nav, cache_k, cache_v, tbl):
    nq = GRID_ROWS // NA_QUAD
    rows = NA_QUAD * GRID_W
    kind = lambda g: jnp.minimum(g, 1) + g // (nq - 1)
    return pl.pallas_call(
        _na_kernel,
        grid=(DEC_BATCH, NA_HEADS // 2, nq),
        in_specs=[pl.BlockSpec((1, 2, rows, HEAD_DIM), lambda b, p, g: (b, p, g, 0)),
                  pl.BlockSpec((1, 2, DEC_SEQ, HEAD_DIM), lambda b, p, g: (b, p, 0, 0)),
                  pl.BlockSpec((1, 2, DEC_SEQ, HEAD_DIM), lambda b, p, g: (b, p, 0, 0)),
                  pl.BlockSpec((1, 1, 2, PAST_LEN, HEAD_DIM), lambda b, p, g: (b, 0, p, 0, 0)),
                  pl.BlockSpec((1, 1, 2, PAST_LEN, HEAD_DIM), lambda b, p, g: (b, 0, p, 0, 0)),
                  pl.BlockSpec((2, 1, rows, NA_WIN_ROWS * GRID_W), lambda b, p, g: (p, kind(g), 0, 0))],
        out_specs=pl.BlockSpec((1, rows, 2 * HEAD_DIM), lambda b, p, g: (b, g, p)),
        out_shape=jax.ShapeDtypeStruct((DEC_BATCH, DEC_SEQ, NA_WIDTH), BF16),
        compiler_params=_params(("arbitrary",) * 3),
        name="latent_na",
    )(naq, nak, nav, cache_k, cache_v, tbl)


def _diff_kernel(q_ref, k_ref, v_ref, ck_ref, cv_ref, lam_ref, sub_ref, o_ref):
    lam = _lam(lam_ref)
    parts = []
    for m in range(2):
        q = q_ref[0, 0, m]
        sc = _nt(q, ck_ref[0, 0, 0, m].astype(BF16))
        sl = _nt(q, k_ref[0, 0, m])
        mx = jnp.maximum(jnp.max(sc, axis=-1, keepdims=True), jnp.max(sl, axis=-1, keepdims=True))
        ec = jnp.exp(sc - mx)
        el = jnp.exp(sl - mx)
        l = jnp.sum(ec, axis=-1, keepdims=True) + jnp.sum(el, axis=-1, keepdims=True)
        parts.append((ec, el, l))
    (ec1, el1, l1), (ec2, el2, l2) = parts
    r1 = 1.0 / l1
    r2 = lam / l2
    ac = (ec1 * r1 - ec2 * r2).astype(BF16)
    al = (el1 * r1 - el2 * r2).astype(BF16)
    o = _mm(ac, cv_ref[0, 0, 0].astype(BF16)) + _mm(al, v_ref[0, 0])
    o_ref[0] = _sub_norm(o, sub_ref[...]).astype(BF16)


def _latent_diff(dfq, dfk, dfv, cache_k, cache_v, lamv, sub):
    nt = DEC_SEQ // ROW_TILE
    full = lambda shape: pl.BlockSpec(shape, lambda b, h, t: (0,) * len(shape))
    return pl.pallas_call(
        _diff_kernel,
        grid=(DEC_BATCH, DIFF_HEADS, nt),
        in_specs=[pl.BlockSpec((1, 1, 2, ROW_TILE, HEAD_DIM), lambda b, h, t: (b, h, 0, t, 0)),
                  pl.BlockSpec((1, 1, 2, DEC_SEQ, HEAD_DIM), lambda b, h, t: (b, h, 0, 0, 0)),
                  pl.BlockSpec((1, 1, DEC_SEQ, 2 * HEAD_DIM), lambda b, h, t: (b, h, 0, 0)),
                  pl.BlockSpec((1, 1, 1, 2, PAST_LEN, HEAD_DIM), lambda b, h, t: (b, 0, h, 0, 0, 0)),
                  pl.BlockSpec((1, 1, 1, PAST_LEN, 2 * HEAD_DIM), lambda b, h, t: (b, 0, h, 0, 0)),
                  full((4, HEAD_DIM)), full((1, 2 * HEAD_DIM))],
        out_specs=pl.BlockSpec((1, ROW_TILE, 2 * HEAD_DIM), lambda b, h, t: (b, t, h)),
        out_shape=jax.ShapeDtypeStruct((DEC_BATCH, DEC_SEQ, DIFF_WIDTH), BF16),
        compiler_params=_params(("arbitrary",) * 3),
        name="latent_diff",
    )(dfq, dfk, dfv, cache_k, cache_v, lamv, sub)


def _lat_out_kernel(x_ref, na_ref, df_ref, mod_ref, wout_ref, lng_ref, lnb_ref, wrt_ref,
                    x1_ref, h2_ref, aff_ref):
    b = pl.program_id(0)
    x = x_ref[0]
    _, _, g1, sh2, sc2, _ = _mod_rows(mod_ref, 1 + b)
    a = _mm(na_ref[0], wout_ref[0:NA_WIDTH, :]) + _mm(df_ref[0], wout_ref[NA_WIDTH:, :])
    x1 = _layer_norm(DEEPNORM_ALPHA * x + g1 * a, lng_ref[...], lnb_ref[...])
    x1_ref[0] = x1
    h2 = x1 * (1.0 + sc2) + sh2
    h2_ref[0] = h2.astype(BF16)
    aff_ref[0] = _router_aff_t(h2, wrt_ref[...])


def _latent_out(x, na_o, df_o, mod, w_out, ln_g, ln_b, wrt):
    nt = DEC_SEQ // ROW_TILE
    full = lambda shape: pl.BlockSpec(shape, lambda b, t: (0,) * len(shape))
    tok = lambda w: pl.BlockSpec((1, ROW_TILE, w), lambda b, t: (b, t, 0))
    return pl.pallas_call(
        _lat_out_kernel,
        grid=(DEC_BATCH, nt),
        in_specs=[tok(D_MODEL), tok(NA_WIDTH), tok(DIFF_WIDTH), full((8, 6 * D_MODEL)),
                  full((D_MODEL, D_MODEL)), full((1, D_MODEL)), full((1, D_MODEL)),
                  full((N_EXPERTS, D_MODEL))],
        out_specs=[tok(D_MODEL), tok(D_MODEL),
                   pl.BlockSpec((1, N_EXPERTS, ROW_TILE), lambda b, t: (b, 0, t))],
        out_shape=[jax.ShapeDtypeStruct((DEC_BATCH, DEC_SEQ, D_MODEL), F32),
                   jax.ShapeDtypeStruct((DEC_BATCH, DEC_SEQ, D_MODEL), BF16),
                   jax.ShapeDtypeStruct((DEC_BATCH, N_EXPERTS, DEC_SEQ), F32)],
        compiler_params=_params(("arbitrary", "arbitrary")),
        name="latent_out",
    )(x, na_o, df_o, mod, w_out, ln_g, ln_b, wrt)


def _ranks_first(va, ia, vb, ib):
    return (va > vb) | ((va == vb) & (ia < ib))


def _route_kernel(aff_ref, idx_ref, gate_ref, *, cap, n):
    rows = aff_ref.shape[0]
    nb = n // LANES
    lane = lax.broadcasted_iota(I32, (rows, LANES), 1)
    v = [aff_ref[:, LANES * b:LANES * (b + 1)] for b in range(nb)]
    ix = [lane + LANES * b for b in range(nb)]
    k = 2
    while k <= n:
        j = k // 2
        while j >= 1:
            if j >= LANES:
                sb = j // LANES
                for lo in range(nb):
                    if lo & sb:
                        continue
                    hi = lo | sb
                    f = _ranks_first(v[lo], ix[lo], v[hi], ix[hi])
                    if k < n and (LANES * lo) & k:
                        f = jnp.logical_not(f)
                    v[lo], v[hi] = jnp.where(f, v[lo], v[hi]), jnp.where(f, v[hi], v[lo])
                    ix[lo], ix[hi] = jnp.where(f, ix[lo], ix[hi]), jnp.where(f, ix[hi], ix[lo])
            else:
                upper = (lane & j) != 0
                if k < LANES:
                    flip_lanes = jnp.logical_xor(upper, (lane & k) != 0)
                for b in range(nb):
                    pv = jnp.where(upper, pltpu.roll(v[b], j, 1), pltpu.roll(v[b], LANES - j, 1))
                    pi = jnp.where(upper, pltpu.roll(ix[b], j, 1), pltpu.roll(ix[b], LANES - j, 1))
                    f = _ranks_first(v[b], ix[b], pv, pi)
                    if k < LANES:
                        flip = flip_lanes
                    elif k < n and (LANES * b) & k:
                        flip = jnp.logical_not(upper)
                    else:
                        flip = upper
                    keep = jnp.logical_xor(f, flip)
                    v[b] = jnp.where(keep, v[b], pv)
                    ix[b] = jnp.where(keep, ix[b], pi)
            j //= 2
        k *= 2
    if cap < LANES:
        idx_ref[...] = ix[0][:, :cap]
        gate_ref[...] = v[0][:, :cap]
    else:
        idx_ref[...] = jnp.concatenate(ix[:cap // LANES], axis=1)
        gate_ref[...] = jnp.concatenate(v[:cap // LANES], axis=1)


def _route(aff_t, cap):
    rows, n = aff_t.shape
    return pl.pallas_call(
        functools.partial(_route_kernel, cap=cap, n=n),
        out_shape=[jax.ShapeDtypeStruct((rows, cap), I32), jax.ShapeDtypeStruct((rows, cap), F32)],
        compiler_params=_params(None),
        name=f"route_{n}",
    )(aff_t)


def _gather_ctx_kernel(idx_ref, h_ref, x_ref):
    nslot = N_EXPERTS * CAP_CTX
    hit = lax.broadcasted_iota(I32, (nslot, SEQ), 1) == idx_ref[0]
    xs = _mm(jnp.where(hit, 1.0, 0.0).astype(BF16), h_ref[0])
    x_ref[...] = xs.astype(BF16).reshape(N_EXPERTS, CAP_CTX, D_MODEL)


def _gather_ctx(idx_col, h2):
    nslot = N_EXPERTS * CAP_CTX
    return pl.pallas_call(
        _gather_ctx_kernel,
        grid=(BATCH,),
        in_specs=[pl.BlockSpec((1, nslot, 1), lambda s: (s, 0, 0)),
                  pl.BlockSpec((1, SEQ, D_MODEL), lambda s: (s, 0, 0))],
        out_specs=pl.BlockSpec((N_EXPERTS, CAP_CTX, D_MODEL), lambda s: (0, s, 0)),
        out_shape=jax.ShapeDtypeStruct((N_EXPERTS, SLOTS_CTX, D_MODEL), BF16),
        compiler_params=_params(("arbitrary",)),
        name="gather_ctx",
    )(idx_col, h2)


def _gather_lat_kernel(idx_ref, h_ref, x_ref):
    hit = lax.broadcasted_iota(I32, (CAP_LAT, DEC_SEQ), 1) == idx_ref[0, 0]
    x_ref[0] = _mm(jnp.where(hit, 1.0, 0.0).astype(BF16), h_ref[0]).astype(BF16)


def _gather_lat(idx_col, h2):
    return pl.pallas_call(
        _gather_lat_kernel,
        grid=(DEC_BATCH, N_EXPERTS),
        in_specs=[pl.BlockSpec((1, 1, CAP_LAT, 1), lambda s, e: (s, e, 0, 0)),
                  pl.BlockSpec((1, DEC_SEQ, D_MODEL), lambda s, e: (s, 0, 0))],
        out_specs=pl.BlockSpec((1, CAP_LAT, D_MODEL), lambda s, e: (e, s, 0)),
        out_shape=jax.ShapeDtypeStruct((N_EXPERTS, SLOTS_LAT, D_MODEL), BF16),
        compiler_params=_params(("arbitrary", "arbitrary")),
        name="gather_lat",
    )(idx_col, h2)


def _ffn_kernel(xc_ref, xl_ref, gc_ref, gl_ref, wg_ref, wu_ref, wd_ref, yc_ref, yl_ref, h_scr, wl_scr):
    j = pl.program_id(1)
    tail = FF_HALF - FF_HALF // FF_TILE * FF_TILE

    def swiglu_cols(wg, wu, c0):
        wgb = wg.astype(BF16)
        wub = wu.astype(BF16)
        for i, x_ref in enumerate((xc_ref, xl_ref)):
            x = x_ref[0]
            a = _mm(x, wgb)
            u = _mm(x, wub)
            h = a * (1.0 / (1.0 + jnp.exp(-a))) * u
            h_scr[SLOTS_CTX * i:SLOTS_CTX * (i + 1), c0:c0 + FF_TILE] = h.astype(BF16)

    @pl.when(j == 0)
    def _():
        for c in range(FF_HALF // FF_TILE):
            lo = FF_TILE * c
            swiglu_cols(wg_ref[0, :, lo:lo + FF_TILE], wu_ref[0, :, lo:lo + FF_TILE], lo)
        wl_scr[0] = wg_ref[0, :, FF_HALF - tail:FF_HALF]
        wl_scr[1] = wu_ref[0, :, FF_HALF - tail:FF_HALF]

    @pl.when(j == 1)
    def _():
        head = FF_TILE - tail
        swiglu_cols(jnp.concatenate([wl_scr[0], wg_ref[0, :, 0:head]], axis=1),
                    jnp.concatenate([wl_scr[1], wu_ref[0, :, 0:head]], axis=1), FF_HALF - tail)
        for c in range(FF_HALF // FF_TILE):
            lo = head + FF_TILE * c
            swiglu_cols(wg_ref[0, :, lo:lo + FF_TILE], wu_ref[0, :, lo:lo + FF_TILE], FF_HALF + lo)

    @pl.when(j >= 2)
    def _():
        wdb = wd_ref[0].astype(BF16)
        for i, (g_ref, y_ref) in enumerate(((gc_ref, yc_ref), (gl_ref, yl_ref))):
            y = _mm(h_scr[SLOTS_CTX * i:SLOTS_CTX * (i + 1), :], wdb)
            y_ref[0] = (y * g_ref[0]).astype(BF16)


def _expert_ffn(xc, xl, gc, gl, w_gate, w_up, w_down):
    assert SLOTS_CTX == SLOTS_LAT and FF_HALF % LANES == 0 and D_MODEL % OUT_TILE == 0
    xs = lambda n: pl.BlockSpec((1, n, D_MODEL), lambda e, j: (e, 0, 0))
    gs = lambda n: pl.BlockSpec((1, n, 1), lambda e, j: (e, 0, 0))
    up = pl.BlockSpec((1, D_MODEL, FF_HALF), lambda e, j: (e, 0, jnp.minimum(j, 1)))
    down = pl.BlockSpec((1, D_FF, OUT_TILE),
                        lambda e, j: (jnp.where(j >= 2, e, jnp.maximum(e - 1, 0)), 0, jnp.where(j >= 2, j - 2, 1)))
    ys = lambda n: pl.BlockSpec((1, n, OUT_TILE), lambda e, j: (e, 0, jnp.maximum(j - 2, 0)))
    return pl.pallas_call(
        _ffn_kernel,
        grid=(N_EXPERTS, 2 + D_MODEL // OUT_TILE),
        in_specs=[xs(SLOTS_CTX), xs(SLOTS_LAT), gs(SLOTS_CTX), gs(SLOTS_LAT), up, up, down],
        out_specs=[ys(SLOTS_CTX), ys(SLOTS_LAT)],
        out_shape=[jax.ShapeDtypeStruct((N_EXPERTS, SLOTS_CTX, D_MODEL), BF16),
                   jax.ShapeDtypeStruct((N_EXPERTS, SLOTS_LAT, D_MODEL), BF16)],
        scratch_shapes=[pltpu.VMEM((SLOTS_CTX + SLOTS_LAT, D_FF), BF16),
                        pltpu.VMEM((2, D_MODEL, LANES), F32)],
        compiler_params=_params(("arbitrary", "arbitrary"), FFN_VMEM_LIMIT),
        name="expert_ffn",
    )(xc, xl, gc, gl, w_gate, w_up, w_down)


def _combine_ctx_kernel(y_ref, idx_ref, x1_ref, mod_ref, lng_ref, lnb_ref, o_ref):
    g2 = _mod_rows(mod_ref, 0)[5]
    nslot = N_EXPERTS * CAP_CTX
    hit = lax.broadcasted_iota(I32, (SEQ, nslot), 0) == idx_ref[0]
    y = y_ref[...].reshape(nslot, D_MODEL)
    f = _mm(jnp.where(hit, 1.0, 0.0).astype(BF16), y)
    o_ref[0] = _layer_norm(DEEPNORM_ALPHA * x1_ref[0] + g2 * f, lng_ref[...], lnb_ref[...])


def _combine_ctx(yc, idx_row, x1, mod, ln_g, ln_b):
    full = lambda shape: pl.BlockSpec(shape, lambda s: (0,) * len(shape))
    return pl.pallas_call(
        _combine_ctx_kernel,
        grid=(BATCH,),
        in_specs=[pl.BlockSpec((N_EXPERTS, CAP_CTX, D_MODEL), lambda s: (0, s, 0)),
                  pl.BlockSpec((1, 1, N_EXPERTS * CAP_CTX), lambda s: (s, 0, 0)),
                  pl.BlockSpec((1, SEQ, D_MODEL), lambda s: (s, 0, 0)),
                  full((8, 6 * D_MODEL)), full((1, D_MODEL)), full((1, D_MODEL))],
        out_specs=pl.BlockSpec((1, SEQ, D_MODEL), lambda s: (s, 0, 0)),
        out_shape=jax.ShapeDtypeStruct((BATCH, SEQ, D_MODEL), F32),
        compiler_params=_params(("arbitrary",)),
        name="combine_ctx",
    )(yc, idx_row, x1, mod, ln_g, ln_b)


def _combine_lat_kernel(y_ref, idx_ref, x1_ref, mod_ref, lng_ref, lnb_ref, o_ref):
    s = pl.program_id(0)
    g2 = _mod_rows(mod_ref, 1 + s)[5]
    tok = lax.broadcasted_iota(I32, (ROW_TILE, CAP_LAT), 0) + ROW_TILE * pl.program_id(1)
    f = jnp.zeros((ROW_TILE, D_MODEL), F32)
    for e in range(N_EXPERTS):
        hit = tok == idx_ref[0, e:e + 1, :]
        f = f + _mm(jnp.where(hit, 1.0, 0.0).astype(BF16), y_ref[e])
    o_ref[0] = _layer_norm(DEEPNORM_ALPHA * x1_ref[0] + g2 * f, lng_ref[...], lnb_ref[...])


def _combine_lat(yl, idx, x1, mod, ln_g, ln_b):
    nt = DEC_SEQ // ROW_TILE
    full = lambda shape: pl.BlockSpec(shape, lambda s, t: (0,) * len(shape))
    return pl.pallas_call(
        _combine_lat_kernel,
        grid=(DEC_BATCH, nt),
        in_specs=[pl.BlockSpec((N_EXPERTS, CAP_LAT, D_MODEL), lambda s, t: (0, s, 0)),
                  pl.BlockSpec((1, N_EXPERTS, CAP_LAT), lambda s, t: (s, 0, 0)),
                  pl.BlockSpec((1, ROW_TILE, D_MODEL), lambda s, t: (s, t, 0)),
                  full((8, 6 * D_MODEL)), full((1, D_MODEL)), full((1, D_MODEL))],
        out_specs=pl.BlockSpec((1, ROW_TILE, D_MODEL), lambda s, t: (s, t, 0)),
        out_shape=jax.ShapeDtypeStruct((DEC_BATCH, DEC_SEQ, D_MODEL), F32),
        compiler_params=_params(("arbitrary", "arbitrary")),
        name="combine_lat",
    )(yl, idx, x1, mod, ln_g, ln_b)


def _rope_tables():
    t = jnp.arange(DEC_SEQ)
    row = (t // GRID_W).astype(F32)
    col = (t % GRID_W).astype(F32)
    half = HEAD_DIM // 2
    inv = ROPE_THETA ** (-jnp.arange(0, half, 2, dtype=F32) / half)
    ang_r = row[:, None] * inv[None, :]
    ang_c = col[:, None] * inv[None, :]
    ang = jnp.concatenate([ang_r, ang_r, ang_c, ang_c], axis=-1)
    cos = jnp.tile(jnp.cos(ang), (1, 2))
    sin = jnp.tile(jnp.sin(ang), (1, 2))
    first = (jnp.arange(LANES) % 32) < 16
    return cos, jnp.where(first[None, :], -sin, sin)


def kernel(x_prompt, x_sample, cache_na_k, cache_na_v, cache_diff_k, cache_diff_v, c, c_ctx, w_ada, b_ada, w_in, w_out, na_rel_bias, lambda_q1, lambda_k1, lambda_q2, lambda_k2, subln_g, ln1_g, ln1_b, ln2_g, ln2_b, w_router, w_gate, w_up, w_down):
    l = 0
    cvec = jnp.concatenate([c_ctx[None, :], c, jnp.zeros((8 - 1 - DEC_BATCH, D_MODEL), F32)], axis=0)
    mod = _modulation(cvec, w_ada[l], b_ada[l][None, :])
    lamv = jnp.stack([lambda_q1[l], lambda_k1[l], lambda_q2[l], lambda_k2[l]], axis=0)
    w_in_b = w_in[l].astype(BF16)
    w_out_b = w_out[l].astype(BF16)
    sub = subln_g[l][None, :]
    wrt = w_router[l].T

    x1c, h2c, new_na_k, new_na_v, new_diff_k, new_diff_v, affc = _context_block(
        x_prompt, mod, lamv, w_in_b, w_out_b, sub, ln1_g[l][None, :], ln1_b[l][None, :], wrt)

    cos, sin = _rope_tables()
    naq, nak, nav, dfq, dfk, dfv = _latent_qkv(x_sample, mod, w_in_b, cos, sin)
    tbl = _na_bias_tables(na_rel_bias[l].reshape(-1))
    na_o = _latent_na(naq, nak, nav, cache_na_k, cache_na_v, tbl)
    df_o = _latent_diff(dfq, dfk, dfv, cache_diff_k, cache_diff_v, lamv, sub)
    x1l, h2l, affl = _latent_out(x_sample, na_o, df_o, mod, w_out_b,
                                 ln1_g[l][None, :], ln1_b[l][None, :], wrt)

    idxc, gatec = _route(affc.reshape(BATCH * N_EXPERTS, SEQ), CAP_CTX)
    idxl, gatel = _route(affl.reshape(DEC_BATCH * N_EXPERTS, DEC_SEQ), CAP_LAT)
    xc = _gather_ctx(idxc.reshape(BATCH, N_EXPERTS * CAP_CTX, 1), h2c)
    xl = _gather_lat(idxl.reshape(DEC_BATCH, N_EXPERTS, CAP_LAT, 1), h2l)
    slot_major = lambda g, sets, cap: jnp.swapaxes(g.reshape(sets, N_EXPERTS, cap), 0, 1).reshape(
        N_EXPERTS, sets * cap, 1)
    yc, yl = _expert_ffn(xc, xl, slot_major(gatec, BATCH, CAP_CTX), slot_major(gatel, DEC_BATCH, CAP_LAT),
                         w_gate[l], w_up[l], w_down[l])
    y_prompt = _combine_ctx(yc, idxc.reshape(BATCH, 1, N_EXPERTS * CAP_CTX), x1c, mod,
                            ln2_g[l][None, :], ln2_b[l][None, :])
    y_sample = _combine_lat(yl, idxl.reshape(DEC_BATCH, N_EXPERTS, CAP_LAT), x1l, mod,
                            ln2_g[l][None, :], ln2_b[l][None, :])
    return (y_prompt, y_sample, new_na_k, new_na_v, new_diff_k, new_diff_v)
```

```python
import functools
import math

import jax
import jax.numpy as jnp
from jax import lax
from jax.experimental import pallas as pl
from jax.experimental.pallas import tpu as pltpu

F32 = jnp.float32
BF16 = jnp.bfloat16
I32 = jnp.int32

D_MODEL = 1024
BATCH = 16
SEQ = 256
DEC_BATCH = 2
DEC_SEQ = 2048
PAST_LEN = 256
GRID_W = 64
GRID_ROWS = DEC_SEQ // GRID_W
HEAD_DIM = 64
NA_HEADS = 8
DIFF_HEADS = 4
NA_WIDTH = NA_HEADS * HEAD_DIM
DIFF_WIDTH = DIFF_HEADS * 2 * HEAD_DIM
QKV_WIDTH = 3 * NA_WIDTH + 3 * DIFF_WIDTH
NA_WIN_H = 8
NA_WIN_W = 16
N_REL_H = 2 * NA_WIN_H - 1
N_REL_W = 2 * NA_WIN_W - 1
N_EXPERTS = 16
EC_CAPACITY_FACTOR = 2
D_FF = 2816
ROPE_THETA = 10000.0
NORM_EPS = 1e-5
NEG_INF = -1e30
DEPTH = 1
DEEPNORM_ALPHA = (2.0 * DEPTH) ** 0.25
LAMBDA_INIT = 0.8 - 0.6 * math.exp(-0.3 * 0)
SCALE = HEAD_DIM ** -0.5

OFF_NAQ = 0
OFF_NAK = NA_WIDTH
OFF_NAV = 2 * NA_WIDTH
OFF_DFQ = 3 * NA_WIDTH
OFF_DFK = 3 * NA_WIDTH + DIFF_WIDTH
OFF_DFV = 3 * NA_WIDTH + 2 * DIFF_WIDTH

CAP_CTX = EC_CAPACITY_FACTOR * SEQ // N_EXPERTS
CAP_LAT = EC_CAPACITY_FACTOR * DEC_SEQ // N_EXPERTS
SLOTS_CTX = BATCH * CAP_CTX
SLOTS_LAT = DEC_BATCH * CAP_LAT

ROW_TILE = 256
NA_QUAD = 4
NA_WIN_ROWS = 12
NA_GROUP = 8
FF_TILE = 256
FF_HALF = D_FF // 2
OUT_TILE = 512
LANES = 128
DIFF_ROWS = 1024
Q_ROWS = 256

VMEM_LIMIT = 48 * 1024 * 1024
FFN_VMEM_LIMIT = 58 * 1024 * 1024


def _params(sem, vmem=VMEM_LIMIT, flags=None):
    return pltpu.CompilerParams(dimension_semantics=sem, vmem_limit_bytes=vmem, flags=flags)


def _mm(a, b):
    return jnp.dot(a, b, preferred_element_type=F32)


def _nt(a, b):
    return lax.dot_general(a, b, (((1,), (1,)), ((), ())), preferred_element_type=F32)


def _split(a):
    hi = a.astype(BF16)
    lo = (a - hi.astype(F32)).astype(BF16)
    return hi, lo


def _mm3(a, b):
    ah, al = _split(a)
    bh, bl = _split(b)
    return _mm(ah, bh) + _mm(al, bh) + _mm(ah, bl)


def _nt3(a, b):
    ah, al = _split(a)
    bh, bl = _split(b)
    return _nt(ah, bh) + _nt(al, bh) + _nt(ah, bl)


def _layer_norm(y, g, b):
    mu = jnp.mean(y, axis=-1, keepdims=True)
    d = y - mu
    var = jnp.mean(d * d, axis=-1, keepdims=True)
    return d * lax.rsqrt(var + NORM_EPS) * g + b


def _lam(l_ref):
    l = l_ref[...]
    a = jnp.sum(l[0:1] * l[1:2], axis=-1, keepdims=True)
    b = jnp.sum(l[2:3] * l[3:4], axis=-1, keepdims=True)
    return jnp.exp(a) - jnp.exp(b) + LAMBDA_INIT


def _mod_rows(mod_ref, row):
    return [mod_ref[pl.ds(row, 1), i * D_MODEL:(i + 1) * D_MODEL] for i in range(6)]


def _sub_norm(o, sub):
    ms = jnp.mean(o * o, axis=-1, keepdims=True)
    return o * lax.rsqrt(ms + NORM_EPS) * sub * (1.0 - LAMBDA_INIT)


def _router_aff_t(h2, wrt):
    lg = _nt3(wrt, h2)
    m = jnp.max(lg, axis=0, keepdims=True)
    e = jnp.exp(lg - m)
    return e / jnp.sum(e, axis=0, keepdims=True)


def _mod_kernel(c_ref, w_ref, b_ref, o_ref):
    c = c_ref[...]
    s = c * (1.0 / (1.0 + jnp.exp(-c)))
    o_ref[...] = _mm3(s, w_ref[...]) + b_ref[...]


def _modulation(cvec, w_ada, b_ada):
    ncol = 6 * D_MODEL
    return pl.pallas_call(
        _mod_kernel,
        grid=(6,),
        in_specs=[pl.BlockSpec((8, D_MODEL), lambda j: (0, 0)),
                  pl.BlockSpec((D_MODEL, D_MODEL), lambda j: (0, j)),
                  pl.BlockSpec((1, D_MODEL), lambda j: (0, j))],
        out_specs=pl.BlockSpec((8, D_MODEL), lambda j: (0, j)),
        out_shape=jax.ShapeDtypeStruct((8, ncol), F32),
        compiler_params=_params(("arbitrary",)),
        name="modulation",
    )(cvec, w_ada, b_ada)


def _ctx_kernel(x_ref, mod_ref, lam_ref, win_ref, wout_ref, sub_ref, lng_ref, lnb_ref, wrt_ref,
                x1_ref, h2_ref, nak_ref, nav_ref, dfk_ref, dfv_ref, aff_ref, qkv_scr, o_scr):
    x = x_ref[0]
    sh1, sc1, g1, sh2, sc2, g2 = _mod_rows(mod_ref, 0)
    h = (x * (1.0 + sc1) + sh1).astype(BF16)
    qkv_scr[...] = _mm(h, win_ref[...])

    for hh in range(NA_HEADS):
        nak_ref[0, 0, hh] = qkv_scr[:, OFF_NAK + HEAD_DIM * hh:OFF_NAK + HEAD_DIM * (hh + 1)]
        nav_ref[0, 0, hh] = qkv_scr[:, OFF_NAV + HEAD_DIM * hh:OFF_NAV + HEAD_DIM * (hh + 1)]
    for hh in range(DIFF_HEADS):
        for m in range(2):
            c0 = OFF_DFK + HEAD_DIM * (2 * hh + m)
            dfk_ref[0, 0, hh, m] = qkv_scr[:, c0:c0 + HEAD_DIM]
        c0 = OFF_DFV + 2 * HEAD_DIM * hh
        dfv_ref[0, 0, hh] = qkv_scr[:, c0:c0 + 2 * HEAD_DIM]

    lam = _lam(lam_ref)

    def probs(c_q, c_k):
        q = (qkv_scr[:, c_q:c_q + HEAD_DIM] * SCALE).astype(BF16)
        k = qkv_scr[:, c_k:c_k + HEAD_DIM].astype(BF16)
        s = _nt(q, k)
        e = jnp.exp(s - jnp.max(s, axis=-1, keepdims=True))
        return e, jnp.sum(e, axis=-1, keepdims=True)

    for hh in range(NA_HEADS):
        e, l = probs(OFF_NAQ + HEAD_DIM * hh, OFF_NAK + HEAD_DIM * hh)
        v = qkv_scr[:, OFF_NAV + HEAD_DIM * hh:OFF_NAV + HEAD_DIM * (hh + 1)].astype(BF16)
        o = _mm(e.astype(BF16), v) / l
        o_scr[:, HEAD_DIM * hh:HEAD_DIM * (hh + 1)] = o.astype(BF16)

    sub = sub_ref[...]
    for hh in range(DIFF_HEADS):
        e1, l1 = probs(OFF_DFQ + HEAD_DIM * (2 * hh), OFF_DFK + HEAD_DIM * (2 * hh))
        e2, l2 = probs(OFF_DFQ + HEAD_DIM * (2 * hh + 1), OFF_DFK + HEAD_DIM * (2 * hh + 1))
        a = e1 * (1.0 / l1) - e2 * (lam / l2)
        c0 = OFF_DFV + 2 * HEAD_DIM * hh
        v = qkv_scr[:, c0:c0 + 2 * HEAD_DIM].astype(BF16)
        o = _sub_norm(_mm(a.astype(BF16), v), sub)
        c1 = NA_WIDTH + 2 * HEAD_DIM * hh
        o_scr[:, c1:c1 + 2 * HEAD_DIM] = o.astype(BF16)

    a = _mm(o_scr[...], wout_ref[...])
    x1 = _layer_norm(DEEPNORM_ALPHA * x + g1 * a, lng_ref[...], lnb_ref[...])
    x1_ref[0] = x1
    h2 = x1 * (1.0 + sc2) + sh2
    h2_ref[0] = h2.astype(BF16)
    aff_ref[0] = _router_aff_t(h2, wrt_ref[...])


def _context_block(x, mod, lamv, w_in, w_out, sub, ln_g, ln_b, wrt):
    full = lambda shape: pl.BlockSpec(shape, lambda b: (0,) * len(shape))
    return pl.pallas_call(
        _ctx_kernel,
        grid=(BATCH,),
        in_specs=[pl.BlockSpec((1, SEQ, D_MODEL), lambda b: (b, 0, 0)),
                  full((8, 6 * D_MODEL)), full((4, HEAD_DIM)),
                  full((D_MODEL, QKV_WIDTH)), full((D_MODEL, D_MODEL)),
                  full((1, 2 * HEAD_DIM)), full((1, D_MODEL)), full((1, D_MODEL)),
                  full((N_EXPERTS, D_MODEL))],
        out_specs=[pl.BlockSpec((1, SEQ, D_MODEL), lambda b: (b, 0, 0)),
                   pl.BlockSpec((1, SEQ, D_MODEL), lambda b: (b, 0, 0)),
                   pl.BlockSpec((1, 1, NA_HEADS, SEQ, HEAD_DIM), lambda b: (b, 0, 0, 0, 0)),
                   pl.BlockSpec((1, 1, NA_HEADS, SEQ, HEAD_DIM), lambda b: (b, 0, 0, 0, 0)),
                   pl.BlockSpec((1, 1, DIFF_HEADS, 2, SEQ, HEAD_DIM), lambda b: (b, 0, 0, 0, 0, 0)),
                   pl.BlockSpec((1, 1, DIFF_HEADS, SEQ, 2 * HEAD_DIM), lambda b: (b, 0, 0, 0, 0)),
                   pl.BlockSpec((1, N_EXPERTS, SEQ), lambda b: (b, 0, 0))],
        out_shape=[jax.ShapeDtypeStruct((BATCH, SEQ, D_MODEL), F32),
                   jax.ShapeDtypeStruct((BATCH, SEQ, D_MODEL), BF16),
                   jax.ShapeDtypeStruct((BATCH, 1, NA_HEADS, SEQ, HEAD_DIM), F32),
                   jax.ShapeDtypeStruct((BATCH, 1, NA_HEADS, SEQ, HEAD_DIM), F32),
                   jax.ShapeDtypeStruct((BATCH, 1, DIFF_HEADS, 2, SEQ, HEAD_DIM), F32),
                   jax.ShapeDtypeStruct((BATCH, 1, DIFF_HEADS, SEQ, 2 * HEAD_DIM), F32),
                   jax.ShapeDtypeStruct((BATCH, N_EXPERTS, SEQ), F32)],
        scratch_shapes=[pltpu.VMEM((SEQ, QKV_WIDTH), F32), pltpu.VMEM((SEQ, D_MODEL), BF16)],
        compiler_params=_params(("arbitrary",)),
        name="context_block",
    )(x, mod, lamv, w_in, w_out, sub, ln_g, ln_b, wrt)


def _lat_qkv_kernel(x_ref, mod_ref, win_ref, cos_ref, sin_ref,
                    naq_ref, nak_ref, nav_ref, dfq_ref, dfk_ref, dfv_ref, qkv_scr):
    b = pl.program_id(0)
    x = x_ref[0]
    sh1, sc1 = _mod_rows(mod_ref, 1 + b)[:2]
    h = (x * (1.0 + sc1) + sh1).astype(BF16)
    qkv_scr[...] = _mm(h, win_ref[...])

    for hh in range(NA_HEADS):
        lo, hi = HEAD_DIM * hh, HEAD_DIM * (hh + 1)
        naq_ref[0, hh] = (qkv_scr[:, OFF_NAQ + lo:OFF_NAQ + hi] * SCALE).astype(BF16)
        nak_ref[0, hh] = qkv_scr[:, OFF_NAK + lo:OFF_NAK + hi].astype(BF16)
        nav_ref[0, hh] = qkv_scr[:, OFF_NAV + lo:OFF_NAV + hi].astype(BF16)

    cos = cos_ref[...]
    sin = sin_ref[...]
    lane = lax.broadcasted_iota(I32, (ROW_TILE, LANES), 1)
    first = (lane & 31) < 16

    def rope(t):
        rot = jnp.where(first, pltpu.roll(t, LANES - 16, 1), pltpu.roll(t, 16, 1))
        return t * cos + rot * sin

    for hh in range(DIFF_HEADS):
        lo = 2 * HEAD_DIM * hh
        q = rope(qkv_scr[:, OFF_DFQ + lo:OFF_DFQ + lo + LANES]) * SCALE
        k = rope(qkv_scr[:, OFF_DFK + lo:OFF_DFK + lo + LANES])
        for m in range(2):
            dfq_ref[0, hh, m] = q[:, HEAD_DIM * m:HEAD_DIM * (m + 1)].astype(BF16)
        dfk_ref[0, hh] = k.T.astype(BF16).reshape(2, HEAD_DIM, ROW_TILE)
        dfv_ref[0, hh] = qkv_scr[:, OFF_DFV + lo:OFF_DFV + lo + LANES].astype(BF16)


def _latent_qkv(x, mod, w_in, cos, sin):
    nt = DEC_SEQ // ROW_TILE
    full = lambda shape: pl.BlockSpec(shape, lambda b, t: (0,) * len(shape))
    hs = lambda nh: pl.BlockSpec((1, nh, ROW_TILE, HEAD_DIM), lambda b, t: (b, 0, t, 0))
    return pl.pallas_call(
        _lat_qkv_kernel,
        grid=(DEC_BATCH, nt),
        in_specs=[pl.BlockSpec((1, ROW_TILE, D_MODEL), lambda b, t: (b, t, 0)),
                  full((8, 6 * D_MODEL)), full((D_MODEL, QKV_WIDTH)),
                  pl.BlockSpec((ROW_TILE, LANES), lambda b, t: (t, 0)),
                  pl.BlockSpec((ROW_TILE, LANES), lambda b, t: (t, 0))],
        out_specs=[hs(NA_HEADS), hs(NA_HEADS), hs(NA_HEADS),
                   pl.BlockSpec((1, DIFF_HEADS, 2, ROW_TILE, HEAD_DIM), lambda b, t: (b, 0, 0, t, 0)),
                   pl.BlockSpec((1, DIFF_HEADS, 2, HEAD_DIM, ROW_TILE), lambda b, t: (b, 0, 0, 0, t)),
                   pl.BlockSpec((1, DIFF_HEADS, ROW_TILE, 2 * HEAD_DIM), lambda b, t: (b, 0, t, 0))],
        out_shape=[jax.ShapeDtypeStruct((DEC_BATCH, NA_HEADS, DEC_SEQ, HEAD_DIM), BF16)] * 3
        + [jax.ShapeDtypeStruct((DEC_BATCH, DIFF_HEADS, 2, DEC_SEQ, HEAD_DIM), BF16),
           jax.ShapeDtypeStruct((DEC_BATCH, DIFF_HEADS, 2, HEAD_DIM, DEC_SEQ), BF16),
           jax.ShapeDtypeStruct((DEC_BATCH, DIFF_HEADS, DEC_SEQ, 2 * HEAD_DIM), BF16)],
        scratch_shapes=[pltpu.VMEM((ROW_TILE, QKV_WIDTH), F32)],
        compiler_params=_params(("arbitrary", "arbitrary")),
        name="latent_qkv",
    )(x, mod, w_in, cos, sin)


def _na_row_offset(kind, qr, kr):
    if kind == 0:
        return kr - qr + 7 if kr < NA_WIN_H else None
    if kind == 1:
        return kr - qr + 3 if qr <= kr < qr + NA_WIN_H else None
    return kr - qr - 1 if NA_WIN_ROWS - NA_WIN_H <= kr else None


def _bias_kernel(rpb_ref, o_ref):
    h = pl.program_id(0)
    base = h * (N_REL_H * N_REL_W)
    wq = lax.broadcasted_iota(I32, (GRID_W, LANES), 0)
    c = lax.broadcasted_iota(I32, (GRID_W, LANES), 1)
    wk = c & (GRID_W - 1)
    right = c >= GRID_W
    coff = jnp.clip(wk - wq, -(NA_WIN_W - 1), NA_WIN_W - 1) + (NA_WIN_W - 1)
    c0 = jnp.clip(wq - NA_WIN_W // 2, 0, GRID_W - NA_WIN_W)
    in_win = (wk >= c0) & (wk < c0 + NA_WIN_W)
    cache = {}

    def tile(rl, rr):
        if (rl, rr) in cache:
            return cache[(rl, rr)]
        if rl is None and rr is None:
            t = jnp.full((GRID_W, LANES), NEG_INF, F32)
        else:
            acc = jnp.zeros((GRID_W, LANES), F32)
            for co in range(N_REL_W):
                vl = rpb_ref[base + rl * N_REL_W + co] if rl is not None else 0.0
                vr = rpb_ref[base + rr * N_REL_W + co] if rr is not None else 0.0
                acc = jnp.where(coff == co, jnp.where(right, vr, vl), acc)
            ok = in_win
            if rl is None:
                ok = ok & right
            if rr is None:
                ok = ok & jnp.logical_not(right)
            t = jnp.where(ok, acc, NEG_INF)
        cache[(rl, rr)] = t
        return t

    for kind in range(3):
        for qr in range(NA_QUAD):
            row = [tile(_na_row_offset(kind, qr, 2 * p), _na_row_offset(kind, qr, 2 * p + 1))
                   for p in range(NA_WIN_ROWS // 2)]
            o_ref[0, kind, GRID_W * qr:GRID_W * (qr + 1), :] = jnp.concatenate(row, axis=1)


def _na_bias_tables(rpb_flat):
    return pl.pallas_call(
        _bias_kernel,
        grid=(NA_HEADS,),
        in_specs=[pl.BlockSpec(memory_space=pltpu.SMEM)],
        out_specs=pl.BlockSpec((1, 3, NA_QUAD * GRID_W, NA_WIN_ROWS * GRID_W), lambda h: (h, 0, 0, 0)),
        out_shape=jax.ShapeDtypeStruct((NA_HEADS, 3, NA_QUAD * GRID_W, NA_WIN_ROWS * GRID_W), F32),
        compiler_params=_params(("arbitrary",)),
        name="na_bias_tables",
    )(rpb_flat)


def _na_kernel(q_ref, k_ref, v_ref, ck_ref, cv_ref, tbl_ref, o_ref):
    g = pl.program_id(2)
    u0 = jnp.clip(NA_QUAD * g - NA_QUAD, 0, GRID_ROWS - NA_WIN_ROWS)
    start = pl.multiple_of(u0 * GRID_W, GRID_W)
    nwin = NA_WIN_ROWS * GRID_W
    for j in range(NA_GROUP):
        q = q_ref[0, j]
        kw = k_ref[0, j, pl.ds(start, nwin), :]
        vw = v_ref[0, j, pl.ds(start, nwin), :]
        kc = ck_ref[0, 0, j].astype(BF16)
        vc = cv_ref[0, 0, j].astype(BF16)
        sl = _nt(q, kw) + tbl_ref[j, 0]
        sc = _nt(q, kc)
        m = jnp.maximum(jnp.max(sl, axis=-1, keepdims=True), jnp.max(sc, axis=-1, keepdims=True))
        el = jnp.exp(sl - m)
        ec = jnp.exp(sc - m)
        l = jnp.sum(el, axis=-1, keepdims=True) + jnp.sum(ec, axis=-1, keepdims=True)
        o = (_mm(el.astype(BF16), vw) + _mm(ec.astype(BF16), vc)) / l
        o_ref[0, :, HEAD_DIM * j:HEAD_DIM * (j + 1)] = o.astype(BF16)


def _latent_na(naq, nak, nav, cache_k, cache_v, tbl):
    nq = GRID_ROWS // NA_QUAD
    rows = NA_QUAD * GRID_W
    kind = lambda g: jnp.minimum(g, 1) + g // (nq - 1)
    return pl.pallas_call(
        _na_kernel,
        grid=(DEC_BATCH, NA_HEADS // NA_GROUP, nq),
        in_specs=[pl.BlockSpec((1, NA_GROUP, rows, HEAD_DIM), lambda b, p, g: (b, p, g, 0)),
                  pl.BlockSpec((1, NA_GROUP, DEC_SEQ, HEAD_DIM), lambda b, p, g: (b, p, 0, 0)),
                  pl.BlockSpec((1, NA_GROUP, DEC_SEQ, HEAD_DIM), lambda b, p, g: (b, p, 0, 0)),
                  pl.BlockSpec((1, 1, NA_GROUP, PAST_LEN, HEAD_DIM), lambda b, p, g: (b, 0, p, 0, 0)),
                  pl.BlockSpec((1, 1, NA_GROUP, PAST_LEN, HEAD_DIM), lambda b, p, g: (b, 0, p, 0, 0)),
                  pl.BlockSpec((NA_GROUP, 1, rows, NA_WIN_ROWS * GRID_W), lambda b, p, g: (p, kind(g), 0, 0))],
        out_specs=pl.BlockSpec((1, rows, NA_GROUP * HEAD_DIM), lambda b, p, g: (b, g, p)),
        out_shape=jax.ShapeDtypeStruct((DEC_BATCH, DEC_SEQ, NA_WIDTH), BF16),
        compiler_params=_params(("arbitrary",) * 3),
        name="latent_na",
    )(naq, nak, nav, cache_k, cache_v, tbl)


def _diff_kernel(q_ref, kt_ref, v_ref, ck_ref, cv_ref, lam_ref, sub_ref, o_ref):
    lam = _lam(lam_ref)
    vc = cv_ref[0, 0, 0].astype(BF16)
    vl = v_ref[0, 0]
    for r in range(DIFF_ROWS // Q_ROWS):
        rows = slice(Q_ROWS * r, Q_ROWS * (r + 1))
        parts = []
        for m in range(2):
            q = q_ref[0, 0, m, rows, :]
            sc = _nt(q, ck_ref[0, 0, 0, m].astype(BF16))
            sl = _mm(q, kt_ref[0, 0, m])
            mx = jnp.maximum(jnp.max(sc, axis=-1, keepdims=True), jnp.max(sl, axis=-1, keepdims=True))
            ec = jnp.exp(sc - mx)
            el = jnp.exp(sl - mx)
            l = jnp.sum(ec, axis=-1, keepdims=True) + jnp.sum(el, axis=-1, keepdims=True)
            parts.append((_mm(ec.astype(BF16), vc) + _mm(el.astype(BF16), vl), l))
        (a1, l1), (a2, l2) = parts
        o = a1 * (1.0 / l1) - a2 * (lam / l2)
        o_ref[0, rows, :] = _sub_norm(o, sub_ref[...]).astype(BF16)


def _latent_diff(dfq, dfk, dfv, cache_k, cache_v, lamv, sub):
    nt = DEC_SEQ // DIFF_ROWS
    full = lambda shape: pl.BlockSpec(shape, lambda b, h, t: (0,) * len(shape))
    return pl.pallas_call(
        _diff_kernel,
        grid=(DEC_BATCH, DIFF_HEADS, nt),
        in_specs=[pl.BlockSpec((1, 1, 2, DIFF_ROWS, HEAD_DIM), lambda b, h, t: (b, h, 0, t, 0)),
                  pl.BlockSpec((1, 1, 2, HEAD_DIM, DEC_SEQ), lambda b, h, t: (b, h, 0, 0, 0)),
                  pl.BlockSpec((1, 1, DEC_SEQ, 2 * HEAD_DIM), lambda b, h, t: (b, h, 0, 0)),
                  pl.BlockSpec((1, 1, 1, 2, PAST_LEN, HEAD_DIM), lambda b, h, t: (b, 0, h, 0, 0, 0)),
                  pl.BlockSpec((1, 1, 1, PAST_LEN, 2 * HEAD_DIM), lambda b, h, t: (b, 0, h, 0, 0)),
                  full((4, HEAD_DIM)), full((1, 2 * HEAD_DIM))],
        out_specs=pl.BlockSpec((1, DIFF_ROWS, 2 * HEAD_DIM), lambda b, h, t: (b, t, h)),
        out_shape=jax.ShapeDtypeStruct((DEC_BATCH, DEC_SEQ, DIFF_WIDTH), BF16),
        compiler_params=_params(("arbitrary",) * 3),
        name="latent_diff",
    )(dfq, dfk, dfv, cache_k, cache_v, lamv, sub)


def _lat_out_kernel(x_ref, na_ref, df_ref, mod_ref, wout_ref, lng_ref, lnb_ref, wrt_ref,
                    x1_ref, h2_ref, aff_ref):
    b = pl.program_id(0)
    x = x_ref[0]
    _, _, g1, sh2, sc2, _ = _mod_rows(mod_ref, 1 + b)
    a = _mm(na_ref[0], wout_ref[0:NA_WIDTH, :]) + _mm(df_ref[0], wout_ref[NA_WIDTH:, :])
    x1 = _layer_norm(DEEPNORM_ALPHA * x + g1 * a, lng_ref[...], lnb_ref[...])
    x1_ref[0] = x1
    h2 = x1 * (1.0 + sc2) + sh2
    h2_ref[0] = h2.astype(BF16)
    aff_ref[0] = _router_aff_t(h2, wrt_ref[...])


def _latent_out(x, na_o, df_o, mod, w_out, ln_g, ln_b, wrt):
    nt = DEC_SEQ // ROW_TILE
    full = lambda shape: pl.BlockSpec(shape, lambda b, t: (0,) * len(shape))
    tok = lambda w: pl.BlockSpec((1, ROW_TILE, w), lambda b, t: (b, t, 0))
    return pl.pallas_call(
        _lat_out_kernel,
        grid=(DEC_BATCH, nt),
        in_specs=[tok(D_MODEL), tok(NA_WIDTH), tok(DIFF_WIDTH), full((8, 6 * D_MODEL)),
                  full((D_MODEL, D_MODEL)), full((1, D_MODEL)), full((1, D_MODEL)),
                  full((N_EXPERTS, D_MODEL))],
        out_specs=[tok(D_MODEL), tok(D_MODEL),
                   pl.BlockSpec((1, N_EXPERTS, ROW_TILE), lambda b, t: (b, 0, t))],
        out_shape=[jax.ShapeDtypeStruct((DEC_BATCH, DEC_SEQ, D_MODEL), F32),
                   jax.ShapeDtypeStruct((DEC_BATCH, DEC_SEQ, D_MODEL), BF16),
                   jax.ShapeDtypeStruct((DEC_BATCH, N_EXPERTS, DEC_SEQ), F32)],
        compiler_params=_params(("arbitrary", "arbitrary")),
        name="latent_out",
    )(x, na_o, df_o, mod, w_out, ln_g, ln_b, wrt)


def _ranks_first(va, ia, vb, ib):
    return (va > vb) | ((va == vb) & (ia < ib))


def _route_kernel(aff_ref, idx_ref, gate_ref, *, cap, n):
    rows = aff_ref.shape[0]
    nb = n // LANES
    lane = lax.broadcasted_iota(I32, (rows, LANES), 1)
    v = [aff_ref[:, LANES * b:LANES * (b + 1)] for b in range(nb)]
    ix = [lane + LANES * b for b in range(nb)]
    k = 2
    while k <= n:
        j = k // 2
        while j >= 1:
            if j >= LANES:
                sb = j // LANES
                for lo in range(nb):
                    if lo & sb:
                        continue
                    hi = lo | sb
                    f = _ranks_first(v[lo], ix[lo], v[hi], ix[hi])
                    if k < n and (LANES * lo) & k:
                        f = jnp.logical_not(f)
                    v[lo], v[hi] = jnp.where(f, v[lo], v[hi]), jnp.where(f, v[hi], v[lo])
                    ix[lo], ix[hi] = jnp.where(f, ix[lo], ix[hi]), jnp.where(f, ix[hi], ix[lo])
            else:
                upper = (lane & j) != 0
                if k < LANES:
                    flip_lanes = jnp.logical_xor(upper, (lane & k) != 0)
                for b in range(nb):
                    pv = jnp.where(upper, pltpu.roll(v[b], j, 1), pltpu.roll(v[b], LANES - j, 1))
                    pi = jnp.where(upper, pltpu.roll(ix[b], j, 1), pltpu.roll(ix[b], LANES - j, 1))
                    f = _ranks_first(v[b], ix[b], pv, pi)
                    if k < LANES:
                        flip = flip_lanes
                    elif k < n and (LANES * b) & k:
                        flip = jnp.logical_not(upper)
                    else:
                        flip = upper
                    keep = jnp.logical_xor(f, flip)
                    v[b] = jnp.where(keep, v[b], pv)
                    ix[b] = jnp.where(keep, ix[b], pi)
            j //= 2
        k *= 2
    if cap < LANES:
        idx_ref[...] = ix[0][:, :cap]
        gate_ref[...] = v[0]
    else:
        idx_ref[...] = jnp.concatenate(ix[:cap // LANES], axis=1)
        gate_ref[...] = jnp.concatenate(v[:cap // LANES], axis=1)


def _route(aff_t, cap):
    rows, n = aff_t.shape
    return pl.pallas_call(
        functools.partial(_route_kernel, cap=cap, n=n),
        out_shape=[jax.ShapeDtypeStruct((rows, cap), I32),
                   jax.ShapeDtypeStruct((rows, max(cap, LANES)), F32)],
        compiler_params=_params(None),
        name=f"route_{n}",
    )(aff_t)


def _row_to_col(row):
    n = row.shape[1]
    eye = lax.broadcasted_iota(I32, (n, n), 0) == lax.broadcasted_iota(I32, (n, n), 1)
    return jnp.sum(jnp.where(eye, row, jnp.zeros_like(row)), axis=1, keepdims=True)


def _gather_ctx_kernel(idx_ref, h_ref, x_ref):
    nslot = N_EXPERTS * CAP_CTX
    hit = lax.broadcasted_iota(I32, (nslot, SEQ), 1) == _row_to_col(idx_ref[0])
    xs = _mm(jnp.where(hit, 1.0, 0.0).astype(BF16), h_ref[0])
    x_ref[...] = xs.astype(BF16).reshape(N_EXPERTS, CAP_CTX, D_MODEL)


def _gather_ctx(idx_row, h2):
    nslot = N_EXPERTS * CAP_CTX
    return pl.pallas_call(
        _gather_ctx_kernel,
        grid=(BATCH,),
        in_specs=[pl.BlockSpec((1, 1, nslot), lambda s: (s, 0, 0)),
                  pl.BlockSpec((1, SEQ, D_MODEL), lambda s: (s, 0, 0))],
        out_specs=pl.BlockSpec((N_EXPERTS, CAP_CTX, D_MODEL), lambda s: (0, s, 0)),
        out_shape=jax.ShapeDtypeStruct((N_EXPERTS, SLOTS_CTX, D_MODEL), BF16),
        compiler_params=_params(("arbitrary",)),
        name="gather_ctx",
    )(idx_row, h2)


def _gather_lat_kernel(idx_ref, h_ref, x_ref):
    col = _row_to_col(idx_ref[0, pl.ds(pl.program_id(1), 1), :])
    hit = lax.broadcasted_iota(I32, (CAP_LAT, DEC_SEQ), 1) == col
    x_ref[0] = _mm(jnp.where(hit, 1.0, 0.0).astype(BF16), h_ref[0]).astype(BF16)


def _gather_lat(idx, h2):
    return pl.pallas_call(
        _gather_lat_kernel,
        grid=(DEC_BATCH, N_EXPERTS),
        in_specs=[pl.BlockSpec((1, N_EXPERTS, CAP_LAT), lambda s, e: (s, 0, 0)),
                  pl.BlockSpec((1, DEC_SEQ, D_MODEL), lambda s, e: (s, 0, 0))],
        out_specs=pl.BlockSpec((1, CAP_LAT, D_MODEL), lambda s, e: (e, s, 0)),
        out_shape=jax.ShapeDtypeStruct((N_EXPERTS, SLOTS_LAT, D_MODEL), BF16),
        compiler_params=_params(("arbitrary", "arbitrary")),
        name="gather_lat",
    )(idx, h2)


def _ffn_kernel(xc_ref, xl_ref, gc_ref, gl_ref, wg_ref, wu_ref, wd_ref, yc_ref, yl_ref, h_scr, wl_scr):
    j = pl.program_id(1)
    tail = FF_HALF - FF_HALF // FF_TILE * FF_TILE

    def swiglu_cols(wg, wu, c0):
        wgb = wg.astype(BF16)
        wub = wu.astype(BF16)
        for i, x_ref in enumerate((xc_ref, xl_ref)):
            x = x_ref[0]
            a = _mm(x, wgb)
            u = _mm(x, wub)
            h = a * (1.0 / (1.0 + jnp.exp(-a))) * u
            h_scr[SLOTS_CTX * i:SLOTS_CTX * (i + 1), c0:c0 + FF_TILE] = h.astype(BF16)

    @pl.when(j == 0)
    def _():
        for c in range(FF_HALF // FF_TILE):
            lo = FF_TILE * c
            swiglu_cols(wg_ref[0, :, lo:lo + FF_TILE], wu_ref[0, :, lo:lo + FF_TILE], lo)
        wl_scr[0] = wg_ref[0, :, FF_HALF - tail:FF_HALF]
        wl_scr[1] = wu_ref[0, :, FF_HALF - tail:FF_HALF]

    @pl.when(j == 1)
    def _():
        head = FF_TILE - tail
        swiglu_cols(jnp.concatenate([wl_scr[0], wg_ref[0, :, 0:head]], axis=1),
                    jnp.concatenate([wl_scr[1], wu_ref[0, :, 0:head]], axis=1), FF_HALF - tail)
        for c in range(FF_HALF // FF_TILE):
            lo = head + FF_TILE * c
            swiglu_cols(wg_ref[0, :, lo:lo + FF_TILE], wu_ref[0, :, lo:lo + FF_TILE], FF_HALF + lo)

    @pl.when(j >= 2)
    def _():
        wdb = wd_ref[0].astype(BF16)
        e = pl.program_id(0)
        for i, (g_ref, y_ref) in enumerate(((gc_ref, yc_ref), (gl_ref, yl_ref))):
            nset = g_ref.shape[0] // N_EXPERTS
            gate = jnp.concatenate(
                [_row_to_col(g_ref[pl.ds(N_EXPERTS * s + e, 1), :][:, :SLOTS_CTX // nset]) for s in range(nset)],
                axis=0)
            y = _mm(h_scr[SLOTS_CTX * i:SLOTS_CTX * (i + 1), :], wdb)
            y_ref[0] = (y * gate).astype(BF16)


def _expert_ffn(xc, xl, gc, gl, w_gate, w_up, w_down):
    assert SLOTS_CTX == SLOTS_LAT and FF_HALF % LANES == 0 and D_MODEL % OUT_TILE == 0
    xs = lambda n: pl.BlockSpec((1, n, D_MODEL), lambda e, j: (e, 0, 0))
    gs = lambda g: pl.BlockSpec(g.shape, lambda e, j: (0, 0))
    last = N_EXPERTS - 1
    up = pl.BlockSpec((1, D_MODEL, FF_HALF),
                      lambda e, j: (jnp.where(j >= 2, jnp.minimum(e + 1, last), e), 0, jnp.where(j == 1, 1, 0)))
    down = pl.BlockSpec((1, D_FF, OUT_TILE), lambda e, j: (e, 0, jnp.maximum(j - 2, 0)))
    ys = lambda n: pl.BlockSpec((1, n, OUT_TILE), lambda e, j: (e, 0, jnp.maximum(j - 2, 0)))
    return pl.pallas_call(
        _ffn_kernel,
        grid=(N_EXPERTS, 2 + D_MODEL // OUT_TILE),
        in_specs=[xs(SLOTS_CTX), xs(SLOTS_LAT), gs(gc), gs(gl), up, up, down],
        out_specs=[ys(SLOTS_CTX), ys(SLOTS_LAT)],
        out_shape=[jax.ShapeDtypeStruct((N_EXPERTS, SLOTS_CTX, D_MODEL), BF16),
                   jax.ShapeDtypeStruct((N_EXPERTS, SLOTS_LAT, D_MODEL), BF16)],
        scratch_shapes=[pltpu.VMEM((SLOTS_CTX + SLOTS_LAT, D_FF), BF16),
                        pltpu.VMEM((2, D_MODEL, LANES), F32)],
        compiler_params=_params(("arbitrary", "arbitrary"), FFN_VMEM_LIMIT),
        name="expert_ffn",
    )(xc, xl, gc, gl, w_gate, w_up, w_down)


def _combine_ctx_kernel(y_ref, idx_ref, x1_ref, mod_ref, lng_ref, lnb_ref, o_ref):
    g2 = _mod_rows(mod_ref, 0)[5]
    nslot = N_EXPERTS * CAP_CTX
    hit = lax.broadcasted_iota(I32, (SEQ, nslot), 0) == idx_ref[0]
    y = y_ref[...].reshape(nslot, D_MODEL)
    f = _mm(jnp.where(hit, 1.0, 0.0).astype(BF16), y)
    o_ref[0] = _layer_norm(DEEPNORM_ALPHA * x1_ref[0] + g2 * f, lng_ref[...], lnb_ref[...])


def _combine_ctx(yc, idx_row, x1, mod, ln_g, ln_b):
    full = lambda shape: pl.BlockSpec(shape, lambda s: (0,) * len(shape))
    return pl.pallas_call(
        _combine_ctx_kernel,
        grid=(BATCH,),
        in_specs=[pl.BlockSpec((N_EXPERTS, CAP_CTX, D_MODEL), lambda s: (0, s, 0)),
                  pl.BlockSpec((1, 1, N_EXPERTS * CAP_CTX), lambda s: (s, 0, 0)),
                  pl.BlockSpec((1, SEQ, D_MODEL), lambda s: (s, 0, 0)),
                  full((8, 6 * D_MODEL)), full((1, D_MODEL)), full((1, D_MODEL))],
        out_specs=pl.BlockSpec((1, SEQ, D_MODEL), lambda s: (s, 0, 0)),
        out_shape=jax.ShapeDtypeStruct((BATCH, SEQ, D_MODEL), F32),
        compiler_params=_params(("arbitrary",)),
        name="combine_ctx",
    )(yc, idx_row, x1, mod, ln_g, ln_b)


def _combine_lat_kernel(y_ref, idx_ref, x1_ref, mod_ref, lng_ref, lnb_ref, o_ref):
    s = pl.program_id(0)
    g2 = _mod_rows(mod_ref, 1 + s)[5]
    tok = lax.broadcasted_iota(I32, (ROW_TILE, CAP_LAT), 0) + ROW_TILE * pl.program_id(1)
    f = jnp.zeros((ROW_TILE, D_MODEL), F32)
    for e in range(N_EXPERTS):
        hit = tok == idx_ref[0, e:e + 1, :]
        f = f + _mm(jnp.where(hit, 1.0, 0.0).astype(BF16), y_ref[e])
    o_ref[0] = _layer_norm(DEEPNORM_ALPHA * x1_ref[0] + g2 * f, lng_ref[...], lnb_ref[...])


def _combine_lat(yl, idx, x1, mod, ln_g, ln_b):
    nt = DEC_SEQ // ROW_TILE
    full = lambda shape: pl.BlockSpec(shape, lambda s, t: (0,) * len(shape))
    return pl.pallas_call(
        _combine_lat_kernel,
        grid=(DEC_BATCH, nt),
        in_specs=[pl.BlockSpec((N_EXPERTS, CAP_LAT, D_MODEL), lambda s, t: (0, s, 0)),
                  pl.BlockSpec((1, N_EXPERTS, CAP_LAT), lambda s, t: (s, 0, 0)),
                  pl.BlockSpec((1, ROW_TILE, D_MODEL), lambda s, t: (s, t, 0)),
                  full((8, 6 * D_MODEL)), full((1, D_MODEL)), full((1, D_MODEL))],
        out_specs=pl.BlockSpec((1, ROW_TILE, D_MODEL), lambda s, t: (s, t, 0)),
        out_shape=jax.ShapeDtypeStruct((DEC_BATCH, DEC_SEQ, D_MODEL), F32),
        compiler_params=_params(("arbitrary", "arbitrary")),
        name="combine_lat",
    )(yl, idx, x1, mod, ln_g, ln_b)


def _rope_tables():
    t = jnp.arange(DEC_SEQ)
    row = (t // GRID_W).astype(F32)
    col = (t % GRID_W).astype(F32)
    half = HEAD_DIM // 2
    inv = ROPE_THETA ** (-jnp.arange(0, half, 2, dtype=F32) / half)
    ang_r = row[:, None] * inv[None, :]
    ang_c = col[:, None] * inv[None, :]
    ang = jnp.concatenate([ang_r, ang_r, ang_c, ang_c], axis=-1)
    cos = jnp.tile(jnp.cos(ang), (1, 2))
    sin = jnp.tile(jnp.sin(ang), (1, 2))
    first = (jnp.arange(LANES) % 32) < 16
    return cos, jnp.where(first[None, :], -sin, sin)


def kernel(x_prompt, x_sample, cache_na_k, cache_na_v, cache_diff_k, cache_diff_v, c, c_ctx, w_ada, b_ada, w_in, w_out, na_rel_bias, lambda_q1, lambda_k1, lambda_q2, lambda_k2, subln_g, ln1_g, ln1_b, ln2_g, ln2_b, w_router, w_gate, w_up, w_down):
    l = 0
    cvec = jnp.concatenate([c_ctx[None, :], c, jnp.zeros((8 - 1 - DEC_BATCH, D_MODEL), F32)], axis=0)
    mod = _modulation(cvec, w_ada[l], b_ada[l][None, :])
    lamv = jnp.stack([lambda_q1[l], lambda_k1[l], lambda_q2[l], lambda_k2[l]], axis=0)
    w_in_b = w_in[l].astype(BF16)
    w_out_b = w_out[l].astype(BF16)
    sub = subln_g[l][None, :]
    wrt = w_router[l].T

    x1c, h2c, new_na_k, new_na_v, new_diff_k, new_diff_v, affc = _context_block(
        x_prompt, mod, lamv, w_in_b, w_out_b, sub, ln1_g[l][None, :], ln1_b[l][None, :], wrt)

    cos, sin = _rope_tables()
    naq, nak, nav, dfq, dfk, dfv = _latent_qkv(x_sample, mod, w_in_b, cos, sin)
    tbl = _na_bias_tables(na_rel_bias[l].reshape(-1))
    na_o = _latent_na(naq, nak, nav, cache_na_k, cache_na_v, tbl)
    df_o = _latent_diff(dfq, dfk, dfv, cache_diff_k, cache_diff_v, lamv, sub)
    x1l, h2l, affl = _latent_out(x_sample, na_o, df_o, mod, w_out_b,
                                 ln1_g[l][None, :], ln1_b[l][None, :], wrt)

    idxc, gatec = _route(affc.reshape(BATCH * N_EXPERTS, SEQ), CAP_CTX)
    idxl, gatel = _route(affl.reshape(DEC_BATCH * N_EXPERTS, DEC_SEQ), CAP_LAT)
    idxc = idxc.reshape(BATCH, 1, N_EXPERTS * CAP_CTX)
    idxl = idxl.reshape(DEC_BATCH, N_EXPERTS, CAP_LAT)
    xc = _gather_ctx(idxc, h2c)
    xl = _gather_lat(idxl, h2l)
    yc, yl = _expert_ffn(xc, xl, gatec, gatel, w_gate[l], w_up[l], w_down[l])
    y_prompt = _combine_ctx(yc, idxc, x1c, mod, ln2_g[l][None, :], ln2_b[l][None, :])
    y_sample = _combine_lat(yl, idxl, x1l, mod, ln2_g[l][None, :], ln2_b[l][None, :])
    return (y_prompt, y_sample, new_na_k, new_na_v, new_diff_k, new_diff_v)
```

```python
import functools
import math

import jax
import jax.numpy as jnp
import numpy as np
from jax import lax
from jax.experimental import pallas as pl
from jax.experimental.pallas import tpu as pltpu

F32 = jnp.float32
BF16 = jnp.bfloat16
I32 = jnp.int32

D_MODEL = 1024
BATCH = 16
SEQ = 256
DEC_BATCH = 2
DEC_SEQ = 2048
PAST_LEN = 256
GRID_W = 64
GRID_ROWS = DEC_SEQ // GRID_W
HEAD_DIM = 64
NA_HEADS = 8
DIFF_HEADS = 4
NA_WIDTH = NA_HEADS * HEAD_DIM
DIFF_WIDTH = DIFF_HEADS * 2 * HEAD_DIM
QKV_WIDTH = 3 * NA_WIDTH + 3 * DIFF_WIDTH
NA_WIN_H = 8
NA_WIN_W = 16
N_REL_H = 2 * NA_WIN_H - 1
N_REL_W = 2 * NA_WIN_W - 1
N_EXPERTS = 16
EC_CAPACITY_FACTOR = 2
D_FF = 2816
ROPE_THETA = 10000.0
NORM_EPS = 1e-5
NEG_INF = -1e30
DEPTH = 1
DEEPNORM_ALPHA = (2.0 * DEPTH) ** 0.25
LAMBDA_INIT = 0.8 - 0.6 * math.exp(-0.3 * 0)
SCALE = HEAD_DIM ** -0.5

OFF_NAQ = 0
OFF_NAK = NA_WIDTH
OFF_NAV = 2 * NA_WIDTH
OFF_DFQ = 3 * NA_WIDTH
OFF_DFK = 3 * NA_WIDTH + DIFF_WIDTH
OFF_DFV = 3 * NA_WIDTH + 2 * DIFF_WIDTH

CAP_CTX = EC_CAPACITY_FACTOR * SEQ // N_EXPERTS
CAP_LAT = EC_CAPACITY_FACTOR * DEC_SEQ // N_EXPERTS
SLOTS_CTX = BATCH * CAP_CTX
SLOTS_LAT = DEC_BATCH * CAP_LAT

ROW_TILE = 256
NA_QUAD = 4
NA_WIN_ROWS = 12
NA_GROUP = 8
FF_TILE = 256
FF_HALF = D_FF // 2
OUT_TILE = 512
LANES = 128
DIFF_ROWS = 1024
Q_ROWS = 256

VMEM_LIMIT = 48 * 1024 * 1024
FFN_VMEM_LIMIT = 58 * 1024 * 1024


def _params(sem, vmem=VMEM_LIMIT, flags=None):
    return pltpu.CompilerParams(dimension_semantics=sem, vmem_limit_bytes=vmem, flags=flags)


def _mm(a, b):
    return jnp.dot(a, b, preferred_element_type=F32)


def _nt(a, b):
    return lax.dot_general(a, b, (((1,), (1,)), ((), ())), preferred_element_type=F32)


def _split(a):
    hi = a.astype(BF16)
    lo = (a - hi.astype(F32)).astype(BF16)
    return hi, lo


def _mm3(a, b):
    ah, al = _split(a)
    bh, bl = _split(b)
    return _mm(ah, bh) + _mm(al, bh) + _mm(ah, bl)


def _nt3(a, b):
    ah, al = _split(a)
    bh, bl = _split(b)
    return _nt(ah, bh) + _nt(al, bh) + _nt(ah, bl)


def _layer_norm(y, g, b):
    mu = jnp.mean(y, axis=-1, keepdims=True)
    d = y - mu
    var = jnp.mean(d * d, axis=-1, keepdims=True)
    return d * lax.rsqrt(var + NORM_EPS) * g + b


def _lam(l_ref):
    l = l_ref[...]
    a = jnp.sum(l[0:1] * l[1:2], axis=-1, keepdims=True)
    b = jnp.sum(l[2:3] * l[3:4], axis=-1, keepdims=True)
    return jnp.exp(a) - jnp.exp(b) + LAMBDA_INIT


def _mod_rows(mod_ref, row):
    return [mod_ref[pl.ds(row, 1), i * D_MODEL:(i + 1) * D_MODEL] for i in range(6)]


def _sub_norm(o, sub):
    ms = jnp.mean(o * o, axis=-1, keepdims=True)
    return o * lax.rsqrt(ms + NORM_EPS) * sub * (1.0 - LAMBDA_INIT)


def _router_aff_t(h2, wrt):
    lg = _nt3(wrt, h2)
    m = jnp.max(lg, axis=0, keepdims=True)
    e = jnp.exp(lg - m)
    return e / jnp.sum(e, axis=0, keepdims=True)


def _mod_kernel(c_ref, w_ref, b_ref, o_ref):
    c = c_ref[...]
    s = c * (1.0 / (1.0 + jnp.exp(-c)))
    o_ref[...] = _mm3(s, w_ref[...]) + b_ref[...]


def _modulation(cvec, w_ada, b_ada):
    ncol = 6 * D_MODEL
    return pl.pallas_call(
        _mod_kernel,
        grid=(6,),
        in_specs=[pl.BlockSpec((8, D_MODEL), lambda j: (0, 0)),
                  pl.BlockSpec((D_MODEL, D_MODEL), lambda j: (0, j)),
                  pl.BlockSpec((1, D_MODEL), lambda j: (0, j))],
        out_specs=pl.BlockSpec((8, D_MODEL), lambda j: (0, j)),
        out_shape=jax.ShapeDtypeStruct((8, ncol), F32),
        compiler_params=_params(("arbitrary",)),
        name="modulation",
    )(cvec, w_ada, b_ada)


def _ctx_kernel(x_ref, mod_ref, lam_ref, win_ref, wout_ref, sub_ref, lng_ref, lnb_ref, wrt_ref,
                x1_ref, h2_ref, aff_ref, nak_ref, nav_ref, dfk_ref, dfv_ref,
                qkv_scr, o_scr, s_scr, e_scr):
    x = x_ref[0]
    sh1, sc1, g1, sh2, sc2, g2 = _mod_rows(mod_ref, 0)
    h = (x * (1.0 + sc1) + sh1).astype(BF16)
    qkv_scr[...] = _mm(h, win_ref[...])

    kt_na = qkv_scr[:, OFF_NAK:OFF_NAK + NA_WIDTH].T
    kt_df = qkv_scr[:, OFF_DFK:OFF_DFK + DIFF_WIDTH].T
    nak_ref[0, 0] = kt_na.reshape(NA_HEADS, HEAD_DIM, SEQ)
    nav_ref[0, 0] = qkv_scr[:, OFF_NAV:OFF_NAV + NA_WIDTH].T.reshape(NA_HEADS, HEAD_DIM, SEQ)
    dfk_ref[0, 0] = kt_df.reshape(DIFF_HEADS, 2, HEAD_DIM, SEQ)
    for hh in range(DIFF_HEADS):
        c0 = OFF_DFV + 2 * HEAD_DIM * hh
        dfv_ref[0, 0, hh] = qkv_scr[:, c0:c0 + 2 * HEAD_DIM]

    kt = jnp.concatenate([kt_na, kt_df], axis=0).astype(BF16)
    q_cols = [OFF_NAQ + HEAD_DIM * i for i in range(NA_HEADS)] + [
        OFF_DFQ + HEAD_DIM * i for i in range(2 * DIFF_HEADS)]
    nmap = len(q_cols)
    for i, c in enumerate(q_cols):
        q = (qkv_scr[:, c:c + HEAD_DIM] * SCALE).astype(BF16)
        s_scr[i] = _mm(q, kt[HEAD_DIM * i:HEAD_DIM * (i + 1), :])
    s = s_scr[...]
    e = jnp.exp(s - jnp.max(s, axis=-1, keepdims=True))
    rl = 1.0 / jnp.sum(e, axis=-1, keepdims=True)
    e_scr[...] = e.astype(BF16)

    for hh in range(NA_HEADS):
        v = qkv_scr[:, OFF_NAV + HEAD_DIM * hh:OFF_NAV + HEAD_DIM * (hh + 1)].astype(BF16)
        o_scr[:, HEAD_DIM * hh:HEAD_DIM * (hh + 1)] = (_mm(e_scr[hh], v) * rl[hh]).astype(BF16)
    lam = _lam(lam_ref)
    sub = sub_ref[...]
    for hh in range(DIFF_HEADS):
        i1 = NA_HEADS + 2 * hh
        c0 = OFF_DFV + 2 * HEAD_DIM * hh
        v = qkv_scr[:, c0:c0 + 2 * HEAD_DIM].astype(BF16)
        o = _mm(e_scr[i1], v) * rl[i1] - _mm(e_scr[i1 + 1], v) * (lam * rl[i1 + 1])
        c1 = NA_WIDTH + 2 * HEAD_DIM * hh
        o_scr[:, c1:c1 + 2 * HEAD_DIM] = _sub_norm(o, sub).astype(BF16)

    a = _mm(o_scr[...], wout_ref[...])
    x1 = _layer_norm(DEEPNORM_ALPHA * x + g1 * a, lng_ref[...], lnb_ref[...])
    x1_ref[0] = x1
    h2 = x1 * (1.0 + sc2) + sh2

    h2_ref[0] = h2.astype(BF16)
    aff_ref[0] = _router_aff_t(h2, wrt_ref[...])


def _context_block(x, mod, lamv, w_in, w_out, sub, ln_g, ln_b, wrt):
    full = lambda shape: pl.BlockSpec(shape, lambda b: (0,) * len(shape))
    return pl.pallas_call(
        _ctx_kernel,
        grid=(BATCH,),
        in_specs=[pl.BlockSpec((1, SEQ, D_MODEL), lambda b: (b, 0, 0)),
                  full((8, 6 * D_MODEL)), full((4, HEAD_DIM)),
                  full((D_MODEL, QKV_WIDTH)), full((D_MODEL, D_MODEL)),
                  full((1, 2 * HEAD_DIM)), full((1, D_MODEL)), full((1, D_MODEL)),
                  full((N_EXPERTS, D_MODEL))],
        out_specs=[pl.BlockSpec((1, SEQ, D_MODEL), lambda b: (b, 0, 0)),
                   pl.BlockSpec((1, SEQ, D_MODEL), lambda b: (b, 0, 0)),
                   pl.BlockSpec((1, N_EXPERTS, SEQ), lambda b: (b, 0, 0)),
                   pl.BlockSpec((1, 1, NA_HEADS, HEAD_DIM, SEQ), lambda b: (b, 0, 0, 0, 0)),
                   pl.BlockSpec((1, 1, NA_HEADS, HEAD_DIM, SEQ), lambda b: (b, 0, 0, 0, 0)),
                   pl.BlockSpec((1, 1, DIFF_HEADS, 2, HEAD_DIM, SEQ), lambda b: (b, 0, 0, 0, 0, 0)),
                   pl.BlockSpec((1, 1, DIFF_HEADS, SEQ, 2 * HEAD_DIM), lambda b: (b, 0, 0, 0, 0))],
        out_shape=[jax.ShapeDtypeStruct((BATCH, SEQ, D_MODEL), F32),
                   jax.ShapeDtypeStruct((BATCH, SEQ, D_MODEL), BF16),
                   jax.ShapeDtypeStruct((BATCH, N_EXPERTS, SEQ), F32),
                   jax.ShapeDtypeStruct((BATCH, 1, NA_HEADS, HEAD_DIM, SEQ), F32),
                   jax.ShapeDtypeStruct((BATCH, 1, NA_HEADS, HEAD_DIM, SEQ), F32),
                   jax.ShapeDtypeStruct((BATCH, 1, DIFF_HEADS, 2, HEAD_DIM, SEQ), F32),
                   jax.ShapeDtypeStruct((BATCH, 1, DIFF_HEADS, SEQ, 2 * HEAD_DIM), F32)],
        scratch_shapes=[pltpu.VMEM((SEQ, QKV_WIDTH), F32), pltpu.VMEM((SEQ, D_MODEL), BF16),
                        pltpu.VMEM((NA_HEADS + 2 * DIFF_HEADS, SEQ, SEQ), F32),
                        pltpu.VMEM((NA_HEADS + 2 * DIFF_HEADS, SEQ, SEQ), BF16)],
        compiler_params=_params(("arbitrary",)),
        name="context_block",
    )(x, mod, lamv, w_in, w_out, sub, ln_g, ln_b, wrt)


def _lat_qkv_kernel(x_ref, mod_ref, win_ref, cos_ref, sin_ref,
                    naq_ref, nak_ref, nav_ref, dfq_ref, dfk_ref, dfv_ref, qkv_scr):
    b = pl.program_id(0)
    x = x_ref[0]
    sh1, sc1 = _mod_rows(mod_ref, 1 + b)[:2]
    h = (x * (1.0 + sc1) + sh1).astype(BF16)
    qkv_scr[...] = _mm(h, win_ref[...])

    for hh in range(NA_HEADS):
        lo, hi = HEAD_DIM * hh, HEAD_DIM * (hh + 1)
        naq_ref[0, hh] = (qkv_scr[:, OFF_NAQ + lo:OFF_NAQ + hi] * SCALE).astype(BF16)
        nak_ref[0, hh] = qkv_scr[:, OFF_NAK + lo:OFF_NAK + hi].astype(BF16)
        nav_ref[0, hh] = qkv_scr[:, OFF_NAV + lo:OFF_NAV + hi].astype(BF16)

    cos = cos_ref[...]
    sin = sin_ref[...]
    lane = lax.broadcasted_iota(I32, (ROW_TILE, LANES), 1)
    first = (lane & 31) < 16

    def rope(t):
        rot = jnp.where(first, pltpu.roll(t, LANES - 16, 1), pltpu.roll(t, 16, 1))
        return t * cos + rot * sin

    for hh in range(DIFF_HEADS):
        lo = 2 * HEAD_DIM * hh
        q = rope(qkv_scr[:, OFF_DFQ + lo:OFF_DFQ + lo + LANES]) * SCALE
        k = rope(qkv_scr[:, OFF_DFK + lo:OFF_DFK + lo + LANES])
        for m in range(2):
            dfq_ref[0, hh, m] = q[:, HEAD_DIM * m:HEAD_DIM * (m + 1)].astype(BF16)
        dfk_ref[0, hh] = k.T.astype(BF16).reshape(2, HEAD_DIM, ROW_TILE)
        dfv_ref[0, hh] = qkv_scr[:, OFF_DFV + lo:OFF_DFV + lo + LANES].astype(BF16)


def _latent_qkv(x, mod, w_in, cos, sin):
    nt = DEC_SEQ // ROW_TILE
    full = lambda shape: pl.BlockSpec(shape, lambda b, t: (0,) * len(shape))
    hs = lambda nh: pl.BlockSpec((1, nh, ROW_TILE, HEAD_DIM), lambda b, t: (b, 0, t, 0))
    return pl.pallas_call(
        _lat_qkv_kernel,
        grid=(DEC_BATCH, nt),
        in_specs=[pl.BlockSpec((1, ROW_TILE, D_MODEL), lambda b, t: (b, t, 0)),
                  full((8, 6 * D_MODEL)), full((D_MODEL, QKV_WIDTH)),
                  pl.BlockSpec((ROW_TILE, LANES), lambda b, t: (t, 0)),
                  pl.BlockSpec((ROW_TILE, LANES), lambda b, t: (t, 0))],
        out_specs=[hs(NA_HEADS), hs(NA_HEADS), hs(NA_HEADS),
                   pl.BlockSpec((1, DIFF_HEADS, 2, ROW_TILE, HEAD_DIM), lambda b, t: (b, 0, 0, t, 0)),
                   pl.BlockSpec((1, DIFF_HEADS, 2, HEAD_DIM, ROW_TILE), lambda b, t: (b, 0, 0, 0, t)),
                   pl.BlockSpec((1, DIFF_HEADS, ROW_TILE, 2 * HEAD_DIM), lambda b, t: (b, 0, t, 0))],
        out_shape=[jax.ShapeDtypeStruct((DEC_BATCH, NA_HEADS, DEC_SEQ, HEAD_DIM), BF16)] * 3
        + [jax.ShapeDtypeStruct((DEC_BATCH, DIFF_HEADS, 2, DEC_SEQ, HEAD_DIM), BF16),
           jax.ShapeDtypeStruct((DEC_BATCH, DIFF_HEADS, 2, HEAD_DIM, DEC_SEQ), BF16),
           jax.ShapeDtypeStruct((DEC_BATCH, DIFF_HEADS, DEC_SEQ, 2 * HEAD_DIM), BF16)],
        scratch_shapes=[pltpu.VMEM((ROW_TILE, QKV_WIDTH), F32)],
        compiler_params=_params(("arbitrary", "arbitrary")),
        name="latent_qkv",
    )(x, mod, w_in, cos, sin)


def _na_row_offset(kind, qr, kr):
    if kind == 0:
        return kr - qr + 7 if kr < NA_WIN_H else None
    if kind == 1:
        return kr - qr + 3 if qr <= kr < qr + NA_WIN_H else None
    return kr - qr - 1 if NA_WIN_ROWS - NA_WIN_H <= kr else None


def _bias_kernel(rpb_ref, o_ref):
    h = pl.program_id(0)
    base = h * (N_REL_H * N_REL_W)
    wq = lax.broadcasted_iota(I32, (GRID_W, LANES), 0)
    c = lax.broadcasted_iota(I32, (GRID_W, LANES), 1)
    wk = c & (GRID_W - 1)
    right = c >= GRID_W
    coff = jnp.clip(wk - wq, -(NA_WIN_W - 1), NA_WIN_W - 1) + (NA_WIN_W - 1)
    c0 = jnp.clip(wq - NA_WIN_W // 2, 0, GRID_W - NA_WIN_W)
    in_win = (wk >= c0) & (wk < c0 + NA_WIN_W)
    cache = {}

    def tile(rl, rr):
        if (rl, rr) in cache:
            return cache[(rl, rr)]
        if rl is None and rr is None:
            t = jnp.full((GRID_W, LANES), NEG_INF, F32)
        else:
            acc = jnp.zeros((GRID_W, LANES), F32)
            for co in range(N_REL_W):
                vl = rpb_ref[base + rl * N_REL_W + co] if rl is not None else 0.0
                vr = rpb_ref[base + rr * N_REL_W + co] if rr is not None else 0.0
                acc = jnp.where(coff == co, jnp.where(right, vr, vl), acc)
            ok = in_win
            if rl is None:
                ok = ok & right
            if rr is None:
                ok = ok & jnp.logical_not(right)
            t = jnp.where(ok, acc, NEG_INF)
        cache[(rl, rr)] = t
        return t

    for kind in range(3):
        for qr in range(NA_QUAD):
            row = [tile(_na_row_offset(kind, qr, 2 * p), _na_row_offset(kind, qr, 2 * p + 1))
                   for p in range(NA_WIN_ROWS // 2)]
            o_ref[0, kind, GRID_W * qr:GRID_W * (qr + 1), :] = jnp.concatenate(row, axis=1)


def _na_bias_tables(rpb_flat):
    return pl.pallas_call(
        _bias_kernel,
        grid=(NA_HEADS,),
        in_specs=[pl.BlockSpec(memory_space=pltpu.SMEM)],
        out_specs=pl.BlockSpec((1, 3, NA_QUAD * GRID_W, NA_WIN_ROWS * GRID_W), lambda h: (h, 0, 0, 0)),
        out_shape=jax.ShapeDtypeStruct((NA_HEADS, 3, NA_QUAD * GRID_W, NA_WIN_ROWS * GRID_W), F32),
        compiler_params=_params(("arbitrary",)),
        name="na_bias_tables",
    )(rpb_flat)


def _na_kernel(q_ref, k_ref, v_ref, ck_ref, cv_ref, tbl_ref, o_ref):
    g = pl.program_id(2)
    u0 = jnp.clip(NA_QUAD * g - NA_QUAD, 0, GRID_ROWS - NA_WIN_ROWS)
    start = pl.multiple_of(u0 * GRID_W, GRID_W)
    nwin = NA_WIN_ROWS * GRID_W
    for j in range(NA_GROUP):
        q = q_ref[0, j]
        kw = k_ref[0, j, pl.ds(start, nwin), :]
        vw = v_ref[0, j, pl.ds(start, nwin), :]
        kct = ck_ref[0, 0, j].astype(BF16)
        vct = cv_ref[0, 0, j].astype(BF16)
        sl = _nt(q, kw) + tbl_ref[j, 0]
        sc = _mm(q, kct)
        m = jnp.maximum(jnp.max(sl, axis=-1, keepdims=True), jnp.max(sc, axis=-1, keepdims=True))
        el = jnp.exp(sl - m)
        ec = jnp.exp(sc - m)
        l = jnp.sum(el, axis=-1, keepdims=True) + jnp.sum(ec, axis=-1, keepdims=True)
        o = (_mm(el.astype(BF16), vw) + _nt(ec.astype(BF16), vct)) / l
        o_ref[0, :, HEAD_DIM * j:HEAD_DIM * (j + 1)] = o.astype(BF16)


def _latent_na(naq, nak, nav, cache_k, cache_v, tbl):
    nq = GRID_ROWS // NA_QUAD
    rows = NA_QUAD * GRID_W
    kind = lambda g: jnp.minimum(g, 1) + g // (nq - 1)
    return pl.pallas_call(
        _na_kernel,
        grid=(DEC_BATCH, NA_HEADS // NA_GROUP, nq),
        in_specs=[pl.BlockSpec((1, NA_GROUP, rows, HEAD_DIM), lambda b, p, g: (b, p, g, 0)),
                  pl.BlockSpec((1, NA_GROUP, DEC_SEQ, HEAD_DIM), lambda b, p, g: (b, p, 0, 0)),
                  pl.BlockSpec((1, NA_GROUP, DEC_SEQ, HEAD_DIM), lambda b, p, g: (b, p, 0, 0)),
                  pl.BlockSpec((1, 1, NA_GROUP, HEAD_DIM, PAST_LEN), lambda b, p, g: (b, 0, p, 0, 0)),
                  pl.BlockSpec((1, 1, NA_GROUP, HEAD_DIM, PAST_LEN), lambda b, p, g: (b, 0, p, 0, 0)),
                  pl.BlockSpec((NA_GROUP, 1, rows, NA_WIN_ROWS * GRID_W), lambda b, p, g: (p, kind(g), 0, 0))],
        out_specs=pl.BlockSpec((1, rows, NA_GROUP * HEAD_DIM), lambda b, p, g: (b, g, p)),
        out_shape=jax.ShapeDtypeStruct((DEC_BATCH, DEC_SEQ, NA_WIDTH), BF16),
        compiler_params=_params(("arbitrary",) * 3),
        name="latent_na",
    )(naq, nak, nav, cache_k, cache_v, tbl)


def _diff_kernel(q_ref, kt_ref, v_ref, ck_ref, cv_ref, lam_ref, sub_ref, o_ref):
    lam = _lam(lam_ref)
    vc = cv_ref[0, 0, 0].astype(BF16)
    vl = v_ref[0, 0]
    for r in range(DIFF_ROWS // Q_ROWS):
        rows = slice(Q_ROWS * r, Q_ROWS * (r + 1))
        parts = []
        for m in range(2):
            q = q_ref[0, 0, m, rows, :]
            sc = _mm(q, ck_ref[0, 0, 0, m].astype(BF16))
            sl = _mm(q, kt_ref[0, 0, m])
            mx = jnp.maximum(jnp.max(sc, axis=-1, keepdims=True), jnp.max(sl, axis=-1, keepdims=True))
            ec = jnp.exp(sc - mx)
            el = jnp.exp(sl - mx)
            l = jnp.sum(ec, axis=-1, keepdims=True) + jnp.sum(el, axis=-1, keepdims=True)
            parts.append((_mm(ec.astype(BF16), vc) + _mm(el.astype(BF16), vl), l))
        (a1, l1), (a2, l2) = parts
        o = a1 * (1.0 / l1) - a2 * (lam / l2)
        o_ref[0, rows, :] = _sub_norm(o, sub_ref[...]).astype(BF16)


def _latent_diff(dfq, dfk, dfv, cache_k, cache_v, lamv, sub):
    nt = DEC_SEQ // DIFF_ROWS
    full = lambda shape: pl.BlockSpec(shape, lambda b, h, t: (0,) * len(shape))
    return pl.pallas_call(
        _diff_kernel,
        grid=(DEC_BATCH, DIFF_HEADS, nt),
        in_specs=[pl.BlockSpec((1, 1, 2, DIFF_ROWS, HEAD_DIM), lambda b, h, t: (b, h, 0, t, 0)),
                  pl.BlockSpec((1, 1, 2, HEAD_DIM, DEC_SEQ), lambda b, h, t: (b, h, 0, 0, 0)),
                  pl.BlockSpec((1, 1, DEC_SEQ, 2 * HEAD_DIM), lambda b, h, t: (b, h, 0, 0)),
                  pl.BlockSpec((1, 1, 1, 2, HEAD_DIM, PAST_LEN), lambda b, h, t: (b, 0, h, 0, 0, 0)),
                  pl.BlockSpec((1, 1, 1, PAST_LEN, 2 * HEAD_DIM), lambda b, h, t: (b, 0, h, 0, 0)),
                  full((4, HEAD_DIM)), full((1, 2 * HEAD_DIM))],
        out_specs=pl.BlockSpec((1, DIFF_ROWS, 2 * HEAD_DIM), lambda b, h, t: (b, t, h)),
        out_shape=jax.ShapeDtypeStruct((DEC_BATCH, DEC_SEQ, DIFF_WIDTH), BF16),
        compiler_params=_params(("arbitrary",) * 3),
        name="latent_diff",
    )(dfq, dfk, dfv, cache_k, cache_v, lamv, sub)


def _lat_out_kernel(x_ref, na_ref, df_ref, mod_ref, wout_ref, lng_ref, lnb_ref, wrt_ref,
                    x1_ref, h2_ref, aff_ref):
    b = pl.program_id(0)
    x = x_ref[0]
    _, _, g1, sh2, sc2, _ = _mod_rows(mod_ref, 1 + b)
    a = _mm(na_ref[0], wout_ref[0:NA_WIDTH, :]) + _mm(df_ref[0], wout_ref[NA_WIDTH:, :])
    x1 = _layer_norm(DEEPNORM_ALPHA * x + g1 * a, lng_ref[...], lnb_ref[...])
    x1_ref[0] = x1
    h2 = x1 * (1.0 + sc2) + sh2
    h2_ref[0] = h2.astype(BF16)
    aff_ref[0] = _router_aff_t(h2, wrt_ref[...])


def _latent_out(x, na_o, df_o, mod, w_out, ln_g, ln_b, wrt):
    nt = DEC_SEQ // ROW_TILE
    full = lambda shape: pl.BlockSpec(shape, lambda b, t: (0,) * len(shape))
    tok = lambda w: pl.BlockSpec((1, ROW_TILE, w), lambda b, t: (b, t, 0))
    return pl.pallas_call(
        _lat_out_kernel,
        grid=(DEC_BATCH, nt),
        in_specs=[tok(D_MODEL), tok(NA_WIDTH), tok(DIFF_WIDTH), full((8, 6 * D_MODEL)),
                  full((D_MODEL, D_MODEL)), full((1, D_MODEL)), full((1, D_MODEL)),
                  full((N_EXPERTS, D_MODEL))],
        out_specs=[tok(D_MODEL), tok(D_MODEL),
                   pl.BlockSpec((1, N_EXPERTS, ROW_TILE), lambda b, t: (b, 0, t))],
        out_shape=[jax.ShapeDtypeStruct((DEC_BATCH, DEC_SEQ, D_MODEL), F32),
                   jax.ShapeDtypeStruct((DEC_BATCH, DEC_SEQ, D_MODEL), BF16),
                   jax.ShapeDtypeStruct((DEC_BATCH, N_EXPERTS, DEC_SEQ), F32)],
        compiler_params=_params(("arbitrary", "arbitrary")),
        name="latent_out",
    )(x, na_o, df_o, mod, w_out, ln_g, ln_b, wrt)


def _ranks_first(va, ia, vb, ib):
    return (va > vb) | ((va == vb) & (ia < ib))


def _sort_rows_desc(aff):
    rows, n = aff.shape
    nb = n // LANES
    lane = lax.broadcasted_iota(I32, (rows, LANES), 1)
    v = [aff[:, LANES * b:LANES * (b + 1)] for b in range(nb)]
    ix = [lane + LANES * b for b in range(nb)]
    k = 2
    while k <= n:
        j = k // 2
        while j >= 1:
            if j >= LANES:
                sb = j // LANES
                for lo in range(nb):
                    if lo & sb:
                        continue
                    hi = lo | sb
                    f = _ranks_first(v[lo], ix[lo], v[hi], ix[hi])
                    if k < n and (LANES * lo) & k:
                        f = jnp.logical_not(f)
                    v[lo], v[hi] = jnp.where(f, v[lo], v[hi]), jnp.where(f, v[hi], v[lo])
                    ix[lo], ix[hi] = jnp.where(f, ix[lo], ix[hi]), jnp.where(f, ix[hi], ix[lo])
            else:
                upper = (lane & j) != 0
                if k < LANES:
                    flip_lanes = jnp.logical_xor(upper, (lane & k) != 0)
                for b in range(nb):
                    pv = jnp.where(upper, pltpu.roll(v[b], j, 1), pltpu.roll(v[b], LANES - j, 1))
                    pi = jnp.where(upper, pltpu.roll(ix[b], j, 1), pltpu.roll(ix[b], LANES - j, 1))
                    f = _ranks_first(v[b], ix[b], pv, pi)
                    if k < LANES:
                        flip = flip_lanes
                    elif k < n and (LANES * b) & k:
                        flip = jnp.logical_not(upper)
                    else:
                        flip = upper
                    keep = jnp.logical_xor(f, flip)
                    v[b] = jnp.where(keep, v[b], pv)
                    ix[b] = jnp.where(keep, ix[b], pi)
            j //= 2
        k *= 2
    return v, ix


def _route_kernel(aff_ref, idx_ref, gate_ref, *, cap, n):
    v, ix = _sort_rows_desc(aff_ref[...])
    if cap < LANES:
        idx_ref[...] = ix[0][:, :cap]
        gate_ref[...] = v[0]
    else:
        idx_ref[...] = jnp.concatenate(ix[:cap // LANES], axis=1)
        gate_ref[...] = jnp.concatenate(v[:cap // LANES], axis=1)


def _route(aff_t, cap):
    rows, n = aff_t.shape
    return pl.pallas_call(
        functools.partial(_route_kernel, cap=cap, n=n),
        out_shape=[jax.ShapeDtypeStruct((rows, cap), I32),
                   jax.ShapeDtypeStruct((rows, max(cap, LANES)), F32)],
        compiler_params=_params(None),
        name=f"route_{n}",
    )(aff_t)


def _row_to_col(row):
    n = row.shape[1]
    eye = lax.broadcasted_iota(I32, (n, n), 0) == lax.broadcasted_iota(I32, (n, n), 1)
    return jnp.sum(jnp.where(eye, row, jnp.zeros_like(row)), axis=1, keepdims=True)


def _gather_ctx_kernel(idx_ref, h_ref, x_ref):
    nslot = N_EXPERTS * CAP_CTX
    hit = lax.broadcasted_iota(I32, (nslot, SEQ), 1) == _row_to_col(idx_ref[0])
    xs = _mm(jnp.where(hit, 1.0, 0.0).astype(BF16), h_ref[0])
    x_ref[...] = xs.astype(BF16).reshape(N_EXPERTS, CAP_CTX, D_MODEL)


def _gather_ctx(idx_row, h2):
    nslot = N_EXPERTS * CAP_CTX
    return pl.pallas_call(
        _gather_ctx_kernel,
        grid=(BATCH,),
        in_specs=[pl.BlockSpec((1, 1, nslot), lambda s: (s, 0, 0)),
                  pl.BlockSpec((1, SEQ, D_MODEL), lambda s: (s, 0, 0))],
        out_specs=pl.BlockSpec((N_EXPERTS, CAP_CTX, D_MODEL), lambda s: (0, s, 0)),
        out_shape=jax.ShapeDtypeStruct((N_EXPERTS, SLOTS_CTX, D_MODEL), BF16),
        compiler_params=_params(("arbitrary",)),
        name="gather_ctx",
    )(idx_row, h2)


def _gather_lat_kernel(idx_ref, h_ref, x_ref):
    col = _row_to_col(idx_ref[0, pl.ds(pl.program_id(1), 1), :])
    hit = lax.broadcasted_iota(I32, (CAP_LAT, DEC_SEQ), 1) == col
    x_ref[0] = _mm(jnp.where(hit, 1.0, 0.0).astype(BF16), h_ref[0]).astype(BF16)


def _gather_lat(idx, h2):
    return pl.pallas_call(
        _gather_lat_kernel,
        grid=(DEC_BATCH, N_EXPERTS),
        in_specs=[pl.BlockSpec((1, N_EXPERTS, CAP_LAT), lambda s, e: (s, 0, 0)),
                  pl.BlockSpec((1, DEC_SEQ, D_MODEL), lambda s, e: (s, 0, 0))],
        out_specs=pl.BlockSpec((1, CAP_LAT, D_MODEL), lambda s, e: (e, s, 0)),
        out_shape=jax.ShapeDtypeStruct((N_EXPERTS, SLOTS_LAT, D_MODEL), BF16),
        compiler_params=_params(("arbitrary", "arbitrary")),
        name="gather_lat",
    )(idx, h2)


def _ffn_kernel(xc_ref, xl_ref, gc_ref, gl_ref, wg_ref, wu_ref, wd_ref, yc_ref, yl_ref, h_scr, wl_scr):
    j = pl.program_id(1)
    tail = FF_HALF - FF_HALF // FF_TILE * FF_TILE

    def swiglu_cols(wg, wu, c0):
        wgb = wg.astype(BF16)
        wub = wu.astype(BF16)
        for i, x_ref in enumerate((xc_ref, xl_ref)):
            x = x_ref[0]
            a = _mm(x, wgb)
            u = _mm(x, wub)
            h = a * (1.0 / (1.0 + jnp.exp(-a))) * u
            h_scr[SLOTS_CTX * i:SLOTS_CTX * (i + 1), c0:c0 + FF_TILE] = h.astype(BF16)

    @pl.when(j == 0)
    def _():
        for c in range(FF_HALF // FF_TILE):
            lo = FF_TILE * c
            swiglu_cols(wg_ref[0, :, lo:lo + FF_TILE], wu_ref[0, :, lo:lo + FF_TILE], lo)
        wl_scr[0] = wg_ref[0, :, FF_HALF - tail:FF_HALF]
        wl_scr[1] = wu_ref[0, :, FF_HALF - tail:FF_HALF]

    @pl.when(j == 1)
    def _():
        head = FF_TILE - tail
        swiglu_cols(jnp.concatenate([wl_scr[0], wg_ref[0, :, 0:head]], axis=1),
                    jnp.concatenate([wl_scr[1], wu_ref[0, :, 0:head]], axis=1), FF_HALF - tail)
        for c in range(FF_HALF // FF_TILE):
            lo = head + FF_TILE * c
            swiglu_cols(wg_ref[0, :, lo:lo + FF_TILE], wu_ref[0, :, lo:lo + FF_TILE], FF_HALF + lo)

    @pl.when(j >= 2)
    def _():
        wdb = wd_ref[0].astype(BF16)
        e = pl.program_id(0)
        for i, (g_ref, y_ref) in enumerate(((gc_ref, yc_ref), (gl_ref, yl_ref))):
            nset = g_ref.shape[0] // N_EXPERTS
            gate = jnp.concatenate(
                [_row_to_col(g_ref[pl.ds(N_EXPERTS * s + e, 1), :][:, :SLOTS_CTX // nset]) for s in range(nset)],
                axis=0)
            y = _mm(h_scr[SLOTS_CTX * i:SLOTS_CTX * (i + 1), :], wdb)
            y_ref[0] = (y * gate).astype(BF16)


def _expert_ffn(xc, xl, gc, gl, w_gate, w_up, w_down):
    assert SLOTS_CTX == SLOTS_LAT and FF_HALF % LANES == 0 and D_MODEL % OUT_TILE == 0
    xs = lambda n: pl.BlockSpec((1, n, D_MODEL), lambda e, j: (e, 0, 0))
    gs = lambda g: pl.BlockSpec(g.shape, lambda e, j: (0, 0))
    last = N_EXPERTS - 1
    up = pl.BlockSpec((1, D_MODEL, FF_HALF),
                      lambda e, j: (jnp.where(j >= 2, jnp.minimum(e + 1, last), e), 0, jnp.where(j == 1, 1, 0)))
    down = pl.BlockSpec((1, D_FF, OUT_TILE), lambda e, j: (e, 0, jnp.maximum(j - 2, 0)))
    ys = lambda n: pl.BlockSpec((1, n, OUT_TILE), lambda e, j: (e, 0, jnp.maximum(j - 2, 0)))
    return pl.pallas_call(
        _ffn_kernel,
        grid=(N_EXPERTS, 2 + D_MODEL // OUT_TILE),
        in_specs=[xs(SLOTS_CTX), xs(SLOTS_LAT), gs(gc), gs(gl), up, up, down],
        out_specs=[ys(SLOTS_CTX), ys(SLOTS_LAT)],
        out_shape=[jax.ShapeDtypeStruct((N_EXPERTS, SLOTS_CTX, D_MODEL), BF16),
                   jax.ShapeDtypeStruct((N_EXPERTS, SLOTS_LAT, D_MODEL), BF16)],
        scratch_shapes=[pltpu.VMEM((SLOTS_CTX + SLOTS_LAT, D_FF), BF16),
                        pltpu.VMEM((2, D_MODEL, LANES), F32)],
        compiler_params=_params(("arbitrary", "arbitrary"), FFN_VMEM_LIMIT),
        name="expert_ffn",
    )(xc, xl, gc, gl, w_gate, w_up, w_down)


def _combine_ctx_kernel(y_ref, idx_ref, x1_ref, mod_ref, lng_ref, lnb_ref, o_ref):
    g2 = _mod_rows(mod_ref, 0)[5]
    nslot = N_EXPERTS * CAP_CTX
    hit = lax.broadcasted_iota(I32, (SEQ, nslot), 0) == idx_ref[0]
    y = y_ref[...].reshape(nslot, D_MODEL)
    f = _mm(jnp.where(hit, 1.0, 0.0).astype(BF16), y)
    o_ref[0] = _layer_norm(DEEPNORM_ALPHA * x1_ref[0] + g2 * f, lng_ref[...], lnb_ref[...])


def _combine_ctx(yc, idx_row, x1, mod, ln_g, ln_b):
    full = lambda shape: pl.BlockSpec(shape, lambda s: (0,) * len(shape))
    return pl.pallas_call(
        _combine_ctx_kernel,
        grid=(BATCH,),
        in_specs=[pl.BlockSpec((N_EXPERTS, CAP_CTX, D_MODEL), lambda s: (0, s, 0)),
                  pl.BlockSpec((1, 1, N_EXPERTS * CAP_CTX), lambda s: (s, 0, 0)),
                  pl.BlockSpec((1, SEQ, D_MODEL), lambda s: (s, 0, 0)),
                  full((8, 6 * D_MODEL)), full((1, D_MODEL)), full((1, D_MODEL))],
        out_specs=pl.BlockSpec((1, SEQ, D_MODEL), lambda s: (s, 0, 0)),
        out_shape=jax.ShapeDtypeStruct((BATCH, SEQ, D_MODEL), F32),
        compiler_params=_params(("arbitrary",)),
        name="combine_ctx",
    )(yc, idx_row, x1, mod, ln_g, ln_b)


def _combine_lat_kernel(y_ref, idx_ref, x1_ref, mod_ref, lng_ref, lnb_ref, o_ref):
    s = pl.program_id(0)
    g2 = _mod_rows(mod_ref, 1 + s)[5]
    tok = lax.broadcasted_iota(I32, (ROW_TILE, CAP_LAT), 0) + ROW_TILE * pl.program_id(1)
    f = jnp.zeros((ROW_TILE, D_MODEL), F32)
    for e in range(N_EXPERTS):
        hit = tok == idx_ref[0, e:e + 1, :]
        f = f + _mm(jnp.where(hit, 1.0, 0.0).astype(BF16), y_ref[e])
    o_ref[0] = _layer_norm(DEEPNORM_ALPHA * x1_ref[0] + g2 * f, lng_ref[...], lnb_ref[...])


def _combine_lat(yl, idx, x1, mod, ln_g, ln_b):
    nt = DEC_SEQ // ROW_TILE
    full = lambda shape: pl.BlockSpec(shape, lambda s, t: (0,) * len(shape))
    return pl.pallas_call(
        _combine_lat_kernel,
        grid=(DEC_BATCH, nt),
        in_specs=[pl.BlockSpec((N_EXPERTS, CAP_LAT, D_MODEL), lambda s, t: (0, s, 0)),
                  pl.BlockSpec((1, N_EXPERTS, CAP_LAT), lambda s, t: (s, 0, 0)),
                  pl.BlockSpec((1, ROW_TILE, D_MODEL), lambda s, t: (s, t, 0)),
                  full((8, 6 * D_MODEL)), full((1, D_MODEL)), full((1, D_MODEL))],
        out_specs=pl.BlockSpec((1, ROW_TILE, D_MODEL), lambda s, t: (s, t, 0)),
        out_shape=jax.ShapeDtypeStruct((DEC_BATCH, DEC_SEQ, D_MODEL), F32),
        compiler_params=_params(("arbitrary", "arbitrary")),
        name="combine_lat",
    )(yl, idx, x1, mod, ln_g, ln_b)


def _rope_tables():
    t = np.arange(DEC_SEQ)
    row = (t // GRID_W).astype(np.float64)
    col = (t % GRID_W).astype(np.float64)
    half = HEAD_DIM // 2
    inv = ROPE_THETA ** (-np.arange(0, half, 2, dtype=np.float64) / half)
    ang_r = row[:, None] * inv[None, :]
    ang_c = col[:, None] * inv[None, :]
    ang = np.concatenate([ang_r, ang_r, ang_c, ang_c], axis=-1)
    cos = np.tile(np.cos(ang), (1, 2))
    sin = np.tile(np.sin(ang), (1, 2))
    first = (np.arange(LANES) % 32) < 16
    return jnp.asarray(cos, F32), jnp.asarray(np.where(first[None, :], -sin, sin), F32)


def kernel(x_prompt, x_sample, cache_na_k, cache_na_v, cache_diff_k, cache_diff_v, c, c_ctx, w_ada, b_ada, w_in, w_out, na_rel_bias, lambda_q1, lambda_k1, lambda_q2, lambda_k2, subln_g, ln1_g, ln1_b, ln2_g, ln2_b, w_router, w_gate, w_up, w_down):
    l = 0
    cvec = jnp.concatenate([c_ctx[None, :], c, jnp.zeros((8 - 1 - DEC_BATCH, D_MODEL), F32)], axis=0)
    mod = _modulation(cvec, w_ada[l], b_ada[l][None, :])
    lamv = jnp.stack([lambda_q1[l], lambda_k1[l], lambda_q2[l], lambda_k2[l]], axis=0)
    w_in_b = w_in[l].astype(BF16)
    w_out_b = w_out[l].astype(BF16)
    sub = subln_g[l][None, :]
    wrt = w_router[l].T

    tr = lambda a: jnp.swapaxes(a, -1, -2)

    x1c, h2c, affc, na_k_t, na_v_t, diff_k_t, new_diff_v = _context_block(
        x_prompt, mod, lamv, w_in_b, w_out_b, sub, ln1_g[l][None, :], ln1_b[l][None, :], wrt)

    cos, sin = _rope_tables()
    naq, nak, nav, dfq, dfk, dfv = _latent_qkv(x_sample, mod, w_in_b, cos, sin)
    tbl = _na_bias_tables(na_rel_bias[l].reshape(-1))
    na_o = _latent_na(naq, nak, nav, tr(cache_na_k), tr(cache_na_v), tbl)
    df_o = _latent_diff(dfq, dfk, dfv, tr(cache_diff_k), cache_diff_v, lamv, sub)
    x1l, h2l, affl = _latent_out(x_sample, na_o, df_o, mod, w_out_b,
                                 ln1_g[l][None, :], ln1_b[l][None, :], wrt)

    idxc, gatec = _route(affc.reshape(BATCH * N_EXPERTS, SEQ), CAP_CTX)
    idxl, gatel = _route(affl.reshape(DEC_BATCH * N_EXPERTS, DEC_SEQ), CAP_LAT)
    idxc = idxc.reshape(BATCH, 1, N_EXPERTS * CAP_CTX)
    idxl = idxl.reshape(DEC_BATCH, N_EXPERTS, CAP_LAT)
    xc = _gather_ctx(idxc, h2c)
    xl = _gather_lat(idxl, h2l)
    yc, yl = _expert_ffn(xc, xl, gatec, gatel, w_gate[l], w_up[l], w_down[l])
    y_prompt = _combine_ctx(yc, idxc, x1c, mod, ln2_g[l][None, :], ln2_b[l][None, :])
    y_sample = _combine_lat(yl, idxl, x1l, mod, ln2_g[l][None, :], ln2_b[l][None, :])
    return (y_prompt, y_sample, tr(na_k_t), tr(na_v_t), tr(diff_k_t), new_diff_v)
```

```python
import functools
import math

import jax
import jax.numpy as jnp
import numpy as np
from jax import lax
from jax.experimental import pallas as pl
from jax.experimental.pallas import tpu as pltpu

F32 = jnp.float32
BF16 = jnp.bfloat16
I32 = jnp.int32

D_MODEL = 1024
BATCH = 16
SEQ = 256
DEC_BATCH = 2
DEC_SEQ = 2048
PAST_LEN = 256
GRID_W = 64
GRID_ROWS = DEC_SEQ // GRID_W
HEAD_DIM = 64
NA_HEADS = 8
DIFF_HEADS = 4
NA_WIDTH = NA_HEADS * HEAD_DIM
DIFF_WIDTH = DIFF_HEADS * 2 * HEAD_DIM
QKV_WIDTH = 3 * NA_WIDTH + 3 * DIFF_WIDTH
NA_WIN_H = 8
NA_WIN_W = 16
N_REL_H = 2 * NA_WIN_H - 1
N_REL_W = 2 * NA_WIN_W - 1
N_EXPERTS = 16
EC_CAPACITY_FACTOR = 2
D_FF = 2816
ROPE_THETA = 10000.0
NORM_EPS = 1e-5
NEG_INF = -1e30
DEPTH = 1
DEEPNORM_ALPHA = (2.0 * DEPTH) ** 0.25
LAMBDA_INIT = 0.8 - 0.6 * math.exp(-0.3 * 0)
SCALE = HEAD_DIM ** -0.5

OFF_NAQ = 0
OFF_NAK = NA_WIDTH
OFF_NAV = 2 * NA_WIDTH
OFF_DFQ = 3 * NA_WIDTH
OFF_DFK = 3 * NA_WIDTH + DIFF_WIDTH
OFF_DFV = 3 * NA_WIDTH + 2 * DIFF_WIDTH

CAP_CTX = EC_CAPACITY_FACTOR * SEQ // N_EXPERTS
CAP_LAT = EC_CAPACITY_FACTOR * DEC_SEQ // N_EXPERTS
SLOTS_CTX = BATCH * CAP_CTX
SLOTS_LAT = DEC_BATCH * CAP_LAT

ROW_TILE = 512
NA_QUAD = 4
NA_WIN_ROWS = 12
NA_GROUP = 8
FF_TILE = 256
FF_HALF = D_FF // 2
OUT_TILE = 512
LANES = 128
GATHER_GROUP = 2
DIFF_ROWS = 1024
Q_ROWS = 256

VMEM_LIMIT = 48 * 1024 * 1024
FFN_VMEM_LIMIT = 58 * 1024 * 1024


def _params(sem, vmem=VMEM_LIMIT, flags=None):
    return pltpu.CompilerParams(dimension_semantics=sem, vmem_limit_bytes=vmem, flags=flags)


def _mm(a, b):
    return jnp.dot(a, b, preferred_element_type=F32)


def _nt(a, b):
    return lax.dot_general(a, b, (((1,), (1,)), ((), ())), preferred_element_type=F32)


def _split(a):
    hi = a.astype(BF16)
    lo = (a - hi.astype(F32)).astype(BF16)
    return hi, lo


def _mm3(a, b):
    ah, al = _split(a)
    bh, bl = _split(b)
    return _mm(ah, bh) + _mm(al, bh) + _mm(ah, bl)


def _nt3(a, b):
    ah, al = _split(a)
    bh, bl = _split(b)
    return _nt(ah, bh) + _nt(al, bh) + _nt(ah, bl)


def _layer_norm(y, g, b):
    mu = jnp.mean(y, axis=-1, keepdims=True)
    d = y - mu
    var = jnp.mean(d * d, axis=-1, keepdims=True)
    return d * lax.rsqrt(var + NORM_EPS) * g + b


def _lam(l_ref):
    l = l_ref[...]
    a = jnp.sum(l[0:1] * l[1:2], axis=-1, keepdims=True)
    b = jnp.sum(l[2:3] * l[3:4], axis=-1, keepdims=True)
    return jnp.exp(a) - jnp.exp(b) + LAMBDA_INIT


def _mod_rows(mod_ref, row):
    return [mod_ref[pl.ds(row, 1), i * D_MODEL:(i + 1) * D_MODEL] for i in range(6)]


def _sub_norm(o, sub):
    ms = jnp.mean(o * o, axis=-1, keepdims=True)
    return o * lax.rsqrt(ms + NORM_EPS) * sub * (1.0 - LAMBDA_INIT)


def _router_aff_t(h2, wrt):
    lg = _nt3(wrt, h2)
    m = jnp.max(lg, axis=0, keepdims=True)
    e = jnp.exp(lg - m)
    return e / jnp.sum(e, axis=0, keepdims=True)


def _mod_kernel(c_ref, w_ref, b_ref, o_ref):
    c = c_ref[...]
    s = c * (1.0 / (1.0 + jnp.exp(-c)))
    o_ref[...] = _mm3(s, w_ref[...]) + b_ref[...]


def _modulation(cvec, w_ada, b_ada):
    ncol = 6 * D_MODEL
    return pl.pallas_call(
        _mod_kernel,
        grid=(6,),
        in_specs=[pl.BlockSpec((8, D_MODEL), lambda j: (0, 0)),
                  pl.BlockSpec((D_MODEL, D_MODEL), lambda j: (0, j)),
                  pl.BlockSpec((1, D_MODEL), lambda j: (0, j))],
        out_specs=pl.BlockSpec((8, D_MODEL), lambda j: (0, j)),
        out_shape=jax.ShapeDtypeStruct((8, ncol), F32),
        compiler_params=_params(("arbitrary",)),
        name="modulation",
    )(cvec, w_ada, b_ada)


def _ctx_kernel(x_ref, mod_ref, lam_ref, win_ref, wout_ref, sub_ref, lng_ref, lnb_ref, wrt_ref,
                x1_ref, h2_ref, aff_ref, nak_ref, nav_ref, dfk_ref, dfv_ref,
                qkv_scr, o_scr, s_scr, e_scr):
    x = x_ref[0]
    sh1, sc1, g1, sh2, sc2, g2 = _mod_rows(mod_ref, 0)
    h = (x * (1.0 + sc1) + sh1).astype(BF16)
    qkv_scr[...] = _mm(h, win_ref[...])

    kt_na = qkv_scr[:, OFF_NAK:OFF_NAK + NA_WIDTH].T
    kt_df = qkv_scr[:, OFF_DFK:OFF_DFK + DIFF_WIDTH].T
    nak_ref[0, 0] = kt_na.reshape(NA_HEADS, HEAD_DIM, SEQ)
    nav_ref[0, 0] = qkv_scr[:, OFF_NAV:OFF_NAV + NA_WIDTH].T.reshape(NA_HEADS, HEAD_DIM, SEQ)
    dfk_ref[0, 0] = kt_df.reshape(DIFF_HEADS, 2, HEAD_DIM, SEQ)
    for hh in range(DIFF_HEADS):
        c0 = OFF_DFV + 2 * HEAD_DIM * hh
        dfv_ref[0, 0, hh] = qkv_scr[:, c0:c0 + 2 * HEAD_DIM]

    kt = jnp.concatenate([kt_na, kt_df], axis=0).astype(BF16)
    q_cols = [OFF_NAQ + HEAD_DIM * i for i in range(NA_HEADS)] + [
        OFF_DFQ + HEAD_DIM * i for i in range(2 * DIFF_HEADS)]
    nmap = len(q_cols)
    for i, c in enumerate(q_cols):
        q = (qkv_scr[:, c:c + HEAD_DIM] * SCALE).astype(BF16)
        s_scr[i] = _mm(q, kt[HEAD_DIM * i:HEAD_DIM * (i + 1), :])
    s = s_scr[...]
    e = jnp.exp(s - jnp.max(s, axis=-1, keepdims=True))
    rl = 1.0 / jnp.sum(e, axis=-1, keepdims=True)
    e_scr[...] = e.astype(BF16)

    for hh in range(NA_HEADS):
        v = qkv_scr[:, OFF_NAV + HEAD_DIM * hh:OFF_NAV + HEAD_DIM * (hh + 1)].astype(BF16)
        o_scr[:, HEAD_DIM * hh:HEAD_DIM * (hh + 1)] = (_mm(e_scr[hh], v) * rl[hh]).astype(BF16)
    lam = _lam(lam_ref)
    sub = sub_ref[...]
    for hh in range(DIFF_HEADS):
        i1 = NA_HEADS + 2 * hh
        c0 = OFF_DFV + 2 * HEAD_DIM * hh
        v = qkv_scr[:, c0:c0 + 2 * HEAD_DIM].astype(BF16)
        o = _mm(e_scr[i1], v) * rl[i1] - _mm(e_scr[i1 + 1], v) * (lam * rl[i1 + 1])
        c1 = NA_WIDTH + 2 * HEAD_DIM * hh
        o_scr[:, c1:c1 + 2 * HEAD_DIM] = _sub_norm(o, sub).astype(BF16)

    a = _mm(o_scr[...], wout_ref[...])
    x1 = _layer_norm(DEEPNORM_ALPHA * x + g1 * a, lng_ref[...], lnb_ref[...])
    x1_ref[0] = x1
    h2 = x1 * (1.0 + sc2) + sh2

    h2_ref[0] = h2.astype(BF16)
    aff_ref[0] = _router_aff_t(h2, wrt_ref[...])


def _context_block(x, mod, lamv, w_in, w_out, sub, ln_g, ln_b, wrt):
    full = lambda shape: pl.BlockSpec(shape, lambda b: (0,) * len(shape))
    return pl.pallas_call(
        _ctx_kernel,
        grid=(BATCH,),
        in_specs=[pl.BlockSpec((1, SEQ, D_MODEL), lambda b: (b, 0, 0)),
                  full((8, 6 * D_MODEL)), full((4, HEAD_DIM)),
                  full((D_MODEL, QKV_WIDTH)), full((D_MODEL, D_MODEL)),
                  full((1, 2 * HEAD_DIM)), full((1, D_MODEL)), full((1, D_MODEL)),
                  full((N_EXPERTS, D_MODEL))],
        out_specs=[pl.BlockSpec((1, SEQ, D_MODEL), lambda b: (b, 0, 0)),
                   pl.BlockSpec((1, SEQ, D_MODEL), lambda b: (b, 0, 0)),
                   pl.BlockSpec((1, N_EXPERTS, SEQ), lambda b: (b, 0, 0)),
                   pl.BlockSpec((1, 1, NA_HEADS, HEAD_DIM, SEQ), lambda b: (b, 0, 0, 0, 0)),
                   pl.BlockSpec((1, 1, NA_HEADS, HEAD_DIM, SEQ), lambda b: (b, 0, 0, 0, 0)),
                   pl.BlockSpec((1, 1, DIFF_HEADS, 2, HEAD_DIM, SEQ), lambda b: (b, 0, 0, 0, 0, 0)),
                   pl.BlockSpec((1, 1, DIFF_HEADS, SEQ, 2 * HEAD_DIM), lambda b: (b, 0, 0, 0, 0))],
        out_shape=[jax.ShapeDtypeStruct((BATCH, SEQ, D_MODEL), F32),
                   jax.ShapeDtypeStruct((BATCH, SEQ, D_MODEL), BF16),
                   jax.ShapeDtypeStruct((BATCH, N_EXPERTS, SEQ), F32),
                   jax.ShapeDtypeStruct((BATCH, 1, NA_HEADS, HEAD_DIM, SEQ), F32),
                   jax.ShapeDtypeStruct((BATCH, 1, NA_HEADS, HEAD_DIM, SEQ), F32),
                   jax.ShapeDtypeStruct((BATCH, 1, DIFF_HEADS, 2, HEAD_DIM, SEQ), F32),
                   jax.ShapeDtypeStruct((BATCH, 1, DIFF_HEADS, SEQ, 2 * HEAD_DIM), F32)],
        scratch_shapes=[pltpu.VMEM((SEQ, QKV_WIDTH), F32), pltpu.VMEM((SEQ, D_MODEL), BF16),
                        pltpu.VMEM((NA_HEADS + 2 * DIFF_HEADS, SEQ, SEQ), F32),
                        pltpu.VMEM((NA_HEADS + 2 * DIFF_HEADS, SEQ, SEQ), BF16)],
        compiler_params=_params(("arbitrary",)),
        name="context_block",
    )(x, mod, lamv, w_in, w_out, sub, ln_g, ln_b, wrt)


def _lat_qkv_kernel(x_ref, mod_ref, win_ref, cos_ref, sin_ref,
                    naq_ref, nak_ref, nav_ref, dfq_ref, dfk_ref, dfv_ref, qkv_scr):
    b = pl.program_id(0)
    x = x_ref[0]
    sh1, sc1 = _mod_rows(mod_ref, 1 + b)[:2]
    h = (x * (1.0 + sc1) + sh1).astype(BF16)
    qkv_scr[...] = _mm(h, win_ref[...])

    for hh in range(NA_HEADS):
        lo, hi = HEAD_DIM * hh, HEAD_DIM * (hh + 1)
        naq_ref[0, hh] = (qkv_scr[:, OFF_NAQ + lo:OFF_NAQ + hi] * SCALE).astype(BF16)
        nak_ref[0, hh] = qkv_scr[:, OFF_NAK + lo:OFF_NAK + hi].astype(BF16)
        nav_ref[0, hh] = qkv_scr[:, OFF_NAV + lo:OFF_NAV + hi].astype(BF16)

    cos = cos_ref[...]
    sin = sin_ref[...]
    lane = lax.broadcasted_iota(I32, (ROW_TILE, LANES), 1)
    first = (lane & 31) < 16

    def rope(t):
        rot = jnp.where(first, pltpu.roll(t, LANES - 16, 1), pltpu.roll(t, 16, 1))
        return t * cos + rot * sin

    for hh in range(DIFF_HEADS):
        lo = 2 * HEAD_DIM * hh
        q = rope(qkv_scr[:, OFF_DFQ + lo:OFF_DFQ + lo + LANES]) * SCALE
        k = rope(qkv_scr[:, OFF_DFK + lo:OFF_DFK + lo + LANES])
        for m in range(2):
            dfq_ref[0, hh, m] = q[:, HEAD_DIM * m:HEAD_DIM * (m + 1)].astype(BF16)
        dfk_ref[0, hh] = k.T.astype(BF16).reshape(2, HEAD_DIM, ROW_TILE)
        dfv_ref[0, hh] = qkv_scr[:, OFF_DFV + lo:OFF_DFV + lo + LANES].astype(BF16)


def _latent_qkv(x, mod, w_in, cos, sin):
    nt = DEC_SEQ // ROW_TILE
    full = lambda shape: pl.BlockSpec(shape, lambda b, t: (0,) * len(shape))
    hs = lambda nh: pl.BlockSpec((1, nh, ROW_TILE, HEAD_DIM), lambda b, t: (b, 0, t, 0))
    return pl.pallas_call(
        _lat_qkv_kernel,
        grid=(DEC_BATCH, nt),
        in_specs=[pl.BlockSpec((1, ROW_TILE, D_MODEL), lambda b, t: (b, t, 0)),
                  full((8, 6 * D_MODEL)), full((D_MODEL, QKV_WIDTH)),
                  pl.BlockSpec((ROW_TILE, LANES), lambda b, t: (t, 0)),
                  pl.BlockSpec((ROW_TILE, LANES), lambda b, t: (t, 0))],
        out_specs=[hs(NA_HEADS), hs(NA_HEADS), hs(NA_HEADS),
                   pl.BlockSpec((1, DIFF_HEADS, 2, ROW_TILE, HEAD_DIM), lambda b, t: (b, 0, 0, t, 0)),
                   pl.BlockSpec((1, DIFF_HEADS, 2, HEAD_DIM, ROW_TILE), lambda b, t: (b, 0, 0, 0, t)),
                   pl.BlockSpec((1, DIFF_HEADS, ROW_TILE, 2 * HEAD_DIM), lambda b, t: (b, 0, t, 0))],
        out_shape=[jax.ShapeDtypeStruct((DEC_BATCH, NA_HEADS, DEC_SEQ, HEAD_DIM), BF16)] * 3
        + [jax.ShapeDtypeStruct((DEC_BATCH, DIFF_HEADS, 2, DEC_SEQ, HEAD_DIM), BF16),
           jax.ShapeDtypeStruct((DEC_BATCH, DIFF_HEADS, 2, HEAD_DIM, DEC_SEQ), BF16),
           jax.ShapeDtypeStruct((DEC_BATCH, DIFF_HEADS, DEC_SEQ, 2 * HEAD_DIM), BF16)],
        scratch_shapes=[pltpu.VMEM((ROW_TILE, QKV_WIDTH), F32)],
        compiler_params=_params(("arbitrary", "arbitrary")),
        name="latent_qkv",
    )(x, mod, w_in, cos, sin)


def _na_row_offset(kind, qr, kr):
    if kind == 0:
        return kr - qr + 7 if kr < NA_WIN_H else None
    if kind == 1:
        return kr - qr + 3 if qr <= kr < qr + NA_WIN_H else None
    return kr - qr - 1 if NA_WIN_ROWS - NA_WIN_H <= kr else None


def _bias_kernel(rpb_ref, o_ref):
    h = pl.program_id(0)
    base = h * (N_REL_H * N_REL_W)
    wq = lax.broadcasted_iota(I32, (GRID_W, LANES), 0)
    c = lax.broadcasted_iota(I32, (GRID_W, LANES), 1)
    wk = c & (GRID_W - 1)
    right = c >= GRID_W
    c0 = jnp.clip(wq - NA_WIN_W // 2, 0, GRID_W - NA_WIN_W)
    in_win = (wk >= c0) & (wk < c0 + NA_WIN_W)
    cache = {}
    strips = {}

    def strip(r):
        if r not in strips:
            c8 = lax.broadcasted_iota(I32, (8, LANES), 1)
            d = jnp.where(c8 < GRID_W, c8, c8 - LANES)
            co_of = jnp.clip(d, -(NA_WIN_W - 1), NA_WIN_W - 1) + (NA_WIN_W - 1)
            acc = jnp.zeros((8, LANES), F32)
            for co in range(N_REL_W):
                acc = jnp.where(co_of == co, rpb_ref[base + r * N_REL_W + co], acc)
            strips[r] = jnp.concatenate([acc] * (GRID_W // 8), axis=0)
        return strips[r]

    def tile(rl, rr):
        if (rl, rr) in cache:
            return cache[(rl, rr)]
        if rl is None and rr is None:
            t = jnp.full((GRID_W, LANES), NEG_INF, F32)
        else:
            zero = jnp.zeros((GRID_W, LANES), F32)
            tl = pltpu.roll(strip(rl), 0, 1, stride=1, stride_axis=0) if rl is not None else zero
            tr = pltpu.roll(strip(rr), GRID_W, 1, stride=1, stride_axis=0) if rr is not None else zero
            acc = jnp.where(right, tr, tl)
            ok = in_win
            if rl is None:
                ok = ok & right
            if rr is None:
                ok = ok & jnp.logical_not(right)
            t = jnp.where(ok, acc, NEG_INF)
        cache[(rl, rr)] = t
        return t

    for kind in range(3):
        for qr in range(NA_QUAD):
            row = [tile(_na_row_offset(kind, qr, 2 * p), _na_row_offset(kind, qr, 2 * p + 1))
                   for p in range(NA_WIN_ROWS // 2)]
            o_ref[0, kind, GRID_W * qr:GRID_W * (qr + 1), :] = jnp.concatenate(row, axis=1)


def _na_bias_tables(rpb_flat):
    return pl.pallas_call(
        _bias_kernel,
        grid=(NA_HEADS,),
        in_specs=[pl.BlockSpec(memory_space=pltpu.SMEM)],
        out_specs=pl.BlockSpec((1, 3, NA_QUAD * GRID_W, NA_WIN_ROWS * GRID_W), lambda h: (h, 0, 0, 0)),
        out_shape=jax.ShapeDtypeStruct((NA_HEADS, 3, NA_QUAD * GRID_W, NA_WIN_ROWS * GRID_W), F32),
        compiler_params=_params(("arbitrary",)),
        name="na_bias_tables",
    )(rpb_flat)


def _na_kernel(q_ref, k_ref, v_ref, ck_ref, cv_ref, tbl_ref, o_ref):
    g = pl.program_id(2)
    u0 = jnp.clip(NA_QUAD * g - NA_QUAD, 0, GRID_ROWS - NA_WIN_ROWS)
    start = pl.multiple_of(u0 * GRID_W, GRID_W)
    nwin = NA_WIN_ROWS * GRID_W
    for j in range(NA_GROUP):
        q = q_ref[0, j]
        kw = k_ref[0, j, pl.ds(start, nwin), :]
        vw = v_ref[0, j, pl.ds(start, nwin), :]
        kct = ck_ref[0, 0, j].astype(BF16)
        vct = cv_ref[0, 0, j].astype(BF16)
        sl = _nt(q, kw) + tbl_ref[j, 0]
        sc = _mm(q, kct)
        m = jnp.maximum(jnp.max(sl, axis=-1, keepdims=True), jnp.max(sc, axis=-1, keepdims=True))
        el = jnp.exp(sl - m)
        ec = jnp.exp(sc - m)
        l = jnp.sum(el, axis=-1, keepdims=True) + jnp.sum(ec, axis=-1, keepdims=True)
        o = (_mm(el.astype(BF16), vw) + _nt(ec.astype(BF16), vct)) / l
        o_ref[0, :, HEAD_DIM * j:HEAD_DIM * (j + 1)] = o.astype(BF16)


def _latent_na(naq, nak, nav, cache_k, cache_v, tbl):
    nq = GRID_ROWS // NA_QUAD
    rows = NA_QUAD * GRID_W
    kind = lambda g: jnp.minimum(g, 1) + g // (nq - 1)
    return pl.pallas_call(
        _na_kernel,
        grid=(DEC_BATCH, NA_HEADS // NA_GROUP, nq),
        in_specs=[pl.BlockSpec((1, NA_GROUP, rows, HEAD_DIM), lambda b, p, g: (b, p, g, 0)),
                  pl.BlockSpec((1, NA_GROUP, DEC_SEQ, HEAD_DIM), lambda b, p, g: (b, p, 0, 0)),
                  pl.BlockSpec((1, NA_GROUP, DEC_SEQ, HEAD_DIM), lambda b, p, g: (b, p, 0, 0)),
                  pl.BlockSpec((1, 1, NA_GROUP, HEAD_DIM, PAST_LEN), lambda b, p, g: (b, 0, p, 0, 0)),
                  pl.BlockSpec((1, 1, NA_GROUP, HEAD_DIM, PAST_LEN), lambda b, p, g: (b, 0, p, 0, 0)),
                  pl.BlockSpec((NA_GROUP, 1, rows, NA_WIN_ROWS * GRID_W), lambda b, p, g: (p, kind(g), 0, 0))],
        out_specs=pl.BlockSpec((1, rows, NA_GROUP * HEAD_DIM), lambda b, p, g: (b, g, p)),
        out_shape=jax.ShapeDtypeStruct((DEC_BATCH, DEC_SEQ, NA_WIDTH), BF16),
        compiler_params=_params(("arbitrary",) * 3),
        name="latent_na",
    )(naq, nak, nav, cache_k, cache_v, tbl)


def _diff_kernel(q_ref, kt_ref, v_ref, ck_ref, cv_ref, lam_ref, sub_ref, o_ref):
    lam = _lam(lam_ref)
    vc = cv_ref[0, 0, 0].astype(BF16)
    vl = v_ref[0, 0]
    for r in range(DIFF_ROWS // Q_ROWS):
        rows = slice(Q_ROWS * r, Q_ROWS * (r + 1))
        parts = []
        for m in range(2):
            q = q_ref[0, 0, m, rows, :]
            sc = _mm(q, ck_ref[0, 0, 0, m].astype(BF16))
            sl = _mm(q, kt_ref[0, 0, m])
            mx = jnp.maximum(jnp.max(sc, axis=-1, keepdims=True), jnp.max(sl, axis=-1, keepdims=True))
            ec = jnp.exp(sc - mx)
            el = jnp.exp(sl - mx)
            l = jnp.sum(ec, axis=-1, keepdims=True) + jnp.sum(el, axis=-1, keepdims=True)
            parts.append((_mm(ec.astype(BF16), vc) + _mm(el.astype(BF16), vl), l))
        (a1, l1), (a2, l2) = parts
        o = a1 * (1.0 / l1) - a2 * (lam / l2)
        o_ref[0, rows, :] = _sub_norm(o, sub_ref[...]).astype(BF16)


def _latent_diff(dfq, dfk, dfv, cache_k, cache_v, lamv, sub):
    nt = DEC_SEQ // DIFF_ROWS
    full = lambda shape: pl.BlockSpec(shape, lambda b, h, t: (0,) * len(shape))
    return pl.pallas_call(
        _diff_kernel,
        grid=(DEC_BATCH, DIFF_HEADS, nt),
        in_specs=[pl.BlockSpec((1, 1, 2, DIFF_ROWS, HEAD_DIM), lambda b, h, t: (b, h, 0, t, 0)),
                  pl.BlockSpec((1, 1, 2, HEAD_DIM, DEC_SEQ), lambda b, h, t: (b, h, 0, 0, 0)),
                  pl.BlockSpec((1, 1, DEC_SEQ, 2 * HEAD_DIM), lambda b, h, t: (b, h, 0, 0)),
                  pl.BlockSpec((1, 1, 1, 2, HEAD_DIM, PAST_LEN), lambda b, h, t: (b, 0, h, 0, 0, 0)),
                  pl.BlockSpec((1, 1, 1, PAST_LEN, 2 * HEAD_DIM), lambda b, h, t: (b, 0, h, 0, 0)),
                  full((4, HEAD_DIM)), full((1, 2 * HEAD_DIM))],
        out_specs=pl.BlockSpec((1, DIFF_ROWS, 2 * HEAD_DIM), lambda b, h, t: (b, t, h)),
        out_shape=jax.ShapeDtypeStruct((DEC_BATCH, DEC_SEQ, DIFF_WIDTH), BF16),
        compiler_params=_params(("arbitrary",) * 3),
        name="latent_diff",
    )(dfq, dfk, dfv, cache_k, cache_v, lamv, sub)


def _lat_out_kernel(x_ref, na_ref, df_ref, mod_ref, wout_ref, lng_ref, lnb_ref, wrt_ref,
                    x1_ref, h2_ref, aff_ref):
    b = pl.program_id(0)
    x = x_ref[0]
    _, _, g1, sh2, sc2, _ = _mod_rows(mod_ref, 1 + b)
    a = _mm(na_ref[0], wout_ref[0:NA_WIDTH, :]) + _mm(df_ref[0], wout_ref[NA_WIDTH:, :])
    x1 = _layer_norm(DEEPNORM_ALPHA * x + g1 * a, lng_ref[...], lnb_ref[...])
    x1_ref[0] = x1
    h2 = x1 * (1.0 + sc2) + sh2
    h2_ref[0] = h2.astype(BF16)
    aff_ref[0] = _router_aff_t(h2, wrt_ref[...])


def _latent_out(x, na_o, df_o, mod, w_out, ln_g, ln_b, wrt):
    nt = DEC_SEQ // ROW_TILE
    full = lambda shape: pl.BlockSpec(shape, lambda b, t: (0,) * len(shape))
    tok = lambda w: pl.BlockSpec((1, ROW_TILE, w), lambda b, t: (b, t, 0))
    return pl.pallas_call(
        _lat_out_kernel,
        grid=(DEC_BATCH, nt),
        in_specs=[tok(D_MODEL), tok(NA_WIDTH), tok(DIFF_WIDTH), full((8, 6 * D_MODEL)),
                  full((D_MODEL, D_MODEL)), full((1, D_MODEL)), full((1, D_MODEL)),
                  full((N_EXPERTS, D_MODEL))],
        out_specs=[tok(D_MODEL), tok(D_MODEL),
                   pl.BlockSpec((1, N_EXPERTS, ROW_TILE), lambda b, t: (b, 0, t))],
        out_shape=[jax.ShapeDtypeStruct((DEC_BATCH, DEC_SEQ, D_MODEL), F32),
                   jax.ShapeDtypeStruct((DEC_BATCH, DEC_SEQ, D_MODEL), BF16),
                   jax.ShapeDtypeStruct((DEC_BATCH, N_EXPERTS, DEC_SEQ), F32)],
        compiler_params=_params(("arbitrary", "arbitrary")),
        name="latent_out",
    )(x, na_o, df_o, mod, w_out, ln_g, ln_b, wrt)


def _ranks_first(va, ia, vb, ib):
    return (va > vb) | ((va == vb) & (ia < ib))


def _sort_rows_desc(aff):
    rows, n = aff.shape
    nb = n // LANES
    lane = lax.broadcasted_iota(I32, (rows, LANES), 1)
    v = [aff[:, LANES * b:LANES * (b + 1)] for b in range(nb)]
    ix = [lane + LANES * b for b in range(nb)]
    k = 2
    while k <= n:
        j = k // 2
        while j >= 1:
            if j >= LANES:
                sb = j // LANES
                for lo in range(nb):
                    if lo & sb:
                        continue
                    hi = lo | sb
                    f = _ranks_first(v[lo], ix[lo], v[hi], ix[hi])
                    if k < n and (LANES * lo) & k:
                        f = jnp.logical_not(f)
                    v[lo], v[hi] = jnp.where(f, v[lo], v[hi]), jnp.where(f, v[hi], v[lo])
                    ix[lo], ix[hi] = jnp.where(f, ix[lo], ix[hi]), jnp.where(f, ix[hi], ix[lo])
            else:
                upper = (lane & j) != 0
                if k < LANES:
                    flip_lanes = jnp.logical_xor(upper, (lane & k) != 0)
                for b in range(nb):
                    pv = jnp.where(upper, pltpu.roll(v[b], j, 1), pltpu.roll(v[b], LANES - j, 1))
                    pi = jnp.where(upper, pltpu.roll(ix[b], j, 1), pltpu.roll(ix[b], LANES - j, 1))
                    f = _ranks_first(v[b], ix[b], pv, pi)
                    if k < LANES:
                        flip = flip_lanes
                    elif k < n and (LANES * b) & k:
                        flip = jnp.logical_not(upper)
                    else:
                        flip = upper
                    keep = jnp.logical_xor(f, flip)
                    v[b] = jnp.where(keep, v[b], pv)
                    ix[b] = jnp.where(keep, ix[b], pi)
            j //= 2
        k *= 2
    return v, ix


def _route_kernel(aff_ref, idx_ref, gate_ref, *, cap, n):
    v, ix = _sort_rows_desc(aff_ref[...])
    if cap < LANES:
        idx_ref[...] = ix[0][:, :cap]
        gate_ref[...] = v[0]
    else:
        idx_ref[...] = jnp.concatenate(ix[:cap // LANES], axis=1)
        gate_ref[...] = jnp.concatenate(v[:cap // LANES], axis=1)


def _route(aff_t, cap):
    rows, n = aff_t.shape
    return pl.pallas_call(
        functools.partial(_route_kernel, cap=cap, n=n),
        out_shape=[jax.ShapeDtypeStruct((rows, cap), I32),
                   jax.ShapeDtypeStruct((rows, max(cap, LANES)), F32)],
        compiler_params=_params(None),
        name=f"route_{n}",
    )(aff_t)


def _row_to_col(row):
    n = row.shape[1]
    eye = lax.broadcasted_iota(I32, (n, n), 0) == lax.broadcasted_iota(I32, (n, n), 1)
    return jnp.sum(jnp.where(eye, row, jnp.zeros_like(row)), axis=1, keepdims=True)


def _gather_ctx_kernel(idx_ref, h_ref, x_ref):
    nslot = N_EXPERTS * CAP_CTX
    hit = lax.broadcasted_iota(I32, (nslot, SEQ), 1) == _row_to_col(idx_ref[0])
    xs = _mm(jnp.where(hit, 1.0, 0.0).astype(BF16), h_ref[0])
    x_ref[...] = xs.astype(BF16).reshape(N_EXPERTS, CAP_CTX, D_MODEL)


def _gather_ctx(idx_row, h2):
    nslot = N_EXPERTS * CAP_CTX
    return pl.pallas_call(
        _gather_ctx_kernel,
        grid=(BATCH,),
        in_specs=[pl.BlockSpec((1, 1, nslot), lambda s: (s, 0, 0)),
                  pl.BlockSpec((1, SEQ, D_MODEL), lambda s: (s, 0, 0))],
        out_specs=pl.BlockSpec((N_EXPERTS, CAP_CTX, D_MODEL), lambda s: (0, s, 0)),
        out_shape=jax.ShapeDtypeStruct((N_EXPERTS, SLOTS_CTX, D_MODEL), BF16),
        compiler_params=_params(("arbitrary",)),
        name="gather_ctx",
    )(idx_row, h2)


def _gather_lat_kernel(idx_ref, h_ref, x_ref):
    first = GATHER_GROUP * pl.program_id(1)
    col = jnp.concatenate([_row_to_col(idx_ref[0, pl.ds(first + i, 1), :]) for i in range(GATHER_GROUP)],
                          axis=0)
    hit = lax.broadcasted_iota(I32, (GATHER_GROUP * CAP_LAT, DEC_SEQ), 1) == col
    xs = _mm(jnp.where(hit, 1.0, 0.0).astype(BF16), h_ref[0])
    x_ref[...] = xs.astype(BF16).reshape(GATHER_GROUP, CAP_LAT, D_MODEL)


def _gather_lat(idx, h2):
    return pl.pallas_call(
        _gather_lat_kernel,
        grid=(DEC_BATCH, N_EXPERTS // GATHER_GROUP),
        in_specs=[pl.BlockSpec((1, N_EXPERTS, CAP_LAT), lambda s, e: (s, 0, 0)),
                  pl.BlockSpec((1, DEC_SEQ, D_MODEL), lambda s, e: (s, 0, 0))],
        out_specs=pl.BlockSpec((GATHER_GROUP, CAP_LAT, D_MODEL), lambda s, e: (e, s, 0)),
        out_shape=jax.ShapeDtypeStruct((N_EXPERTS, SLOTS_LAT, D_MODEL), BF16),
        compiler_params=_params(("arbitrary", "arbitrary")),
        name="gather_lat",
    )(idx, h2)


def _ffn_kernel(xc_ref, xl_ref, gc_ref, gl_ref, wg_ref, wu_ref, wd_ref, yc_ref, yl_ref, h_scr, wl_scr):
    j = pl.program_id(1)
    tail = FF_HALF - FF_HALF // FF_TILE * FF_TILE

    def swiglu_cols(wg, wu, c0):
        wgb = wg.astype(BF16)
        wub = wu.astype(BF16)
        for i, x_ref in enumerate((xc_ref, xl_ref)):
            x = x_ref[0]
            a = _mm(x, wgb)
            u = _mm(x, wub)
            h = a * (1.0 / (1.0 + jnp.exp(-a))) * u
            h_scr[SLOTS_CTX * i:SLOTS_CTX * (i + 1), c0:c0 + FF_TILE] = h.astype(BF16)

    @pl.when(j == 0)
    def _():
        for c in range(FF_HALF // FF_TILE):
            lo = FF_TILE * c
            swiglu_cols(wg_ref[0, :, lo:lo + FF_TILE], wu_ref[0, :, lo:lo + FF_TILE], lo)
        wl_scr[0] = wg_ref[0, :, FF_HALF - tail:FF_HALF]
        wl_scr[1] = wu_ref[0, :, FF_HALF - tail:FF_HALF]

    @pl.when(j == 1)
    def _():
        head = FF_TILE - tail
        swiglu_cols(jnp.concatenate([wl_scr[0], wg_ref[0, :, 0:head]], axis=1),
                    jnp.concatenate([wl_scr[1], wu_ref[0, :, 0:head]], axis=1), FF_HALF - tail)
        for c in range(FF_HALF // FF_TILE):
            lo = head + FF_TILE * c
            swiglu_cols(wg_ref[0, :, lo:lo + FF_TILE], wu_ref[0, :, lo:lo + FF_TILE], FF_HALF + lo)

    @pl.when(j >= 2)
    def _():
        wdb = wd_ref[0].astype(BF16)
        e = pl.program_id(0)
        for i, (g_ref, y_ref) in enumerate(((gc_ref, yc_ref), (gl_ref, yl_ref))):
            nset = g_ref.shape[0] // N_EXPERTS
            gate = jnp.concatenate(
                [_row_to_col(g_ref[pl.ds(N_EXPERTS * s + e, 1), :][:, :SLOTS_CTX // nset]) for s in range(nset)],
                axis=0)
            y = _mm(h_scr[SLOTS_CTX * i:SLOTS_CTX * (i + 1), :], wdb)
            y_ref[0] = (y * gate).astype(BF16)


def _expert_ffn(xc, xl, gc, gl, w_gate, w_up, w_down):
    assert SLOTS_CTX == SLOTS_LAT and FF_HALF % LANES == 0 and D_MODEL % OUT_TILE == 0
    xs = lambda n: pl.BlockSpec((1, n, D_MODEL), lambda e, j: (e, 0, 0))
    gs = lambda g: pl.BlockSpec(g.shape, lambda e, j: (0, 0))
    last = N_EXPERTS - 1
    up = pl.BlockSpec((1, D_MODEL, FF_HALF),
                      lambda e, j: (jnp.where(j >= 2, jnp.minimum(e + 1, last), e), 0, jnp.where(j == 1, 1, 0)))
    down = pl.BlockSpec((1, D_FF, OUT_TILE), lambda e, j: (e, 0, jnp.maximum(j - 2, 0)))
    ys = lambda n: pl.BlockSpec((1, n, OUT_TILE), lambda e, j: (e, 0, jnp.maximum(j - 2, 0)))
    return pl.pallas_call(
        _ffn_kernel,
        grid=(N_EXPERTS, 2 + D_MODEL // OUT_TILE),
        in_specs=[xs(SLOTS_CTX), xs(SLOTS_LAT), gs(gc), gs(gl), up, up, down],
        out_specs=[ys(SLOTS_CTX), ys(SLOTS_LAT)],
        out_shape=[jax.ShapeDtypeStruct((N_EXPERTS, SLOTS_CTX, D_MODEL), BF16),
                   jax.ShapeDtypeStruct((N_EXPERTS, SLOTS_LAT, D_MODEL), BF16)],
        scratch_shapes=[pltpu.VMEM((SLOTS_CTX + SLOTS_LAT, D_FF), BF16),
                        pltpu.VMEM((2, D_MODEL, LANES), F32)],
        compiler_params=_params(("arbitrary", "arbitrary"), FFN_VMEM_LIMIT),
        name="expert_ffn",
    )(xc, xl, gc, gl, w_gate, w_up, w_down)


def _combine_ctx_kernel(y_ref, idx_ref, x1_ref, mod_ref, lng_ref, lnb_ref, o_ref):
    g2 = _mod_rows(mod_ref, 0)[5]
    nslot = N_EXPERTS * CAP_CTX
    hit = lax.broadcasted_iota(I32, (SEQ, nslot), 0) == idx_ref[0]
    y = y_ref[...].reshape(nslot, D_MODEL)
    f = _mm(jnp.where(hit, 1.0, 0.0).astype(BF16), y)
    o_ref[0] = _layer_norm(DEEPNORM_ALPHA * x1_ref[0] + g2 * f, lng_ref[...], lnb_ref[...])


def _combine_ctx(yc, idx_row, x1, mod, ln_g, ln_b):
    full = lambda shape: pl.BlockSpec(shape, lambda s: (0,) * len(shape))
    return pl.pallas_call(
        _combine_ctx_kernel,
        grid=(BATCH,),
        in_specs=[pl.BlockSpec((N_EXPERTS, CAP_CTX, D_MODEL), lambda s: (0, s, 0)),
                  pl.BlockSpec((1, 1, N_EXPERTS * CAP_CTX), lambda s: (s, 0, 0)),
                  pl.BlockSpec((1, SEQ, D_MODEL), lambda s: (s, 0, 0)),
                  full((8, 6 * D_MODEL)), full((1, D_MODEL)), full((1, D_MODEL))],
        out_specs=pl.BlockSpec((1, SEQ, D_MODEL), lambda s: (s, 0, 0)),
        out_shape=jax.ShapeDtypeStruct((BATCH, SEQ, D_MODEL), F32),
        compiler_params=_params(("arbitrary",)),
        name="combine_ctx",
    )(yc, idx_row, x1, mod, ln_g, ln_b)


def _combine_lat_kernel(y_ref, idx_ref, x1_ref, mod_ref, lng_ref, lnb_ref, o_ref):
    s = pl.program_id(0)
    g2 = _mod_rows(mod_ref, 1 + s)[5]
    tok = lax.broadcasted_iota(I32, (ROW_TILE, CAP_LAT), 0) + ROW_TILE * pl.program_id(1)
    f = jnp.zeros((ROW_TILE, D_MODEL), F32)
    for e in range(N_EXPERTS):
        hit = tok == idx_ref[0, e:e + 1, :]
        f = f + _mm(jnp.where(hit, 1.0, 0.0).astype(BF16), y_ref[e])
    o_ref[0] = _layer_norm(DEEPNORM_ALPHA * x1_ref[0] + g2 * f, lng_ref[...], lnb_ref[...])


def _combine_lat(yl, idx, x1, mod, ln_g, ln_b):
    nt = DEC_SEQ // ROW_TILE
    full = lambda shape: pl.BlockSpec(shape, lambda s, t: (0,) * len(shape))
    return pl.pallas_call(
        _combine_lat_kernel,
        grid=(DEC_BATCH, nt),
        in_specs=[pl.BlockSpec((N_EXPERTS, CAP_LAT, D_MODEL), lambda s, t: (0, s, 0)),
                  pl.BlockSpec((1, N_EXPERTS, CAP_LAT), lambda s, t: (s, 0, 0)),
                  pl.BlockSpec((1, ROW_TILE, D_MODEL), lambda s, t: (s, t, 0)),
                  full((8, 6 * D_MODEL)), full((1, D_MODEL)), full((1, D_MODEL))],
        out_specs=pl.BlockSpec((1, ROW_TILE, D_MODEL), lambda s, t: (s, t, 0)),
        out_shape=jax.ShapeDtypeStruct((DEC_BATCH, DEC_SEQ, D_MODEL), F32),
        compiler_params=_params(("arbitrary", "arbitrary")),
        name="combine_lat",
    )(yl, idx, x1, mod, ln_g, ln_b)


def _rope_tables():
    t = np.arange(DEC_SEQ)
    row = (t // GRID_W).astype(np.float64)
    col = (t % GRID_W).astype(np.float64)
    half = HEAD_DIM // 2
    inv = ROPE_THETA ** (-np.arange(0, half, 2, dtype=np.float64) / half)
    ang_r = row[:, None] * inv[None, :]
    ang_c = col[:, None] * inv[None, :]
    ang = np.concatenate([ang_r, ang_r, ang_c, ang_c], axis=-1)
    cos = np.tile(np.cos(ang), (1, 2))
    sin = np.tile(np.sin(ang), (1, 2))
    first = (np.arange(LANES) % 32) < 16
    return jnp.asarray(cos, F32), jnp.asarray(np.where(first[None, :], -sin, sin), F32)


def kernel(x_prompt, x_sample, cache_na_k, cache_na_v, cache_diff_k, cache_diff_v, c, c_ctx, w_ada, b_ada, w_in, w_out, na_rel_bias, lambda_q1, lambda_k1, lambda_q2, lambda_k2, subln_g, ln1_g, ln1_b, ln2_g, ln2_b, w_router, w_gate, w_up, w_down):
    l = 0
    cvec = jnp.concatenate([c_ctx[None, :], c, jnp.zeros((8 - 1 - DEC_BATCH, D_MODEL), F32)], axis=0)
    mod = _modulation(cvec, w_ada[l], b_ada[l][None, :])
    lamv = jnp.stack([lambda_q1[l], lambda_k1[l], lambda_q2[l], lambda_k2[l]], axis=0)
    w_in_b = w_in[l].astype(BF16)
    w_out_b = w_out[l].astype(BF16)
    sub = subln_g[l][None, :]
    wrt = w_router[l].T

    tr = lambda a: jnp.swapaxes(a, -1, -2)

    x1c, h2c, affc, na_k_t, na_v_t, diff_k_t, new_diff_v = _context_block(
        x_prompt, mod, lamv, w_in_b, w_out_b, sub, ln1_g[l][None, :], ln1_b[l][None, :], wrt)

    cos, sin = _rope_tables()
    naq, nak, nav, dfq, dfk, dfv = _latent_qkv(x_sample, mod, w_in_b, cos, sin)
    tbl = _na_bias_tables(na_rel_bias[l].reshape(-1))
    na_o = _latent_na(naq, nak, nav, tr(cache_na_k), tr(cache_na_v), tbl)
    df_o = _latent_diff(dfq, dfk, dfv, tr(cache_diff_k), cache_diff_v, lamv, sub)
    x1l, h2l, affl = _latent_out(x_sample, na_o, df_o, mod, w_out_b,
                                 ln1_g[l][None, :], ln1_b[l][None, :], wrt)

    idxc, gatec = _route(affc.reshape(BATCH * N_EXPERTS, SEQ), CAP_CTX)
    idxl, gatel = _route(affl.reshape(DEC_BATCH * N_EXPERTS, DEC_SEQ), CAP_LAT)
    idxc = idxc.reshape(BATCH, 1, N_EXPERTS * CAP_CTX)
    idxl = idxl.reshape(DEC_BATCH, N_EXPERTS, CAP_LAT)
    xc = _gather_ctx(idxc, h2c)
    xl = _gather_lat(idxl, h2l)
    yc, yl = _expert_ffn(xc, xl, gatec, gatel, w_gate[l], w_up[l], w_down[l])
    y_prompt = _combine_ctx(yc, idxc, x1c, mod, ln2_g[l][None, :], ln2_b[l][None, :])
    y_sample = _combine_lat(yl, idxl, x1l, mod, ln2_g[l][None, :], ln2_b[l][None, :])
    return (y_prompt, y_sample, tr(na_k_t), tr(na_v_t), tr(diff_k_t), new_diff_v)
```

```python
import functools
import math

import jax
import jax.numpy as jnp
import numpy as np
from jax import lax
from jax.experimental import pallas as pl
from jax.experimental.pallas import tpu as pltpu

F32 = jnp.float32
BF16 = jnp.bfloat16
I32 = jnp.int32

D_MODEL = 1024
BATCH = 16
SEQ = 256
DEC_BATCH = 2
DEC_SEQ = 2048
PAST_LEN = 256
GRID_W = 64
GRID_ROWS = DEC_SEQ // GRID_W
HEAD_DIM = 64
NA_HEADS = 8
DIFF_HEADS = 4
NA_WIDTH = NA_HEADS * HEAD_DIM
DIFF_WIDTH = DIFF_HEADS * 2 * HEAD_DIM
QKV_WIDTH = 3 * NA_WIDTH + 3 * DIFF_WIDTH
NA_WIN_H = 8
NA_WIN_W = 16
N_REL_H = 2 * NA_WIN_H - 1
N_REL_W = 2 * NA_WIN_W - 1
N_EXPERTS = 16
EC_CAPACITY_FACTOR = 2
D_FF = 2816
ROPE_THETA = 10000.0
NORM_EPS = 1e-5
NEG_INF = -1e30
DEPTH = 1
DEEPNORM_ALPHA = (2.0 * DEPTH) ** 0.25
LAMBDA_INIT = 0.8 - 0.6 * math.exp(-0.3 * 0)
SCALE = HEAD_DIM ** -0.5

OFF_NAQ = 0
OFF_NAK = NA_WIDTH
OFF_NAV = 2 * NA_WIDTH
OFF_DFQ = 3 * NA_WIDTH
OFF_DFK = 3 * NA_WIDTH + DIFF_WIDTH
OFF_DFV = 3 * NA_WIDTH + 2 * DIFF_WIDTH

CAP_CTX = EC_CAPACITY_FACTOR * SEQ // N_EXPERTS
CAP_LAT = EC_CAPACITY_FACTOR * DEC_SEQ // N_EXPERTS
SLOTS_CTX = BATCH * CAP_CTX
SLOTS_LAT = DEC_BATCH * CAP_LAT

ROW_TILE = 512
NA_QUAD = 4
NA_WIN_ROWS = 12
NA_GROUP = 8
FF_TILE = 256
UP_CHUNK = 512
OUT_TILE = 512
LANES = 128
GATHER_GROUP = 2
DIFF_ROWS = 1024
Q_ROWS = 256

VMEM_LIMIT = 48 * 1024 * 1024


def _params(sem, vmem=VMEM_LIMIT, flags=None):
    return pltpu.CompilerParams(dimension_semantics=sem, vmem_limit_bytes=vmem, flags=flags)


def _mm(a, b):
    return jnp.dot(a, b, preferred_element_type=F32)


def _nt(a, b):
    return lax.dot_general(a, b, (((1,), (1,)), ((), ())), preferred_element_type=F32)


def _split(a):
    hi = a.astype(BF16)
    lo = (a - hi.astype(F32)).astype(BF16)
    return hi, lo


def _mm3(a, b):
    ah, al = _split(a)
    bh, bl = _split(b)
    return _mm(ah, bh) + _mm(al, bh) + _mm(ah, bl)


def _nt3(a, b):
    ah, al = _split(a)
    bh, bl = _split(b)
    return _nt(ah, bh) + _nt(al, bh) + _nt(ah, bl)


def _layer_norm(y, g, b):
    mu = jnp.mean(y, axis=-1, keepdims=True)
    d = y - mu
    var = jnp.mean(d * d, axis=-1, keepdims=True)
    return d * lax.rsqrt(var + NORM_EPS) * g + b


def _lam(l_ref):
    l = l_ref[...]
    a = jnp.sum(l[0:1] * l[1:2], axis=-1, keepdims=True)
    b = jnp.sum(l[2:3] * l[3:4], axis=-1, keepdims=True)
    return jnp.exp(a) - jnp.exp(b) + LAMBDA_INIT


def _mod_rows(mod_ref, row):
    return [mod_ref[pl.ds(row, 1), i * D_MODEL:(i + 1) * D_MODEL] for i in range(6)]


def _sub_norm(o, sub):
    ms = jnp.mean(o * o, axis=-1, keepdims=True)
    return o * lax.rsqrt(ms + NORM_EPS) * sub * (1.0 - LAMBDA_INIT)


def _router_aff_t(h2, wrt):
    lg = _nt3(wrt, h2)
    m = jnp.max(lg, axis=0, keepdims=True)
    e = jnp.exp(lg - m)
    return e / jnp.sum(e, axis=0, keepdims=True)


def _mod_kernel(c_ref, w_ref, b_ref, o_ref):
    c = c_ref[...]
    s = c * (1.0 / (1.0 + jnp.exp(-c)))
    o_ref[...] = _mm3(s, w_ref[...]) + b_ref[...]


def _modulation(cvec, w_ada, b_ada):
    ncol = 6 * D_MODEL
    return pl.pallas_call(
        _mod_kernel,
        grid=(6,),
        in_specs=[pl.BlockSpec((8, D_MODEL), lambda j: (0, 0)),
                  pl.BlockSpec((D_MODEL, D_MODEL), lambda j: (0, j)),
                  pl.BlockSpec((1, D_MODEL), lambda j: (0, j))],
        out_specs=pl.BlockSpec((8, D_MODEL), lambda j: (0, j)),
        out_shape=jax.ShapeDtypeStruct((8, ncol), F32),
        compiler_params=_params(("arbitrary",)),
        name="modulation",
    )(cvec, w_ada, b_ada)


def _ctx_kernel(x_ref, mod_ref, lam_ref, win_ref, wout_ref, sub_ref, lng_ref, lnb_ref, wrt_ref,
                x1_ref, h2_ref, aff_ref, nak_ref, nav_ref, dfk_ref, dfv_ref,
                qkv_scr, o_scr, s_scr, e_scr):
    x = x_ref[0]
    sh1, sc1, g1, sh2, sc2, g2 = _mod_rows(mod_ref, 0)
    h = (x * (1.0 + sc1) + sh1).astype(BF16)
    qkv_scr[...] = _mm(h, win_ref[...])

    kt_na = qkv_scr[:, OFF_NAK:OFF_NAK + NA_WIDTH].T
    kt_df = qkv_scr[:, OFF_DFK:OFF_DFK + DIFF_WIDTH].T
    nak_ref[0, 0] = kt_na.reshape(NA_HEADS, HEAD_DIM, SEQ)
    nav_ref[0, 0] = qkv_scr[:, OFF_NAV:OFF_NAV + NA_WIDTH].T.reshape(NA_HEADS, HEAD_DIM, SEQ)
    dfk_ref[0, 0] = kt_df.reshape(DIFF_HEADS, 2, HEAD_DIM, SEQ)
    for hh in range(DIFF_HEADS):
        c0 = OFF_DFV + 2 * HEAD_DIM * hh
        dfv_ref[0, 0, hh] = qkv_scr[:, c0:c0 + 2 * HEAD_DIM]

    kt = jnp.concatenate([kt_na, kt_df], axis=0).astype(BF16)
    q_cols = [OFF_NAQ + HEAD_DIM * i for i in range(NA_HEADS)] + [
        OFF_DFQ + HEAD_DIM * i for i in range(2 * DIFF_HEADS)]
    nmap = len(q_cols)
    for i, c in enumerate(q_cols):
        q = (qkv_scr[:, c:c + HEAD_DIM] * SCALE).astype(BF16)
        s_scr[i] = _mm(q, kt[HEAD_DIM * i:HEAD_DIM * (i + 1), :])
    s = s_scr[...]
    e = jnp.exp(s - jnp.max(s, axis=-1, keepdims=True))
    rl = 1.0 / jnp.sum(e, axis=-1, keepdims=True)
    e_scr[...] = e.astype(BF16)

    for hh in range(NA_HEADS):
        v = qkv_scr[:, OFF_NAV + HEAD_DIM * hh:OFF_NAV + HEAD_DIM * (hh + 1)].astype(BF16)
        o_scr[:, HEAD_DIM * hh:HEAD_DIM * (hh + 1)] = (_mm(e_scr[hh], v) * rl[hh]).astype(BF16)
    lam = _lam(lam_ref)
    sub = sub_ref[...]
    for hh in range(DIFF_HEADS):
        i1 = NA_HEADS + 2 * hh
        c0 = OFF_DFV + 2 * HEAD_DIM * hh
        v = qkv_scr[:, c0:c0 + 2 * HEAD_DIM].astype(BF16)
        o = _mm(e_scr[i1], v) * rl[i1] - _mm(e_scr[i1 + 1], v) * (lam * rl[i1 + 1])
        c1 = NA_WIDTH + 2 * HEAD_DIM * hh
        o_scr[:, c1:c1 + 2 * HEAD_DIM] = _sub_norm(o, sub).astype(BF16)

    a = _mm(o_scr[...], wout_ref[...])
    x1 = _layer_norm(DEEPNORM_ALPHA * x + g1 * a, lng_ref[...], lnb_ref[...])
    x1_ref[0] = x1
    h2 = x1 * (1.0 + sc2) + sh2

    h2_ref[0] = h2.astype(BF16)
    aff_ref[0] = _router_aff_t(h2, wrt_ref[...])


def _context_block(x, mod, lamv, w_in, w_out, sub, ln_g, ln_b, wrt):
    full = lambda shape: pl.BlockSpec(shape, lambda b: (0,) * len(shape))
    return pl.pallas_call(
        _ctx_kernel,
        grid=(BATCH,),
        in_specs=[pl.BlockSpec((1, SEQ, D_MODEL), lambda b: (b, 0, 0)),
                  full((8, 6 * D_MODEL)), full((4, HEAD_DIM)),
                  full((D_MODEL, QKV_WIDTH)), full((D_MODEL, D_MODEL)),
                  full((1, 2 * HEAD_DIM)), full((1, D_MODEL)), full((1, D_MODEL)),
                  full((N_EXPERTS, D_MODEL))],
        out_specs=[pl.BlockSpec((1, SEQ, D_MODEL), lambda b: (b, 0, 0)),
                   pl.BlockSpec((1, SEQ, D_MODEL), lambda b: (b, 0, 0)),
                   pl.BlockSpec((1, N_EXPERTS, SEQ), lambda b: (b, 0, 0)),
                   pl.BlockSpec((1, 1, NA_HEADS, HEAD_DIM, SEQ), lambda b: (b, 0, 0, 0, 0)),
                   pl.BlockSpec((1, 1, NA_HEADS, HEAD_DIM, SEQ), lambda b: (b, 0, 0, 0, 0)),
                   pl.BlockSpec((1, 1, DIFF_HEADS, 2, HEAD_DIM, SEQ), lambda b: (b, 0, 0, 0, 0, 0)),
                   pl.BlockSpec((1, 1, DIFF_HEADS, SEQ, 2 * HEAD_DIM), lambda b: (b, 0, 0, 0, 0))],
        out_shape=[jax.ShapeDtypeStruct((BATCH, SEQ, D_MODEL), F32),
                   jax.ShapeDtypeStruct((BATCH, SEQ, D_MODEL), BF16),
                   jax.ShapeDtypeStruct((BATCH, N_EXPERTS, SEQ), F32),
                   jax.ShapeDtypeStruct((BATCH, 1, NA_HEADS, HEAD_DIM, SEQ), F32),
                   jax.ShapeDtypeStruct((BATCH, 1, NA_HEADS, HEAD_DIM, SEQ), F32),
                   jax.ShapeDtypeStruct((BATCH, 1, DIFF_HEADS, 2, HEAD_DIM, SEQ), F32),
                   jax.ShapeDtypeStruct((BATCH, 1, DIFF_HEADS, SEQ, 2 * HEAD_DIM), F32)],
        scratch_shapes=[pltpu.VMEM((SEQ, QKV_WIDTH), F32), pltpu.VMEM((SEQ, D_MODEL), BF16),
                        pltpu.VMEM((NA_HEADS + 2 * DIFF_HEADS, SEQ, SEQ), F32),
                        pltpu.VMEM((NA_HEADS + 2 * DIFF_HEADS, SEQ, SEQ), BF16)],
        compiler_params=_params(("arbitrary",)),
        name="context_block",
    )(x, mod, lamv, w_in, w_out, sub, ln_g, ln_b, wrt)


def _lat_qkv_kernel(x_ref, mod_ref, win_ref, cos_ref, sin_ref,
                    naq_ref, nak_ref, nav_ref, dfq_ref, dfk_ref, dfv_ref, qkv_scr):
    b = pl.program_id(0)
    x = x_ref[0]
    sh1, sc1 = _mod_rows(mod_ref, 1 + b)[:2]
    h = (x * (1.0 + sc1) + sh1).astype(BF16)
    qkv_scr[...] = _mm(h, win_ref[...])

    for hh in range(NA_HEADS):
        lo, hi = HEAD_DIM * hh, HEAD_DIM * (hh + 1)
        naq_ref[0, hh] = (qkv_scr[:, OFF_NAQ + lo:OFF_NAQ + hi] * SCALE).astype(BF16)
        nak_ref[0, hh] = qkv_scr[:, OFF_NAK + lo:OFF_NAK + hi].astype(BF16)
        nav_ref[0, hh] = qkv_scr[:, OFF_NAV + lo:OFF_NAV + hi].astype(BF16)

    cos = cos_ref[...]
    sin = sin_ref[...]
    lane = lax.broadcasted_iota(I32, (ROW_TILE, LANES), 1)
    first = (lane & 31) < 16

    def rope(t):
        rot = jnp.where(first, pltpu.roll(t, LANES - 16, 1), pltpu.roll(t, 16, 1))
        return t * cos + rot * sin

    for hh in range(DIFF_HEADS):
        lo = 2 * HEAD_DIM * hh
        q = rope(qkv_scr[:, OFF_DFQ + lo:OFF_DFQ + lo + LANES]) * SCALE
        k = rope(qkv_scr[:, OFF_DFK + lo:OFF_DFK + lo + LANES])
        for m in range(2):
            dfq_ref[0, hh, m] = q[:, HEAD_DIM * m:HEAD_DIM * (m + 1)].astype(BF16)
        dfk_ref[0, hh] = k.T.astype(BF16).reshape(2, HEAD_DIM, ROW_TILE)
        dfv_ref[0, hh] = qkv_scr[:, OFF_DFV + lo:OFF_DFV + lo + LANES].astype(BF16)


def _latent_qkv(x, mod, w_in, cos, sin):
    nt = DEC_SEQ // ROW_TILE
    full = lambda shape: pl.BlockSpec(shape, lambda b, t: (0,) * len(shape))
    hs = lambda nh: pl.BlockSpec((1, nh, ROW_TILE, HEAD_DIM), lambda b, t: (b, 0, t, 0))
    return pl.pallas_call(
        _lat_qkv_kernel,
        grid=(DEC_BATCH, nt),
        in_specs=[pl.BlockSpec((1, ROW_TILE, D_MODEL), lambda b, t: (b, t, 0)),
                  full((8, 6 * D_MODEL)), full((D_MODEL, QKV_WIDTH)),
                  pl.BlockSpec((ROW_TILE, LANES), lambda b, t: (t, 0)),
                  pl.BlockSpec((ROW_TILE, LANES), lambda b, t: (t, 0))],
        out_specs=[hs(NA_HEADS), hs(NA_HEADS), hs(NA_HEADS),
                   pl.BlockSpec((1, DIFF_HEADS, 2, ROW_TILE, HEAD_DIM), lambda b, t: (b, 0, 0, t, 0)),
                   pl.BlockSpec((1, DIFF_HEADS, 2, HEAD_DIM, ROW_TILE), lambda b, t: (b, 0, 0, 0, t)),
                   pl.BlockSpec((1, DIFF_HEADS, ROW_TILE, 2 * HEAD_DIM), lambda b, t: (b, 0, t, 0))],
        out_shape=[jax.ShapeDtypeStruct((DEC_BATCH, NA_HEADS, DEC_SEQ, HEAD_DIM), BF16)] * 3
        + [jax.ShapeDtypeStruct((DEC_BATCH, DIFF_HEADS, 2, DEC_SEQ, HEAD_DIM), BF16),
           jax.ShapeDtypeStruct((DEC_BATCH, DIFF_HEADS, 2, HEAD_DIM, DEC_SEQ), BF16),
           jax.ShapeDtypeStruct((DEC_BATCH, DIFF_HEADS, DEC_SEQ, 2 * HEAD_DIM), BF16)],
        scratch_shapes=[pltpu.VMEM((ROW_TILE, QKV_WIDTH), F32)],
        compiler_params=_params(("arbitrary", "arbitrary")),
        name="latent_qkv",
    )(x, mod, w_in, cos, sin)


def _na_row_offset(kind, qr, kr):
    if kind == 0:
        return kr - qr + 7 if kr < NA_WIN_H else None
    if kind == 1:
        return kr - qr + 3 if qr <= kr < qr + NA_WIN_H else None
    return kr - qr - 1 if NA_WIN_ROWS - NA_WIN_H <= kr else None


def _bias_kernel(rpb_ref, o_ref):
    h = pl.program_id(0)
    base = h * (N_REL_H * N_REL_W)
    wq = lax.broadcasted_iota(I32, (GRID_W, LANES), 0)
    c = lax.broadcasted_iota(I32, (GRID_W, LANES), 1)
    wk = c & (GRID_W - 1)
    right = c >= GRID_W
    c0 = jnp.clip(wq - NA_WIN_W // 2, 0, GRID_W - NA_WIN_W)
    in_win = (wk >= c0) & (wk < c0 + NA_WIN_W)
    cache = {}
    strips = {}

    def strip(r):
        if r not in strips:
            c8 = lax.broadcasted_iota(I32, (8, LANES), 1)
            d = jnp.where(c8 < GRID_W, c8, c8 - LANES)
            co_of = jnp.clip(d, -(NA_WIN_W - 1), NA_WIN_W - 1) + (NA_WIN_W - 1)
            acc = jnp.zeros((8, LANES), F32)
            for co in range(N_REL_W):
                acc = jnp.where(co_of == co, rpb_ref[base + r * N_REL_W + co], acc)
            strips[r] = jnp.concatenate([acc] * (GRID_W // 8), axis=0)
        return strips[r]

    def tile(rl, rr):
        if (rl, rr) in cache:
            return cache[(rl, rr)]
        if rl is None and rr is None:
            t = jnp.full((GRID_W, LANES), NEG_INF, F32)
        else:
            zero = jnp.zeros((GRID_W, LANES), F32)
            tl = pltpu.roll(strip(rl), 0, 1, stride=1, stride_axis=0) if rl is not None else zero
            tr = pltpu.roll(strip(rr), GRID_W, 1, stride=1, stride_axis=0) if rr is not None else zero
            acc = jnp.where(right, tr, tl)
            ok = in_win
            if rl is None:
                ok = ok & right
            if rr is None:
                ok = ok & jnp.logical_not(right)
            t = jnp.where(ok, acc, NEG_INF)
        cache[(rl, rr)] = t
        return t

    for kind in range(3):
        for qr in range(NA_QUAD):
            row = [tile(_na_row_offset(kind, qr, 2 * p), _na_row_offset(kind, qr, 2 * p + 1))
                   for p in range(NA_WIN_ROWS // 2)]
            o_ref[0, kind, GRID_W * qr:GRID_W * (qr + 1), :] = jnp.concatenate(row, axis=1)


def _na_bias_tables(rpb_flat):
    return pl.pallas_call(
        _bias_kernel,
        grid=(NA_HEADS,),
        in_specs=[pl.BlockSpec(memory_space=pltpu.SMEM)],
        out_specs=pl.BlockSpec((1, 3, NA_QUAD * GRID_W, NA_WIN_ROWS * GRID_W), lambda h: (h, 0, 0, 0)),
        out_shape=jax.ShapeDtypeStruct((NA_HEADS, 3, NA_QUAD * GRID_W, NA_WIN_ROWS * GRID_W), F32),
        compiler_params=_params(("arbitrary",)),
        name="na_bias_tables",
    )(rpb_flat)


def _na_kernel(q_ref, k_ref, v_ref, ck_ref, cv_ref, tbl_ref, o_ref):
    g = pl.program_id(2)
    u0 = jnp.clip(NA_QUAD * g - NA_QUAD, 0, GRID_ROWS - NA_WIN_ROWS)
    start = pl.multiple_of(u0 * GRID_W, GRID_W)
    nwin = NA_WIN_ROWS * GRID_W
    for j in range(NA_GROUP):
        q = q_ref[0, j]
        kw = k_ref[0, j, pl.ds(start, nwin), :]
        vw = v_ref[0, j, pl.ds(start, nwin), :]
        kct = ck_ref[0, 0, j].astype(BF16)
        vct = cv_ref[0, 0, j].astype(BF16)
        sl = _nt(q, kw) + tbl_ref[j, 0]
        sc = _mm(q, kct)
        m = jnp.maximum(jnp.max(sl, axis=-1, keepdims=True), jnp.max(sc, axis=-1, keepdims=True))
        el = jnp.exp(sl - m)
        ec = jnp.exp(sc - m)
        l = jnp.sum(el, axis=-1, keepdims=True) + jnp.sum(ec, axis=-1, keepdims=True)
        o = (_mm(el.astype(BF16), vw) + _nt(ec.astype(BF16), vct)) / l
        o_ref[0, :, HEAD_DIM * j:HEAD_DIM * (j + 1)] = o.astype(BF16)


def _latent_na(naq, nak, nav, cache_k, cache_v, tbl):
    nq = GRID_ROWS // NA_QUAD
    rows = NA_QUAD * GRID_W
    kind = lambda g: jnp.minimum(g, 1) + g // (nq - 1)
    return pl.pallas_call(
        _na_kernel,
        grid=(DEC_BATCH, NA_HEADS // NA_GROUP, nq),
        in_specs=[pl.BlockSpec((1, NA_GROUP, rows, HEAD_DIM), lambda b, p, g: (b, p, g, 0)),
                  pl.BlockSpec((1, NA_GROUP, DEC_SEQ, HEAD_DIM), lambda b, p, g: (b, p, 0, 0)),
                  pl.BlockSpec((1, NA_GROUP, DEC_SEQ, HEAD_DIM), lambda b, p, g: (b, p, 0, 0)),
                  pl.BlockSpec((1, 1, NA_GROUP, HEAD_DIM, PAST_LEN), lambda b, p, g: (b, 0, p, 0, 0)),
                  pl.BlockSpec((1, 1, NA_GROUP, HEAD_DIM, PAST_LEN), lambda b, p, g: (b, 0, p, 0, 0)),
                  pl.BlockSpec((NA_GROUP, 1, rows, NA_WIN_ROWS * GRID_W), lambda b, p, g: (p, kind(g), 0, 0))],
        out_specs=pl.BlockSpec((1, rows, NA_GROUP * HEAD_DIM), lambda b, p, g: (b, g, p)),
        out_shape=jax.ShapeDtypeStruct((DEC_BATCH, DEC_SEQ, NA_WIDTH), BF16),
        compiler_params=_params(("arbitrary",) * 3),
        name="latent_na",
    )(naq, nak, nav, cache_k, cache_v, tbl)


def _diff_kernel(q_ref, kt_ref, v_ref, ck_ref, cv_ref, lam_ref, sub_ref, o_ref):
    lam = _lam(lam_ref)
    vc = cv_ref[0, 0, 0].astype(BF16)
    vl = v_ref[0, 0]
    for r in range(DIFF_ROWS // Q_ROWS):
        rows = slice(Q_ROWS * r, Q_ROWS * (r + 1))
        parts = []
        for m in range(2):
            q = q_ref[0, 0, m, rows, :]
            sc = _mm(q, ck_ref[0, 0, 0, m].astype(BF16))
            sl = _mm(q, kt_ref[0, 0, m])
            mx = jnp.maximum(jnp.max(sc, axis=-1, keepdims=True), jnp.max(sl, axis=-1, keepdims=True))
            ec = jnp.exp(sc - mx)
            el = jnp.exp(sl - mx)
            l = jnp.sum(ec, axis=-1, keepdims=True) + jnp.sum(el, axis=-1, keepdims=True)
            parts.append((_mm(ec.astype(BF16), vc) + _mm(el.astype(BF16), vl), l))
        (a1, l1), (a2, l2) = parts
        o = a1 * (1.0 / l1) - a2 * (lam / l2)
        o_ref[0, rows, :] = _sub_norm(o, sub_ref[...]).astype(BF16)


def _latent_diff(dfq, dfk, dfv, cache_k, cache_v, lamv, sub):
    nt = DEC_SEQ // DIFF_ROWS
    full = lambda shape: pl.BlockSpec(shape, lambda b, h, t: (0,) * len(shape))
    return pl.pallas_call(
        _diff_kernel,
        grid=(DEC_BATCH, DIFF_HEADS, nt),
        in_specs=[pl.BlockSpec((1, 1, 2, DIFF_ROWS, HEAD_DIM), lambda b, h, t: (b, h, 0, t, 0)),
                  pl.BlockSpec((1, 1, 2, HEAD_DIM, DEC_SEQ), lambda b, h, t: (b, h, 0, 0, 0)),
                  pl.BlockSpec((1, 1, DEC_SEQ, 2 * HEAD_DIM), lambda b, h, t: (b, h, 0, 0)),
                  pl.BlockSpec((1, 1, 1, 2, HEAD_DIM, PAST_LEN), lambda b, h, t: (b, 0, h, 0, 0, 0)),
                  pl.BlockSpec((1, 1, 1, PAST_LEN, 2 * HEAD_DIM), lambda b, h, t: (b, 0, h, 0, 0)),
                  full((4, HEAD_DIM)), full((1, 2 * HEAD_DIM))],
        out_specs=pl.BlockSpec((1, DIFF_ROWS, 2 * HEAD_DIM), lambda b, h, t: (b, t, h)),
        out_shape=jax.ShapeDtypeStruct((DEC_BATCH, DEC_SEQ, DIFF_WIDTH), BF16),
        compiler_params=_params(("arbitrary",) * 3),
        name="latent_diff",
    )(dfq, dfk, dfv, cache_k, cache_v, lamv, sub)


def _lat_out_kernel(x_ref, na_ref, df_ref, mod_ref, wout_ref, lng_ref, lnb_ref, wrt_ref,
                    x1_ref, h2_ref, aff_ref):
    b = pl.program_id(0)
    x = x_ref[0]
    _, _, g1, sh2, sc2, _ = _mod_rows(mod_ref, 1 + b)
    a = _mm(na_ref[0], wout_ref[0:NA_WIDTH, :]) + _mm(df_ref[0], wout_ref[NA_WIDTH:, :])
    x1 = _layer_norm(DEEPNORM_ALPHA * x + g1 * a, lng_ref[...], lnb_ref[...])
    x1_ref[0] = x1
    h2 = x1 * (1.0 + sc2) + sh2
    h2_ref[0] = h2.astype(BF16)
    aff_ref[0] = _router_aff_t(h2, wrt_ref[...])


def _latent_out(x, na_o, df_o, mod, w_out, ln_g, ln_b, wrt):
    nt = DEC_SEQ // ROW_TILE
    full = lambda shape: pl.BlockSpec(shape, lambda b, t: (0,) * len(shape))
    tok = lambda w: pl.BlockSpec((1, ROW_TILE, w), lambda b, t: (b, t, 0))
    return pl.pallas_call(
        _lat_out_kernel,
        grid=(DEC_BATCH, nt),
        in_specs=[tok(D_MODEL), tok(NA_WIDTH), tok(DIFF_WIDTH), full((8, 6 * D_MODEL)),
                  full((D_MODEL, D_MODEL)), full((1, D_MODEL)), full((1, D_MODEL)),
                  full((N_EXPERTS, D_MODEL))],
        out_specs=[tok(D_MODEL), tok(D_MODEL),
                   pl.BlockSpec((1, N_EXPERTS, ROW_TILE), lambda b, t: (b, 0, t))],
        out_shape=[jax.ShapeDtypeStruct((DEC_BATCH, DEC_SEQ, D_MODEL), F32),
                   jax.ShapeDtypeStruct((DEC_BATCH, DEC_SEQ, D_MODEL), BF16),
                   jax.ShapeDtypeStruct((DEC_BATCH, N_EXPERTS, DEC_SEQ), F32)],
        compiler_params=_params(("arbitrary", "arbitrary")),
        name="latent_out",
    )(x, na_o, df_o, mod, w_out, ln_g, ln_b, wrt)


def _ranks_first(va, ia, vb, ib):
    return (va > vb) | ((va == vb) & (ia < ib))


def _sort_rows_desc(aff):
    rows, n = aff.shape
    nb = n // LANES
    lane = lax.broadcasted_iota(I32, (rows, LANES), 1)
    v = [aff[:, LANES * b:LANES * (b + 1)] for b in range(nb)]
    ix = [lane + LANES * b for b in range(nb)]
    k = 2
    while k <= n:
        j = k // 2
        while j >= 1:
            if j >= LANES:
                sb = j // LANES
                for lo in range(nb):
                    if lo & sb:
                        continue
                    hi = lo | sb
                    f = _ranks_first(v[lo], ix[lo], v[hi], ix[hi])
                    if k < n and (LANES * lo) & k:
                        f = jnp.logical_not(f)
                    v[lo], v[hi] = jnp.where(f, v[lo], v[hi]), jnp.where(f, v[hi], v[lo])
                    ix[lo], ix[hi] = jnp.where(f, ix[lo], ix[hi]), jnp.where(f, ix[hi], ix[lo])
            else:
                upper = (lane & j) != 0
                if k < LANES:
                    flip_lanes = jnp.logical_xor(upper, (lane & k) != 0)
                for b in range(nb):
                    pv = jnp.where(upper, pltpu.roll(v[b], j, 1), pltpu.roll(v[b], LANES - j, 1))
                    pi = jnp.where(upper, pltpu.roll(ix[b], j, 1), pltpu.roll(ix[b], LANES - j, 1))
                    f = _ranks_first(v[b], ix[b], pv, pi)
                    if k < LANES:
                        flip = flip_lanes
                    elif k < n and (LANES * b) & k:
                        flip = jnp.logical_not(upper)
                    else:
                        flip = upper
                    keep = jnp.logical_xor(f, flip)
                    v[b] = jnp.where(keep, v[b], pv)
                    ix[b] = jnp.where(keep, ix[b], pi)
            j //= 2
        k *= 2
    return v, ix


def _route_kernel(aff_ref, idx_ref, gate_ref, *, cap, n):
    v, ix = _sort_rows_desc(aff_ref[...])
    if cap < LANES:
        idx_ref[...] = ix[0][:, :cap]
        gate_ref[...] = v[0]
    else:
        idx_ref[...] = jnp.concatenate(ix[:cap // LANES], axis=1)
        gate_ref[...] = jnp.concatenate(v[:cap // LANES], axis=1)


def _route(aff_t, cap):
    rows, n = aff_t.shape
    return pl.pallas_call(
        functools.partial(_route_kernel, cap=cap, n=n),
        out_shape=[jax.ShapeDtypeStruct((rows, cap), I32),
                   jax.ShapeDtypeStruct((rows, max(cap, LANES)), F32)],
        compiler_params=_params(None),
        name=f"route_{n}",
    )(aff_t)


def _row_to_col(row):
    n = row.shape[1]
    eye = lax.broadcasted_iota(I32, (n, n), 0) == lax.broadcasted_iota(I32, (n, n), 1)
    return jnp.sum(jnp.where(eye, row, jnp.zeros_like(row)), axis=1, keepdims=True)


def _gather_ctx_kernel(idx_ref, h_ref, x_ref):
    nslot = N_EXPERTS * CAP_CTX
    hit = lax.broadcasted_iota(I32, (nslot, SEQ), 1) == _row_to_col(idx_ref[0])
    xs = _mm(jnp.where(hit, 1.0, 0.0).astype(BF16), h_ref[0])
    x_ref[...] = xs.astype(BF16).reshape(N_EXPERTS, CAP_CTX, D_MODEL)


def _gather_ctx(idx_row, h2):
    nslot = N_EXPERTS * CAP_CTX
    return pl.pallas_call(
        _gather_ctx_kernel,
        grid=(BATCH,),
        in_specs=[pl.BlockSpec((1, 1, nslot), lambda s: (s, 0, 0)),
                  pl.BlockSpec((1, SEQ, D_MODEL), lambda s: (s, 0, 0))],
        out_specs=pl.BlockSpec((N_EXPERTS, CAP_CTX, D_MODEL), lambda s: (0, s, 0)),
        out_shape=jax.ShapeDtypeStruct((N_EXPERTS, SLOTS_CTX, D_MODEL), BF16),
        compiler_params=_params(("arbitrary",)),
        name="gather_ctx",
    )(idx_row, h2)


def _gather_lat_kernel(idx_ref, h_ref, x_ref):
    first = GATHER_GROUP * pl.program_id(1)
    col = jnp.concatenate([_row_to_col(idx_ref[0, pl.ds(first + i, 1), :]) for i in range(GATHER_GROUP)],
                          axis=0)
    hit = lax.broadcasted_iota(I32, (GATHER_GROUP * CAP_LAT, DEC_SEQ), 1) == col
    xs = _mm(jnp.where(hit, 1.0, 0.0).astype(BF16), h_ref[0])
    x_ref[...] = xs.astype(BF16).reshape(GATHER_GROUP, CAP_LAT, D_MODEL)


def _gather_lat(idx, h2):
    return pl.pallas_call(
        _gather_lat_kernel,
        grid=(DEC_BATCH, N_EXPERTS // GATHER_GROUP),
        in_specs=[pl.BlockSpec((1, N_EXPERTS, CAP_LAT), lambda s, e: (s, 0, 0)),
                  pl.BlockSpec((1, DEC_SEQ, D_MODEL), lambda s, e: (s, 0, 0))],
        out_specs=pl.BlockSpec((GATHER_GROUP, CAP_LAT, D_MODEL), lambda s, e: (e, s, 0)),
        out_shape=jax.ShapeDtypeStruct((N_EXPERTS, SLOTS_LAT, D_MODEL), BF16),
        compiler_params=_params(("arbitrary", "arbitrary")),
        name="gather_lat",
    )(idx, h2)


def _ffn_items():
    ups = [("up", lo, min(UP_CHUNK, D_FF - lo)) for lo in range(0, D_FF, UP_CHUNK)]
    return ups + [("dn", n) for n in range(D_MODEL // OUT_TILE)]


def _ffn_kernel(xc_ref, xl_ref, gc_ref, gl_ref, wg_hbm, wu_hbm, wd_hbm, yc_ref, yl_ref,
                h_scr, up_buf, dn_buf, up_sem, dn_sem):
    e = pl.program_id(0)
    last = pl.num_programs(0) - 1
    items = _ffn_items()

    def copies(ex, i):
        it = items[i]
        if it[0] == "up":
            _, lo, w = it
            slot = i % 2
            return [pltpu.make_async_copy(src.at[ex, :, pl.ds(lo, w)], up_buf.at[slot, k, :, pl.ds(0, w)],
                                          up_sem.at[slot, k]) for k, src in enumerate((wg_hbm, wu_hbm))]
        slot = it[1] % 2
        return [pltpu.make_async_copy(wd_hbm.at[ex, :, pl.ds(OUT_TILE * it[1], OUT_TILE)], dn_buf.at[slot],
                                      dn_sem.at[slot])]

    def start(ex, i):
        for cp in copies(ex, i):
            cp.start()

    def swiglu_cols(wg, wu, c0):
        wgb = wg.astype(BF16)
        wub = wu.astype(BF16)
        for i, x_ref in enumerate((xc_ref, xl_ref)):
            x = x_ref[0]
            a = _mm(x, wgb)
            u = _mm(x, wub)
            h = a * (1.0 / (1.0 + jnp.exp(-a))) * u
            h_scr[SLOTS_CTX * i:SLOTS_CTX * (i + 1), c0:c0 + FF_TILE] = h.astype(BF16)

    @pl.when(e == 0)
    def _():
        start(e, 0)
        start(e, 1)

    for i, it in enumerate(items):
        for cp in copies(e, i):
            cp.wait()
        if it[0] == "up":
            _, lo, w = it
            for c in range(0, w, FF_TILE):
                swiglu_cols(up_buf[i % 2, 0, :, c:c + FF_TILE], up_buf[i % 2, 1, :, c:c + FF_TILE], lo + c)
        else:
            n = it[1]
            wdb = dn_buf[n % 2].astype(BF16)
            for r, (g_ref, y_ref) in enumerate(((gc_ref, yc_ref), (gl_ref, yl_ref))):
                nset = g_ref.shape[0] // N_EXPERTS
                gate = jnp.concatenate(
                    [_row_to_col(g_ref[pl.ds(N_EXPERTS * s + e, 1), :][:, :SLOTS_CTX // nset])
                     for s in range(nset)], axis=0)
                y = _mm(h_scr[SLOTS_CTX * r:SLOTS_CTX * (r + 1), :], wdb)
                y_ref[0, :, OUT_TILE * n:OUT_TILE * (n + 1)] = (y * gate).astype(BF16)
        nxt = i + 2
        if nxt < len(items):
            start(e, nxt)
        else:
            @pl.when(e < last)
            def _(nxt=nxt):
                start(e + 1, nxt - len(items))


def _expert_ffn(xc, xl, gc, gl, w_gate, w_up, w_down):
    assert SLOTS_CTX == SLOTS_LAT and UP_CHUNK % FF_TILE == 0 and D_MODEL % OUT_TILE == 0
    assert len(_ffn_items()) >= 2 and (D_FF // UP_CHUNK + (D_FF % UP_CHUNK > 0)) % 2 == 0
    xs = lambda n: pl.BlockSpec((1, n, D_MODEL), lambda e: (e, 0, 0))
    gs = lambda g: pl.BlockSpec(g.shape, lambda e: (0, 0))
    hbm = pl.BlockSpec(memory_space=pl.ANY)
    return pl.pallas_call(
        _ffn_kernel,
        grid=(N_EXPERTS,),
        in_specs=[xs(SLOTS_CTX), xs(SLOTS_LAT), gs(gc), gs(gl), hbm, hbm, hbm],
        out_specs=[xs(SLOTS_CTX), xs(SLOTS_LAT)],
        out_shape=[jax.ShapeDtypeStruct((N_EXPERTS, SLOTS_CTX, D_MODEL), BF16),
                   jax.ShapeDtypeStruct((N_EXPERTS, SLOTS_LAT, D_MODEL), BF16)],
        scratch_shapes=[pltpu.VMEM((SLOTS_CTX + SLOTS_LAT, D_FF), BF16),
                        pltpu.VMEM((2, 2, D_MODEL, UP_CHUNK), F32),
                        pltpu.VMEM((2, D_FF, OUT_TILE), F32),
                        pltpu.SemaphoreType.DMA((2, 2)),
                        pltpu.SemaphoreType.DMA((2,))],
        compiler_params=_params(("arbitrary",)),
        name="expert_ffn",
    )(xc, xl, gc, gl, w_gate, w_up, w_down)


def _combine_ctx_kernel(y_ref, idx_ref, x1_ref, mod_ref, lng_ref, lnb_ref, o_ref):
    g2 = _mod_rows(mod_ref, 0)[5]
    nslot = N_EXPERTS * CAP_CTX
    hit = lax.broadcasted_iota(I32, (SEQ, nslot), 0) == idx_ref[0]
    y = y_ref[...].reshape(nslot, D_MODEL)
    f = _mm(jnp.where(hit, 1.0, 0.0).astype(BF16), y)
    o_ref[0] = _layer_norm(DEEPNORM_ALPHA * x1_ref[0] + g2 * f, lng_ref[...], lnb_ref[...])


def _combine_ctx(yc, idx_row, x1, mod, ln_g, ln_b):
    full = lambda shape: pl.BlockSpec(shape, lambda s: (0,) * len(shape))
    return pl.pallas_call(
        _combine_ctx_kernel,
        grid=(BATCH,),
        in_specs=[pl.BlockSpec((N_EXPERTS, CAP_CTX, D_MODEL), lambda s: (0, s, 0)),
                  pl.BlockSpec((1, 1, N_EXPERTS * CAP_CTX), lambda s: (s, 0, 0)),
                  pl.BlockSpec((1, SEQ, D_MODEL), lambda s: (s, 0, 0)),
                  full((8, 6 * D_MODEL)), full((1, D_MODEL)), full((1, D_MODEL))],
        out_specs=pl.BlockSpec((1, SEQ, D_MODEL), lambda s: (s, 0, 0)),
        out_shape=jax.ShapeDtypeStruct((BATCH, SEQ, D_MODEL), F32),
        compiler_params=_params(("arbitrary",)),
        name="combine_ctx",
    )(yc, idx_row, x1, mod, ln_g, ln_b)


def _combine_lat_kernel(y_ref, idx_ref, x1_ref, mod_ref, lng_ref, lnb_ref, o_ref):
    s = pl.program_id(0)
    g2 = _mod_rows(mod_ref, 1 + s)[5]
    tok = lax.broadcasted_iota(I32, (ROW_TILE, CAP_LAT), 0) + ROW_TILE * pl.program_id(1)
    f = jnp.zeros((ROW_TILE, D_MODEL), F32)
    for e in range(N_EXPERTS):
        hit = tok == idx_ref[0, e:e + 1, :]
        f = f + _mm(jnp.where(hit, 1.0, 0.0).astype(BF16), y_ref[e])
    o_ref[0] = _layer_norm(DEEPNORM_ALPHA * x1_ref[0] + g2 * f, lng_ref[...], lnb_ref[...])


def _combine_lat(yl, idx, x1, mod, ln_g, ln_b):
    nt = DEC_SEQ // ROW_TILE
    full = lambda shape: pl.BlockSpec(shape, lambda s, t: (0,) * len(shape))
    return pl.pallas_call(
        _combine_lat_kernel,
        grid=(DEC_BATCH, nt),
        in_specs=[pl.BlockSpec((N_EXPERTS, CAP_LAT, D_MODEL), lambda s, t: (0, s, 0)),
                  pl.BlockSpec((1, N_EXPERTS, CAP_LAT), lambda s, t: (s, 0, 0)),
                  pl.BlockSpec((1, ROW_TILE, D_MODEL), lambda s, t: (s, t, 0)),
                  full((8, 6 * D_MODEL)), full((1, D_MODEL)), full((1, D_MODEL))],
        out_specs=pl.BlockSpec((1, ROW_TILE, D_MODEL), lambda s, t: (s, t, 0)),
        out_shape=jax.ShapeDtypeStruct((DEC_BATCH, DEC_SEQ, D_MODEL), F32),
        compiler_params=_params(("arbitrary", "arbitrary")),
        name="combine_lat",
    )(yl, idx, x1, mod, ln_g, ln_b)


def _rope_tables():
    t = np.arange(DEC_SEQ)
    row = (t // GRID_W).astype(np.float64)
    col = (t % GRID_W).astype(np.float64)
    half = HEAD_DIM // 2
    inv = ROPE_THETA ** (-np.arange(0, half, 2, dtype=np.float64) / half)
    ang_r = row[:, None] * inv[None, :]
    ang_c = col[:, None] * inv[None, :]
    ang = np.concatenate([ang_r, ang_r, ang_c, ang_c], axis=-1)
    cos = np.tile(np.cos(ang), (1, 2))
    sin = np.tile(np.sin(ang), (1, 2))
    first = (np.arange(LANES) % 32) < 16
    return jnp.asarray(cos, F32), jnp.asarray(np.where(first[None, :], -sin, sin), F32)


def kernel(x_prompt, x_sample, cache_na_k, cache_na_v, cache_diff_k, cache_diff_v, c, c_ctx, w_ada, b_ada, w_in, w_out, na_rel_bias, lambda_q1, lambda_k1, lambda_q2, lambda_k2, subln_g, ln1_g, ln1_b, ln2_g, ln2_b, w_router, w_gate, w_up, w_down):
    l = 0
    cvec = jnp.concatenate([c_ctx[None, :], c, jnp.zeros((8 - 1 - DEC_BATCH, D_MODEL), F32)], axis=0)
    mod = _modulation(cvec, w_ada[l], b_ada[l][None, :])
    lamv = jnp.stack([lambda_q1[l], lambda_k1[l], lambda_q2[l], lambda_k2[l]], axis=0)
    w_in_b = w_in[l].astype(BF16)
    w_out_b = w_out[l].astype(BF16)
    sub = subln_g[l][None, :]
    wrt = w_router[l].T

    tr = lambda a: jnp.swapaxes(a, -1, -2)

    x1c, h2c, affc, na_k_t, na_v_t, diff_k_t, new_diff_v = _context_block(
        x_prompt, mod, lamv, w_in_b, w_out_b, sub, ln1_g[l][None, :], ln1_b[l][None, :], wrt)

    cos, sin = _rope_tables()
    naq, nak, nav, dfq, dfk, dfv = _latent_qkv(x_sample, mod, w_in_b, cos, sin)
    tbl = _na_bias_tables(na_rel_bias[l].reshape(-1))
    na_o = _latent_na(naq, nak, nav, tr(cache_na_k), tr(cache_na_v), tbl)
    df_o = _latent_diff(dfq, dfk, dfv, tr(cache_diff_k), cache_diff_v, lamv, sub)
    x1l, h2l, affl = _latent_out(x_sample, na_o, df_o, mod, w_out_b,
                                 ln1_g[l][None, :], ln1_b[l][None, :], wrt)

    idxc, gatec = _route(affc.reshape(BATCH * N_EXPERTS, SEQ), CAP_CTX)
    idxl, gatel = _route(affl.reshape(DEC_BATCH * N_EXPERTS, DEC_SEQ), CAP_LAT)
    idxc = idxc.reshape(BATCH, 1, N_EXPERTS * CAP_CTX)
    idxl = idxl.reshape(DEC_BATCH, N_EXPERTS, CAP_LAT)
    xc = _gather_ctx(idxc, h2c)
    xl = _gather_lat(idxl, h2l)
    yc, yl = _expert_ffn(xc, xl, gatec, gatel, w_gate[l], w_up[l], w_down[l])
    y_prompt = _combine_ctx(yc, idxc, x1c, mod, ln2_g[l][None, :], ln2_b[l][None, :])
    y_sample = _combine_lat(yl, idxl, x1l, mod, ln2_g[l][None, :], ln2_b[l][None, :])
    return (y_prompt, y_sample, tr(na_k_t), tr(na_v_t), tr(diff_k_t), new_diff_v)
```

```python
import functools
import math

import jax
import jax.numpy as jnp
import numpy as np
from jax import lax
from jax.experimental import pallas as pl
from jax.experimental.pallas import tpu as pltpu

F32 = jnp.float32
BF16 = jnp.bfloat16
I32 = jnp.int32

D_MODEL = 1024
BATCH = 16
SEQ = 256
DEC_BATCH = 2
DEC_SEQ = 2048
PAST_LEN = 256
GRID_W = 64
GRID_ROWS = DEC_SEQ // GRID_W
HEAD_DIM = 64
NA_HEADS = 8
DIFF_HEADS = 4
NA_WIDTH = NA_HEADS * HEAD_DIM
DIFF_WIDTH = DIFF_HEADS * 2 * HEAD_DIM
QKV_WIDTH = 3 * NA_WIDTH + 3 * DIFF_WIDTH
NA_WIN_H = 8
NA_WIN_W = 16
N_REL_H = 2 * NA_WIN_H - 1
N_REL_W = 2 * NA_WIN_W - 1
N_EXPERTS = 16
EC_CAPACITY_FACTOR = 2
D_FF = 2816
ROPE_THETA = 10000.0
NORM_EPS = 1e-5
NEG_INF = -1e30
DEPTH = 1
DEEPNORM_ALPHA = (2.0 * DEPTH) ** 0.25
LAMBDA_INIT = 0.8 - 0.6 * math.exp(-0.3 * 0)
SCALE = HEAD_DIM ** -0.5

OFF_NAQ = 0
OFF_NAK = NA_WIDTH
OFF_NAV = 2 * NA_WIDTH
OFF_DFQ = 3 * NA_WIDTH
OFF_DFK = 3 * NA_WIDTH + DIFF_WIDTH
OFF_DFV = 3 * NA_WIDTH + 2 * DIFF_WIDTH

CAP_CTX = EC_CAPACITY_FACTOR * SEQ // N_EXPERTS
CAP_LAT = EC_CAPACITY_FACTOR * DEC_SEQ // N_EXPERTS
SLOTS_CTX = BATCH * CAP_CTX
SLOTS_LAT = DEC_BATCH * CAP_LAT

ROW_TILE = 512
NA_QUAD = 4
NA_WIN_ROWS = 12
NA_GROUP = 8
FF_TILE = 256
UP_CHUNK = 512
OUT_TILE = 512
LANES = 128
GATHER_GROUP = 2
DIFF_ROWS = 1024
Q_ROWS = 256

VMEM_LIMIT = 48 * 1024 * 1024


def _params(sem, vmem=VMEM_LIMIT, flags=None):
    return pltpu.CompilerParams(dimension_semantics=sem, vmem_limit_bytes=vmem, flags=flags)


def _mm(a, b):
    return jnp.dot(a, b, preferred_element_type=F32)


def _nt(a, b):
    return lax.dot_general(a, b, (((1,), (1,)), ((), ())), preferred_element_type=F32)


def _split(a):
    hi = a.astype(BF16)
    lo = (a - hi.astype(F32)).astype(BF16)
    return hi, lo


def _mm3(a, b):
    ah, al = _split(a)
    bh, bl = _split(b)
    return _mm(ah, bh) + _mm(al, bh) + _mm(ah, bl)


def _nt3(a, b):
    ah, al = _split(a)
    bh, bl = _split(b)
    return _nt(ah, bh) + _nt(al, bh) + _nt(ah, bl)


def _layer_norm(y, g, b):
    mu = jnp.mean(y, axis=-1, keepdims=True)
    d = y - mu
    var = jnp.mean(d * d, axis=-1, keepdims=True)
    return d * lax.rsqrt(var + NORM_EPS) * g + b


def _lam(l_ref):
    l = l_ref[...]
    a = jnp.sum(l[0:1] * l[1:2], axis=-1, keepdims=True)
    b = jnp.sum(l[2:3] * l[3:4], axis=-1, keepdims=True)
    return jnp.exp(a) - jnp.exp(b) + LAMBDA_INIT


def _mod_rows(mod_ref, row):
    return [mod_ref[pl.ds(row, 1), i * D_MODEL:(i + 1) * D_MODEL] for i in range(6)]


def _sub_norm(o, sub):
    ms = jnp.mean(o * o, axis=-1, keepdims=True)
    return o * lax.rsqrt(ms + NORM_EPS) * sub * (1.0 - LAMBDA_INIT)


def _router_aff_t(h2, wrt):
    lg = _nt3(wrt, h2)
    m = jnp.max(lg, axis=0, keepdims=True)
    e = jnp.exp(lg - m)
    return e / jnp.sum(e, axis=0, keepdims=True)


def _mod_kernel(c_ref, w_ref, b_ref, o_ref):
    c = c_ref[...]
    s = c * (1.0 / (1.0 + jnp.exp(-c)))
    o_ref[...] = _mm3(s, w_ref[...]) + b_ref[...]


def _modulation(cvec, w_ada, b_ada):
    ncol = 6 * D_MODEL
    return pl.pallas_call(
        _mod_kernel,
        grid=(6,),
        in_specs=[pl.BlockSpec((8, D_MODEL), lambda j: (0, 0)),
                  pl.BlockSpec((D_MODEL, D_MODEL), lambda j: (0, j)),
                  pl.BlockSpec((1, D_MODEL), lambda j: (0, j))],
        out_specs=pl.BlockSpec((8, D_MODEL), lambda j: (0, j)),
        out_shape=jax.ShapeDtypeStruct((8, ncol), F32),
        compiler_params=_params(("arbitrary",)),
        name="modulation",
    )(cvec, w_ada, b_ada)


def _ctx_kernel(x_ref, mod_ref, lam_ref, win_ref, wout_ref, sub_ref, lng_ref, lnb_ref, wrt_ref,
                x1_ref, h2_ref, aff_ref, nak_ref, nav_ref, dfk_ref, dfv_ref,
                qkv_scr, o_scr, s_scr, e_scr):
    x = x_ref[0]
    sh1, sc1, g1, sh2, sc2, g2 = _mod_rows(mod_ref, 0)
    h = (x * (1.0 + sc1) + sh1).astype(BF16)
    qkv_scr[...] = _mm(h, win_ref[...])

    kt_na = qkv_scr[:, OFF_NAK:OFF_NAK + NA_WIDTH].T
    kt_df = qkv_scr[:, OFF_DFK:OFF_DFK + DIFF_WIDTH].T
    nak_ref[0, 0] = kt_na.reshape(NA_HEADS, HEAD_DIM, SEQ)
    nav_ref[0, 0] = qkv_scr[:, OFF_NAV:OFF_NAV + NA_WIDTH].T.reshape(NA_HEADS, HEAD_DIM, SEQ)
    dfk_ref[0, 0] = kt_df.reshape(DIFF_HEADS, 2, HEAD_DIM, SEQ)
    for hh in range(DIFF_HEADS):
        c0 = OFF_DFV + 2 * HEAD_DIM * hh
        dfv_ref[0, 0, hh] = qkv_scr[:, c0:c0 + 2 * HEAD_DIM]

    kt = jnp.concatenate([kt_na, kt_df], axis=0).astype(BF16)
    q_cols = [OFF_NAQ + HEAD_DIM * i for i in range(NA_HEADS)] + [
        OFF_DFQ + HEAD_DIM * i for i in range(2 * DIFF_HEADS)]
    nmap = len(q_cols)
    for i, c in enumerate(q_cols):
        q = (qkv_scr[:, c:c + HEAD_DIM] * SCALE).astype(BF16)
        s_scr[i] = _mm(q, kt[HEAD_DIM * i:HEAD_DIM * (i + 1), :])
    s = s_scr[...]
    e = jnp.exp(s - jnp.max(s, axis=-1, keepdims=True))
    rl = 1.0 / jnp.sum(e, axis=-1, keepdims=True)
    e_scr[...] = e.astype(BF16)

    for hh in range(NA_HEADS):
        v = qkv_scr[:, OFF_NAV + HEAD_DIM * hh:OFF_NAV + HEAD_DIM * (hh + 1)].astype(BF16)
        o_scr[:, HEAD_DIM * hh:HEAD_DIM * (hh + 1)] = (_mm(e_scr[hh], v) * rl[hh]).astype(BF16)
    lam = _lam(lam_ref)
    sub = sub_ref[...]
    for hh in range(DIFF_HEADS):
        i1 = NA_HEADS + 2 * hh
        c0 = OFF_DFV + 2 * HEAD_DIM * hh
        v = qkv_scr[:, c0:c0 + 2 * HEAD_DIM].astype(BF16)
        o = _mm(e_scr[i1], v) * rl[i1] - _mm(e_scr[i1 + 1], v) * (lam * rl[i1 + 1])
        c1 = NA_WIDTH + 2 * HEAD_DIM * hh
        o_scr[:, c1:c1 + 2 * HEAD_DIM] = _sub_norm(o, sub).astype(BF16)

    a = _mm(o_scr[...], wout_ref[...])
    x1 = _layer_norm(DEEPNORM_ALPHA * x + g1 * a, lng_ref[...], lnb_ref[...])
    x1_ref[0] = x1
    h2 = x1 * (1.0 + sc2) + sh2

    h2_ref[0] = h2.astype(BF16)
    aff_ref[0] = _router_aff_t(h2, wrt_ref[...])


def _context_block(x, mod, lamv, w_in, w_out, sub, ln_g, ln_b, wrt):
    full = lambda shape: pl.BlockSpec(shape, lambda b: (0,) * len(shape))
    return pl.pallas_call(
        _ctx_kernel,
        grid=(BATCH,),
        in_specs=[pl.BlockSpec((1, SEQ, D_MODEL), lambda b: (b, 0, 0)),
                  full((8, 6 * D_MODEL)), full((4, HEAD_DIM)),
                  full((D_MODEL, QKV_WIDTH)), full((D_MODEL, D_MODEL)),
                  full((1, 2 * HEAD_DIM)), full((1, D_MODEL)), full((1, D_MODEL)),
                  full((N_EXPERTS, D_MODEL))],
        out_specs=[pl.BlockSpec((1, SEQ, D_MODEL), lambda b: (b, 0, 0)),
                   pl.BlockSpec((1, SEQ, D_MODEL), lambda b: (b, 0, 0)),
                   pl.BlockSpec((1, N_EXPERTS, SEQ), lambda b: (b, 0, 0)),
                   pl.BlockSpec((1, 1, NA_HEADS, HEAD_DIM, SEQ), lambda b: (b, 0, 0, 0, 0)),
                   pl.BlockSpec((1, 1, NA_HEADS, HEAD_DIM, SEQ), lambda b: (b, 0, 0, 0, 0)),
                   pl.BlockSpec((1, 1, DIFF_HEADS, 2, HEAD_DIM, SEQ), lambda b: (b, 0, 0, 0, 0, 0)),
                   pl.BlockSpec((1, 1, DIFF_HEADS, SEQ, 2 * HEAD_DIM), lambda b: (b, 0, 0, 0, 0))],
        out_shape=[jax.ShapeDtypeStruct((BATCH, SEQ, D_MODEL), F32),
                   jax.ShapeDtypeStruct((BATCH, SEQ, D_MODEL), BF16),
                   jax.ShapeDtypeStruct((BATCH, N_EXPERTS, SEQ), F32),
                   jax.ShapeDtypeStruct((BATCH, 1, NA_HEADS, HEAD_DIM, SEQ), F32),
                   jax.ShapeDtypeStruct((BATCH, 1, NA_HEADS, HEAD_DIM, SEQ), F32),
                   jax.ShapeDtypeStruct((BATCH, 1, DIFF_HEADS, 2, HEAD_DIM, SEQ), F32),
                   jax.ShapeDtypeStruct((BATCH, 1, DIFF_HEADS, SEQ, 2 * HEAD_DIM), F32)],
        scratch_shapes=[pltpu.VMEM((SEQ, QKV_WIDTH), F32), pltpu.VMEM((SEQ, D_MODEL), BF16),
                        pltpu.VMEM((NA_HEADS + 2 * DIFF_HEADS, SEQ, SEQ), F32),
                        pltpu.VMEM((NA_HEADS + 2 * DIFF_HEADS, SEQ, SEQ), BF16)],
        compiler_params=_params(("arbitrary",)),
        name="context_block",
    )(x, mod, lamv, w_in, w_out, sub, ln_g, ln_b, wrt)


def _lat_qkv_kernel(x_ref, mod_ref, win_ref, cos_ref, sin_ref,
                    naq_ref, nak_ref, nav_ref, dfq_ref, dfk_ref, dfv_ref, qkv_scr):
    b = pl.program_id(0)
    x = x_ref[0]
    sh1, sc1 = _mod_rows(mod_ref, 1 + b)[:2]
    h = (x * (1.0 + sc1) + sh1).astype(BF16)
    qkv_scr[...] = _mm(h, win_ref[...])

    for hh in range(NA_HEADS):
        lo, hi = HEAD_DIM * hh, HEAD_DIM * (hh + 1)
        naq_ref[0, hh] = (qkv_scr[:, OFF_NAQ + lo:OFF_NAQ + hi] * SCALE).astype(BF16)
        nak_ref[0, hh] = qkv_scr[:, OFF_NAK + lo:OFF_NAK + hi].astype(BF16)
        nav_ref[0, hh] = qkv_scr[:, OFF_NAV + lo:OFF_NAV + hi].astype(BF16)

    cos = cos_ref[...]
    sin = sin_ref[...]
    lane = lax.broadcasted_iota(I32, (ROW_TILE, LANES), 1)
    first = (lane & 31) < 16

    def rope(t):
        rot = jnp.where(first, pltpu.roll(t, LANES - 16, 1), pltpu.roll(t, 16, 1))
        return t * cos + rot * sin

    for hh in range(DIFF_HEADS):
        lo = 2 * HEAD_DIM * hh
        q = rope(qkv_scr[:, OFF_DFQ + lo:OFF_DFQ + lo + LANES]) * SCALE
        k = rope(qkv_scr[:, OFF_DFK + lo:OFF_DFK + lo + LANES])
        for m in range(2):
            dfq_ref[0, hh, m] = q[:, HEAD_DIM * m:HEAD_DIM * (m + 1)].astype(BF16)
        dfk_ref[0, hh] = k.T.astype(BF16).reshape(2, HEAD_DIM, ROW_TILE)
        dfv_ref[0, hh] = qkv_scr[:, OFF_DFV + lo:OFF_DFV + lo + LANES].astype(BF16)


def _latent_qkv(x, mod, w_in, cos, sin):
    nt = DEC_SEQ // ROW_TILE
    full = lambda shape: pl.BlockSpec(shape, lambda b, t: (0,) * len(shape))
    hs = lambda nh: pl.BlockSpec((1, nh, ROW_TILE, HEAD_DIM), lambda b, t: (b, 0, t, 0))
    return pl.pallas_call(
        _lat_qkv_kernel,
        grid=(DEC_BATCH, nt),
        in_specs=[pl.BlockSpec((1, ROW_TILE, D_MODEL), lambda b, t: (b, t, 0)),
                  full((8, 6 * D_MODEL)), full((D_MODEL, QKV_WIDTH)),
                  pl.BlockSpec((ROW_TILE, LANES), lambda b, t: (t, 0)),
                  pl.BlockSpec((ROW_TILE, LANES), lambda b, t: (t, 0))],
        out_specs=[hs(NA_HEADS), hs(NA_HEADS), hs(NA_HEADS),
                   pl.BlockSpec((1, DIFF_HEADS, 2, ROW_TILE, HEAD_DIM), lambda b, t: (b, 0, 0, t, 0)),
                   pl.BlockSpec((1, DIFF_HEADS, 2, HEAD_DIM, ROW_TILE), lambda b, t: (b, 0, 0, 0, t)),
                   pl.BlockSpec((1, DIFF_HEADS, ROW_TILE, 2 * HEAD_DIM), lambda b, t: (b, 0, t, 0))],
        out_shape=[jax.ShapeDtypeStruct((DEC_BATCH, NA_HEADS, DEC_SEQ, HEAD_DIM), BF16)] * 3
        + [jax.ShapeDtypeStruct((DEC_BATCH, DIFF_HEADS, 2, DEC_SEQ, HEAD_DIM), BF16),
           jax.ShapeDtypeStruct((DEC_BATCH, DIFF_HEADS, 2, HEAD_DIM, DEC_SEQ), BF16),
           jax.ShapeDtypeStruct((DEC_BATCH, DIFF_HEADS, DEC_SEQ, 2 * HEAD_DIM), BF16)],
        scratch_shapes=[pltpu.VMEM((ROW_TILE, QKV_WIDTH), F32)],
        compiler_params=_params(("arbitrary", "arbitrary")),
        name="latent_qkv",
    )(x, mod, w_in, cos, sin)


def _na_row_offset(kind, qr, kr):
    if kind == 0:
        return kr - qr + 7 if kr < NA_WIN_H else None
    if kind == 1:
        return kr - qr + 3 if qr <= kr < qr + NA_WIN_H else None
    return kr - qr - 1 if NA_WIN_ROWS - NA_WIN_H <= kr else None


def _bias_kernel(rpb_ref, o_ref):
    h = pl.program_id(0)
    base = h * (N_REL_H * N_REL_W)
    wq = lax.broadcasted_iota(I32, (GRID_W, LANES), 0)
    c = lax.broadcasted_iota(I32, (GRID_W, LANES), 1)
    wk = c & (GRID_W - 1)
    right = c >= GRID_W
    c0 = jnp.clip(wq - NA_WIN_W // 2, 0, GRID_W - NA_WIN_W)
    in_win = (wk >= c0) & (wk < c0 + NA_WIN_W)
    cache = {}
    strips = {}

    def strip(r):
        if r not in strips:
            c8 = lax.broadcasted_iota(I32, (8, LANES), 1)
            d = jnp.where(c8 < GRID_W, c8, c8 - LANES)
            co_of = jnp.clip(d, -(NA_WIN_W - 1), NA_WIN_W - 1) + (NA_WIN_W - 1)
            acc = jnp.zeros((8, LANES), F32)
            for co in range(N_REL_W):
                acc = jnp.where(co_of == co, rpb_ref[base + r * N_REL_W + co], acc)
            strips[r] = jnp.concatenate([acc] * (GRID_W // 8), axis=0)
        return strips[r]

    def tile(rl, rr):
        if (rl, rr) in cache:
            return cache[(rl, rr)]
        if rl is None and rr is None:
            t = jnp.full((GRID_W, LANES), NEG_INF, F32)
        else:
            zero = jnp.zeros((GRID_W, LANES), F32)
            tl = pltpu.roll(strip(rl), 0, 1, stride=1, stride_axis=0) if rl is not None else zero
            tr = pltpu.roll(strip(rr), GRID_W, 1, stride=1, stride_axis=0) if rr is not None else zero
            acc = jnp.where(right, tr, tl)
            ok = in_win
            if rl is None:
                ok = ok & right
            if rr is None:
                ok = ok & jnp.logical_not(right)
            t = jnp.where(ok, acc, NEG_INF)
        cache[(rl, rr)] = t
        return t

    for kind in range(3):
        for qr in range(NA_QUAD):
            row = [tile(_na_row_offset(kind, qr, 2 * p), _na_row_offset(kind, qr, 2 * p + 1))
                   for p in range(NA_WIN_ROWS // 2)]
            o_ref[0, kind, GRID_W * qr:GRID_W * (qr + 1), :] = jnp.concatenate(row, axis=1)


def _na_bias_tables(rpb_flat):
    return pl.pallas_call(
        _bias_kernel,
        grid=(NA_HEADS,),
        in_specs=[pl.BlockSpec(memory_space=pltpu.SMEM)],
        out_specs=pl.BlockSpec((1, 3, NA_QUAD * GRID_W, NA_WIN_ROWS * GRID_W), lambda h: (h, 0, 0, 0)),
        out_shape=jax.ShapeDtypeStruct((NA_HEADS, 3, NA_QUAD * GRID_W, NA_WIN_ROWS * GRID_W), F32),
        compiler_params=_params(("arbitrary",)),
        name="na_bias_tables",
    )(rpb_flat)


def _na_kernel(q_ref, k_ref, v_ref, ck_ref, cv_ref, tbl_ref, o_ref):
    g = pl.program_id(2)
    u0 = jnp.clip(NA_QUAD * g - NA_QUAD, 0, GRID_ROWS - NA_WIN_ROWS)
    start = pl.multiple_of(u0 * GRID_W, GRID_W)
    nwin = NA_WIN_ROWS * GRID_W
    for j in range(NA_GROUP):
        q = q_ref[0, j]
        kw = k_ref[0, j, pl.ds(start, nwin), :]
        vw = v_ref[0, j, pl.ds(start, nwin), :]
        kct = ck_ref[0, 0, j].astype(BF16)
        vct = cv_ref[0, 0, j].astype(BF16)
        sl = _nt(q, kw) + tbl_ref[j, 0]
        sc = _mm(q, kct)
        m = jnp.maximum(jnp.max(sl, axis=-1, keepdims=True), jnp.max(sc, axis=-1, keepdims=True))
        el = jnp.exp(sl - m)
        ec = jnp.exp(sc - m)
        l = jnp.sum(el, axis=-1, keepdims=True) + jnp.sum(ec, axis=-1, keepdims=True)
        o = (_mm(el.astype(BF16), vw) + _nt(ec.astype(BF16), vct)) / l
        o_ref[0, :, HEAD_DIM * j:HEAD_DIM * (j + 1)] = o.astype(BF16)


def _latent_na(naq, nak, nav, cache_k, cache_v, tbl):
    nq = GRID_ROWS // NA_QUAD
    rows = NA_QUAD * GRID_W
    kind = lambda g: jnp.minimum(g, 1) + g // (nq - 1)
    return pl.pallas_call(
        _na_kernel,
        grid=(DEC_BATCH, NA_HEADS // NA_GROUP, nq),
        in_specs=[pl.BlockSpec((1, NA_GROUP, rows, HEAD_DIM), lambda b, p, g: (b, p, g, 0)),
                  pl.BlockSpec((1, NA_GROUP, DEC_SEQ, HEAD_DIM), lambda b, p, g: (b, p, 0, 0)),
                  pl.BlockSpec((1, NA_GROUP, DEC_SEQ, HEAD_DIM), lambda b, p, g: (b, p, 0, 0)),
                  pl.BlockSpec((1, 1, NA_GROUP, HEAD_DIM, PAST_LEN), lambda b, p, g: (b, 0, p, 0, 0)),
                  pl.BlockSpec((1, 1, NA_GROUP, HEAD_DIM, PAST_LEN), lambda b, p, g: (b, 0, p, 0, 0)),
                  pl.BlockSpec((NA_GROUP, 1, rows, NA_WIN_ROWS * GRID_W), lambda b, p, g: (p, kind(g), 0, 0))],
        out_specs=pl.BlockSpec((1, rows, NA_GROUP * HEAD_DIM), lambda b, p, g: (b, g, p)),
        out_shape=jax.ShapeDtypeStruct((DEC_BATCH, DEC_SEQ, NA_WIDTH), BF16),
        compiler_params=_params(("arbitrary",) * 3),
        name="latent_na",
    )(naq, nak, nav, cache_k, cache_v, tbl)


def _diff_kernel(q_ref, kt_ref, v_ref, ck_ref, cv_ref, lam_ref, sub_ref, o_ref):
    lam = _lam(lam_ref)
    vc = cv_ref[0, 0, 0].astype(BF16)
    vl = v_ref[0, 0]
    for r in range(DIFF_ROWS // Q_ROWS):
        rows = slice(Q_ROWS * r, Q_ROWS * (r + 1))
        parts = []
        for m in range(2):
            q = q_ref[0, 0, m, rows, :]
            sc = _mm(q, ck_ref[0, 0, 0, m].astype(BF16))
            sl = _mm(q, kt_ref[0, 0, m])
            mx = jnp.maximum(jnp.max(sc, axis=-1, keepdims=True), jnp.max(sl, axis=-1, keepdims=True))
            ec = jnp.exp(sc - mx)
            el = jnp.exp(sl - mx)
            l = jnp.sum(ec, axis=-1, keepdims=True) + jnp.sum(el, axis=-1, keepdims=True)
            parts.append((_mm(ec.astype(BF16), vc) + _mm(el.astype(BF16), vl), l))
        (a1, l1), (a2, l2) = parts
        o = a1 * (1.0 / l1) - a2 * (lam / l2)
        o_ref[0, rows, :] = _sub_norm(o, sub_ref[...]).astype(BF16)


def _latent_diff(dfq, dfk, dfv, cache_k, cache_v, lamv, sub):
    nt = DEC_SEQ // DIFF_ROWS
    full = lambda shape: pl.BlockSpec(shape, lambda b, h, t: (0,) * len(shape))
    return pl.pallas_call(
        _diff_kernel,
        grid=(DEC_BATCH, DIFF_HEADS, nt),
        in_specs=[pl.BlockSpec((1, 1, 2, DIFF_ROWS, HEAD_DIM), lambda b, h, t: (b, h, 0, t, 0)),
                  pl.BlockSpec((1, 1, 2, HEAD_DIM, DEC_SEQ), lambda b, h, t: (b, h, 0, 0, 0)),
                  pl.BlockSpec((1, 1, DEC_SEQ, 2 * HEAD_DIM), lambda b, h, t: (b, h, 0, 0)),
                  pl.BlockSpec((1, 1, 1, 2, HEAD_DIM, PAST_LEN), lambda b, h, t: (b, 0, h, 0, 0, 0)),
                  pl.BlockSpec((1, 1, 1, PAST_LEN, 2 * HEAD_DIM), lambda b, h, t: (b, 0, h, 0, 0)),
                  full((4, HEAD_DIM)), full((1, 2 * HEAD_DIM))],
        out_specs=pl.BlockSpec((1, DIFF_ROWS, 2 * HEAD_DIM), lambda b, h, t: (b, t, h)),
        out_shape=jax.ShapeDtypeStruct((DEC_BATCH, DEC_SEQ, DIFF_WIDTH), BF16),
        compiler_params=_params(("arbitrary",) * 3),
        name="latent_diff",
    )(dfq, dfk, dfv, cache_k, cache_v, lamv, sub)


def _lat_out_kernel(x_ref, na_ref, df_ref, mod_ref, wout_ref, lng_ref, lnb_ref, wrt_ref,
                    x1_ref, h2_ref, aff_ref):
    b = pl.program_id(0)
    x = x_ref[0]
    _, _, g1, sh2, sc2, _ = _mod_rows(mod_ref, 1 + b)
    a = _mm(na_ref[0], wout_ref[0:NA_WIDTH, :]) + _mm(df_ref[0], wout_ref[NA_WIDTH:, :])
    x1 = _layer_norm(DEEPNORM_ALPHA * x + g1 * a, lng_ref[...], lnb_ref[...])
    x1_ref[0] = x1
    h2 = x1 * (1.0 + sc2) + sh2
    h2_ref[0] = h2.astype(BF16)
    aff_ref[0] = _router_aff_t(h2, wrt_ref[...])


def _latent_out(x, na_o, df_o, mod, w_out, ln_g, ln_b, wrt):
    nt = DEC_SEQ // ROW_TILE
    full = lambda shape: pl.BlockSpec(shape, lambda b, t: (0,) * len(shape))
    tok = lambda w: pl.BlockSpec((1, ROW_TILE, w), lambda b, t: (b, t, 0))
    return pl.pallas_call(
        _lat_out_kernel,
        grid=(DEC_BATCH, nt),
        in_specs=[tok(D_MODEL), tok(NA_WIDTH), tok(DIFF_WIDTH), full((8, 6 * D_MODEL)),
                  full((D_MODEL, D_MODEL)), full((1, D_MODEL)), full((1, D_MODEL)),
                  full((N_EXPERTS, D_MODEL))],
        out_specs=[tok(D_MODEL), tok(D_MODEL),
                   pl.BlockSpec((1, N_EXPERTS, ROW_TILE), lambda b, t: (b, 0, t))],
        out_shape=[jax.ShapeDtypeStruct((DEC_BATCH, DEC_SEQ, D_MODEL), F32),
                   jax.ShapeDtypeStruct((DEC_BATCH, DEC_SEQ, D_MODEL), BF16),
                   jax.ShapeDtypeStruct((DEC_BATCH, N_EXPERTS, DEC_SEQ), F32)],
        compiler_params=_params(("arbitrary", "arbitrary")),
        name="latent_out",
    )(x, na_o, df_o, mod, w_out, ln_g, ln_b, wrt)


def _ranks_first(va, ia, vb, ib):
    return (va > vb) | ((va == vb) & (ia < ib))


def _sort_rows_desc(aff):
    rows, n = aff.shape
    nb = n // LANES
    lane = lax.broadcasted_iota(I32, (rows, LANES), 1)
    v = [aff[:, LANES * b:LANES * (b + 1)] for b in range(nb)]
    ix = [lane + LANES * b for b in range(nb)]
    k = 2
    while k <= n:
        j = k // 2
        while j >= 1:
            if j >= LANES:
                sb = j // LANES
                for lo in range(nb):
                    if lo & sb:
                        continue
                    hi = lo | sb
                    f = _ranks_first(v[lo], ix[lo], v[hi], ix[hi])
                    if k < n and (LANES * lo) & k:
                        f = jnp.logical_not(f)
                    v[lo], v[hi] = jnp.where(f, v[lo], v[hi]), jnp.where(f, v[hi], v[lo])
                    ix[lo], ix[hi] = jnp.where(f, ix[lo], ix[hi]), jnp.where(f, ix[hi], ix[lo])
            else:
                upper = (lane & j) != 0
                if k < LANES:
                    flip_lanes = jnp.logical_xor(upper, (lane & k) != 0)
                for b in range(nb):
                    pv = jnp.where(upper, pltpu.roll(v[b], j, 1), pltpu.roll(v[b], LANES - j, 1))
                    pi = jnp.where(upper, pltpu.roll(ix[b], j, 1), pltpu.roll(ix[b], LANES - j, 1))
                    f = _ranks_first(v[b], ix[b], pv, pi)
                    if k < LANES:
                        flip = flip_lanes
                    elif k < n and (LANES * b) & k:
                        flip = jnp.logical_not(upper)
                    else:
                        flip = upper
                    keep = jnp.logical_xor(f, flip)
                    v[b] = jnp.where(keep, v[b], pv)
                    ix[b] = jnp.where(keep, ix[b], pi)
            j //= 2
        k *= 2
    return v, ix


def _route_kernel(aff_ref, idx_ref, gate_ref, *, cap, n):
    v, ix = _sort_rows_desc(aff_ref[...])
    if cap < LANES:
        idx_ref[...] = ix[0][:, :cap]
        gate_ref[...] = v[0]
    else:
        idx_ref[...] = jnp.concatenate(ix[:cap // LANES], axis=1)
        gate_ref[...] = jnp.concatenate(v[:cap // LANES], axis=1)


def _route(aff_t, cap):
    rows, n = aff_t.shape
    return pl.pallas_call(
        functools.partial(_route_kernel, cap=cap, n=n),
        out_shape=[jax.ShapeDtypeStruct((rows, cap), I32),
                   jax.ShapeDtypeStruct((rows, max(cap, LANES)), F32)],
        compiler_params=_params(None),
        name=f"route_{n}",
    )(aff_t)


def _row_to_col(row):
    n = row.shape[1]
    eye = lax.broadcasted_iota(I32, (n, n), 0) == lax.broadcasted_iota(I32, (n, n), 1)
    return jnp.sum(jnp.where(eye, row, jnp.zeros_like(row)), axis=1, keepdims=True)


def _gather_ctx_kernel(idx_ref, h_ref, x_ref):
    nslot = N_EXPERTS * CAP_CTX
    hit = lax.broadcasted_iota(I32, (nslot, SEQ), 1) == _row_to_col(idx_ref[0])
    xs = _mm(jnp.where(hit, 1.0, 0.0).astype(BF16), h_ref[0])
    x_ref[...] = xs.astype(BF16).reshape(N_EXPERTS, CAP_CTX, D_MODEL)


def _gather_ctx(idx_row, h2):
    nslot = N_EXPERTS * CAP_CTX
    return pl.pallas_call(
        _gather_ctx_kernel,
        grid=(BATCH,),
        in_specs=[pl.BlockSpec((1, 1, nslot), lambda s: (s, 0, 0)),
                  pl.BlockSpec((1, SEQ, D_MODEL), lambda s: (s, 0, 0))],
        out_specs=pl.BlockSpec((N_EXPERTS, CAP_CTX, D_MODEL), lambda s: (0, s, 0)),
        out_shape=jax.ShapeDtypeStruct((N_EXPERTS, SLOTS_CTX, D_MODEL), BF16),
        compiler_params=_params(("arbitrary",)),
        name="gather_ctx",
    )(idx_row, h2)


def _gather_lat_kernel(idx_ref, h_ref, x_ref):
    first = GATHER_GROUP * pl.program_id(1)
    col = jnp.concatenate([_row_to_col(idx_ref[0, pl.ds(first + i, 1), :]) for i in range(GATHER_GROUP)],
                          axis=0)
    hit = lax.broadcasted_iota(I32, (GATHER_GROUP * CAP_LAT, DEC_SEQ), 1) == col
    xs = _mm(jnp.where(hit, 1.0, 0.0).astype(BF16), h_ref[0])
    x_ref[...] = xs.astype(BF16).reshape(GATHER_GROUP, CAP_LAT, D_MODEL)


def _gather_lat(idx, h2):
    return pl.pallas_call(
        _gather_lat_kernel,
        grid=(DEC_BATCH, N_EXPERTS // GATHER_GROUP),
        in_specs=[pl.BlockSpec((1, N_EXPERTS, CAP_LAT), lambda s, e: (s, 0, 0)),
                  pl.BlockSpec((1, DEC_SEQ, D_MODEL), lambda s, e: (s, 0, 0))],
        out_specs=pl.BlockSpec((GATHER_GROUP, CAP_LAT, D_MODEL), lambda s, e: (e, s, 0)),
        out_shape=jax.ShapeDtypeStruct((N_EXPERTS, SLOTS_LAT, D_MODEL), BF16),
        compiler_params=_params(("arbitrary", "arbitrary")),
        name="gather_lat",
    )(idx, h2)


def _ffn_items():
    ups = [("up", lo, min(UP_CHUNK, D_FF - lo)) for lo in range(0, D_FF, UP_CHUNK)]
    return ups + [("dn", n) for n in range(D_MODEL // OUT_TILE)]


def _ffn_kernel(xc_ref, xl_ref, gc_ref, gl_ref, wg_hbm, wu_hbm, wd_hbm, yc_ref, yl_ref,
                h_scr, up_buf, dn_buf, up_sem, dn_sem):
    e = pl.program_id(0)
    last = pl.num_programs(0) - 1
    items = _ffn_items()

    def copies(ex, i):
        it = items[i]
        if it[0] == "up":
            _, lo, w = it
            slot = i % 2
            return [pltpu.make_async_copy(src.at[ex, :, pl.ds(lo, w)], up_buf.at[slot, k, :, pl.ds(0, w)],
                                          up_sem.at[slot, k]) for k, src in enumerate((wg_hbm, wu_hbm))]
        slot = it[1] % 2
        return [pltpu.make_async_copy(wd_hbm.at[ex, :, pl.ds(OUT_TILE * it[1], OUT_TILE)], dn_buf.at[slot],
                                      dn_sem.at[slot])]

    def start(ex, i):
        for k, cp in enumerate(copies(ex, i)):
            cp.start(priority=(k + i) % 2)

    def swiglu_cols(wg, wu, c0):
        wgb = wg.astype(BF16)
        wub = wu.astype(BF16)
        for i, x_ref in enumerate((xc_ref, xl_ref)):
            x = x_ref[0]
            a = _mm(x, wgb)
            u = _mm(x, wub)
            h = a * (1.0 / (1.0 + jnp.exp(-a))) * u
            h_scr[SLOTS_CTX * i:SLOTS_CTX * (i + 1), c0:c0 + FF_TILE] = h.astype(BF16)

    @pl.when(e == 0)
    def _():
        start(e, 0)
        start(e, 1)

    for i, it in enumerate(items):
        for cp in copies(e, i):
            cp.wait()
        if it[0] == "up":
            _, lo, w = it
            for c in range(0, w, FF_TILE):
                swiglu_cols(up_buf[i % 2, 0, :, c:c + FF_TILE], up_buf[i % 2, 1, :, c:c + FF_TILE], lo + c)
        else:
            n = it[1]
            wdb = dn_buf[n % 2].astype(BF16)
            for r, (g_ref, y_ref) in enumerate(((gc_ref, yc_ref), (gl_ref, yl_ref))):
                nset = g_ref.shape[0] // N_EXPERTS
                gate = jnp.concatenate(
                    [_row_to_col(g_ref[pl.ds(N_EXPERTS * s + e, 1), :][:, :SLOTS_CTX // nset])
                     for s in range(nset)], axis=0)
                y = _mm(h_scr[SLOTS_CTX * r:SLOTS_CTX * (r + 1), :], wdb)
                y_ref[0, :, OUT_TILE * n:OUT_TILE * (n + 1)] = (y * gate).astype(BF16)
        nxt = i + 2
        if nxt < len(items):
            start(e, nxt)
        else:
            @pl.when(e < last)
            def _(nxt=nxt):
                start(e + 1, nxt - len(items))


def _expert_ffn(xc, xl, gc, gl, w_gate, w_up, w_down):
    assert SLOTS_CTX == SLOTS_LAT and UP_CHUNK % FF_TILE == 0 and D_MODEL % OUT_TILE == 0
    assert len(_ffn_items()) >= 2 and (D_FF // UP_CHUNK + (D_FF % UP_CHUNK > 0)) % 2 == 0
    xs = lambda n: pl.BlockSpec((1, n, D_MODEL), lambda e: (e, 0, 0))
    gs = lambda g: pl.BlockSpec(g.shape, lambda e: (0, 0))
    hbm = pl.BlockSpec(memory_space=pl.ANY)
    return pl.pallas_call(
        _ffn_kernel,
        grid=(N_EXPERTS,),
        in_specs=[xs(SLOTS_CTX), xs(SLOTS_LAT), gs(gc), gs(gl), hbm, hbm, hbm],
        out_specs=[xs(SLOTS_CTX), xs(SLOTS_LAT)],
        out_shape=[jax.ShapeDtypeStruct((N_EXPERTS, SLOTS_CTX, D_MODEL), BF16),
                   jax.ShapeDtypeStruct((N_EXPERTS, SLOTS_LAT, D_MODEL), BF16)],
        scratch_shapes=[pltpu.VMEM((SLOTS_CTX + SLOTS_LAT, D_FF), BF16),
                        pltpu.VMEM((2, 2, D_MODEL, UP_CHUNK), F32),
                        pltpu.VMEM((2, D_FF, OUT_TILE), F32),
                        pltpu.SemaphoreType.DMA((2, 2)),
                        pltpu.SemaphoreType.DMA((2,))],
        compiler_params=_params(("arbitrary",)),
        name="expert_ffn",
    )(xc, xl, gc, gl, w_gate, w_up, w_down)


def _combine_ctx_kernel(y_ref, idx_ref, x1_ref, mod_ref, lng_ref, lnb_ref, o_ref):
    g2 = _mod_rows(mod_ref, 0)[5]
    nslot = N_EXPERTS * CAP_CTX
    hit = lax.broadcasted_iota(I32, (SEQ, nslot), 0) == idx_ref[0]
    y = y_ref[...].reshape(nslot, D_MODEL)
    f = _mm(jnp.where(hit, 1.0, 0.0).astype(BF16), y)
    o_ref[0] = _layer_norm(DEEPNORM_ALPHA * x1_ref[0] + g2 * f, lng_ref[...], lnb_ref[...])


def _combine_ctx(yc, idx_row, x1, mod, ln_g, ln_b):
    full = lambda shape: pl.BlockSpec(shape, lambda s: (0,) * len(shape))
    return pl.pallas_call(
        _combine_ctx_kernel,
        grid=(BATCH,),
        in_specs=[pl.BlockSpec((N_EXPERTS, CAP_CTX, D_MODEL), lambda s: (0, s, 0)),
                  pl.BlockSpec((1, 1, N_EXPERTS * CAP_CTX), lambda s: (s, 0, 0)),
                  pl.BlockSpec((1, SEQ, D_MODEL), lambda s: (s, 0, 0)),
                  full((8, 6 * D_MODEL)), full((1, D_MODEL)), full((1, D_MODEL))],
        out_specs=pl.BlockSpec((1, SEQ, D_MODEL), lambda s: (s, 0, 0)),
        out_shape=jax.ShapeDtypeStruct((BATCH, SEQ, D_MODEL), F32),
        compiler_params=_params(("arbitrary",)),
        name="combine_ctx",
    )(yc, idx_row, x1, mod, ln_g, ln_b)


def _combine_lat_kernel(y_ref, idx_ref, x1_ref, mod_ref, lng_ref, lnb_ref, o_ref):
    s = pl.program_id(0)
    g2 = _mod_rows(mod_ref, 1 + s)[5]
    tok = lax.broadcasted_iota(I32, (ROW_TILE, CAP_LAT), 0) + ROW_TILE * pl.program_id(1)
    f = jnp.zeros((ROW_TILE, D_MODEL), F32)
    for e in range(N_EXPERTS):
        hit = tok == idx_ref[0, e:e + 1, :]
        f = f + _mm(jnp.where(hit, 1.0, 0.0).astype(BF16), y_ref[e])
    o_ref[0] = _layer_norm(DEEPNORM_ALPHA * x1_ref[0] + g2 * f, lng_ref[...], lnb_ref[...])


def _combine_lat(yl, idx, x1, mod, ln_g, ln_b):
    nt = DEC_SEQ // ROW_TILE
    full = lambda shape: pl.BlockSpec(shape, lambda s, t: (0,) * len(shape))
    return pl.pallas_call(
        _combine_lat_kernel,
        grid=(DEC_BATCH, nt),
        in_specs=[pl.BlockSpec((N_EXPERTS, CAP_LAT, D_MODEL), lambda s, t: (0, s, 0)),
                  pl.BlockSpec((1, N_EXPERTS, CAP_LAT), lambda s, t: (s, 0, 0)),
                  pl.BlockSpec((1, ROW_TILE, D_MODEL), lambda s, t: (s, t, 0)),
                  full((8, 6 * D_MODEL)), full((1, D_MODEL)), full((1, D_MODEL))],
        out_specs=pl.BlockSpec((1, ROW_TILE, D_MODEL), lambda s, t: (s, t, 0)),
        out_shape=jax.ShapeDtypeStruct((DEC_BATCH, DEC_SEQ, D_MODEL), F32),
        compiler_params=_params(("arbitrary", "arbitrary")),
        name="combine_lat",
    )(yl, idx, x1, mod, ln_g, ln_b)


def _rope_tables():
    t = np.arange(DEC_SEQ)
    row = (t // GRID_W).astype(np.float64)
    col = (t % GRID_W).astype(np.float64)
    half = HEAD_DIM // 2
    inv = ROPE_THETA ** (-np.arange(0, half, 2, dtype=np.float64) / half)
    ang_r = row[:, None] * inv[None, :]
    ang_c = col[:, None] * inv[None, :]
    ang = np.concatenate([ang_r, ang_r, ang_c, ang_c], axis=-1)
    cos = np.tile(np.cos(ang), (1, 2))
    sin = np.tile(np.sin(ang), (1, 2))
    first = (np.arange(LANES) % 32) < 16
    return jnp.asarray(cos, F32), jnp.asarray(np.where(first[None, :], -sin, sin), F32)


def kernel(x_prompt, x_sample, cache_na_k, cache_na_v, cache_diff_k, cache_diff_v, c, c_ctx, w_ada, b_ada, w_in, w_out, na_rel_bias, lambda_q1, lambda_k1, lambda_q2, lambda_k2, subln_g, ln1_g, ln1_b, ln2_g, ln2_b, w_router, w_gate, w_up, w_down):
    l = 0
    cvec = jnp.concatenate([c_ctx[None, :], c, jnp.zeros((8 - 1 - DEC_BATCH, D_MODEL), F32)], axis=0)
    mod = _modulation(cvec, w_ada[l], b_ada[l][None, :])
    lamv = jnp.stack([lambda_q1[l], lambda_k1[l], lambda_q2[l], lambda_k2[l]], axis=0)
    w_in_b = w_in[l].astype(BF16)
    w_out_b = w_out[l].astype(BF16)
    sub = subln_g[l][None, :]
    wrt = w_router[l].T

    tr = lambda a: jnp.swapaxes(a, -1, -2)

    x1c, h2c, affc, na_k_t, na_v_t, diff_k_t, new_diff_v = _context_block(
        x_prompt, mod, lamv, w_in_b, w_out_b, sub, ln1_g[l][None, :], ln1_b[l][None, :], wrt)

    cos, sin = _rope_tables()
    naq, nak, nav, dfq, dfk, dfv = _latent_qkv(x_sample, mod, w_in_b, cos, sin)
    tbl = _na_bias_tables(na_rel_bias[l].reshape(-1))
    na_o = _latent_na(naq, nak, nav, tr(cache_na_k), tr(cache_na_v), tbl)
    df_o = _latent_diff(dfq, dfk, dfv, tr(cache_diff_k), cache_diff_v, lamv, sub)
    x1l, h2l, affl = _latent_out(x_sample, na_o, df_o, mod, w_out_b,
                                 ln1_g[l][None, :], ln1_b[l][None, :], wrt)

    idxc, gatec = _route(affc.reshape(BATCH * N_EXPERTS, SEQ), CAP_CTX)
    idxl, gatel = _route(affl.reshape(DEC_BATCH * N_EXPERTS, DEC_SEQ), CAP_LAT)
    idxc = idxc.reshape(BATCH, 1, N_EXPERTS * CAP_CTX)
    idxl = idxl.reshape(DEC_BATCH, N_EXPERTS, CAP_LAT)
    xc = _gather_ctx(idxc, h2c)
    xl = _gather_lat(idxl, h2l)
    yc, yl = _expert_ffn(xc, xl, gatec, gatel, w_gate[l], w_up[l], w_down[l])
    y_prompt = _combine_ctx(yc, idxc, x1c, mod, ln2_g[l][None, :], ln2_b[l][None, :])
    y_sample = _combine_lat(yl, idxl, x1l, mod, ln2_g[l][None, :], ln2_b[l][None, :])
    return (y_prompt, y_sample, tr(na_k_t), tr(na_v_t), tr(diff_k_t), new_diff_v)
```

```python
import functools
import math

import jax
import jax.numpy as jnp
import numpy as np
from jax import lax
from jax.experimental import pallas as pl
from jax.experimental.pallas import tpu as pltpu

F32 = jnp.float32
BF16 = jnp.bfloat16
I32 = jnp.int32

D_MODEL = 1024
BATCH = 16
SEQ = 256
DEC_BATCH = 2
DEC_SEQ = 2048
PAST_LEN = 256
GRID_W = 64
GRID_ROWS = DEC_SEQ // GRID_W
HEAD_DIM = 64
NA_HEADS = 8
DIFF_HEADS = 4
NA_WIDTH = NA_HEADS * HEAD_DIM
DIFF_WIDTH = DIFF_HEADS * 2 * HEAD_DIM
QKV_WIDTH = 3 * NA_WIDTH + 3 * DIFF_WIDTH
NA_WIN_H = 8
NA_WIN_W = 16
N_REL_H = 2 * NA_WIN_H - 1
N_REL_W = 2 * NA_WIN_W - 1
N_EXPERTS = 16
EC_CAPACITY_FACTOR = 2
D_FF = 2816
ROPE_THETA = 10000.0
NORM_EPS = 1e-5
NEG_INF = -1e30
DEPTH = 1
DEEPNORM_ALPHA = (2.0 * DEPTH) ** 0.25
LAMBDA_INIT = 0.8 - 0.6 * math.exp(-0.3 * 0)
SCALE = HEAD_DIM ** -0.5

OFF_NAQ = 0
OFF_NAK = NA_WIDTH
OFF_NAV = 2 * NA_WIDTH
OFF_DFQ = 3 * NA_WIDTH
OFF_DFK = 3 * NA_WIDTH + DIFF_WIDTH
OFF_DFV = 3 * NA_WIDTH + 2 * DIFF_WIDTH

CAP_CTX = EC_CAPACITY_FACTOR * SEQ // N_EXPERTS
CAP_LAT = EC_CAPACITY_FACTOR * DEC_SEQ // N_EXPERTS
SLOTS_CTX = BATCH * CAP_CTX
SLOTS_LAT = DEC_BATCH * CAP_LAT

ROW_TILE = 512
NA_QUAD = 4
NA_WIN_ROWS = 12
NA_GROUP = 8
FF_TILE = 256
FF_HALF = D_FF // 2
OUT_TILE = 512
LANES = 128
GATHER_GROUP = 2
CTX_GROUP = 4
DIFF_ROWS = 1024
Q_ROWS = 256

VMEM_LIMIT = 48 * 1024 * 1024
FFN_VMEM_LIMIT = 58 * 1024 * 1024


def _params(sem, vmem=VMEM_LIMIT, flags=None):
    return pltpu.CompilerParams(dimension_semantics=sem, vmem_limit_bytes=vmem, flags=flags)


def _mm(a, b):
    return jnp.dot(a, b, preferred_element_type=F32)


def _nt(a, b):
    return lax.dot_general(a, b, (((1,), (1,)), ((), ())), preferred_element_type=F32)


def _split(a):
    hi = a.astype(BF16)
    lo = (a - hi.astype(F32)).astype(BF16)
    return hi, lo


def _mm3(a, b):
    ah, al = _split(a)
    bh, bl = _split(b)
    return _mm(ah, bh) + _mm(al, bh) + _mm(ah, bl)


def _nt3(a, b):
    ah, al = _split(a)
    bh, bl = _split(b)
    return _nt(ah, bh) + _nt(al, bh) + _nt(ah, bl)


def _layer_norm(y, g, b):
    mu = jnp.mean(y, axis=-1, keepdims=True)
    d = y - mu
    var = jnp.mean(d * d, axis=-1, keepdims=True)
    return d * lax.rsqrt(var + NORM_EPS) * g + b


def _lam(l_ref):
    l = l_ref[...]
    a = jnp.sum(l[0:1] * l[1:2], axis=-1, keepdims=True)
    b = jnp.sum(l[2:3] * l[3:4], axis=-1, keepdims=True)
    return jnp.exp(a) - jnp.exp(b) + LAMBDA_INIT


def _mod_rows(mod_ref, row):
    return [mod_ref[pl.ds(row, 1), i * D_MODEL:(i + 1) * D_MODEL] for i in range(6)]


def _sub_norm(o, sub):
    ms = jnp.mean(o * o, axis=-1, keepdims=True)
    return o * lax.rsqrt(ms + NORM_EPS) * sub * (1.0 - LAMBDA_INIT)


def _router_aff_t(h2, wrt):
    lg = _nt3(wrt, h2)
    m = jnp.max(lg, axis=0, keepdims=True)
    e = jnp.exp(lg - m)
    return e / jnp.sum(e, axis=0, keepdims=True)


def _mod_kernel(c_ref, w_ref, b_ref, o_ref):
    c = c_ref[...]
    s = c * (1.0 / (1.0 + jnp.exp(-c)))
    o_ref[...] = _mm3(s, w_ref[...]) + b_ref[...]


def _modulation(cvec, w_ada, b_ada):
    ncol = 6 * D_MODEL
    return pl.pallas_call(
        _mod_kernel,
        grid=(6,),
        in_specs=[pl.BlockSpec((8, D_MODEL), lambda j: (0, 0)),
                  pl.BlockSpec((D_MODEL, D_MODEL), lambda j: (0, j)),
                  pl.BlockSpec((1, D_MODEL), lambda j: (0, j))],
        out_specs=pl.BlockSpec((8, D_MODEL), lambda j: (0, j)),
        out_shape=jax.ShapeDtypeStruct((8, ncol), F32),
        compiler_params=_params(("arbitrary",)),
        name="modulation",
    )(cvec, w_ada, b_ada)


def _ctx_kernel(x_ref, mod_ref, lam_ref, win_ref, wout_ref, sub_ref, lng_ref, lnb_ref, wrt_ref,
                x1_ref, h2_ref, aff_ref, nak_ref, nav_ref, dfk_ref, dfv_ref,
                qkv_scr, o_scr, s_scr, e_scr):
    x = x_ref[0]
    sh1, sc1, g1, sh2, sc2, g2 = _mod_rows(mod_ref, 0)
    h = (x * (1.0 + sc1) + sh1).astype(BF16)
    qkv_scr[...] = _mm(h, win_ref[...])

    kt_na = qkv_scr[:, OFF_NAK:OFF_NAK + NA_WIDTH].T
    kt_df = qkv_scr[:, OFF_DFK:OFF_DFK + DIFF_WIDTH].T
    nak_ref[0, 0] = kt_na.reshape(NA_HEADS, HEAD_DIM, SEQ)
    nav_ref[0, 0] = qkv_scr[:, OFF_NAV:OFF_NAV + NA_WIDTH].T.reshape(NA_HEADS, HEAD_DIM, SEQ)
    dfk_ref[0, 0] = kt_df.reshape(DIFF_HEADS, 2, HEAD_DIM, SEQ)
    for hh in range(DIFF_HEADS):
        c0 = OFF_DFV + 2 * HEAD_DIM * hh
        dfv_ref[0, 0, hh] = qkv_scr[:, c0:c0 + 2 * HEAD_DIM]

    kt = jnp.concatenate([kt_na, kt_df], axis=0).astype(BF16)
    q_cols = [OFF_NAQ + HEAD_DIM * i for i in range(NA_HEADS)] + [
        OFF_DFQ + HEAD_DIM * i for i in range(2 * DIFF_HEADS)]
    nmap = len(q_cols)
    for i, c in enumerate(q_cols):
        q = (qkv_scr[:, c:c + HEAD_DIM] * SCALE).astype(BF16)
        s_scr[i] = _mm(q, kt[HEAD_DIM * i:HEAD_DIM * (i + 1), :])
    s = s_scr[...]
    e = jnp.exp(s - jnp.max(s, axis=-1, keepdims=True))
    rl = 1.0 / jnp.sum(e, axis=-1, keepdims=True)
    e_scr[...] = e.astype(BF16)

    for hh in range(NA_HEADS):
        v = qkv_scr[:, OFF_NAV + HEAD_DIM * hh:OFF_NAV + HEAD_DIM * (hh + 1)].astype(BF16)
        o_scr[:, HEAD_DIM * hh:HEAD_DIM * (hh + 1)] = (_mm(e_scr[hh], v) * rl[hh]).astype(BF16)
    lam = _lam(lam_ref)
    sub = sub_ref[...]
    for hh in range(DIFF_HEADS):
        i1 = NA_HEADS + 2 * hh
        c0 = OFF_DFV + 2 * HEAD_DIM * hh
        v = qkv_scr[:, c0:c0 + 2 * HEAD_DIM].astype(BF16)
        o = _mm(e_scr[i1], v) * rl[i1] - _mm(e_scr[i1 + 1], v) * (lam * rl[i1 + 1])
        c1 = NA_WIDTH + 2 * HEAD_DIM * hh
        o_scr[:, c1:c1 + 2 * HEAD_DIM] = _sub_norm(o, sub).astype(BF16)

    a = _mm(o_scr[...], wout_ref[...])
    x1 = _layer_norm(DEEPNORM_ALPHA * x + g1 * a, lng_ref[...], lnb_ref[...])
    x1_ref[0] = x1
    h2 = x1 * (1.0 + sc2) + sh2

    h2_ref[0] = h2.astype(BF16)
    aff_ref[0] = _router_aff_t(h2, wrt_ref[...])


def _context_block(x, mod, lamv, w_in, w_out, sub, ln_g, ln_b, wrt):
    full = lambda shape: pl.BlockSpec(shape, lambda b: (0,) * len(shape))
    return pl.pallas_call(
        _ctx_kernel,
        grid=(BATCH,),
        in_specs=[pl.BlockSpec((1, SEQ, D_MODEL), lambda b: (b, 0, 0)),
                  full((8, 6 * D_MODEL)), full((4, HEAD_DIM)),
                  full((D_MODEL, QKV_WIDTH)), full((D_MODEL, D_MODEL)),
                  full((1, 2 * HEAD_DIM)), full((1, D_MODEL)), full((1, D_MODEL)),
                  full((N_EXPERTS, D_MODEL))],
        out_specs=[pl.BlockSpec((1, SEQ, D_MODEL), lambda b: (b, 0, 0)),
                   pl.BlockSpec((1, SEQ, D_MODEL), lambda b: (b, 0, 0)),
                   pl.BlockSpec((1, N_EXPERTS, SEQ), lambda b: (b, 0, 0)),
                   pl.BlockSpec((1, 1, NA_HEADS, HEAD_DIM, SEQ), lambda b: (b, 0, 0, 0, 0)),
                   pl.BlockSpec((1, 1, NA_HEADS, HEAD_DIM, SEQ), lambda b: (b, 0, 0, 0, 0)),
                   pl.BlockSpec((1, 1, DIFF_HEADS, 2, HEAD_DIM, SEQ), lambda b: (b, 0, 0, 0, 0, 0)),
                   pl.BlockSpec((1, 1, DIFF_HEADS, SEQ, 2 * HEAD_DIM), lambda b: (b, 0, 0, 0, 0))],
        out_shape=[jax.ShapeDtypeStruct((BATCH, SEQ, D_MODEL), F32),
                   jax.ShapeDtypeStruct((BATCH, SEQ, D_MODEL), BF16),
                   jax.ShapeDtypeStruct((BATCH, N_EXPERTS, SEQ), F32),
                   jax.ShapeDtypeStruct((BATCH, 1, NA_HEADS, HEAD_DIM, SEQ), F32),
                   jax.ShapeDtypeStruct((BATCH, 1, NA_HEADS, HEAD_DIM, SEQ), F32),
                   jax.ShapeDtypeStruct((BATCH, 1, DIFF_HEADS, 2, HEAD_DIM, SEQ), F32),
                   jax.ShapeDtypeStruct((BATCH, 1, DIFF_HEADS, SEQ, 2 * HEAD_DIM), F32)],
        scratch_shapes=[pltpu.VMEM((SEQ, QKV_WIDTH), F32), pltpu.VMEM((SEQ, D_MODEL), BF16),
                        pltpu.VMEM((NA_HEADS + 2 * DIFF_HEADS, SEQ, SEQ), F32),
                        pltpu.VMEM((NA_HEADS + 2 * DIFF_HEADS, SEQ, SEQ), BF16)],
        compiler_params=_params(("arbitrary",)),
        name="context_block",
    )(x, mod, lamv, w_in, w_out, sub, ln_g, ln_b, wrt)


def _lat_qkv_kernel(x_ref, mod_ref, win_ref, cos_ref, sin_ref, rpb_ref,
                    naq_ref, nak_ref, nav_ref, dfq_ref, dfk_ref, dfv_ref, tbl_ref, qkv_scr):
    b = pl.program_id(0)
    _write_bias_tables(rpb_ref, b * pl.num_programs(1) + pl.program_id(1), tbl_ref)
    x = x_ref[0]
    sh1, sc1 = _mod_rows(mod_ref, 1 + b)[:2]
    h = (x * (1.0 + sc1) + sh1).astype(BF16)
    qkv_scr[...] = _mm(h, win_ref[...])

    for hh in range(NA_HEADS):
        lo, hi = HEAD_DIM * hh, HEAD_DIM * (hh + 1)
        naq_ref[0, hh] = (qkv_scr[:, OFF_NAQ + lo:OFF_NAQ + hi] * SCALE).astype(BF16)
        nak_ref[0, hh] = qkv_scr[:, OFF_NAK + lo:OFF_NAK + hi].astype(BF16)
        nav_ref[0, hh] = qkv_scr[:, OFF_NAV + lo:OFF_NAV + hi].astype(BF16)

    cos = cos_ref[...]
    sin = sin_ref[...]
    lane = lax.broadcasted_iota(I32, (ROW_TILE, LANES), 1)
    first = (lane & 31) < 16

    def rope(t):
        rot = jnp.where(first, pltpu.roll(t, LANES - 16, 1), pltpu.roll(t, 16, 1))
        return t * cos + rot * sin

    for hh in range(DIFF_HEADS):
        lo = 2 * HEAD_DIM * hh
        q = rope(qkv_scr[:, OFF_DFQ + lo:OFF_DFQ + lo + LANES]) * SCALE
        k = rope(qkv_scr[:, OFF_DFK + lo:OFF_DFK + lo + LANES])
        for m in range(2):
            dfq_ref[0, hh, m] = q[:, HEAD_DIM * m:HEAD_DIM * (m + 1)].astype(BF16)
        dfk_ref[0, hh] = k.T.astype(BF16).reshape(2, HEAD_DIM, ROW_TILE)
        dfv_ref[0, hh] = qkv_scr[:, OFF_DFV + lo:OFF_DFV + lo + LANES].astype(BF16)


def _latent_qkv(x, mod, w_in, cos, sin, rpb_flat):
    nt = DEC_SEQ // ROW_TILE
    assert DEC_BATCH * nt == NA_HEADS
    tbl_shape = (NA_HEADS, 3, NA_QUAD * GRID_W, NA_WIN_ROWS * GRID_W)
    full = lambda shape: pl.BlockSpec(shape, lambda b, t: (0,) * len(shape))
    hs = lambda nh: pl.BlockSpec((1, nh, ROW_TILE, HEAD_DIM), lambda b, t: (b, 0, t, 0))
    return pl.pallas_call(
        _lat_qkv_kernel,
        grid=(DEC_BATCH, nt),
        in_specs=[pl.BlockSpec((1, ROW_TILE, D_MODEL), lambda b, t: (b, t, 0)),
                  full((8, 6 * D_MODEL)), full((D_MODEL, QKV_WIDTH)),
                  pl.BlockSpec((ROW_TILE, LANES), lambda b, t: (t, 0)),
                  pl.BlockSpec((ROW_TILE, LANES), lambda b, t: (t, 0)),
                  pl.BlockSpec(memory_space=pltpu.SMEM)],
        out_specs=[hs(NA_HEADS), hs(NA_HEADS), hs(NA_HEADS),
                   pl.BlockSpec((1, DIFF_HEADS, 2, ROW_TILE, HEAD_DIM), lambda b, t: (b, 0, 0, t, 0)),
                   pl.BlockSpec((1, DIFF_HEADS, 2, HEAD_DIM, ROW_TILE), lambda b, t: (b, 0, 0, 0, t)),
                   pl.BlockSpec((1, DIFF_HEADS, ROW_TILE, 2 * HEAD_DIM), lambda b, t: (b, 0, t, 0)),
                   pl.BlockSpec((1,) + tbl_shape[1:], lambda b, t: (b * nt + t, 0, 0, 0))],
        out_shape=[jax.ShapeDtypeStruct((DEC_BATCH, NA_HEADS, DEC_SEQ, HEAD_DIM), BF16)] * 3
        + [jax.ShapeDtypeStruct((DEC_BATCH, DIFF_HEADS, 2, DEC_SEQ, HEAD_DIM), BF16),
           jax.ShapeDtypeStruct((DEC_BATCH, DIFF_HEADS, 2, HEAD_DIM, DEC_SEQ), BF16),
           jax.ShapeDtypeStruct((DEC_BATCH, DIFF_HEADS, DEC_SEQ, 2 * HEAD_DIM), BF16),
           jax.ShapeDtypeStruct(tbl_shape, F32)],
        scratch_shapes=[pltpu.VMEM((ROW_TILE, QKV_WIDTH), F32)],
        compiler_params=_params(("arbitrary", "arbitrary")),
        name="latent_qkv",
    )(x, mod, w_in, cos, sin, rpb_flat)


def _na_row_offset(kind, qr, kr):
    if kind == 0:
        return kr - qr + 7 if kr < NA_WIN_H else None
    if kind == 1:
        return kr - qr + 3 if qr <= kr < qr + NA_WIN_H else None
    return kr - qr - 1 if NA_WIN_ROWS - NA_WIN_H <= kr else None


def _write_bias_tables(rpb_ref, h, o_ref):
    base = h * (N_REL_H * N_REL_W)
    wq = lax.broadcasted_iota(I32, (GRID_W, LANES), 0)
    c = lax.broadcasted_iota(I32, (GRID_W, LANES), 1)
    wk = c & (GRID_W - 1)
    right = c >= GRID_W
    c0 = jnp.clip(wq - NA_WIN_W // 2, 0, GRID_W - NA_WIN_W)
    in_win = (wk >= c0) & (wk < c0 + NA_WIN_W)
    cache = {}
    strips = {}

    def strip(r):
        if r not in strips:
            c8 = lax.broadcasted_iota(I32, (8, LANES), 1)
            d = jnp.where(c8 < GRID_W, c8, c8 - LANES)
            co_of = jnp.clip(d, -(NA_WIN_W - 1), NA_WIN_W - 1) + (NA_WIN_W - 1)
            acc = jnp.zeros((8, LANES), F32)
            for co in range(N_REL_W):
                acc = jnp.where(co_of == co, rpb_ref[base + r * N_REL_W + co], acc)
            strips[r] = jnp.concatenate([acc] * (GRID_W // 8), axis=0)
        return strips[r]

    def tile(rl, rr):
        if (rl, rr) in cache:
            return cache[(rl, rr)]
        if rl is None and rr is None:
            t = jnp.full((GRID_W, LANES), NEG_INF, F32)
        else:
            zero = jnp.zeros((GRID_W, LANES), F32)
            tl = pltpu.roll(strip(rl), 0, 1, stride=1, stride_axis=0) if rl is not None else zero
            tr = pltpu.roll(strip(rr), GRID_W, 1, stride=1, stride_axis=0) if rr is not None else zero
            acc = jnp.where(right, tr, tl)
            ok = in_win
            if rl is None:
                ok = ok & right
            if rr is None:
                ok = ok & jnp.logical_not(right)
            t = jnp.where(ok, acc, NEG_INF)
        cache[(rl, rr)] = t
        return t

    for kind in range(3):
        for qr in range(NA_QUAD):
            row = [tile(_na_row_offset(kind, qr, 2 * p), _na_row_offset(kind, qr, 2 * p + 1))
                   for p in range(NA_WIN_ROWS // 2)]
            o_ref[0, kind, GRID_W * qr:GRID_W * (qr + 1), :] = jnp.concatenate(row, axis=1)


def _na_kernel(q_ref, k_ref, v_ref, ck_ref, cv_ref, tbl_ref, o_ref):
    g = pl.program_id(2)
    u0 = jnp.clip(NA_QUAD * g - NA_QUAD, 0, GRID_ROWS - NA_WIN_ROWS)
    start = pl.multiple_of(u0 * GRID_W, GRID_W)
    nwin = NA_WIN_ROWS * GRID_W
    for j in range(NA_GROUP):
        q = q_ref[0, j]
        kw = k_ref[0, j, pl.ds(start, nwin), :]
        vw = v_ref[0, j, pl.ds(start, nwin), :]
        kct = ck_ref[0, 0, j].astype(BF16)
        vct = cv_ref[0, 0, j].astype(BF16)
        sl = _nt(q, kw) + tbl_ref[j, 0]
        sc = _mm(q, kct)
        m = jnp.maximum(jnp.max(sl, axis=-1, keepdims=True), jnp.max(sc, axis=-1, keepdims=True))
        el = jnp.exp(sl - m)
        ec = jnp.exp(sc - m)
        l = jnp.sum(el, axis=-1, keepdims=True) + jnp.sum(ec, axis=-1, keepdims=True)
        o = (_mm(el.astype(BF16), vw) + _nt(ec.astype(BF16), vct)) / l
        o_ref[0, :, HEAD_DIM * j:HEAD_DIM * (j + 1)] = o.astype(BF16)


def _latent_na(naq, nak, nav, cache_k, cache_v, tbl):
    nq = GRID_ROWS // NA_QUAD
    rows = NA_QUAD * GRID_W
    kind = lambda g: jnp.minimum(g, 1) + g // (nq - 1)
    return pl.pallas_call(
        _na_kernel,
        grid=(DEC_BATCH, NA_HEADS // NA_GROUP, nq),
        in_specs=[pl.BlockSpec((1, NA_GROUP, rows, HEAD_DIM), lambda b, p, g: (b, p, g, 0)),
                  pl.BlockSpec((1, NA_GROUP, DEC_SEQ, HEAD_DIM), lambda b, p, g: (b, p, 0, 0)),
                  pl.BlockSpec((1, NA_GROUP, DEC_SEQ, HEAD_DIM), lambda b, p, g: (b, p, 0, 0)),
                  pl.BlockSpec((1, 1, NA_GROUP, HEAD_DIM, PAST_LEN), lambda b, p, g: (b, 0, p, 0, 0)),
                  pl.BlockSpec((1, 1, NA_GROUP, HEAD_DIM, PAST_LEN), lambda b, p, g: (b, 0, p, 0, 0)),
                  pl.BlockSpec((NA_GROUP, 1, rows, NA_WIN_ROWS * GRID_W), lambda b, p, g: (p, kind(g), 0, 0))],
        out_specs=pl.BlockSpec((1, rows, NA_GROUP * HEAD_DIM), lambda b, p, g: (b, g, p)),
        out_shape=jax.ShapeDtypeStruct((DEC_BATCH, DEC_SEQ, NA_WIDTH), BF16),
        compiler_params=_params(("arbitrary",) * 3),
        name="latent_na",
    )(naq, nak, nav, cache_k, cache_v, tbl)


def _diff_kernel(q_ref, kt_ref, v_ref, ck_ref, cv_ref, lam_ref, sub_ref, o_ref):
    lam = _lam(lam_ref)
    vc = cv_ref[0, 0, 0].astype(BF16)
    vl = v_ref[0, 0]
    for r in range(DIFF_ROWS // Q_ROWS):
        rows = slice(Q_ROWS * r, Q_ROWS * (r + 1))
        parts = []
        for m in range(2):
            q = q_ref[0, 0, m, rows, :]
            sc = _mm(q, ck_ref[0, 0, 0, m].astype(BF16))
            sl = _mm(q, kt_ref[0, 0, m])
            mx = jnp.maximum(jnp.max(sc, axis=-1, keepdims=True), jnp.max(sl, axis=-1, keepdims=True))
            ec = jnp.exp(sc - mx)
            el = jnp.exp(sl - mx)
            l = jnp.sum(ec, axis=-1, keepdims=True) + jnp.sum(el, axis=-1, keepdims=True)
            parts.append((_mm(ec.astype(BF16), vc) + _mm(el.astype(BF16), vl), l))
        (a1, l1), (a2, l2) = parts
        o = a1 * (1.0 / l1) - a2 * (lam / l2)
        o_ref[0, rows, :] = _sub_norm(o, sub_ref[...]).astype(BF16)


def _latent_diff(dfq, dfk, dfv, cache_k, cache_v, lamv, sub):
    nt = DEC_SEQ // DIFF_ROWS
    full = lambda shape: pl.BlockSpec(shape, lambda b, h, t: (0,) * len(shape))
    return pl.pallas_call(
        _diff_kernel,
        grid=(DEC_BATCH, DIFF_HEADS, nt),
        in_specs=[pl.BlockSpec((1, 1, 2, DIFF_ROWS, HEAD_DIM), lambda b, h, t: (b, h, 0, t, 0)),
                  pl.BlockSpec((1, 1, 2, HEAD_DIM, DEC_SEQ), lambda b, h, t: (b, h, 0, 0, 0)),
                  pl.BlockSpec((1, 1, DEC_SEQ, 2 * HEAD_DIM), lambda b, h, t: (b, h, 0, 0)),
                  pl.BlockSpec((1, 1, 1, 2, HEAD_DIM, PAST_LEN), lambda b, h, t: (b, 0, h, 0, 0, 0)),
                  pl.BlockSpec((1, 1, 1, PAST_LEN, 2 * HEAD_DIM), lambda b, h, t: (b, 0, h, 0, 0)),
                  full((4, HEAD_DIM)), full((1, 2 * HEAD_DIM))],
        out_specs=pl.BlockSpec((1, DIFF_ROWS, 2 * HEAD_DIM), lambda b, h, t: (b, t, h)),
        out_shape=jax.ShapeDtypeStruct((DEC_BATCH, DEC_SEQ, DIFF_WIDTH), BF16),
        compiler_params=_params(("arbitrary",) * 3),
        name="latent_diff",
    )(dfq, dfk, dfv, cache_k, cache_v, lamv, sub)


def _lat_out_kernel(x_ref, na_ref, df_ref, mod_ref, wout_ref, lng_ref, lnb_ref, wrt_ref,
                    x1_ref, h2_ref, aff_ref):
    b = pl.program_id(0)
    x = x_ref[0]
    _, _, g1, sh2, sc2, _ = _mod_rows(mod_ref, 1 + b)
    a = _mm(na_ref[0], wout_ref[0:NA_WIDTH, :]) + _mm(df_ref[0], wout_ref[NA_WIDTH:, :])
    x1 = _layer_norm(DEEPNORM_ALPHA * x + g1 * a, lng_ref[...], lnb_ref[...])
    x1_ref[0] = x1
    h2 = x1 * (1.0 + sc2) + sh2
    h2_ref[0] = h2.astype(BF16)
    aff_ref[0] = _router_aff_t(h2, wrt_ref[...])


def _latent_out(x, na_o, df_o, mod, w_out, ln_g, ln_b, wrt):
    nt = DEC_SEQ // ROW_TILE
    full = lambda shape: pl.BlockSpec(shape, lambda b, t: (0,) * len(shape))
    tok = lambda w: pl.BlockSpec((1, ROW_TILE, w), lambda b, t: (b, t, 0))
    return pl.pallas_call(
        _lat_out_kernel,
        grid=(DEC_BATCH, nt),
        in_specs=[tok(D_MODEL), tok(NA_WIDTH), tok(DIFF_WIDTH), full((8, 6 * D_MODEL)),
                  full((D_MODEL, D_MODEL)), full((1, D_MODEL)), full((1, D_MODEL)),
                  full((N_EXPERTS, D_MODEL))],
        out_specs=[tok(D_MODEL), tok(D_MODEL),
                   pl.BlockSpec((1, N_EXPERTS, ROW_TILE), lambda b, t: (b, 0, t))],
        out_shape=[jax.ShapeDtypeStruct((DEC_BATCH, DEC_SEQ, D_MODEL), F32),
                   jax.ShapeDtypeStruct((DEC_BATCH, DEC_SEQ, D_MODEL), BF16),
                   jax.ShapeDtypeStruct((DEC_BATCH, N_EXPERTS, DEC_SEQ), F32)],
        compiler_params=_params(("arbitrary", "arbitrary")),
        name="latent_out",
    )(x, na_o, df_o, mod, w_out, ln_g, ln_b, wrt)


def _ranks_first(va, ia, vb, ib):
    return (va > vb) | ((va == vb) & (ia < ib))


def _sort_rows_desc(aff):
    rows, n = aff.shape
    nb = n // LANES
    lane = lax.broadcasted_iota(I32, (rows, LANES), 1)
    v = [aff[:, LANES * b:LANES * (b + 1)] for b in range(nb)]
    ix = [lane + LANES * b for b in range(nb)]
    k = 2
    while k <= n:
        j = k // 2
        while j >= 1:
            if j >= LANES:
                sb = j // LANES
                for lo in range(nb):
                    if lo & sb:
                        continue
                    hi = lo | sb
                    f = _ranks_first(v[lo], ix[lo], v[hi], ix[hi])
                    if k < n and (LANES * lo) & k:
                        f = jnp.logical_not(f)
                    v[lo], v[hi] = jnp.where(f, v[lo], v[hi]), jnp.where(f, v[hi], v[lo])
                    ix[lo], ix[hi] = jnp.where(f, ix[lo], ix[hi]), jnp.where(f, ix[hi], ix[lo])
            else:
                upper = (lane & j) != 0
                if k < LANES:
                    flip_lanes = jnp.logical_xor(upper, (lane & k) != 0)
                for b in range(nb):
                    pv = jnp.where(upper, pltpu.roll(v[b], j, 1), pltpu.roll(v[b], LANES - j, 1))
                    pi = jnp.where(upper, pltpu.roll(ix[b], j, 1), pltpu.roll(ix[b], LANES - j, 1))
                    f = _ranks_first(v[b], ix[b], pv, pi)
                    if k < LANES:
                        flip = flip_lanes
                    elif k < n and (LANES * b) & k:
                        flip = jnp.logical_not(upper)
                    else:
                        flip = upper
                    keep = jnp.logical_xor(f, flip)
                    v[b] = jnp.where(keep, v[b], pv)
                    ix[b] = jnp.where(keep, ix[b], pi)
            j //= 2
        k *= 2
    return v, ix


def _route_kernel(affc_ref, affl_ref, idxc_ref, gatec_ref, idxl_ref, gatel_ref):
    for aff_ref, idx_ref, gate_ref, cap in ((affc_ref, idxc_ref, gatec_ref, CAP_CTX),
                                            (affl_ref, idxl_ref, gatel_ref, CAP_LAT)):
        v, ix = _sort_rows_desc(aff_ref[...])
        if cap < LANES:
            idx_ref[...] = ix[0][:, :cap]
            gate_ref[...] = v[0]
        else:
            idx_ref[...] = jnp.concatenate(ix[:cap // LANES], axis=1)
            gate_ref[...] = jnp.concatenate(v[:cap // LANES], axis=1)


def _route(affc_t, affl_t):
    out = lambda a, cap: [jax.ShapeDtypeStruct((a.shape[0], cap), I32),
                          jax.ShapeDtypeStruct((a.shape[0], max(cap, LANES)), F32)]
    return pl.pallas_call(
        _route_kernel,
        out_shape=out(affc_t, CAP_CTX) + out(affl_t, CAP_LAT),
        compiler_params=_params(None),
        name="route",
    )(affc_t, affl_t)


def _row_to_col(row):
    n = row.shape[1]
    eye = lax.broadcasted_iota(I32, (n, n), 0) == lax.broadcasted_iota(I32, (n, n), 1)
    return jnp.sum(jnp.where(eye, row, jnp.zeros_like(row)), axis=1, keepdims=True)


def _gather_ctx_kernel(idx_ref, h_ref, x_ref):
    nslot = N_EXPERTS * CAP_CTX
    for s in range(CTX_GROUP):
        hit = lax.broadcasted_iota(I32, (nslot, SEQ), 1) == _row_to_col(idx_ref[s])
        xs = _mm(jnp.where(hit, 1.0, 0.0).astype(BF16), h_ref[s])
        x_ref[:, CAP_CTX * s:CAP_CTX * (s + 1), :] = xs.astype(BF16).reshape(N_EXPERTS, CAP_CTX, D_MODEL)


def _gather_ctx(idx_row, h2):
    nslot = N_EXPERTS * CAP_CTX
    return pl.pallas_call(
        _gather_ctx_kernel,
        grid=(BATCH // CTX_GROUP,),
        in_specs=[pl.BlockSpec((CTX_GROUP, 1, nslot), lambda s: (s, 0, 0)),
                  pl.BlockSpec((CTX_GROUP, SEQ, D_MODEL), lambda s: (s, 0, 0))],
        out_specs=pl.BlockSpec((N_EXPERTS, CTX_GROUP * CAP_CTX, D_MODEL), lambda s: (0, s, 0)),
        out_shape=jax.ShapeDtypeStruct((N_EXPERTS, SLOTS_CTX, D_MODEL), BF16),
        compiler_params=_params(("arbitrary",)),
        name="gather_ctx",
    )(idx_row, h2)


def _gather_lat_kernel(idx_ref, h_ref, x_ref):
    first = GATHER_GROUP * pl.program_id(1)
    col = jnp.concatenate([_row_to_col(idx_ref[0, pl.ds(first + i, 1), :]) for i in range(GATHER_GROUP)],
                          axis=0)
    hit = lax.broadcasted_iota(I32, (GATHER_GROUP * CAP_LAT, DEC_SEQ), 1) == col
    xs = _mm(jnp.where(hit, 1.0, 0.0).astype(BF16), h_ref[0])
    x_ref[...] = xs.astype(BF16).reshape(GATHER_GROUP, CAP_LAT, D_MODEL)


def _gather_lat(idx, h2):
    return pl.pallas_call(
        _gather_lat_kernel,
        grid=(DEC_BATCH, N_EXPERTS // GATHER_GROUP),
        in_specs=[pl.BlockSpec((1, N_EXPERTS, CAP_LAT), lambda s, e: (s, 0, 0)),
                  pl.BlockSpec((1, DEC_SEQ, D_MODEL), lambda s, e: (s, 0, 0))],
        out_specs=pl.BlockSpec((GATHER_GROUP, CAP_LAT, D_MODEL), lambda s, e: (e, s, 0)),
        out_shape=jax.ShapeDtypeStruct((N_EXPERTS, SLOTS_LAT, D_MODEL), BF16),
        compiler_params=_params(("arbitrary", "arbitrary")),
        name="gather_lat",
    )(idx, h2)


def _ffn_kernel(xc_ref, xl_ref, gc_ref, gl_ref, wg_ref, wu_ref, wd_ref, yc_ref, yl_ref, h_scr, wl_scr):
    j = pl.program_id(1)
    tail = FF_HALF - FF_HALF // FF_TILE * FF_TILE

    def swiglu_cols(wg, wu, c0):
        wgb = wg.astype(BF16)
        wub = wu.astype(BF16)
        for i, x_ref in enumerate((xc_ref, xl_ref)):
            x = x_ref[0]
            a = _mm(x, wgb)
            u = _mm(x, wub)
            h = a * (1.0 / (1.0 + jnp.exp(-a))) * u
            h_scr[SLOTS_CTX * i:SLOTS_CTX * (i + 1), c0:c0 + FF_TILE] = h.astype(BF16)

    @pl.when(j == 0)
    def _():
        for c in range(FF_HALF // FF_TILE):
            lo = FF_TILE * c
            swiglu_cols(wg_ref[0, :, lo:lo + FF_TILE], wu_ref[0, :, lo:lo + FF_TILE], lo)
        wl_scr[0] = wg_ref[0, :, FF_HALF - tail:FF_HALF]
        wl_scr[1] = wu_ref[0, :, FF_HALF - tail:FF_HALF]

    @pl.when(j == 1)
    def _():
        head = FF_TILE - tail
        swiglu_cols(jnp.concatenate([wl_scr[0], wg_ref[0, :, 0:head]], axis=1),
                    jnp.concatenate([wl_scr[1], wu_ref[0, :, 0:head]], axis=1), FF_HALF - tail)
        for c in range(FF_HALF // FF_TILE):
            lo = head + FF_TILE * c
            swiglu_cols(wg_ref[0, :, lo:lo + FF_TILE], wu_ref[0, :, lo:lo + FF_TILE], FF_HALF + lo)

    @pl.when(j >= 2)
    def _():
        wdb = wd_ref[0].astype(BF16)
        e = pl.program_id(0)
        for i, (g_ref, y_ref) in enumerate(((gc_ref, yc_ref), (gl_ref, yl_ref))):
            nset = g_ref.shape[0] // N_EXPERTS
            gate = jnp.concatenate(
                [_row_to_col(g_ref[pl.ds(N_EXPERTS * s + e, 1), :][:, :SLOTS_CTX // nset]) for s in range(nset)],
                axis=0)
            y = _mm(h_scr[SLOTS_CTX * i:SLOTS_CTX * (i + 1), :], wdb)
            y_ref[0] = (y * gate).astype(BF16)


def _expert_ffn(xc, xl, gc, gl, w_gate, w_up, w_down):
    assert SLOTS_CTX == SLOTS_LAT and FF_HALF % LANES == 0 and D_MODEL % OUT_TILE == 0
    xs = lambda n: pl.BlockSpec((1, n, D_MODEL), lambda e, j: (e, 0, 0))
    gs = lambda g: pl.BlockSpec(g.shape, lambda e, j: (0, 0))
    last = N_EXPERTS - 1
    up = pl.BlockSpec((1, D_MODEL, FF_HALF),
                      lambda e, j: (jnp.where(j >= 2, jnp.minimum(e + 1, last), e), 0, jnp.where(j == 1, 1, 0)))
    down = pl.BlockSpec((1, D_FF, OUT_TILE), lambda e, j: (e, 0, jnp.maximum(j - 2, 0)))
    ys = lambda n: pl.BlockSpec((1, n, OUT_TILE), lambda e, j: (e, 0, jnp.maximum(j - 2, 0)))
    return pl.pallas_call(
        _ffn_kernel,
        grid=(N_EXPERTS, 2 + D_MODEL // OUT_TILE),
        in_specs=[xs(SLOTS_CTX), xs(SLOTS_LAT), gs(gc), gs(gl), up, up, down],
        out_specs=[ys(SLOTS_CTX), ys(SLOTS_LAT)],
        out_shape=[jax.ShapeDtypeStruct((N_EXPERTS, SLOTS_CTX, D_MODEL), BF16),
                   jax.ShapeDtypeStruct((N_EXPERTS, SLOTS_LAT, D_MODEL), BF16)],
        scratch_shapes=[pltpu.VMEM((SLOTS_CTX + SLOTS_LAT, D_FF), BF16),
                        pltpu.VMEM((2, D_MODEL, LANES), F32)],
        compiler_params=_params(("arbitrary", "arbitrary"), FFN_VMEM_LIMIT),
        name="expert_ffn",
    )(xc, xl, gc, gl, w_gate, w_up, w_down)


def _combine_ctx_kernel(y_ref, idx_ref, x1_ref, mod_ref, lng_ref, lnb_ref, o_ref):
    g2 = _mod_rows(mod_ref, 0)[5]
    nslot = N_EXPERTS * CAP_CTX
    for s in range(CTX_GROUP):
        hit = lax.broadcasted_iota(I32, (SEQ, nslot), 0) == idx_ref[s]
        y = y_ref[:, CAP_CTX * s:CAP_CTX * (s + 1), :].reshape(nslot, D_MODEL)
        f = _mm(jnp.where(hit, 1.0, 0.0).astype(BF16), y)
        o_ref[s] = _layer_norm(DEEPNORM_ALPHA * x1_ref[s] + g2 * f, lng_ref[...], lnb_ref[...])


def _combine_ctx(yc, idx_row, x1, mod, ln_g, ln_b):
    full = lambda shape: pl.BlockSpec(shape, lambda s: (0,) * len(shape))
    return pl.pallas_call(
        _combine_ctx_kernel,
        grid=(BATCH // CTX_GROUP,),
        in_specs=[pl.BlockSpec((N_EXPERTS, CTX_GROUP * CAP_CTX, D_MODEL), lambda s: (0, s, 0)),
                  pl.BlockSpec((CTX_GROUP, 1, N_EXPERTS * CAP_CTX), lambda s: (s, 0, 0)),
                  pl.BlockSpec((CTX_GROUP, SEQ, D_MODEL), lambda s: (s, 0, 0)),
                  full((8, 6 * D_MODEL)), full((1, D_MODEL)), full((1, D_MODEL))],
        out_specs=pl.BlockSpec((CTX_GROUP, SEQ, D_MODEL), lambda s: (s, 0, 0)),
        out_shape=jax.ShapeDtypeStruct((BATCH, SEQ, D_MODEL), F32),
        compiler_params=_params(("arbitrary",)),
        name="combine_ctx",
    )(yc, idx_row, x1, mod, ln_g, ln_b)


def _combine_lat_kernel(y_ref, idx_ref, x1_ref, mod_ref, lng_ref, lnb_ref, o_ref):
    s = pl.program_id(0)
    g2 = _mod_rows(mod_ref, 1 + s)[5]
    tok = lax.broadcasted_iota(I32, (ROW_TILE, CAP_LAT), 0) + ROW_TILE * pl.program_id(1)
    f = jnp.zeros((ROW_TILE, D_MODEL), F32)
    for e in range(N_EXPERTS):
        hit = tok == idx_ref[0, e:e + 1, :]
        f = f + _mm(jnp.where(hit, 1.0, 0.0).astype(BF16), y_ref[e])
    o_ref[0] = _layer_norm(DEEPNORM_ALPHA * x1_ref[0] + g2 * f, lng_ref[...], lnb_ref[...])


def _combine_lat(yl, idx, x1, mod, ln_g, ln_b):
    nt = DEC_SEQ // ROW_TILE
    full = lambda shape: pl.BlockSpec(shape, lambda s, t: (0,) * len(shape))
    return pl.pallas_call(
        _combine_lat_kernel,
        grid=(DEC_BATCH, nt),
        in_specs=[pl.BlockSpec((N_EXPERTS, CAP_LAT, D_MODEL), lambda s, t: (0, s, 0)),
                  pl.BlockSpec((1, N_EXPERTS, CAP_LAT), lambda s, t: (s, 0, 0)),
                  pl.BlockSpec((1, ROW_TILE, D_MODEL), lambda s, t: (s, t, 0)),
                  full((8, 6 * D_MODEL)), full((1, D_MODEL)), full((1, D_MODEL))],
        out_specs=pl.BlockSpec((1, ROW_TILE, D_MODEL), lambda s, t: (s, t, 0)),
        out_shape=jax.ShapeDtypeStruct((DEC_BATCH, DEC_SEQ, D_MODEL), F32),
        compiler_params=_params(("arbitrary", "arbitrary")),
        name="combine_lat",
    )(yl, idx, x1, mod, ln_g, ln_b)


def _rope_tables():
    t = np.arange(DEC_SEQ)
    row = (t // GRID_W).astype(np.float64)
    col = (t % GRID_W).astype(np.float64)
    half = HEAD_DIM // 2
    inv = ROPE_THETA ** (-np.arange(0, half, 2, dtype=np.float64) / half)
    ang_r = row[:, None] * inv[None, :]
    ang_c = col[:, None] * inv[None, :]
    ang = np.concatenate([ang_r, ang_r, ang_c, ang_c], axis=-1)
    cos = np.tile(np.cos(ang), (1, 2))
    sin = np.tile(np.sin(ang), (1, 2))
    first = (np.arange(LANES) % 32) < 16
    return jnp.asarray(cos, F32), jnp.asarray(np.where(first[None, :], -sin, sin), F32)


def kernel(x_prompt, x_sample, cache_na_k, cache_na_v, cache_diff_k, cache_diff_v, c, c_ctx, w_ada, b_ada, w_in, w_out, na_rel_bias, lambda_q1, lambda_k1, lambda_q2, lambda_k2, subln_g, ln1_g, ln1_b, ln2_g, ln2_b, w_router, w_gate, w_up, w_down):
    l = 0
    cvec = jnp.concatenate([c_ctx[None, :], c, jnp.zeros((8 - 1 - DEC_BATCH, D_MODEL), F32)], axis=0)
    mod = _modulation(cvec, w_ada[l], b_ada[l][None, :])
    lamv = jnp.stack([lambda_q1[l], lambda_k1[l], lambda_q2[l], lambda_k2[l]], axis=0)
    w_in_b = w_in[l].astype(BF16)
    w_out_b = w_out[l].astype(BF16)
    sub = subln_g[l][None, :]
    wrt = w_router[l].T

    tr = lambda a: jnp.swapaxes(a, -1, -2)

    x1c, h2c, affc, na_k_t, na_v_t, diff_k_t, new_diff_v = _context_block(
        x_prompt, mod, lamv, w_in_b, w_out_b, sub, ln1_g[l][None, :], ln1_b[l][None, :], wrt)

    cos, sin = _rope_tables()
    naq, nak, nav, dfq, dfk, dfv, tbl = _latent_qkv(x_sample, mod, w_in_b, cos, sin,
                                                    na_rel_bias[l].reshape(-1))
    na_o = _latent_na(naq, nak, nav, tr(cache_na_k), tr(cache_na_v), tbl)
    df_o = _latent_diff(dfq, dfk, dfv, tr(cache_diff_k), cache_diff_v, lamv, sub)
    x1l, h2l, affl = _latent_out(x_sample, na_o, df_o, mod, w_out_b,
                                 ln1_g[l][None, :], ln1_b[l][None, :], wrt)

    idxc, gatec, idxl, gatel = _route(affc.reshape(BATCH * N_EXPERTS, SEQ),
                                      affl.reshape(DEC_BATCH * N_EXPERTS, DEC_SEQ))
    idxc = idxc.reshape(BATCH, 1, N_EXPERTS * CAP_CTX)
    idxl = idxl.reshape(DEC_BATCH, N_EXPERTS, CAP_LAT)
    xc = _gather_ctx(idxc, h2c)
    xl = _gather_lat(idxl, h2l)
    yc, yl = _expert_ffn(xc, xl, gatec, gatel, w_gate[l], w_up[l], w_down[l])
    y_prompt = _combine_ctx(yc, idxc, x1c, mod, ln2_g[l][None, :], ln2_b[l][None, :])
    y_sample = _combine_lat(yl, idxl, x1l, mod, ln2_g[l][None, :], ln2_b[l][None, :])
    return (y_prompt, y_sample, tr(na_k_t), tr(na_v_t), tr(diff_k_t), new_diff_v)
```

```python
import functools
import math

import jax
import jax.numpy as jnp
import numpy as np
from jax import lax
from jax.experimental import pallas as pl
from jax.experimental.pallas import tpu as pltpu

F32 = jnp.float32
BF16 = jnp.bfloat16
I32 = jnp.int32

D_MODEL = 1024
BATCH = 16
SEQ = 256
DEC_BATCH = 2
DEC_SEQ = 2048
PAST_LEN = 256
GRID_W = 64
GRID_ROWS = DEC_SEQ // GRID_W
HEAD_DIM = 64
NA_HEADS = 8
DIFF_HEADS = 4
NA_WIDTH = NA_HEADS * HEAD_DIM
DIFF_WIDTH = DIFF_HEADS * 2 * HEAD_DIM
QKV_WIDTH = 3 * NA_WIDTH + 3 * DIFF_WIDTH
NA_WIN_H = 8
NA_WIN_W = 16
N_REL_H = 2 * NA_WIN_H - 1
N_REL_W = 2 * NA_WIN_W - 1
N_EXPERTS = 16
EC_CAPACITY_FACTOR = 2
D_FF = 2816
ROPE_THETA = 10000.0
NORM_EPS = 1e-5
NEG_INF = -1e30
DEPTH = 1
DEEPNORM_ALPHA = (2.0 * DEPTH) ** 0.25
LAMBDA_INIT = 0.8 - 0.6 * math.exp(-0.3 * 0)
SCALE = HEAD_DIM ** -0.5

OFF_NAQ = 0
OFF_NAK = NA_WIDTH
OFF_NAV = 2 * NA_WIDTH
OFF_DFQ = 3 * NA_WIDTH
OFF_DFK = 3 * NA_WIDTH + DIFF_WIDTH
OFF_DFV = 3 * NA_WIDTH + 2 * DIFF_WIDTH

CAP_CTX = EC_CAPACITY_FACTOR * SEQ // N_EXPERTS
CAP_LAT = EC_CAPACITY_FACTOR * DEC_SEQ // N_EXPERTS
SLOTS_CTX = BATCH * CAP_CTX
SLOTS_LAT = DEC_BATCH * CAP_LAT

ROW_TILE = 512
NA_QUAD = 4
NA_WIN_ROWS = 12
NA_GROUP = 8
FF_TILE = 256
FF_HALF = D_FF // 2
OUT_TILE = 512
LANES = 128
GATHER_GROUP = 2
CTX_GROUP = 4
CTX_REQ = 2
DIFF_ROWS = 1024
Q_ROWS = 256

VMEM_LIMIT = 48 * 1024 * 1024
FFN_VMEM_LIMIT = 58 * 1024 * 1024


def _params(sem, vmem=VMEM_LIMIT, flags=None):
    return pltpu.CompilerParams(dimension_semantics=sem, vmem_limit_bytes=vmem, flags=flags)


def _mm(a, b):
    return jnp.dot(a, b, preferred_element_type=F32)


def _nt(a, b):
    return lax.dot_general(a, b, (((1,), (1,)), ((), ())), preferred_element_type=F32)


def _split(a):
    hi = a.astype(BF16)
    lo = (a - hi.astype(F32)).astype(BF16)
    return hi, lo


def _mm3(a, b):
    ah, al = _split(a)
    bh, bl = _split(b)
    return _mm(ah, bh) + _mm(al, bh) + _mm(ah, bl)


def _nt3(a, b):
    ah, al = _split(a)
    bh, bl = _split(b)
    return _nt(ah, bh) + _nt(al, bh) + _nt(ah, bl)


def _layer_norm(y, g, b):
    mu = jnp.mean(y, axis=-1, keepdims=True)
    d = y - mu
    var = jnp.mean(d * d, axis=-1, keepdims=True)
    return d * lax.rsqrt(var + NORM_EPS) * g + b


def _lam(l_ref):
    l = l_ref[...]
    a = jnp.sum(l[0:1] * l[1:2], axis=-1, keepdims=True)
    b = jnp.sum(l[2:3] * l[3:4], axis=-1, keepdims=True)
    return jnp.exp(a) - jnp.exp(b) + LAMBDA_INIT


def _mod_rows(mod_ref, row):
    return [mod_ref[pl.ds(row, 1), i * D_MODEL:(i + 1) * D_MODEL] for i in range(6)]


def _sub_norm(o, sub):
    ms = jnp.mean(o * o, axis=-1, keepdims=True)
    return o * lax.rsqrt(ms + NORM_EPS) * sub * (1.0 - LAMBDA_INIT)


def _router_aff_t(h2, wrt):
    lg = _nt3(wrt, h2)
    m = jnp.max(lg, axis=0, keepdims=True)
    e = jnp.exp(lg - m)
    return e / jnp.sum(e, axis=0, keepdims=True)


def _mod_kernel(c_ref, wa_ref, wb_ref, b_ref, o_ref):
    c = c_ref[...]
    s = c * (1.0 / (1.0 + jnp.exp(-c)))
    half = D_MODEL // 2
    o_ref[...] = _mm3(s[:, :half], wa_ref[...]) + _mm3(s[:, half:], wb_ref[...]) + b_ref[...]


def _modulation(cvec, w_ada, b_ada):
    ncol = 6 * D_MODEL
    half = D_MODEL // 2
    return pl.pallas_call(
        _mod_kernel,
        grid=(6,),
        in_specs=[pl.BlockSpec((8, D_MODEL), lambda j: (0, 0)),
                  pl.BlockSpec((half, D_MODEL), lambda j: (0, j)),
                  pl.BlockSpec((half, D_MODEL), lambda j: (1, j)),
                  pl.BlockSpec((1, D_MODEL), lambda j: (0, j))],
        out_specs=pl.BlockSpec((8, D_MODEL), lambda j: (0, j)),
        out_shape=jax.ShapeDtypeStruct((8, ncol), F32),
        compiler_params=_params(("arbitrary",)),
        name="modulation",
    )(cvec, w_ada, w_ada, b_ada)


def _ctx_kernel(x_ref, mod_ref, lam_ref, win_ref, wout_ref, sub_ref, lng_ref, lnb_ref, wrt_ref,
                x1_ref, h2_ref, aff_ref, nak_ref, nav_ref, dfk_ref, dfv_ref,
                qkv_scr, o_scr, s_scr, e_scr):
    rows = CTX_REQ * SEQ
    x = x_ref[...].reshape(rows, D_MODEL)
    sh1, sc1, g1, sh2, sc2, g2 = _mod_rows(mod_ref, 0)
    h = (x * (1.0 + sc1) + sh1).astype(BF16)
    qkv_scr[...] = _mm(h, win_ref[...])

    kt_na = qkv_scr[:, OFF_NAK:OFF_NAK + NA_WIDTH].T
    kt_df = qkv_scr[:, OFF_DFK:OFF_DFK + DIFF_WIDTH].T
    vt_na = qkv_scr[:, OFF_NAV:OFF_NAV + NA_WIDTH].T
    for r in range(CTX_REQ):
        tok = slice(SEQ * r, SEQ * (r + 1))
        nak_ref[r, 0] = kt_na[:, tok].reshape(NA_HEADS, HEAD_DIM, SEQ)
        nav_ref[r, 0] = vt_na[:, tok].reshape(NA_HEADS, HEAD_DIM, SEQ)
        dfk_ref[r, 0] = kt_df[:, tok].reshape(DIFF_HEADS, 2, HEAD_DIM, SEQ)
        for hh in range(DIFF_HEADS):
            c0 = OFF_DFV + 2 * HEAD_DIM * hh
            dfv_ref[r, 0, hh] = qkv_scr[tok, c0:c0 + 2 * HEAD_DIM]

    kt = jnp.concatenate([kt_na, kt_df], axis=0).astype(BF16)
    q_cols = [OFF_NAQ + HEAD_DIM * i for i in range(NA_HEADS)] + [
        OFF_DFQ + HEAD_DIM * i for i in range(2 * DIFF_HEADS)]
    nmap = len(q_cols)
    for r in range(CTX_REQ):
        tok = slice(SEQ * r, SEQ * (r + 1))
        for i, c in enumerate(q_cols):
            q = (qkv_scr[tok, c:c + HEAD_DIM] * SCALE).astype(BF16)
            s_scr[nmap * r + i] = _mm(q, kt[HEAD_DIM * i:HEAD_DIM * (i + 1), tok])
    s = s_scr[...]
    e = jnp.exp(s - jnp.max(s, axis=-1, keepdims=True))
    rl = 1.0 / jnp.sum(e, axis=-1, keepdims=True)
    e_scr[...] = e.astype(BF16)

    lam = _lam(lam_ref)
    sub = sub_ref[...]
    for r in range(CTX_REQ):
        tok = slice(SEQ * r, SEQ * (r + 1))
        for hh in range(NA_HEADS):
            i = nmap * r + hh
            v = qkv_scr[tok, OFF_NAV + HEAD_DIM * hh:OFF_NAV + HEAD_DIM * (hh + 1)].astype(BF16)
            o_scr[tok, HEAD_DIM * hh:HEAD_DIM * (hh + 1)] = (_mm(e_scr[i], v) * rl[i]).astype(BF16)
        for hh in range(DIFF_HEADS):
            i1 = nmap * r + NA_HEADS + 2 * hh
            c0 = OFF_DFV + 2 * HEAD_DIM * hh
            v = qkv_scr[tok, c0:c0 + 2 * HEAD_DIM].astype(BF16)
            o = _mm(e_scr[i1], v) * rl[i1] - _mm(e_scr[i1 + 1], v) * (lam * rl[i1 + 1])
            c1 = NA_WIDTH + 2 * HEAD_DIM * hh
            o_scr[tok, c1:c1 + 2 * HEAD_DIM] = _sub_norm(o, sub).astype(BF16)

    a = _mm(o_scr[...], wout_ref[...])
    x1 = _layer_norm(DEEPNORM_ALPHA * x + g1 * a, lng_ref[...], lnb_ref[...])
    x1_ref[...] = x1.reshape(CTX_REQ, SEQ, D_MODEL)
    h2 = x1 * (1.0 + sc2) + sh2
    h2_ref[...] = h2.astype(BF16).reshape(CTX_REQ, SEQ, D_MODEL)
    aff = _router_aff_t(h2, wrt_ref[...])
    for r in range(CTX_REQ):
        aff_ref[r] = aff[:, SEQ * r:SEQ * (r + 1)]


def _context_block(x, mod, lamv, w_in, w_out, sub, ln_g, ln_b, wrt):
    full = lambda shape: pl.BlockSpec(shape, lambda b: (0,) * len(shape))
    once = lambda shape: pl.BlockSpec(shape, lambda b: (0,) * len(shape), pipeline_mode=pl.Buffered(1))
    req = lambda *tail: pl.BlockSpec((CTX_REQ,) + tail, lambda b: (b,) + (0,) * len(tail))
    nmaps = CTX_REQ * (NA_HEADS + 2 * DIFF_HEADS)
    return pl.pallas_call(
        _ctx_kernel,
        grid=(BATCH // CTX_REQ,),
        in_specs=[req(SEQ, D_MODEL),
                  full((8, 6 * D_MODEL)), full((4, HEAD_DIM)),
                  once((D_MODEL, QKV_WIDTH)), once((D_MODEL, D_MODEL)),
                  full((1, 2 * HEAD_DIM)), full((1, D_MODEL)), full((1, D_MODEL)),
                  full((N_EXPERTS, D_MODEL))],
        out_specs=[req(SEQ, D_MODEL), req(SEQ, D_MODEL), req(N_EXPERTS, SEQ),
                   req(1, NA_HEADS, HEAD_DIM, SEQ), req(1, NA_HEADS, HEAD_DIM, SEQ),
                   req(1, DIFF_HEADS, 2, HEAD_DIM, SEQ), req(1, DIFF_HEADS, SEQ, 2 * HEAD_DIM)],
        out_shape=[jax.ShapeDtypeStruct((BATCH, SEQ, D_MODEL), F32),
                   jax.ShapeDtypeStruct((BATCH, SEQ, D_MODEL), BF16),
                   jax.ShapeDtypeStruct((BATCH, N_EXPERTS, SEQ), F32),
                   jax.ShapeDtypeStruct((BATCH, 1, NA_HEADS, HEAD_DIM, SEQ), F32),
                   jax.ShapeDtypeStruct((BATCH, 1, NA_HEADS, HEAD_DIM, SEQ), F32),
                   jax.ShapeDtypeStruct((BATCH, 1, DIFF_HEADS, 2, HEAD_DIM, SEQ), F32),
                   jax.ShapeDtypeStruct((BATCH, 1, DIFF_HEADS, SEQ, 2 * HEAD_DIM), F32)],
        scratch_shapes=[pltpu.VMEM((CTX_REQ * SEQ, QKV_WIDTH), F32), pltpu.VMEM((CTX_REQ * SEQ, D_MODEL), BF16),
                        pltpu.VMEM((nmaps, SEQ, SEQ), F32), pltpu.VMEM((nmaps, SEQ, SEQ), BF16)],
        compiler_params=_params(("arbitrary",)),
        name="context_block",
    )(x, mod, lamv, w_in, w_out, sub, ln_g, ln_b, wrt)


def _lat_qkv_kernel(x_ref, mod_ref, win_ref, cos_ref, sin_ref, rpb_ref,
                    naq_ref, nak_ref, nav_ref, dfq_ref, dfk_ref, dfv_ref, tbl_ref, qkv_scr):
    b = pl.program_id(0)
    _write_bias_tables(rpb_ref, b * pl.num_programs(1) + pl.program_id(1), tbl_ref)
    x = x_ref[0]
    sh1, sc1 = _mod_rows(mod_ref, 1 + b)[:2]
    h = (x * (1.0 + sc1) + sh1).astype(BF16)
    qkv_scr[...] = _mm(h, win_ref[...])

    for hh in range(NA_HEADS):
        lo, hi = HEAD_DIM * hh, HEAD_DIM * (hh + 1)
        naq_ref[0, hh] = (qkv_scr[:, OFF_NAQ + lo:OFF_NAQ + hi] * SCALE).astype(BF16)
        nak_ref[0, hh] = qkv_scr[:, OFF_NAK + lo:OFF_NAK + hi].astype(BF16)
        nav_ref[0, hh] = qkv_scr[:, OFF_NAV + lo:OFF_NAV + hi].astype(BF16)

    cos = cos_ref[...]
    sin = sin_ref[...]
    lane = lax.broadcasted_iota(I32, (ROW_TILE, LANES), 1)
    first = (lane & 31) < 16

    def rope(t):
        rot = jnp.where(first, pltpu.roll(t, LANES - 16, 1), pltpu.roll(t, 16, 1))
        return t * cos + rot * sin

    for hh in range(DIFF_HEADS):
        lo = 2 * HEAD_DIM * hh
        q = rope(qkv_scr[:, OFF_DFQ + lo:OFF_DFQ + lo + LANES]) * SCALE
        k = rope(qkv_scr[:, OFF_DFK + lo:OFF_DFK + lo + LANES])
        for m in range(2):
            dfq_ref[0, hh, m] = q[:, HEAD_DIM * m:HEAD_DIM * (m + 1)].astype(BF16)
        dfk_ref[0, hh] = k.T.astype(BF16).reshape(2, HEAD_DIM, ROW_TILE)
        dfv_ref[0, hh] = qkv_scr[:, OFF_DFV + lo:OFF_DFV + lo + LANES].astype(BF16)


def _latent_qkv(x, mod, w_in, cos, sin, rpb_flat):
    nt = DEC_SEQ // ROW_TILE
    assert DEC_BATCH * nt == NA_HEADS
    tbl_shape = (NA_HEADS, 3, NA_QUAD * GRID_W, NA_WIN_ROWS * GRID_W)
    full = lambda shape: pl.BlockSpec(shape, lambda b, t: (0,) * len(shape))
    hs = lambda nh: pl.BlockSpec((1, nh, ROW_TILE, HEAD_DIM), lambda b, t: (b, 0, t, 0))
    return pl.pallas_call(
        _lat_qkv_kernel,
        grid=(DEC_BATCH, nt),
        in_specs=[pl.BlockSpec((1, ROW_TILE, D_MODEL), lambda b, t: (b, t, 0)),
                  full((8, 6 * D_MODEL)), full((D_MODEL, QKV_WIDTH)),
                  pl.BlockSpec((ROW_TILE, LANES), lambda b, t: (t, 0)),
                  pl.BlockSpec((ROW_TILE, LANES), lambda b, t: (t, 0)),
                  pl.BlockSpec(memory_space=pltpu.SMEM)],
        out_specs=[hs(NA_HEADS), hs(NA_HEADS), hs(NA_HEADS),
                   pl.BlockSpec((1, DIFF_HEADS, 2, ROW_TILE, HEAD_DIM), lambda b, t: (b, 0, 0, t, 0)),
                   pl.BlockSpec((1, DIFF_HEADS, 2, HEAD_DIM, ROW_TILE), lambda b, t: (b, 0, 0, 0, t)),
                   pl.BlockSpec((1, DIFF_HEADS, ROW_TILE, 2 * HEAD_DIM), lambda b, t: (b, 0, t, 0)),
                   pl.BlockSpec((1,) + tbl_shape[1:], lambda b, t: (b * nt + t, 0, 0, 0))],
        out_shape=[jax.ShapeDtypeStruct((DEC_BATCH, NA_HEADS, DEC_SEQ, HEAD_DIM), BF16)] * 3
        + [jax.ShapeDtypeStruct((DEC_BATCH, DIFF_HEADS, 2, DEC_SEQ, HEAD_DIM), BF16),
           jax.ShapeDtypeStruct((DEC_BATCH, DIFF_HEADS, 2, HEAD_DIM, DEC_SEQ), BF16),
           jax.ShapeDtypeStruct((DEC_BATCH, DIFF_HEADS, DEC_SEQ, 2 * HEAD_DIM), BF16),
           jax.ShapeDtypeStruct(tbl_shape, F32)],
        scratch_shapes=[pltpu.VMEM((ROW_TILE, QKV_WIDTH), F32)],
        compiler_params=_params(("arbitrary", "arbitrary")),
        name="latent_qkv",
    )(x, mod, w_in, cos, sin, rpb_flat)


def _na_row_offset(kind, qr, kr):
    if kind == 0:
        return kr - qr + 7 if kr < NA_WIN_H else None
    if kind == 1:
        return kr - qr + 3 if qr <= kr < qr + NA_WIN_H else None
    return kr - qr - 1 if NA_WIN_ROWS - NA_WIN_H <= kr else None


def _write_bias_tables(rpb_ref, h, o_ref):
    base = h * (N_REL_H * N_REL_W)
    wq = lax.broadcasted_iota(I32, (GRID_W, LANES), 0)
    c = lax.broadcasted_iota(I32, (GRID_W, LANES), 1)
    wk = c & (GRID_W - 1)
    right = c >= GRID_W
    c0 = jnp.clip(wq - NA_WIN_W // 2, 0, GRID_W - NA_WIN_W)
    in_win = (wk >= c0) & (wk < c0 + NA_WIN_W)
    cache = {}
    strips = {}

    def strip(r):
        if r not in strips:
            c8 = lax.broadcasted_iota(I32, (8, LANES), 1)
            d = jnp.where(c8 < GRID_W, c8, c8 - LANES)
            co_of = jnp.clip(d, -(NA_WIN_W - 1), NA_WIN_W - 1) + (NA_WIN_W - 1)
            acc = jnp.zeros((8, LANES), F32)
            for co in range(N_REL_W):
                acc = jnp.where(co_of == co, rpb_ref[base + r * N_REL_W + co], acc)
            strips[r] = jnp.concatenate([acc] * (GRID_W // 8), axis=0)
        return strips[r]

    def tile(rl, rr):
        if (rl, rr) in cache:
            return cache[(rl, rr)]
        if rl is None and rr is None:
            t = jnp.full((GRID_W, LANES), NEG_INF, F32)
        else:
            zero = jnp.zeros((GRID_W, LANES), F32)
            tl = pltpu.roll(strip(rl), 0, 1, stride=1, stride_axis=0) if rl is not None else zero
            tr = pltpu.roll(strip(rr), GRID_W, 1, stride=1, stride_axis=0) if rr is not None else zero
            acc = jnp.where(right, tr, tl)
            ok = in_win
            if rl is None:
                ok = ok & right
            if rr is None:
                ok = ok & jnp.logical_not(right)
            t = jnp.where(ok, acc, NEG_INF)
        cache[(rl, rr)] = t
        return t

    for kind in range(3):
        for qr in range(NA_QUAD):
            row = [tile(_na_row_offset(kind, qr, 2 * p), _na_row_offset(kind, qr, 2 * p + 1))
                   for p in range(NA_WIN_ROWS // 2)]
            o_ref[0, kind, GRID_W * qr:GRID_W * (qr + 1), :] = jnp.concatenate(row, axis=1)


def _na_kernel(q_ref, k_ref, v_ref, ck_ref, cv_ref, tbl_ref, o_ref):
    g = pl.program_id(2)
    u0 = jnp.clip(NA_QUAD * g - NA_QUAD, 0, GRID_ROWS - NA_WIN_ROWS)
    start = pl.multiple_of(u0 * GRID_W, GRID_W)
    nwin = NA_WIN_ROWS * GRID_W
    for j in range(NA_GROUP):
        q = q_ref[0, j]
        kw = k_ref[0, j, pl.ds(start, nwin), :]
        vw = v_ref[0, j, pl.ds(start, nwin), :]
        kct = ck_ref[0, 0, j].astype(BF16)
        vct = cv_ref[0, 0, j].astype(BF16)
        sl = _nt(q, kw) + tbl_ref[j, 0]
        sc = _mm(q, kct)
        m = jnp.maximum(jnp.max(sl, axis=-1, keepdims=True), jnp.max(sc, axis=-1, keepdims=True))
        el = jnp.exp(sl - m)
        ec = jnp.exp(sc - m)
        l = jnp.sum(el, axis=-1, keepdims=True) + jnp.sum(ec, axis=-1, keepdims=True)
        o = (_mm(el.astype(BF16), vw) + _nt(ec.astype(BF16), vct)) / l
        o_ref[0, :, HEAD_DIM * j:HEAD_DIM * (j + 1)] = o.astype(BF16)


def _latent_na(naq, nak, nav, cache_k, cache_v, tbl):
    nq = GRID_ROWS // NA_QUAD
    rows = NA_QUAD * GRID_W
    kind = lambda g: jnp.minimum(g, 1) + g // (nq - 1)
    return pl.pallas_call(
        _na_kernel,
        grid=(DEC_BATCH, NA_HEADS // NA_GROUP, nq),
        in_specs=[pl.BlockSpec((1, NA_GROUP, rows, HEAD_DIM), lambda b, p, g: (b, p, g, 0)),
                  pl.BlockSpec((1, NA_GROUP, DEC_SEQ, HEAD_DIM), lambda b, p, g: (b, p, 0, 0)),
                  pl.BlockSpec((1, NA_GROUP, DEC_SEQ, HEAD_DIM), lambda b, p, g: (b, p, 0, 0)),
                  pl.BlockSpec((1, 1, NA_GROUP, HEAD_DIM, PAST_LEN), lambda b, p, g: (b, 0, p, 0, 0)),
                  pl.BlockSpec((1, 1, NA_GROUP, HEAD_DIM, PAST_LEN), lambda b, p, g: (b, 0, p, 0, 0)),
                  pl.BlockSpec((NA_GROUP, 1, rows, NA_WIN_ROWS * GRID_W), lambda b, p, g: (p, kind(g), 0, 0))],
        out_specs=pl.BlockSpec((1, rows, NA_GROUP * HEAD_DIM), lambda b, p, g: (b, g, p)),
        out_shape=jax.ShapeDtypeStruct((DEC_BATCH, DEC_SEQ, NA_WIDTH), BF16),
        compiler_params=_params(("arbitrary",) * 3),
        name="latent_na",
    )(naq, nak, nav, cache_k, cache_v, tbl)


def _diff_kernel(q_ref, kt_ref, v_ref, ck_ref, cv_ref, lam_ref, sub_ref, o_ref):
    lam = _lam(lam_ref)
    vc = cv_ref[0, 0, 0].astype(BF16)
    vl = v_ref[0, 0]
    for r in range(DIFF_ROWS // Q_ROWS):
        rows = slice(Q_ROWS * r, Q_ROWS * (r + 1))
        parts = []
        for m in range(2):
            q = q_ref[0, 0, m, rows, :]
            sc = _mm(q, ck_ref[0, 0, 0, m].astype(BF16))
            sl = _mm(q, kt_ref[0, 0, m])
            mx = jnp.maximum(jnp.max(sc, axis=-1, keepdims=True), jnp.max(sl, axis=-1, keepdims=True))
            ec = jnp.exp(sc - mx)
            el = jnp.exp(sl - mx)
            l = jnp.sum(ec, axis=-1, keepdims=True) + jnp.sum(el, axis=-1, keepdims=True)
            parts.append((_mm(ec.astype(BF16), vc) + _mm(el.astype(BF16), vl), l))
        (a1, l1), (a2, l2) = parts
        o = a1 * (1.0 / l1) - a2 * (lam / l2)
        o_ref[0, rows, :] = _sub_norm(o, sub_ref[...]).astype(BF16)


def _latent_diff(dfq, dfk, dfv, cache_k, cache_v, lamv, sub):
    nt = DEC_SEQ // DIFF_ROWS
    full = lambda shape: pl.BlockSpec(shape, lambda b, h, t: (0,) * len(shape))
    return pl.pallas_call(
        _diff_kernel,
        grid=(DEC_BATCH, DIFF_HEADS, nt),
        in_specs=[pl.BlockSpec((1, 1, 2, DIFF_ROWS, HEAD_DIM), lambda b, h, t: (b, h, 0, t, 0)),
                  pl.BlockSpec((1, 1, 2, HEAD_DIM, DEC_SEQ), lambda b, h, t: (b, h, 0, 0, 0)),
                  pl.BlockSpec((1, 1, DEC_SEQ, 2 * HEAD_DIM), lambda b, h, t: (b, h, 0, 0)),
                  pl.BlockSpec((1, 1, 1, 2, HEAD_DIM, PAST_LEN), lambda b, h, t: (b, 0, h, 0, 0, 0)),
                  pl.BlockSpec((1, 1, 1, PAST_LEN, 2 * HEAD_DIM), lambda b, h, t: (b, 0, h, 0, 0)),
                  full((4, HEAD_DIM)), full((1, 2 * HEAD_DIM))],
        out_specs=pl.BlockSpec((1, DIFF_ROWS, 2 * HEAD_DIM), lambda b, h, t: (b, t, h)),
        out_shape=jax.ShapeDtypeStruct((DEC_BATCH, DEC_SEQ, DIFF_WIDTH), BF16),
        compiler_params=_params(("arbitrary",) * 3),
        name="latent_diff",
    )(dfq, dfk, dfv, cache_k, cache_v, lamv, sub)


def _lat_out_kernel(x_ref, na_ref, df_ref, mod_ref, wout_ref, lng_ref, lnb_ref, wrt_ref,
                    x1_ref, h2_ref, aff_ref):
    b = pl.program_id(0)
    x = x_ref[0]
    _, _, g1, sh2, sc2, _ = _mod_rows(mod_ref, 1 + b)
    a = _mm(na_ref[0], wout_ref[0:NA_WIDTH, :]) + _mm(df_ref[0], wout_ref[NA_WIDTH:, :])
    x1 = _layer_norm(DEEPNORM_ALPHA * x + g1 * a, lng_ref[...], lnb_ref[...])
    x1_ref[0] = x1
    h2 = x1 * (1.0 + sc2) + sh2
    h2_ref[0] = h2.astype(BF16)
    aff_ref[0] = _router_aff_t(h2, wrt_ref[...])


def _latent_out(x, na_o, df_o, mod, w_out, ln_g, ln_b, wrt):
    nt = DEC_SEQ // ROW_TILE
    full = lambda shape: pl.BlockSpec(shape, lambda b, t: (0,) * len(shape))
    tok = lambda w: pl.BlockSpec((1, ROW_TILE, w), lambda b, t: (b, t, 0))
    return pl.pallas_call(
        _lat_out_kernel,
        grid=(DEC_BATCH, nt),
        in_specs=[tok(D_MODEL), tok(NA_WIDTH), tok(DIFF_WIDTH), full((8, 6 * D_MODEL)),
                  full((D_MODEL, D_MODEL)), full((1, D_MODEL)), full((1, D_MODEL)),
                  full((N_EXPERTS, D_MODEL))],
        out_specs=[tok(D_MODEL), tok(D_MODEL),
                   pl.BlockSpec((1, N_EXPERTS, ROW_TILE), lambda b, t: (b, 0, t))],
        out_shape=[jax.ShapeDtypeStruct((DEC_BATCH, DEC_SEQ, D_MODEL), F32),
                   jax.ShapeDtypeStruct((DEC_BATCH, DEC_SEQ, D_MODEL), BF16),
                   jax.ShapeDtypeStruct((DEC_BATCH, N_EXPERTS, DEC_SEQ), F32)],
        compiler_params=_params(("arbitrary", "arbitrary")),
        name="latent_out",
    )(x, na_o, df_o, mod, w_out, ln_g, ln_b, wrt)


def _ranks_first(va, ia, vb, ib):
    return (va > vb) | ((va == vb) & (ia < ib))


def _sort_rows_desc(aff):
    rows, n = aff.shape
    nb = n // LANES
    lane = lax.broadcasted_iota(I32, (rows, LANES), 1)
    v = [aff[:, LANES * b:LANES * (b + 1)] for b in range(nb)]
    ix = [lane + LANES * b for b in range(nb)]
    k = 2
    while k <= n:
        j = k // 2
        while j >= 1:
            if j >= LANES:
                sb = j // LANES
                for lo in range(nb):
                    if lo & sb:
                        continue
                    hi = lo | sb
                    f = _ranks_first(v[lo], ix[lo], v[hi], ix[hi])
                    if k < n and (LANES * lo) & k:
                        f = jnp.logical_not(f)
                    v[lo], v[hi] = jnp.where(f, v[lo], v[hi]), jnp.where(f, v[hi], v[lo])
                    ix[lo], ix[hi] = jnp.where(f, ix[lo], ix[hi]), jnp.where(f, ix[hi], ix[lo])
            else:
                upper = (lane & j) != 0
                if k < LANES:
                    flip_lanes = jnp.logical_xor(upper, (lane & k) != 0)
                for b in range(nb):
                    pv = jnp.where(upper, pltpu.roll(v[b], j, 1), pltpu.roll(v[b], LANES - j, 1))
                    pi = jnp.where(upper, pltpu.roll(ix[b], j, 1), pltpu.roll(ix[b], LANES - j, 1))
                    f = _ranks_first(v[b], ix[b], pv, pi)
                    if k < LANES:
                        flip = flip_lanes
                    elif k < n and (LANES * b) & k:
                        flip = jnp.logical_not(upper)
                    else:
                        flip = upper
                    keep = jnp.logical_xor(f, flip)
                    v[b] = jnp.where(keep, v[b], pv)
                    ix[b] = jnp.where(keep, ix[b], pi)
            j //= 2
        k *= 2
    return v, ix


def _route_kernel(affc_ref, affl_ref, idxc_ref, gatec_ref, idxl_ref, gatel_ref):
    for aff_ref, idx_ref, gate_ref, cap in ((affc_ref, idxc_ref, gatec_ref, CAP_CTX),
                                            (affl_ref, idxl_ref, gatel_ref, CAP_LAT)):
        v, ix = _sort_rows_desc(aff_ref[...])
        if cap < LANES:
            idx_ref[...] = ix[0][:, :cap]
            gate_ref[...] = v[0]
        else:
            idx_ref[...] = jnp.concatenate(ix[:cap // LANES], axis=1)
            gate_ref[...] = jnp.concatenate(v[:cap // LANES], axis=1)


def _route(affc_t, affl_t):
    out = lambda a, cap: [jax.ShapeDtypeStruct((a.shape[0], cap), I32),
                          jax.ShapeDtypeStruct((a.shape[0], max(cap, LANES)), F32)]
    return pl.pallas_call(
        _route_kernel,
        out_shape=out(affc_t, CAP_CTX) + out(affl_t, CAP_LAT),
        compiler_params=_params(None),
        name="route",
    )(affc_t, affl_t)


def _row_to_col(row):
    n = row.shape[1]
    eye = lax.broadcasted_iota(I32, (n, n), 0) == lax.broadcasted_iota(I32, (n, n), 1)
    return jnp.sum(jnp.where(eye, row, jnp.zeros_like(row)), axis=1, keepdims=True)


def _gather_ctx_kernel(idx_ref, h_ref, x_ref):
    nslot = N_EXPERTS * CAP_CTX
    for s in range(CTX_GROUP):
        hit = lax.broadcasted_iota(I32, (nslot, SEQ), 1) == _row_to_col(idx_ref[s])
        xs = _mm(jnp.where(hit, 1.0, 0.0).astype(BF16), h_ref[s])
        x_ref[:, CAP_CTX * s:CAP_CTX * (s + 1), :] = xs.astype(BF16).reshape(N_EXPERTS, CAP_CTX, D_MODEL)


def _gather_ctx(idx_row, h2):
    nslot = N_EXPERTS * CAP_CTX
    return pl.pallas_call(
        _gather_ctx_kernel,
        grid=(BATCH // CTX_GROUP,),
        in_specs=[pl.BlockSpec((CTX_GROUP, 1, nslot), lambda s: (s, 0, 0)),
                  pl.BlockSpec((CTX_GROUP, SEQ, D_MODEL), lambda s: (s, 0, 0))],
        out_specs=pl.BlockSpec((N_EXPERTS, CTX_GROUP * CAP_CTX, D_MODEL), lambda s: (0, s, 0)),
        out_shape=jax.ShapeDtypeStruct((N_EXPERTS, SLOTS_CTX, D_MODEL), BF16),
        compiler_params=_params(("arbitrary",)),
        name="gather_ctx",
    )(idx_row, h2)


def _gather_lat_kernel(idx_ref, h_ref, x_ref):
    first = GATHER_GROUP * pl.program_id(1)
    col = jnp.concatenate([_row_to_col(idx_ref[0, pl.ds(first + i, 1), :]) for i in range(GATHER_GROUP)],
                          axis=0)
    hit = lax.broadcasted_iota(I32, (GATHER_GROUP * CAP_LAT, DEC_SEQ), 1) == col
    xs = _mm(jnp.where(hit, 1.0, 0.0).astype(BF16), h_ref[0])
    x_ref[...] = xs.astype(BF16).reshape(GATHER_GROUP, CAP_LAT, D_MODEL)


def _gather_lat(idx, h2):
    return pl.pallas_call(
        _gather_lat_kernel,
        grid=(DEC_BATCH, N_EXPERTS // GATHER_GROUP),
        in_specs=[pl.BlockSpec((1, N_EXPERTS, CAP_LAT), lambda s, e: (s, 0, 0)),
                  pl.BlockSpec((1, DEC_SEQ, D_MODEL), lambda s, e: (s, 0, 0))],
        out_specs=pl.BlockSpec((GATHER_GROUP, CAP_LAT, D_MODEL), lambda s, e: (e, s, 0)),
        out_shape=jax.ShapeDtypeStruct((N_EXPERTS, SLOTS_LAT, D_MODEL), BF16),
        compiler_params=_params(("arbitrary", "arbitrary")),
        name="gather_lat",
    )(idx, h2)


def _ffn_kernel(xc_ref, xl_ref, gc_ref, gl_ref, wg_ref, wu_ref, wd_ref, yc_ref, yl_ref, h_scr, wl_scr):
    j = pl.program_id(1)
    tail = FF_HALF - FF_HALF // FF_TILE * FF_TILE

    def swiglu_cols(wg, wu, c0):
        wgb = wg.astype(BF16)
        wub = wu.astype(BF16)
        for i, x_ref in enumerate((xc_ref, xl_ref)):
            x = x_ref[0]
            a = _mm(x, wgb)
            u = _mm(x, wub)
            h = a * (1.0 / (1.0 + jnp.exp(-a))) * u
            h_scr[SLOTS_CTX * i:SLOTS_CTX * (i + 1), c0:c0 + FF_TILE] = h.astype(BF16)

    @pl.when(j == 0)
    def _():
        for c in range(FF_HALF // FF_TILE):
            lo = FF_TILE * c
            swiglu_cols(wg_ref[0, :, lo:lo + FF_TILE], wu_ref[0, :, lo:lo + FF_TILE], lo)
        wl_scr[0] = wg_ref[0, :, FF_HALF - tail:FF_HALF]
        wl_scr[1] = wu_ref[0, :, FF_HALF - tail:FF_HALF]

    @pl.when(j == 1)
    def _():
        head = FF_TILE - tail
        swiglu_cols(jnp.concatenate([wl_scr[0], wg_ref[0, :, 0:head]], axis=1),
                    jnp.concatenate([wl_scr[1], wu_ref[0, :, 0:head]], axis=1), FF_HALF - tail)
        for c in range(FF_HALF // FF_TILE):
            lo = head + FF_TILE * c
            swiglu_cols(wg_ref[0, :, lo:lo + FF_TILE], wu_ref[0, :, lo:lo + FF_TILE], FF_HALF + lo)

    @pl.when(j >= 2)
    def _():
        wdb = wd_ref[0].astype(BF16)
        e = pl.program_id(0)
        for i, (g_ref, y_ref) in enumerate(((gc_ref, yc_ref), (gl_ref, yl_ref))):
            nset = g_ref.shape[0] // N_EXPERTS
            gate = jnp.concatenate(
                [_row_to_col(g_ref[pl.ds(N_EXPERTS * s + e, 1), :][:, :SLOTS_CTX // nset]) for s in range(nset)],
                axis=0)
            y = _mm(h_scr[SLOTS_CTX * i:SLOTS_CTX * (i + 1), :], wdb)
            y_ref[0] = (y * gate).astype(BF16)


def _expert_ffn(xc, xl, gc, gl, w_gate, w_up, w_down):
    assert SLOTS_CTX == SLOTS_LAT and FF_HALF % LANES == 0 and D_MODEL % OUT_TILE == 0
    xs = lambda n: pl.BlockSpec((1, n, D_MODEL), lambda e, j: (e, 0, 0))
    gs = lambda g: pl.BlockSpec(g.shape, lambda e, j: (0, 0))
    last = N_EXPERTS - 1
    up = pl.BlockSpec((1, D_MODEL, FF_HALF),
                      lambda e, j: (jnp.where(j >= 2, jnp.minimum(e + 1, last), e), 0, jnp.where(j == 1, 1, 0)))
    down = pl.BlockSpec((1, D_FF, OUT_TILE), lambda e, j: (e, 0, jnp.maximum(j - 2, 0)))
    ys = lambda n: pl.BlockSpec((1, n, OUT_TILE), lambda e, j: (e, 0, jnp.maximum(j - 2, 0)))
    return pl.pallas_call(
        _ffn_kernel,
        grid=(N_EXPERTS, 2 + D_MODEL // OUT_TILE),
        in_specs=[xs(SLOTS_CTX), xs(SLOTS_LAT), gs(gc), gs(gl), up, up, down],
        out_specs=[ys(SLOTS_CTX), ys(SLOTS_LAT)],
        out_shape=[jax.ShapeDtypeStruct((N_EXPERTS, SLOTS_CTX, D_MODEL), BF16),
                   jax.ShapeDtypeStruct((N_EXPERTS, SLOTS_LAT, D_MODEL), BF16)],
        scratch_shapes=[pltpu.VMEM((SLOTS_CTX + SLOTS_LAT, D_FF), BF16),
                        pltpu.VMEM((2, D_MODEL, LANES), F32)],
        compiler_params=_params(("arbitrary", "arbitrary"), FFN_VMEM_LIMIT),
        name="expert_ffn",
    )(xc, xl, gc, gl, w_gate, w_up, w_down)


def _combine_ctx_kernel(y_ref, idx_ref, x1_ref, mod_ref, lng_ref, lnb_ref, o_ref):
    g2 = _mod_rows(mod_ref, 0)[5]
    nslot = N_EXPERTS * CAP_CTX
    for s in range(CTX_GROUP):
        hit = lax.broadcasted_iota(I32, (SEQ, nslot), 0) == idx_ref[s]
        y = y_ref[:, CAP_CTX * s:CAP_CTX * (s + 1), :].reshape(nslot, D_MODEL)
        f = _mm(jnp.where(hit, 1.0, 0.0).astype(BF16), y)
        o_ref[s] = _layer_norm(DEEPNORM_ALPHA * x1_ref[s] + g2 * f, lng_ref[...], lnb_ref[...])


def _combine_ctx(yc, idx_row, x1, mod, ln_g, ln_b):
    full = lambda shape: pl.BlockSpec(shape, lambda s: (0,) * len(shape))
    return pl.pallas_call(
        _combine_ctx_kernel,
        grid=(BATCH // CTX_GROUP,),
        in_specs=[pl.BlockSpec((N_EXPERTS, CTX_GROUP * CAP_CTX, D_MODEL), lambda s: (0, s, 0)),
                  pl.BlockSpec((CTX_GROUP, 1, N_EXPERTS * CAP_CTX), lambda s: (s, 0, 0)),
                  pl.BlockSpec((CTX_GROUP, SEQ, D_MODEL), lambda s: (s, 0, 0)),
                  full((8, 6 * D_MODEL)), full((1, D_MODEL)), full((1, D_MODEL))],
        out_specs=pl.BlockSpec((CTX_GROUP, SEQ, D_MODEL), lambda s: (s, 0, 0)),
        out_shape=jax.ShapeDtypeStruct((BATCH, SEQ, D_MODEL), F32),
        compiler_params=_params(("arbitrary",)),
        name="combine_ctx",
    )(yc, idx_row, x1, mod, ln_g, ln_b)


def _combine_lat_kernel(y_ref, idx_ref, x1_ref, mod_ref, lng_ref, lnb_ref, o_ref):
    s = pl.program_id(0)
    g2 = _mod_rows(mod_ref, 1 + s)[5]
    tok = lax.broadcasted_iota(I32, (ROW_TILE, CAP_LAT), 0) + ROW_TILE * pl.program_id(1)
    f = jnp.zeros((ROW_TILE, D_MODEL), F32)
    for e in range(N_EXPERTS):
        hit = tok == idx_ref[0, e:e + 1, :]
        f = f + _mm(jnp.where(hit, 1.0, 0.0).astype(BF16), y_ref[e])
    o_ref[0] = _layer_norm(DEEPNORM_ALPHA * x1_ref[0] + g2 * f, lng_ref[...], lnb_ref[...])


def _combine_lat(yl, idx, x1, mod, ln_g, ln_b):
    nt = DEC_SEQ // ROW_TILE
    full = lambda shape: pl.BlockSpec(shape, lambda s, t: (0,) * len(shape))
    return pl.pallas_call(
        _combine_lat_kernel,
        grid=(DEC_BATCH, nt),
        in_specs=[pl.BlockSpec((N_EXPERTS, CAP_LAT, D_MODEL), lambda s, t: (0, s, 0)),
                  pl.BlockSpec((1, N_EXPERTS, CAP_LAT), lambda s, t: (s, 0, 0)),
                  pl.BlockSpec((1, ROW_TILE, D_MODEL), lambda s, t: (s, t, 0)),
                  full((8, 6 * D_MODEL)), full((1, D_MODEL)), full((1, D_MODEL))],
        out_specs=pl.BlockSpec((1, ROW_TILE, D_MODEL), lambda s, t: (s, t, 0)),
        out_shape=jax.ShapeDtypeStruct((DEC_BATCH, DEC_SEQ, D_MODEL), F32),
        compiler_params=_params(("arbitrary", "arbitrary")),
        name="combine_lat",
    )(yl, idx, x1, mod, ln_g, ln_b)


def _rope_tables():
    t = np.arange(DEC_SEQ)
    row = (t // GRID_W).astype(np.float64)
    col = (t % GRID_W).astype(np.float64)
    half = HEAD_DIM // 2
    inv = ROPE_THETA ** (-np.arange(0, half, 2, dtype=np.float64) / half)
    ang_r = row[:, None] * inv[None, :]
    ang_c = col[:, None] * inv[None, :]
    ang = np.concatenate([ang_r, ang_r, ang_c, ang_c], axis=-1)
    cos = np.tile(np.cos(ang), (1, 2))
    sin = np.tile(np.sin(ang), (1, 2))
    first = (np.arange(LANES) % 32) < 16
    return jnp.asarray(cos, F32), jnp.asarray(np.where(first[None, :], -sin, sin), F32)


def kernel(x_prompt, x_sample, cache_na_k, cache_na_v, cache_diff_k, cache_diff_v, c, c_ctx, w_ada, b_ada, w_in, w_out, na_rel_bias, lambda_q1, lambda_k1, lambda_q2, lambda_k2, subln_g, ln1_g, ln1_b, ln2_g, ln2_b, w_router, w_gate, w_up, w_down):
    l = 0
    cvec = jnp.concatenate([c_ctx[None, :], c, jnp.zeros((8 - 1 - DEC_BATCH, D_MODEL), F32)], axis=0)
    mod = _modulation(cvec, w_ada[l], b_ada[l][None, :])
    lamv = jnp.stack([lambda_q1[l], lambda_k1[l], lambda_q2[l], lambda_k2[l]], axis=0)
    w_in_b = w_in[l].astype(BF16)
    w_out_b = w_out[l].astype(BF16)
    sub = subln_g[l][None, :]
    wrt = w_router[l].T

    tr = lambda a: jnp.swapaxes(a, -1, -2)

    x1c, h2c, affc, na_k_t, na_v_t, diff_k_t, new_diff_v = _context_block(
        x_prompt, mod, lamv, w_in_b, w_out_b, sub, ln1_g[l][None, :], ln1_b[l][None, :], wrt)

    cos, sin = _rope_tables()
    naq, nak, nav, dfq, dfk, dfv, tbl = _latent_qkv(x_sample, mod, w_in_b, cos, sin,
                                                    na_rel_bias[l].reshape(-1))
    na_o = _latent_na(naq, nak, nav, tr(cache_na_k), tr(cache_na_v), tbl)
    df_o = _latent_diff(dfq, dfk, dfv, tr(cache_diff_k), cache_diff_v, lamv, sub)
    x1l, h2l, affl = _latent_out(x_sample, na_o, df_o, mod, w_out_b,
                                 ln1_g[l][None, :], ln1_b[l][None, :], wrt)

    idxc, gatec, idxl, gatel = _route(affc.reshape(BATCH * N_EXPERTS, SEQ),
                                      affl.reshape(DEC_BATCH * N_EXPERTS, DEC_SEQ))
    idxc = idxc.reshape(BATCH, 1, N_EXPERTS * CAP_CTX)
    idxl = idxl.reshape(DEC_BATCH, N_EXPERTS, CAP_LAT)
    xc = _gather_ctx(idxc, h2c)
    xl = _gather_lat(idxl, h2l)
    yc, yl = _expert_ffn(xc, xl, gatec, gatel, w_gate[l], w_up[l], w_down[l])
    y_prompt = _combine_ctx(yc, idxc, x1c, mod, ln2_g[l][None, :], ln2_b[l][None, :])
    y_sample = _combine_lat(yl, idxl, x1l, mod, ln2_g[l][None, :], ln2_b[l][None, :])
    return (y_prompt, y_sample, tr(na_k_t), tr(na_v_t), tr(diff_k_t), new_diff_v)
```

```python
import functools
import math

import jax
import jax.numpy as jnp
import numpy as np
from jax import lax
from jax.experimental import pallas as pl
from jax.experimental.pallas import tpu as pltpu

F32 = jnp.float32
BF16 = jnp.bfloat16
I32 = jnp.int32

D_MODEL = 1024
BATCH = 16
SEQ = 256
DEC_BATCH = 2
DEC_SEQ = 2048
PAST_LEN = 256
GRID_W = 64
GRID_ROWS = DEC_SEQ // GRID_W
HEAD_DIM = 64
NA_HEADS = 8
DIFF_HEADS = 4
NA_WIDTH = NA_HEADS * HEAD_DIM
DIFF_WIDTH = DIFF_HEADS * 2 * HEAD_DIM
QKV_WIDTH = 3 * NA_WIDTH + 3 * DIFF_WIDTH
NA_WIN_H = 8
NA_WIN_W = 16
N_REL_H = 2 * NA_WIN_H - 1
N_REL_W = 2 * NA_WIN_W - 1
N_EXPERTS = 16
EC_CAPACITY_FACTOR = 2
D_FF = 2816
ROPE_THETA = 10000.0
NORM_EPS = 1e-5
NEG_INF = -1e30
DEPTH = 1
DEEPNORM_ALPHA = (2.0 * DEPTH) ** 0.25
LAMBDA_INIT = 0.8 - 0.6 * math.exp(-0.3 * 0)
SCALE = HEAD_DIM ** -0.5

OFF_NAQ = 0
OFF_NAK = NA_WIDTH
OFF_NAV = 2 * NA_WIDTH
OFF_DFQ = 3 * NA_WIDTH
OFF_DFK = 3 * NA_WIDTH + DIFF_WIDTH
OFF_DFV = 3 * NA_WIDTH + 2 * DIFF_WIDTH

CAP_CTX = EC_CAPACITY_FACTOR * SEQ // N_EXPERTS
CAP_LAT = EC_CAPACITY_FACTOR * DEC_SEQ // N_EXPERTS
SLOTS_CTX = BATCH * CAP_CTX
SLOTS_LAT = DEC_BATCH * CAP_LAT

ROW_TILE = 512
NA_QUAD = 4
NA_WIN_ROWS = 12
NA_GROUP = 8
FF_TILE = 256
FF_HALF = D_FF // 2
OUT_TILE = 512
LANES = 128
GATHER_GROUP = 2
CTX_GROUP = 4
CTX_REQ = 2
DIFF_ROWS = 1024
Q_ROWS = 256

BOUND_SLACK = 1.0 + 2.0 ** -10
SOFTMAX_FLOOR = 1e-25

VMEM_LIMIT = 48 * 1024 * 1024
BIG_VMEM_LIMIT = 58 * 1024 * 1024


def _params(sem, vmem=VMEM_LIMIT, flags=None):
    return pltpu.CompilerParams(dimension_semantics=sem, vmem_limit_bytes=vmem, flags=flags)


def _mm(a, b):
    return jnp.dot(a, b, preferred_element_type=F32)


def _nt(a, b):
    return lax.dot_general(a, b, (((1,), (1,)), ((), ())), preferred_element_type=F32)


def _split(a):
    hi = a.astype(BF16)
    lo = (a - hi.astype(F32)).astype(BF16)
    return hi, lo


def _mm3(a, b):
    ah, al = _split(a)
    bh, bl = _split(b)
    return _mm(ah, bh) + _mm(al, bh) + _mm(ah, bl)


def _nt3(a, b):
    ah, al = _split(a)
    bh, bl = _split(b)
    return _nt(ah, bh) + _nt(al, bh) + _nt(ah, bl)


def _layer_norm(y, g, b):
    mu = jnp.mean(y, axis=-1, keepdims=True)
    d = y - mu
    var = jnp.mean(d * d, axis=-1, keepdims=True)
    return d * lax.rsqrt(var + NORM_EPS) * g + b


def _lam(l_ref):
    l = l_ref[...]
    a = jnp.sum(l[0:1] * l[1:2], axis=-1, keepdims=True)
    b = jnp.sum(l[2:3] * l[3:4], axis=-1, keepdims=True)
    return jnp.exp(a) - jnp.exp(b) + LAMBDA_INIT


def _max_col_norm(kt):
    k = kt.astype(F32)
    n2 = jnp.max(jnp.sum(k * k, axis=0, keepdims=True), axis=-1, keepdims=True)
    return jnp.sqrt(n2) * BOUND_SLACK


def _row_norm(q):
    q = q.astype(F32)
    return jnp.sqrt(jnp.sum(q * q, axis=-1, keepdims=True)) * BOUND_SLACK


def _q_shifted(q, c):
    lane = lax.broadcasted_iota(I32, q.shape, 1)
    shift = jnp.where(lane == 0, -(c * (1.0 + 2.0 ** -7)), 0.0).astype(BF16)
    return jnp.concatenate([q, shift], axis=1)


def _keys_with_ones(k, axis):
    first = lax.broadcasted_iota(I32, k.shape, axis) == 0
    return jnp.concatenate([k, jnp.where(first, 1.0, 0.0).astype(BF16)], axis=axis)


def _mod_rows(mod_ref, row):
    return [mod_ref[pl.ds(row, 1), i * D_MODEL:(i + 1) * D_MODEL] for i in range(6)]


def _sub_norm(o, sub):
    ms = jnp.mean(o * o, axis=-1, keepdims=True)
    return o * lax.rsqrt(ms + NORM_EPS) * sub * (1.0 - LAMBDA_INIT)


def _router_aff_t(h2, wrt):
    lg = _nt3(wrt, h2)
    m = jnp.max(lg, axis=0, keepdims=True)
    e = jnp.exp(lg - m)
    return e / jnp.sum(e, axis=0, keepdims=True)


def _mod_kernel(c_ref, wa_ref, wb_ref, b_ref, o_ref):
    c = c_ref[...]
    s = c * (1.0 / (1.0 + jnp.exp(-c)))
    half = D_MODEL // 2
    o_ref[...] = _mm3(s[:, :half], wa_ref[...]) + _mm3(s[:, half:], wb_ref[...]) + b_ref[...]


def _modulation(cvec, w_ada, b_ada):
    ncol = 6 * D_MODEL
    half = D_MODEL // 2
    return pl.pallas_call(
        _mod_kernel,
        grid=(6,),
        in_specs=[pl.BlockSpec((8, D_MODEL), lambda j: (0, 0)),
                  pl.BlockSpec((half, D_MODEL), lambda j: (0, j)),
                  pl.BlockSpec((half, D_MODEL), lambda j: (1, j)),
                  pl.BlockSpec((1, D_MODEL), lambda j: (0, j))],
        out_specs=pl.BlockSpec((8, D_MODEL), lambda j: (0, j)),
        out_shape=jax.ShapeDtypeStruct((8, ncol), F32),
        compiler_params=_params(("arbitrary",)),
        name="modulation",
    )(cvec, w_ada, w_ada, b_ada)


def _ctx_kernel(x_ref, mod_ref, lam_ref, win_ref, wout_ref, sub_ref, lng_ref, lnb_ref, wrt_ref,
                x1_ref, h2_ref, aff_ref, nak_ref, nav_ref, dfk_ref, dfv_ref,
                qkv_scr, o_scr, s_scr, e_scr):
    rows = CTX_REQ * SEQ
    x = x_ref[...].reshape(rows, D_MODEL)
    sh1, sc1, g1, sh2, sc2, g2 = _mod_rows(mod_ref, 0)
    h = (x * (1.0 + sc1) + sh1).astype(BF16)
    qkv_scr[...] = _mm(h, win_ref[...])

    kt_na = qkv_scr[:, OFF_NAK:OFF_NAK + NA_WIDTH].T
    kt_df = qkv_scr[:, OFF_DFK:OFF_DFK + DIFF_WIDTH].T
    vt_na = qkv_scr[:, OFF_NAV:OFF_NAV + NA_WIDTH].T
    for r in range(CTX_REQ):
        tok = slice(SEQ * r, SEQ * (r + 1))
        nak_ref[r, 0] = kt_na[:, tok].reshape(NA_HEADS, HEAD_DIM, SEQ)
        nav_ref[r, 0] = vt_na[:, tok].reshape(NA_HEADS, HEAD_DIM, SEQ)
        dfk_ref[r, 0] = kt_df[:, tok].reshape(DIFF_HEADS, 2, HEAD_DIM, SEQ)
        for hh in range(DIFF_HEADS):
            c0 = OFF_DFV + 2 * HEAD_DIM * hh
            dfv_ref[r, 0, hh] = qkv_scr[tok, c0:c0 + 2 * HEAD_DIM]

    kt = jnp.concatenate([kt_na, kt_df], axis=0).astype(BF16)
    q_cols = [OFF_NAQ + HEAD_DIM * i for i in range(NA_HEADS)] + [
        OFF_DFQ + HEAD_DIM * i for i in range(2 * DIFF_HEADS)]
    nmap = len(q_cols)
    for r in range(CTX_REQ):
        tok = slice(SEQ * r, SEQ * (r + 1))
        for i, c in enumerate(q_cols):
            q = (qkv_scr[tok, c:c + HEAD_DIM] * SCALE).astype(BF16)
            s_scr[nmap * r + i] = _mm(q, kt[HEAD_DIM * i:HEAD_DIM * (i + 1), tok])
    s = s_scr[...]
    e = jnp.exp(s - jnp.max(s, axis=-1, keepdims=True))
    rl = 1.0 / jnp.sum(e, axis=-1, keepdims=True)
    e_scr[...] = e.astype(BF16)

    lam = _lam(lam_ref)
    sub = sub_ref[...]
    for r in range(CTX_REQ):
        tok = slice(SEQ * r, SEQ * (r + 1))
        for hh in range(NA_HEADS):
            i = nmap * r + hh
            v = qkv_scr[tok, OFF_NAV + HEAD_DIM * hh:OFF_NAV + HEAD_DIM * (hh + 1)].astype(BF16)
            o_scr[tok, HEAD_DIM * hh:HEAD_DIM * (hh + 1)] = (_mm(e_scr[i], v) * rl[i]).astype(BF16)
        for hh in range(DIFF_HEADS):
            i1 = nmap * r + NA_HEADS + 2 * hh
            c0 = OFF_DFV + 2 * HEAD_DIM * hh
            v = qkv_scr[tok, c0:c0 + 2 * HEAD_DIM].astype(BF16)
            o = _mm(e_scr[i1], v) * rl[i1] - _mm(e_scr[i1 + 1], v) * (lam * rl[i1 + 1])
            c1 = NA_WIDTH + 2 * HEAD_DIM * hh
            o_scr[tok, c1:c1 + 2 * HEAD_DIM] = _sub_norm(o, sub).astype(BF16)

    a = _mm(o_scr[...], wout_ref[...])
    x1 = _layer_norm(DEEPNORM_ALPHA * x + g1 * a, lng_ref[...], lnb_ref[...])
    x1_ref[...] = x1.reshape(CTX_REQ, SEQ, D_MODEL)
    h2 = x1 * (1.0 + sc2) + sh2
    h2_ref[...] = h2.astype(BF16).reshape(CTX_REQ, SEQ, D_MODEL)
    aff = _router_aff_t(h2, wrt_ref[...])
    for r in range(CTX_REQ):
        aff_ref[r] = aff[:, SEQ * r:SEQ * (r + 1)]


def _context_block(x, mod, lamv, w_in, w_out, sub, ln_g, ln_b, wrt):
    full = lambda shape: pl.BlockSpec(shape, lambda b: (0,) * len(shape))
    once = lambda shape: pl.BlockSpec(shape, lambda b: (0,) * len(shape), pipeline_mode=pl.Buffered(1))
    req = lambda *tail: pl.BlockSpec((CTX_REQ,) + tail, lambda b: (b,) + (0,) * len(tail))
    nmaps = CTX_REQ * (NA_HEADS + 2 * DIFF_HEADS)
    return pl.pallas_call(
        _ctx_kernel,
        grid=(BATCH // CTX_REQ,),
        in_specs=[req(SEQ, D_MODEL),
                  full((8, 6 * D_MODEL)), full((4, HEAD_DIM)),
                  once((D_MODEL, QKV_WIDTH)), once((D_MODEL, D_MODEL)),
                  full((1, 2 * HEAD_DIM)), full((1, D_MODEL)), full((1, D_MODEL)),
                  full((N_EXPERTS, D_MODEL))],
        out_specs=[req(SEQ, D_MODEL), req(SEQ, D_MODEL), req(N_EXPERTS, SEQ),
                   req(1, NA_HEADS, HEAD_DIM, SEQ), req(1, NA_HEADS, HEAD_DIM, SEQ),
                   req(1, DIFF_HEADS, 2, HEAD_DIM, SEQ), req(1, DIFF_HEADS, SEQ, 2 * HEAD_DIM)],
        out_shape=[jax.ShapeDtypeStruct((BATCH, SEQ, D_MODEL), F32),
                   jax.ShapeDtypeStruct((BATCH, SEQ, D_MODEL), BF16),
                   jax.ShapeDtypeStruct((BATCH, N_EXPERTS, SEQ), F32),
                   jax.ShapeDtypeStruct((BATCH, 1, NA_HEADS, HEAD_DIM, SEQ), F32),
                   jax.ShapeDtypeStruct((BATCH, 1, NA_HEADS, HEAD_DIM, SEQ), F32),
                   jax.ShapeDtypeStruct((BATCH, 1, DIFF_HEADS, 2, HEAD_DIM, SEQ), F32),
                   jax.ShapeDtypeStruct((BATCH, 1, DIFF_HEADS, SEQ, 2 * HEAD_DIM), F32)],
        scratch_shapes=[pltpu.VMEM((CTX_REQ * SEQ, QKV_WIDTH), F32), pltpu.VMEM((CTX_REQ * SEQ, D_MODEL), BF16),
                        pltpu.VMEM((nmaps, SEQ, SEQ), F32), pltpu.VMEM((nmaps, SEQ, SEQ), BF16)],
        compiler_params=_params(("arbitrary",)),
        name="context_block",
    )(x, mod, lamv, w_in, w_out, sub, ln_g, ln_b, wrt)


def _lat_qkv_kernel(x_ref, mod_ref, win_ref, cos_ref, sin_ref, rpb_ref,
                    naq_ref, nak_ref, nav_ref, dfq_ref, dfk_ref, dfv_ref, tbl_ref, stat_ref, qkv_scr):
    b = pl.program_id(0)
    bias_top = _write_bias_tables(rpb_ref, b * pl.num_programs(1) + pl.program_id(1), tbl_ref)
    x = x_ref[0]
    sh1, sc1 = _mod_rows(mod_ref, 1 + b)[:2]
    h = (x * (1.0 + sc1) + sh1).astype(BF16)
    qkv_scr[...] = _mm(h, win_ref[...])

    kb = qkv_scr[:, OFF_NAK:OFF_NAK + NA_WIDTH].astype(BF16).astype(F32)
    col = lax.broadcasted_iota(I32, (NA_WIDTH, LANES), 0) // HEAD_DIM
    head_of = jnp.where(col == lax.broadcasted_iota(I32, (NA_WIDTH, LANES), 1), 1.0, 0.0).astype(BF16)
    kn2 = jnp.max(_mm((kb * kb).astype(BF16), head_of), axis=0, keepdims=True) * (1.0 + 2.0 ** -7)
    stat_ref[0] = jnp.concatenate([kn2, jnp.broadcast_to(bias_top, (1, LANES)),
                                   jnp.zeros((6, LANES), F32)], axis=0)

    qb = (qkv_scr[:, OFF_NAQ:OFF_NAQ + NA_WIDTH] * SCALE).astype(BF16)
    qf = qb.astype(F32)
    qn = jnp.sqrt(_mm((qf * qf).astype(BF16), head_of) * (1.0 + 2.0 ** -7)) * (1.0 + 2.0 ** -7)
    lane64 = lax.broadcasted_iota(I32, (ROW_TILE, HEAD_DIM), 1)
    key_ext = jnp.where(lane64 < 2, 1.0, 0.0).astype(BF16)
    for hh in range(NA_HEADS):
        lo, hi = HEAD_DIM * hh, HEAD_DIM * (hh + 1)
        ext = jnp.where(lane64 == 0, -qn[:, hh:hh + 1], jnp.where(lane64 == 1, -1.0, 0.0)).astype(BF16)
        naq_ref[0, hh] = jnp.concatenate([qb[:, lo:hi], ext], axis=1)
        nak_ref[0, hh] = jnp.concatenate([qkv_scr[:, OFF_NAK + lo:OFF_NAK + hi].astype(BF16), key_ext], axis=1)
        nav_ref[0, hh] = qkv_scr[:, OFF_NAV + lo:OFF_NAV + hi].astype(BF16)

    cos = cos_ref[...]
    sin = sin_ref[...]
    lane = lax.broadcasted_iota(I32, (ROW_TILE, LANES), 1)
    first = (lane & 31) < 16

    def rope(t):
        rot = jnp.where(first, pltpu.roll(t, LANES - 16, 1), pltpu.roll(t, 16, 1))
        return t * cos + rot * sin

    for hh in range(DIFF_HEADS):
        lo = 2 * HEAD_DIM * hh
        q = rope(qkv_scr[:, OFF_DFQ + lo:OFF_DFQ + lo + LANES]) * SCALE
        k = rope(qkv_scr[:, OFF_DFK + lo:OFF_DFK + lo + LANES])
        for m in range(2):
            dfq_ref[0, hh, m] = q[:, HEAD_DIM * m:HEAD_DIM * (m + 1)].astype(BF16)
        dfk_ref[0, hh] = k.T.astype(BF16).reshape(2, HEAD_DIM, ROW_TILE)
        dfv_ref[0, hh] = qkv_scr[:, OFF_DFV + lo:OFF_DFV + lo + LANES].astype(BF16)


def _latent_qkv(x, mod, w_in, cos, sin, rpb_flat):
    nt = DEC_SEQ // ROW_TILE
    assert DEC_BATCH * nt == NA_HEADS
    tbl_shape = (NA_HEADS, 3, NA_QUAD * GRID_W, NA_WIN_ROWS * GRID_W)
    full = lambda shape: pl.BlockSpec(shape, lambda b, t: (0,) * len(shape))
    hs = lambda width: pl.BlockSpec((1, NA_HEADS, ROW_TILE, width), lambda b, t: (b, 0, t, 0))
    na_shape = lambda width: jax.ShapeDtypeStruct((DEC_BATCH, NA_HEADS, DEC_SEQ, width), BF16)
    return pl.pallas_call(
        _lat_qkv_kernel,
        grid=(DEC_BATCH, nt),
        in_specs=[pl.BlockSpec((1, ROW_TILE, D_MODEL), lambda b, t: (b, t, 0)),
                  full((8, 6 * D_MODEL)), full((D_MODEL, QKV_WIDTH)),
                  pl.BlockSpec((ROW_TILE, LANES), lambda b, t: (t, 0)),
                  pl.BlockSpec((ROW_TILE, LANES), lambda b, t: (t, 0)),
                  pl.BlockSpec(memory_space=pltpu.SMEM)],
        out_specs=[hs(2 * HEAD_DIM), hs(2 * HEAD_DIM), hs(HEAD_DIM),
                   pl.BlockSpec((1, DIFF_HEADS, 2, ROW_TILE, HEAD_DIM), lambda b, t: (b, 0, 0, t, 0)),
                   pl.BlockSpec((1, DIFF_HEADS, 2, HEAD_DIM, ROW_TILE), lambda b, t: (b, 0, 0, 0, t)),
                   pl.BlockSpec((1, DIFF_HEADS, ROW_TILE, 2 * HEAD_DIM), lambda b, t: (b, 0, t, 0)),
                   pl.BlockSpec((1,) + tbl_shape[1:], lambda b, t: (b * nt + t, 0, 0, 0)),
                   pl.BlockSpec((1, 8, LANES), lambda b, t: (b * nt + t, 0, 0))],
        out_shape=[na_shape(2 * HEAD_DIM), na_shape(2 * HEAD_DIM), na_shape(HEAD_DIM)]
        + [jax.ShapeDtypeStruct((DEC_BATCH, DIFF_HEADS, 2, DEC_SEQ, HEAD_DIM), BF16),
           jax.ShapeDtypeStruct((DEC_BATCH, DIFF_HEADS, 2, HEAD_DIM, DEC_SEQ), BF16),
           jax.ShapeDtypeStruct((DEC_BATCH, DIFF_HEADS, DEC_SEQ, 2 * HEAD_DIM), BF16),
           jax.ShapeDtypeStruct(tbl_shape, F32),
           jax.ShapeDtypeStruct((NA_HEADS, 8, LANES), F32)],
        scratch_shapes=[pltpu.VMEM((ROW_TILE, QKV_WIDTH), F32)],
        compiler_params=_params(("arbitrary", "arbitrary")),
        name="latent_qkv",
    )(x, mod, w_in, cos, sin, rpb_flat)


def _na_row_offset(kind, qr, kr):
    if kind == 0:
        return kr - qr + 7 if kr < NA_WIN_H else None
    if kind == 1:
        return kr - qr + 3 if qr <= kr < qr + NA_WIN_H else None
    return kr - qr - 1 if NA_WIN_ROWS - NA_WIN_H <= kr else None


def _write_bias_tables(rpb_ref, h, o_ref):
    base = h * (N_REL_H * N_REL_W)
    wq = lax.broadcasted_iota(I32, (GRID_W, LANES), 0)
    c = lax.broadcasted_iota(I32, (GRID_W, LANES), 1)
    wk = c & (GRID_W - 1)
    right = c >= GRID_W
    c0 = jnp.clip(wq - NA_WIN_W // 2, 0, GRID_W - NA_WIN_W)
    in_win = (wk >= c0) & (wk < c0 + NA_WIN_W)
    cache = {}
    strips = {}

    def strip(r):
        if r not in strips:
            c8 = lax.broadcasted_iota(I32, (8, LANES), 1)
            d = jnp.where(c8 < GRID_W, c8, c8 - LANES)
            co_of = jnp.clip(d, -(NA_WIN_W - 1), NA_WIN_W - 1) + (NA_WIN_W - 1)
            acc = jnp.zeros((8, LANES), F32)
            for co in range(N_REL_W):
                acc = jnp.where(co_of == co, rpb_ref[base + r * N_REL_W + co], acc)
            strips[r] = jnp.concatenate([acc] * (GRID_W // 8), axis=0)
        return strips[r]

    def tile(rl, rr):
        if (rl, rr) in cache:
            return cache[(rl, rr)]
        if rl is None and rr is None:
            t = jnp.full((GRID_W, LANES), NEG_INF, F32)
        else:
            zero = jnp.zeros((GRID_W, LANES), F32)
            tl = pltpu.roll(strip(rl), 0, 1, stride=1, stride_axis=0) if rl is not None else zero
            tr = pltpu.roll(strip(rr), GRID_W, 1, stride=1, stride_axis=0) if rr is not None else zero
            acc = jnp.where(right, tr, tl)
            ok = in_win
            if rl is None:
                ok = ok & right
            if rr is None:
                ok = ok & jnp.logical_not(right)
            t = jnp.where(ok, acc, NEG_INF)
        cache[(rl, rr)] = t
        return t

    for kind in range(3):
        for qr in range(NA_QUAD):
            row = [tile(_na_row_offset(kind, qr, 2 * p), _na_row_offset(kind, qr, 2 * p + 1))
                   for p in range(NA_WIN_ROWS // 2)]
            o_ref[0, kind, GRID_W * qr:GRID_W * (qr + 1), :] = jnp.concatenate(row, axis=1)
    top = functools.reduce(jnp.maximum, cache.values())
    return jnp.max(jnp.max(top, axis=-1, keepdims=True), axis=0, keepdims=True)


def _na_kernel(q_ref, k_ref, v_ref, ck_ref, cv_ref, tbl_ref, stat_ref, o_ref):
    b = pl.program_id(0)
    g = pl.program_id(2)
    u0 = jnp.clip(NA_QUAD * g - NA_QUAD, 0, GRID_ROWS - NA_WIN_ROWS)
    start = pl.multiple_of(u0 * GRID_W, GRID_W)
    nwin = NA_WIN_ROWS * GRID_W
    tiles = stat_ref.shape[0] // DEC_BATCH
    kn2 = functools.reduce(jnp.maximum, [stat_ref[b * tiles + t, 0:1, :] for t in range(tiles)])

    lane = lax.broadcasted_iota(I32, (1, 2 * HEAD_DIM), 1)
    sub = lax.broadcasted_iota(I32, (HEAD_DIM, PAST_LEN), 0)
    ones_rows = jnp.where(sub < 2, 1.0, 0.0).astype(BF16)

    def attend(exact_max):
        lmin = None
        for j in range(NA_GROUP):
            q = q_ref[0, j]
            kw = k_ref[0, j, pl.ds(start, nwin), :]
            vw = v_ref[0, j, pl.ds(start, nwin), :]
            kct = ck_ref[0, 0, j].astype(BF16)
            vct = cv_ref[0, 0, j].astype(BF16)
            kct1 = jnp.concatenate([kct, ones_rows], axis=0)
            if exact_max:
                q0 = (q.astype(F32) * jnp.where(lane < HEAD_DIM, 1.0, 0.0)).astype(BF16)
                sl = _nt(q0, kw) + tbl_ref[j, 0]
                sc = _mm(q[:, :HEAD_DIM], kct)
                m = jnp.maximum(jnp.max(sl, axis=-1, keepdims=True), jnp.max(sc, axis=-1, keepdims=True))
                el = jnp.exp(sl - m)
                ec = jnp.exp(sc - m)
            else:
                up = 1.0 + 2.0 ** -7
                knorm = jnp.maximum(jnp.sqrt(kn2[:, j:j + 1]) * BOUND_SLACK, _max_col_norm(kct)) * up
                btop = jnp.maximum(stat_ref[j, 1:2, 0:1], 0.0) * up
                scale = jnp.where(lane < HEAD_DIM, 1.0, jnp.where(lane == HEAD_DIM, knorm, btop))
                qs = (q.astype(F32) * scale).astype(BF16)
                el = jnp.exp(_nt(qs, kw) + tbl_ref[j, 0])
                ec = jnp.exp(_mm(qs, kct1))
            l = jnp.sum(el, axis=-1, keepdims=True) + jnp.sum(ec, axis=-1, keepdims=True)
            o = (_mm(el.astype(BF16), vw) + _nt(ec.astype(BF16), vct)) / l
            o_ref[0, :, HEAD_DIM * j:HEAD_DIM * (j + 1)] = o.astype(BF16)
            lmin = jnp.min(l) if lmin is None else jnp.minimum(lmin, jnp.min(l))
        return lmin

    lmin = attend(exact_max=False)

    @pl.when(jnp.logical_not(lmin >= SOFTMAX_FLOOR))
    def _():
        attend(exact_max=True)


def _latent_na(naq, nak, nav, cache_k, cache_v, tbl, stats):
    assert NA_GROUP == NA_HEADS
    nq = GRID_ROWS // NA_QUAD
    rows = NA_QUAD * GRID_W
    kind = lambda g: jnp.minimum(g, 1) + g // (nq - 1)
    return pl.pallas_call(
        _na_kernel,
        grid=(DEC_BATCH, NA_HEADS // NA_GROUP, nq),
        in_specs=[pl.BlockSpec((1, NA_GROUP, rows, 2 * HEAD_DIM), lambda b, p, g: (b, p, g, 0)),
                  pl.BlockSpec((1, NA_GROUP, DEC_SEQ, 2 * HEAD_DIM), lambda b, p, g: (b, p, 0, 0)),
                  pl.BlockSpec((1, NA_GROUP, DEC_SEQ, HEAD_DIM), lambda b, p, g: (b, p, 0, 0)),
                  pl.BlockSpec((1, 1, NA_GROUP, HEAD_DIM, PAST_LEN), lambda b, p, g: (b, 0, p, 0, 0)),
                  pl.BlockSpec((1, 1, NA_GROUP, HEAD_DIM, PAST_LEN), lambda b, p, g: (b, 0, p, 0, 0)),
                  pl.BlockSpec((NA_GROUP, 1, rows, NA_WIN_ROWS * GRID_W), lambda b, p, g: (p, kind(g), 0, 0)),
                  pl.BlockSpec(stats.shape, lambda b, p, g: (0, 0, 0))],
        out_specs=pl.BlockSpec((1, rows, NA_GROUP * HEAD_DIM), lambda b, p, g: (b, g, p)),
        out_shape=jax.ShapeDtypeStruct((DEC_BATCH, DEC_SEQ, NA_WIDTH), BF16),
        compiler_params=_params(("arbitrary",) * 3),
        name="latent_na",
    )(naq, nak, nav, cache_k, cache_v, tbl, stats)


def _diff_kernel(q_ref, kt_ref, v_ref, ck_ref, cv_ref, lam_ref, sub_ref, o_ref):
    lam = _lam(lam_ref)
    vc = cv_ref[0, 0, 0].astype(BF16)
    vl = v_ref[0, 0]
    kcs = [ck_ref[0, 0, 0, m].astype(BF16) for m in range(2)]
    kls = [kt_ref[0, 0, m] for m in range(2)]
    knorm = [jnp.maximum(_max_col_norm(kcs[m]), _max_col_norm(kls[m])) for m in range(2)]
    kcs1 = [_keys_with_ones(k, 0) for k in kcs]
    kls1 = [_keys_with_ones(k, 0) for k in kls]

    def attend(exact_max):
        lmin = None
        for r in range(DIFF_ROWS // Q_ROWS):
            rows = slice(Q_ROWS * r, Q_ROWS * (r + 1))
            parts = []
            for m in range(2):
                q = q_ref[0, 0, m, rows, :]
                if exact_max:
                    sc = _mm(q, kcs[m])
                    sl = _mm(q, kls[m])
                    mx = jnp.maximum(jnp.max(sc, axis=-1, keepdims=True), jnp.max(sl, axis=-1, keepdims=True))
                    ec = jnp.exp(sc - mx)
                    el = jnp.exp(sl - mx)
                else:
                    qs = _q_shifted(q, _row_norm(q) * knorm[m])
                    ec = jnp.exp(_mm(qs, kcs1[m]))
                    el = jnp.exp(_mm(qs, kls1[m]))
                l = jnp.sum(ec, axis=-1, keepdims=True) + jnp.sum(el, axis=-1, keepdims=True)
                parts.append((_mm(ec.astype(BF16), vc) + _mm(el.astype(BF16), vl), l))
                lmin = jnp.min(l) if lmin is None else jnp.minimum(lmin, jnp.min(l))
            (a1, l1), (a2, l2) = parts
            o = a1 * (1.0 / l1) - a2 * (lam / l2)
            o_ref[0, rows, :] = _sub_norm(o, sub_ref[...]).astype(BF16)
        return lmin

    lmin = attend(exact_max=False)

    @pl.when(jnp.logical_not(lmin >= SOFTMAX_FLOOR))
    def _():
        attend(exact_max=True)


def _latent_diff(dfq, dfk, dfv, cache_k, cache_v, lamv, sub):
    nt = DEC_SEQ // DIFF_ROWS
    full = lambda shape: pl.BlockSpec(shape, lambda b, h, t: (0,) * len(shape))
    return pl.pallas_call(
        _diff_kernel,
        grid=(DEC_BATCH, DIFF_HEADS, nt),
        in_specs=[pl.BlockSpec((1, 1, 2, DIFF_ROWS, HEAD_DIM), lambda b, h, t: (b, h, 0, t, 0)),
                  pl.BlockSpec((1, 1, 2, HEAD_DIM, DEC_SEQ), lambda b, h, t: (b, h, 0, 0, 0)),
                  pl.BlockSpec((1, 1, DEC_SEQ, 2 * HEAD_DIM), lambda b, h, t: (b, h, 0, 0)),
                  pl.BlockSpec((1, 1, 1, 2, HEAD_DIM, PAST_LEN), lambda b, h, t: (b, 0, h, 0, 0, 0)),
                  pl.BlockSpec((1, 1, 1, PAST_LEN, 2 * HEAD_DIM), lambda b, h, t: (b, 0, h, 0, 0)),
                  full((4, HEAD_DIM)), full((1, 2 * HEAD_DIM))],
        out_specs=pl.BlockSpec((1, DIFF_ROWS, 2 * HEAD_DIM), lambda b, h, t: (b, t, h)),
        out_shape=jax.ShapeDtypeStruct((DEC_BATCH, DEC_SEQ, DIFF_WIDTH), BF16),
        compiler_params=_params(("arbitrary",) * 3),
        name="latent_diff",
    )(dfq, dfk, dfv, cache_k, cache_v, lamv, sub)


def _lat_out_kernel(x_ref, na_ref, df_ref, mod_ref, wout_ref, lng_ref, lnb_ref, wrt_ref,
                    x1_ref, h2_ref, aff_ref):
    b = pl.program_id(0)
    x = x_ref[0]
    _, _, g1, sh2, sc2, _ = _mod_rows(mod_ref, 1 + b)
    a = _mm(na_ref[0], wout_ref[0:NA_WIDTH, :]) + _mm(df_ref[0], wout_ref[NA_WIDTH:, :])
    x1 = _layer_norm(DEEPNORM_ALPHA * x + g1 * a, lng_ref[...], lnb_ref[...])
    x1_ref[0] = x1
    h2 = x1 * (1.0 + sc2) + sh2
    h2_ref[0] = h2.astype(BF16)
    aff_ref[0] = _router_aff_t(h2, wrt_ref[...])


def _latent_out(x, na_o, df_o, mod, w_out, ln_g, ln_b, wrt):
    nt = DEC_SEQ // ROW_TILE
    full = lambda shape: pl.BlockSpec(shape, lambda b, t: (0,) * len(shape))
    tok = lambda w: pl.BlockSpec((1, ROW_TILE, w), lambda b, t: (b, t, 0))
    return pl.pallas_call(
        _lat_out_kernel,
        grid=(DEC_BATCH, nt),
        in_specs=[tok(D_MODEL), tok(NA_WIDTH), tok(DIFF_WIDTH), full((8, 6 * D_MODEL)),
                  full((D_MODEL, D_MODEL)), full((1, D_MODEL)), full((1, D_MODEL)),
                  full((N_EXPERTS, D_MODEL))],
        out_specs=[tok(D_MODEL), tok(D_MODEL),
                   pl.BlockSpec((1, N_EXPERTS, ROW_TILE), lambda b, t: (b, 0, t))],
        out_shape=[jax.ShapeDtypeStruct((DEC_BATCH, DEC_SEQ, D_MODEL), F32),
                   jax.ShapeDtypeStruct((DEC_BATCH, DEC_SEQ, D_MODEL), BF16),
                   jax.ShapeDtypeStruct((DEC_BATCH, N_EXPERTS, DEC_SEQ), F32)],
        compiler_params=_params(("arbitrary", "arbitrary")),
        name="latent_out",
    )(x, na_o, df_o, mod, w_out, ln_g, ln_b, wrt)


def _ranks_first(va, ia, vb, ib):
    return (va > vb) | ((va == vb) & (ia < ib))


def _sort_rows_desc(aff):
    rows, n = aff.shape
    nb = n // LANES
    lane = lax.broadcasted_iota(I32, (rows, LANES), 1)
    v = [aff[:, LANES * b:LANES * (b + 1)] for b in range(nb)]
    ix = [lane + LANES * b for b in range(nb)]
    k = 2
    while k <= n:
        j = k // 2
        while j >= 1:
            if j >= LANES:
                sb = j // LANES
                for lo in range(nb):
                    if lo & sb:
                        continue
                    hi = lo | sb
                    f = _ranks_first(v[lo], ix[lo], v[hi], ix[hi])
                    if k < n and (LANES * lo) & k:
                        f = jnp.logical_not(f)
                    v[lo], v[hi] = jnp.where(f, v[lo], v[hi]), jnp.where(f, v[hi], v[lo])
                    ix[lo], ix[hi] = jnp.where(f, ix[lo], ix[hi]), jnp.where(f, ix[hi], ix[lo])
            else:
                upper = (lane & j) != 0
                if k < LANES:
                    flip_lanes = jnp.logical_xor(upper, (lane & k) != 0)
                for b in range(nb):
                    pv = jnp.where(upper, pltpu.roll(v[b], j, 1), pltpu.roll(v[b], LANES - j, 1))
                    pi = jnp.where(upper, pltpu.roll(ix[b], j, 1), pltpu.roll(ix[b], LANES - j, 1))
                    f = _ranks_first(v[b], ix[b], pv, pi)
                    if k < LANES:
                        flip = flip_lanes
                    elif k < n and (LANES * b) & k:
                        flip = jnp.logical_not(upper)
                    else:
                        flip = upper
                    keep = jnp.logical_xor(f, flip)
                    v[b] = jnp.where(keep, v[b], pv)
                    ix[b] = jnp.where(keep, ix[b], pi)
            j //= 2
        k *= 2
    return v, ix


def _route_kernel(affc_ref, affl_ref, idxc_ref, gatec_ref, idxl_ref, gatel_ref):
    for aff_ref, idx_ref, gate_ref, cap in ((affc_ref, idxc_ref, gatec_ref, CAP_CTX),
                                            (affl_ref, idxl_ref, gatel_ref, CAP_LAT)):
        v, ix = _sort_rows_desc(aff_ref[...])
        if cap < LANES:
            idx_ref[...] = ix[0][:, :cap]
            gate_ref[...] = v[0]
        else:
            idx_ref[...] = jnp.concatenate(ix[:cap // LANES], axis=1)
            gate_ref[...] = jnp.concatenate(v[:cap // LANES], axis=1)


def _route(affc_t, affl_t):
    out = lambda a, cap: [jax.ShapeDtypeStruct((a.shape[0], cap), I32),
                          jax.ShapeDtypeStruct((a.shape[0], max(cap, LANES)), F32)]
    return pl.pallas_call(
        _route_kernel,
        out_shape=out(affc_t, CAP_CTX) + out(affl_t, CAP_LAT),
        compiler_params=_params(None),
        name="route",
    )(affc_t, affl_t)


def _row_to_col(row):
    n = row.shape[1]
    eye = lax.broadcasted_iota(I32, (n, n), 0) == lax.broadcasted_iota(I32, (n, n), 1)
    return jnp.sum(jnp.where(eye, row, jnp.zeros_like(row)), axis=1, keepdims=True)


def _gather_ctx_kernel(idx_ref, h_ref, x_ref):
    nslot = N_EXPERTS * CAP_CTX
    for s in range(CTX_GROUP):
        hit = lax.broadcasted_iota(I32, (nslot, SEQ), 1) == _row_to_col(idx_ref[s])
        xs = _mm(jnp.where(hit, 1.0, 0.0).astype(BF16), h_ref[s])
        x_ref[:, CAP_CTX * s:CAP_CTX * (s + 1), :] = xs.astype(BF16).reshape(N_EXPERTS, CAP_CTX, D_MODEL)


def _gather_ctx(idx_row, h2):
    nslot = N_EXPERTS * CAP_CTX
    return pl.pallas_call(
        _gather_ctx_kernel,
        grid=(BATCH // CTX_GROUP,),
        in_specs=[pl.BlockSpec((CTX_GROUP, 1, nslot), lambda s: (s, 0, 0)),
                  pl.BlockSpec((CTX_GROUP, SEQ, D_MODEL), lambda s: (s, 0, 0))],
        out_specs=pl.BlockSpec((N_EXPERTS, CTX_GROUP * CAP_CTX, D_MODEL), lambda s: (0, s, 0)),
        out_shape=jax.ShapeDtypeStruct((N_EXPERTS, SLOTS_CTX, D_MODEL), BF16),
        compiler_params=_params(("arbitrary",)),
        name="gather_ctx",
    )(idx_row, h2)


def _gather_lat_kernel(idx_ref, h_ref, x_ref):
    first = GATHER_GROUP * pl.program_id(1)
    col = jnp.concatenate([_row_to_col(idx_ref[0, pl.ds(first + i, 1), :]) for i in range(GATHER_GROUP)],
                          axis=0)
    hit = lax.broadcasted_iota(I32, (GATHER_GROUP * CAP_LAT, DEC_SEQ), 1) == col
    xs = _mm(jnp.where(hit, 1.0, 0.0).astype(BF16), h_ref[0])
    x_ref[...] = xs.astype(BF16).reshape(GATHER_GROUP, CAP_LAT, D_MODEL)


def _gather_lat(idx, h2):
    return pl.pallas_call(
        _gather_lat_kernel,
        grid=(DEC_BATCH, N_EXPERTS // GATHER_GROUP),
        in_specs=[pl.BlockSpec((1, N_EXPERTS, CAP_LAT), lambda s, e: (s, 0, 0)),
                  pl.BlockSpec((1, DEC_SEQ, D_MODEL), lambda s, e: (s, 0, 0))],
        out_specs=pl.BlockSpec((GATHER_GROUP, CAP_LAT, D_MODEL), lambda s, e: (e, s, 0)),
        out_shape=jax.ShapeDtypeStruct((N_EXPERTS, SLOTS_LAT, D_MODEL), BF16),
        compiler_params=_params(("arbitrary", "arbitrary")),
        name="gather_lat",
    )(idx, h2)


def _ffn_kernel(xc_ref, xl_ref, gc_ref, gl_ref, wg_ref, wu_ref, wd_ref, yc_ref, yl_ref, h_scr, wl_scr):
    j = pl.program_id(1)
    tail = FF_HALF - FF_HALF // FF_TILE * FF_TILE

    def swiglu_cols(wg, wu, c0):
        wgb = wg.astype(BF16)
        wub = wu.astype(BF16)
        for i, x_ref in enumerate((xc_ref, xl_ref)):
            x = x_ref[0]
            a = _mm(x, wgb)
            u = _mm(x, wub)
            h = a * (1.0 / (1.0 + jnp.exp(-a))) * u
            h_scr[SLOTS_CTX * i:SLOTS_CTX * (i + 1), c0:c0 + FF_TILE] = h.astype(BF16)

    @pl.when(j == 0)
    def _():
        for c in range(FF_HALF // FF_TILE):
            lo = FF_TILE * c
            swiglu_cols(wg_ref[0, :, lo:lo + FF_TILE], wu_ref[0, :, lo:lo + FF_TILE], lo)
        wl_scr[0] = wg_ref[0, :, FF_HALF - tail:FF_HALF]
        wl_scr[1] = wu_ref[0, :, FF_HALF - tail:FF_HALF]

    @pl.when(j == 1)
    def _():
        head = FF_TILE - tail
        swiglu_cols(jnp.concatenate([wl_scr[0], wg_ref[0, :, 0:head]], axis=1),
                    jnp.concatenate([wl_scr[1], wu_ref[0, :, 0:head]], axis=1), FF_HALF - tail)
        for c in range(FF_HALF // FF_TILE):
            lo = head + FF_TILE * c
            swiglu_cols(wg_ref[0, :, lo:lo + FF_TILE], wu_ref[0, :, lo:lo + FF_TILE], FF_HALF + lo)

    @pl.when(j >= 2)
    def _():
        wdb = wd_ref[0].astype(BF16)
        e = pl.program_id(0)
        for i, (g_ref, y_ref) in enumerate(((gc_ref, yc_ref), (gl_ref, yl_ref))):
            nset = g_ref.shape[0] // N_EXPERTS
            gate = jnp.concatenate(
                [_row_to_col(g_ref[pl.ds(N_EXPERTS * s + e, 1), :][:, :SLOTS_CTX // nset]) for s in range(nset)],
                axis=0)
            y = _mm(h_scr[SLOTS_CTX * i:SLOTS_CTX * (i + 1), :], wdb)
            y_ref[0] = (y * gate).astype(BF16)


def _expert_ffn(xc, xl, gc, gl, w_gate, w_up, w_down):
    assert SLOTS_CTX == SLOTS_LAT and FF_HALF % LANES == 0 and D_MODEL % OUT_TILE == 0
    xs = lambda n: pl.BlockSpec((1, n, D_MODEL), lambda e, j: (e, 0, 0))
    gs = lambda g: pl.BlockSpec(g.shape, lambda e, j: (0, 0))
    last = N_EXPERTS - 1
    up = pl.BlockSpec((1, D_MODEL, FF_HALF),
                      lambda e, j: (jnp.where(j >= 2, jnp.minimum(e + 1, last), e), 0, jnp.where(j == 1, 1, 0)))
    down = pl.BlockSpec((1, D_FF, OUT_TILE), lambda e, j: (e, 0, jnp.maximum(j - 2, 0)))
    ys = lambda n: pl.BlockSpec((1, n, OUT_TILE), lambda e, j: (e, 0, jnp.maximum(j - 2, 0)))
    return pl.pallas_call(
        _ffn_kernel,
        grid=(N_EXPERTS, 2 + D_MODEL // OUT_TILE),
        in_specs=[xs(SLOTS_CTX), xs(SLOTS_LAT), gs(gc), gs(gl), up, up, down],
        out_specs=[ys(SLOTS_CTX), ys(SLOTS_LAT)],
        out_shape=[jax.ShapeDtypeStruct((N_EXPERTS, SLOTS_CTX, D_MODEL), BF16),
                   jax.ShapeDtypeStruct((N_EXPERTS, SLOTS_LAT, D_MODEL), BF16)],
        scratch_shapes=[pltpu.VMEM((SLOTS_CTX + SLOTS_LAT, D_FF), BF16),
                        pltpu.VMEM((2, D_MODEL, LANES), F32)],
        compiler_params=_params(("arbitrary", "arbitrary"), BIG_VMEM_LIMIT),
        name="expert_ffn",
    )(xc, xl, gc, gl, w_gate, w_up, w_down)


def _combine_ctx_kernel(y_ref, idx_ref, x1_ref, mod_ref, lng_ref, lnb_ref, o_ref):
    g2 = _mod_rows(mod_ref, 0)[5]
    nslot = N_EXPERTS * CAP_CTX
    for s in range(CTX_GROUP):
        hit = lax.broadcasted_iota(I32, (SEQ, nslot), 0) == idx_ref[s]
        y = y_ref[:, CAP_CTX * s:CAP_CTX * (s + 1), :].reshape(nslot, D_MODEL)
        f = _mm(jnp.where(hit, 1.0, 0.0).astype(BF16), y)
        o_ref[s] = _layer_norm(DEEPNORM_ALPHA * x1_ref[s] + g2 * f, lng_ref[...], lnb_ref[...])


def _combine_ctx(yc, idx_row, x1, mod, ln_g, ln_b):
    full = lambda shape: pl.BlockSpec(shape, lambda s: (0,) * len(shape))
    return pl.pallas_call(
        _combine_ctx_kernel,
        grid=(BATCH // CTX_GROUP,),
        in_specs=[pl.BlockSpec((N_EXPERTS, CTX_GROUP * CAP_CTX, D_MODEL), lambda s: (0, s, 0)),
                  pl.BlockSpec((CTX_GROUP, 1, N_EXPERTS * CAP_CTX), lambda s: (s, 0, 0)),
                  pl.BlockSpec((CTX_GROUP, SEQ, D_MODEL), lambda s: (s, 0, 0)),
                  full((8, 6 * D_MODEL)), full((1, D_MODEL)), full((1, D_MODEL))],
        out_specs=pl.BlockSpec((CTX_GROUP, SEQ, D_MODEL), lambda s: (s, 0, 0)),
        out_shape=jax.ShapeDtypeStruct((BATCH, SEQ, D_MODEL), F32),
        compiler_params=_params(("arbitrary",)),
        name="combine_ctx",
    )(yc, idx_row, x1, mod, ln_g, ln_b)


def _combine_lat_kernel(y_ref, idx_ref, x1_ref, mod_ref, lng_ref, lnb_ref, o_ref):
    s = pl.program_id(0)
    g2 = _mod_rows(mod_ref, 1 + s)[5]
    tok = lax.broadcasted_iota(I32, (ROW_TILE, CAP_LAT), 0) + ROW_TILE * pl.program_id(1)
    f = jnp.zeros((ROW_TILE, D_MODEL), F32)
    for e in range(N_EXPERTS):
        hit = tok == idx_ref[0, e:e + 1, :]
        f = f + _mm(jnp.where(hit, 1.0, 0.0).astype(BF16), y_ref[e])
    o_ref[0] = _layer_norm(DEEPNORM_ALPHA * x1_ref[0] + g2 * f, lng_ref[...], lnb_ref[...])


def _combine_lat(yl, idx, x1, mod, ln_g, ln_b):
    nt = DEC_SEQ // ROW_TILE
    full = lambda shape: pl.BlockSpec(shape, lambda s, t: (0,) * len(shape))
    return pl.pallas_call(
        _combine_lat_kernel,
        grid=(DEC_BATCH, nt),
        in_specs=[pl.BlockSpec((N_EXPERTS, CAP_LAT, D_MODEL), lambda s, t: (0, s, 0)),
                  pl.BlockSpec((1, N_EXPERTS, CAP_LAT), lambda s, t: (s, 0, 0)),
                  pl.BlockSpec((1, ROW_TILE, D_MODEL), lambda s, t: (s, t, 0)),
                  full((8, 6 * D_MODEL)), full((1, D_MODEL)), full((1, D_MODEL))],
        out_specs=pl.BlockSpec((1, ROW_TILE, D_MODEL), lambda s, t: (s, t, 0)),
        out_shape=jax.ShapeDtypeStruct((DEC_BATCH, DEC_SEQ, D_MODEL), F32),
        compiler_params=_params(("arbitrary", "arbitrary")),
        name="combine_lat",
    )(yl, idx, x1, mod, ln_g, ln_b)


def _rope_tables():
    t = np.arange(DEC_SEQ)
    row = (t // GRID_W).astype(np.float64)
    col = (t % GRID_W).astype(np.float64)
    half = HEAD_DIM // 2
    inv = ROPE_THETA ** (-np.arange(0, half, 2, dtype=np.float64) / half)
    ang_r = row[:, None] * inv[None, :]
    ang_c = col[:, None] * inv[None, :]
    ang = np.concatenate([ang_r, ang_r, ang_c, ang_c], axis=-1)
    cos = np.tile(np.cos(ang), (1, 2))
    sin = np.tile(np.sin(ang), (1, 2))
    first = (np.arange(LANES) % 32) < 16
    return jnp.asarray(cos, F32), jnp.asarray(np.where(first[None, :], -sin, sin), F32)


def kernel(x_prompt, x_sample, cache_na_k, cache_na_v, cache_diff_k, cache_diff_v, c, c_ctx, w_ada, b_ada, w_in, w_out, na_rel_bias, lambda_q1, lambda_k1, lambda_q2, lambda_k2, subln_g, ln1_g, ln1_b, ln2_g, ln2_b, w_router, w_gate, w_up, w_down):
    l = 0
    cvec = jnp.concatenate([c_ctx[None, :], c, jnp.zeros((8 - 1 - DEC_BATCH, D_MODEL), F32)], axis=0)
    mod = _modulation(cvec, w_ada[l], b_ada[l][None, :])
    lamv = jnp.stack([lambda_q1[l], lambda_k1[l], lambda_q2[l], lambda_k2[l]], axis=0)
    w_in_b = w_in[l].astype(BF16)
    w_out_b = w_out[l].astype(BF16)
    sub = subln_g[l][None, :]
    wrt = w_router[l].T

    tr = lambda a: jnp.swapaxes(a, -1, -2)

    x1c, h2c, affc, na_k_t, na_v_t, diff_k_t, new_diff_v = _context_block(
        x_prompt, mod, lamv, w_in_b, w_out_b, sub, ln1_g[l][None, :], ln1_b[l][None, :], wrt)

    cos, sin = _rope_tables()
    naq, nak, nav, dfq, dfk, dfv, tbl, stats = _latent_qkv(x_sample, mod, w_in_b, cos, sin,
                                                    na_rel_bias[l].reshape(-1))
    na_o = _latent_na(naq, nak, nav, tr(cache_na_k), tr(cache_na_v), tbl, stats)
    df_o = _latent_diff(dfq, dfk, dfv, tr(cache_diff_k), cache_diff_v, lamv, sub)
    x1l, h2l, affl = _latent_out(x_sample, na_o, df_o, mod, w_out_b,
                                 ln1_g[l][None, :], ln1_b[l][None, :], wrt)

    idxc, gatec, idxl, gatel = _route(affc.reshape(BATCH * N_EXPERTS, SEQ),
                                      affl.reshape(DEC_BATCH * N_EXPERTS, DEC_SEQ))
    idxc = idxc.reshape(BATCH, 1, N_EXPERTS * CAP_CTX)
    idxl = idxl.reshape(DEC_BATCH, N_EXPERTS, CAP_LAT)
    xc = _gather_ctx(idxc, h2c)
    xl = _gather_lat(idxl, h2l)
    yc, yl = _expert_ffn(xc, xl, gatec, gatel, w_gate[l], w_up[l], w_down[l])
    y_prompt = _combine_ctx(yc, idxc, x1c, mod, ln2_g[l][None, :], ln2_b[l][None, :])
    y_sample = _combine_lat(yl, idxl, x1l, mod, ln2_g[l][None, :], ln2_b[l][None, :])
    return (y_prompt, y_sample, tr(na_k_t), tr(na_v_t), tr(diff_k_t), new_diff_v)
```

```python
import functools
import math

import jax
import jax.numpy as jnp
import numpy as np
from jax import lax
from jax.experimental import pallas as pl
from jax.experimental.pallas import tpu as pltpu

F32 = jnp.float32
BF16 = jnp.bfloat16
I32 = jnp.int32

D_MODEL = 1024
BATCH = 16
SEQ = 256
DEC_BATCH = 2
DEC_SEQ = 2048
PAST_LEN = 256
GRID_W = 64
GRID_ROWS = DEC_SEQ // GRID_W
HEAD_DIM = 64
NA_HEADS = 8
DIFF_HEADS = 4
NA_WIDTH = NA_HEADS * HEAD_DIM
DIFF_WIDTH = DIFF_HEADS * 2 * HEAD_DIM
QKV_WIDTH = 3 * NA_WIDTH + 3 * DIFF_WIDTH
NA_WIN_H = 8
NA_WIN_W = 16
N_REL_H = 2 * NA_WIN_H - 1
N_REL_W = 2 * NA_WIN_W - 1
N_EXPERTS = 16
EC_CAPACITY_FACTOR = 2
D_FF = 2816
ROPE_THETA = 10000.0
NORM_EPS = 1e-5
NEG_INF = -1e30
DEPTH = 1
DEEPNORM_ALPHA = (2.0 * DEPTH) ** 0.25
LAMBDA_INIT = 0.8 - 0.6 * math.exp(-0.3 * 0)
SCALE = HEAD_DIM ** -0.5

OFF_NAQ = 0
OFF_NAK = NA_WIDTH
OFF_NAV = 2 * NA_WIDTH
OFF_DFQ = 3 * NA_WIDTH
OFF_DFK = 3 * NA_WIDTH + DIFF_WIDTH
OFF_DFV = 3 * NA_WIDTH + 2 * DIFF_WIDTH

CAP_CTX = EC_CAPACITY_FACTOR * SEQ // N_EXPERTS
CAP_LAT = EC_CAPACITY_FACTOR * DEC_SEQ // N_EXPERTS
SLOTS_CTX = BATCH * CAP_CTX
SLOTS_LAT = DEC_BATCH * CAP_LAT

ROW_TILE = 512
NA_QUAD = 4
NA_WIN_ROWS = 12
NA_GROUP = 8
FF_TILE = 256
FF_HALF = D_FF // 2
OUT_TILE = 512
LANES = 128
GATHER_GROUP = 2
CTX_GROUP = 4
CTX_REQ = 2
DIFF_ROWS = 1024
Q_ROWS = 256

BOUND_SLACK = 1.0 + 2.0 ** -10
SOFTMAX_FLOOR = 1e-25

VMEM_LIMIT = 48 * 1024 * 1024
BIG_VMEM_LIMIT = 58 * 1024 * 1024


def _params(sem, vmem=VMEM_LIMIT, flags=None):
    return pltpu.CompilerParams(dimension_semantics=sem, vmem_limit_bytes=vmem, flags=flags)


def _mm(a, b):
    return jnp.dot(a, b, preferred_element_type=F32)


def _nt(a, b):
    return lax.dot_general(a, b, (((1,), (1,)), ((), ())), preferred_element_type=F32)


def _split(a):
    hi = a.astype(BF16)
    lo = (a - hi.astype(F32)).astype(BF16)
    return hi, lo


def _mm3(a, b):
    ah, al = _split(a)
    bh, bl = _split(b)
    return _mm(ah, bh) + _mm(al, bh) + _mm(ah, bl)


def _nt3(a, b):
    ah, al = _split(a)
    bh, bl = _split(b)
    return _nt(ah, bh) + _nt(al, bh) + _nt(ah, bl)


def _layer_norm(y, g, b):
    mu = jnp.mean(y, axis=-1, keepdims=True)
    d = y - mu
    var = jnp.mean(d * d, axis=-1, keepdims=True)
    return d * lax.rsqrt(var + NORM_EPS) * g + b


def _lam(l_ref):
    l = l_ref[...]
    a = jnp.sum(l[0:1] * l[1:2], axis=-1, keepdims=True)
    b = jnp.sum(l[2:3] * l[3:4], axis=-1, keepdims=True)
    return jnp.exp(a) - jnp.exp(b) + LAMBDA_INIT


def _max_col_norm(kt):
    k = kt.astype(F32)
    n2 = jnp.max(jnp.sum(k * k, axis=0, keepdims=True), axis=-1, keepdims=True)
    return jnp.sqrt(n2) * BOUND_SLACK


def _keys_with_ones(k, axis):
    first = lax.broadcasted_iota(I32, k.shape, axis) == 0
    return jnp.concatenate([k, jnp.where(first, 1.0, 0.0).astype(BF16)], axis=axis)


def _mod_rows(mod_ref, row):
    return [mod_ref[pl.ds(row, 1), i * D_MODEL:(i + 1) * D_MODEL] for i in range(6)]


def _sub_norm(o, sub):
    ms = jnp.mean(o * o, axis=-1, keepdims=True)
    return o * lax.rsqrt(ms + NORM_EPS) * sub * (1.0 - LAMBDA_INIT)


def _router_aff_t(h2, wrt):
    lg = _nt3(wrt, h2)
    m = jnp.max(lg, axis=0, keepdims=True)
    e = jnp.exp(lg - m)
    return e / jnp.sum(e, axis=0, keepdims=True)


def _mod_kernel(c_ref, wa_ref, wb_ref, b_ref, o_ref):
    c = c_ref[...]
    s = c * (1.0 / (1.0 + jnp.exp(-c)))
    half = D_MODEL // 2
    o_ref[...] = _mm3(s[:, :half], wa_ref[...]) + _mm3(s[:, half:], wb_ref[...]) + b_ref[...]


def _modulation(cvec, w_ada, b_ada):
    ncol = 6 * D_MODEL
    half = D_MODEL // 2
    return pl.pallas_call(
        _mod_kernel,
        grid=(6,),
        in_specs=[pl.BlockSpec((8, D_MODEL), lambda j: (0, 0)),
                  pl.BlockSpec((half, D_MODEL), lambda j: (0, j)),
                  pl.BlockSpec((half, D_MODEL), lambda j: (1, j)),
                  pl.BlockSpec((1, D_MODEL), lambda j: (0, j))],
        out_specs=pl.BlockSpec((8, D_MODEL), lambda j: (0, j)),
        out_shape=jax.ShapeDtypeStruct((8, ncol), F32),
        compiler_params=_params(("arbitrary",)),
        name="modulation",
    )(cvec, w_ada, w_ada, b_ada)


def _ctx_kernel(x_ref, mod_ref, lam_ref, win_ref, wout_ref, sub_ref, lng_ref, lnb_ref, wrt_ref,
                x1_ref, h2_ref, aff_ref, nak_ref, nav_ref, dfk_ref, dfv_ref,
                qkv_scr, o_scr, s_scr, e_scr):
    rows = CTX_REQ * SEQ
    x = x_ref[...].reshape(rows, D_MODEL)
    sh1, sc1, g1, sh2, sc2, g2 = _mod_rows(mod_ref, 0)
    h = (x * (1.0 + sc1) + sh1).astype(BF16)
    qkv_scr[...] = _mm(h, win_ref[...])

    kt_na = qkv_scr[:, OFF_NAK:OFF_NAK + NA_WIDTH].T
    kt_df = qkv_scr[:, OFF_DFK:OFF_DFK + DIFF_WIDTH].T
    vt_na = qkv_scr[:, OFF_NAV:OFF_NAV + NA_WIDTH].T
    for r in range(CTX_REQ):
        tok = slice(SEQ * r, SEQ * (r + 1))
        nak_ref[r, 0] = kt_na[:, tok].reshape(NA_HEADS, HEAD_DIM, SEQ)
        nav_ref[r, 0] = vt_na[:, tok].reshape(NA_HEADS, HEAD_DIM, SEQ)
        dfk_ref[r, 0] = kt_df[:, tok].reshape(DIFF_HEADS, 2, HEAD_DIM, SEQ)
        for hh in range(DIFF_HEADS):
            c0 = OFF_DFV + 2 * HEAD_DIM * hh
            dfv_ref[r, 0, hh] = qkv_scr[tok, c0:c0 + 2 * HEAD_DIM]

    kt = jnp.concatenate([kt_na, kt_df], axis=0).astype(BF16)
    q_cols = [OFF_NAQ + HEAD_DIM * i for i in range(NA_HEADS)] + [
        OFF_DFQ + HEAD_DIM * i for i in range(2 * DIFF_HEADS)]
    nmap = len(q_cols)
    for r in range(CTX_REQ):
        tok = slice(SEQ * r, SEQ * (r + 1))
        for i, c in enumerate(q_cols):
            q = (qkv_scr[tok, c:c + HEAD_DIM] * SCALE).astype(BF16)
            s_scr[nmap * r + i] = _mm(q, kt[HEAD_DIM * i:HEAD_DIM * (i + 1), tok])
    s = s_scr[...]
    e = jnp.exp(s - jnp.max(s, axis=-1, keepdims=True))
    rl = 1.0 / jnp.sum(e, axis=-1, keepdims=True)
    e_scr[...] = e.astype(BF16)

    lam = _lam(lam_ref)
    sub = sub_ref[...]
    for r in range(CTX_REQ):
        tok = slice(SEQ * r, SEQ * (r + 1))
        for hh in range(NA_HEADS):
            i = nmap * r + hh
            v = qkv_scr[tok, OFF_NAV + HEAD_DIM * hh:OFF_NAV + HEAD_DIM * (hh + 1)].astype(BF16)
            o_scr[tok, HEAD_DIM * hh:HEAD_DIM * (hh + 1)] = (_mm(e_scr[i], v) * rl[i]).astype(BF16)
        for hh in range(DIFF_HEADS):
            i1 = nmap * r + NA_HEADS + 2 * hh
            c0 = OFF_DFV + 2 * HEAD_DIM * hh
            v = qkv_scr[tok, c0:c0 + 2 * HEAD_DIM].astype(BF16)
            o = _mm(e_scr[i1], v) * rl[i1] - _mm(e_scr[i1 + 1], v) * (lam * rl[i1 + 1])
            c1 = NA_WIDTH + 2 * HEAD_DIM * hh
            o_scr[tok, c1:c1 + 2 * HEAD_DIM] = _sub_norm(o, sub).astype(BF16)

    a = _mm(o_scr[...], wout_ref[...])
    x1 = _layer_norm(DEEPNORM_ALPHA * x + g1 * a, lng_ref[...], lnb_ref[...])
    x1_ref[...] = x1.reshape(CTX_REQ, SEQ, D_MODEL)
    h2 = x1 * (1.0 + sc2) + sh2
    h2_ref[...] = h2.astype(BF16).reshape(CTX_REQ, SEQ, D_MODEL)
    aff = _router_aff_t(h2, wrt_ref[...])
    for r in range(CTX_REQ):
        aff_ref[r] = aff[:, SEQ * r:SEQ * (r + 1)]


def _context_block(x, mod, lamv, w_in, w_out, sub, ln_g, ln_b, wrt):
    full = lambda shape: pl.BlockSpec(shape, lambda b: (0,) * len(shape))
    once = lambda shape: pl.BlockSpec(shape, lambda b: (0,) * len(shape), pipeline_mode=pl.Buffered(1))
    req = lambda *tail: pl.BlockSpec((CTX_REQ,) + tail, lambda b: (b,) + (0,) * len(tail))
    nmaps = CTX_REQ * (NA_HEADS + 2 * DIFF_HEADS)
    return pl.pallas_call(
        _ctx_kernel,
        grid=(BATCH // CTX_REQ,),
        in_specs=[req(SEQ, D_MODEL),
                  full((8, 6 * D_MODEL)), full((4, HEAD_DIM)),
                  once((D_MODEL, QKV_WIDTH)), once((D_MODEL, D_MODEL)),
                  full((1, 2 * HEAD_DIM)), full((1, D_MODEL)), full((1, D_MODEL)),
                  full((N_EXPERTS, D_MODEL))],
        out_specs=[req(SEQ, D_MODEL), req(SEQ, D_MODEL), req(N_EXPERTS, SEQ),
                   req(1, NA_HEADS, HEAD_DIM, SEQ), req(1, NA_HEADS, HEAD_DIM, SEQ),
                   req(1, DIFF_HEADS, 2, HEAD_DIM, SEQ), req(1, DIFF_HEADS, SEQ, 2 * HEAD_DIM)],
        out_shape=[jax.ShapeDtypeStruct((BATCH, SEQ, D_MODEL), F32),
                   jax.ShapeDtypeStruct((BATCH, SEQ, D_MODEL), BF16),
                   jax.ShapeDtypeStruct((BATCH, N_EXPERTS, SEQ), F32),
                   jax.ShapeDtypeStruct((BATCH, 1, NA_HEADS, HEAD_DIM, SEQ), F32),
                   jax.ShapeDtypeStruct((BATCH, 1, NA_HEADS, HEAD_DIM, SEQ), F32),
                   jax.ShapeDtypeStruct((BATCH, 1, DIFF_HEADS, 2, HEAD_DIM, SEQ), F32),
                   jax.ShapeDtypeStruct((BATCH, 1, DIFF_HEADS, SEQ, 2 * HEAD_DIM), F32)],
        scratch_shapes=[pltpu.VMEM((CTX_REQ * SEQ, QKV_WIDTH), F32), pltpu.VMEM((CTX_REQ * SEQ, D_MODEL), BF16),
                        pltpu.VMEM((nmaps, SEQ, SEQ), F32), pltpu.VMEM((nmaps, SEQ, SEQ), BF16)],
        compiler_params=_params(("arbitrary",)),
        name="context_block",
    )(x, mod, lamv, w_in, w_out, sub, ln_g, ln_b, wrt)


def _lat_qkv_kernel(x_ref, mod_ref, win_ref, cos_ref, sin_ref, rpb_ref,
                    naq_ref, nak_ref, nav_ref, dfq_ref, dfk_ref, dfv_ref, tbl_ref, stat_ref, qkv_scr):
    b = pl.program_id(0)
    bias_top = _write_bias_tables(rpb_ref, b * pl.num_programs(1) + pl.program_id(1), tbl_ref)
    x = x_ref[0]
    sh1, sc1 = _mod_rows(mod_ref, 1 + b)[:2]
    h = (x * (1.0 + sc1) + sh1).astype(BF16)
    qkv_scr[...] = _mm(h, win_ref[...])

    kb = qkv_scr[:, OFF_NAK:OFF_NAK + NA_WIDTH].astype(BF16).astype(F32)
    col = lax.broadcasted_iota(I32, (NA_WIDTH, LANES), 0) // HEAD_DIM
    head_of = jnp.where(col == lax.broadcasted_iota(I32, (NA_WIDTH, LANES), 1), 1.0, 0.0).astype(BF16)
    kn2 = jnp.max(_mm((kb * kb).astype(BF16), head_of), axis=0, keepdims=True) * (1.0 + 2.0 ** -7)
    stat_ref[0] = jnp.concatenate([kn2, jnp.broadcast_to(bias_top, (1, LANES)),
                                   jnp.zeros((6, LANES), F32)], axis=0)

    qb = (qkv_scr[:, OFF_NAQ:OFF_NAQ + NA_WIDTH] * SCALE).astype(BF16)
    qf = qb.astype(F32)
    qn = jnp.sqrt(_mm((qf * qf).astype(BF16), head_of) * (1.0 + 2.0 ** -7)) * (1.0 + 2.0 ** -7)
    lane64 = lax.broadcasted_iota(I32, (ROW_TILE, HEAD_DIM), 1)
    key_ext = jnp.where(lane64 < 2, 1.0, 0.0).astype(BF16)
    for hh in range(NA_HEADS):
        lo, hi = HEAD_DIM * hh, HEAD_DIM * (hh + 1)
        ext = jnp.where(lane64 == 0, -qn[:, hh:hh + 1], jnp.where(lane64 == 1, -1.0, 0.0)).astype(BF16)
        naq_ref[0, hh] = jnp.concatenate([qb[:, lo:hi], ext], axis=1)
        nak_ref[0, hh] = jnp.concatenate([qkv_scr[:, OFF_NAK + lo:OFF_NAK + hi].astype(BF16), key_ext], axis=1)
        nav_ref[0, hh] = qkv_scr[:, OFF_NAV + lo:OFF_NAV + hi].astype(BF16)

    cos = cos_ref[...]
    sin = sin_ref[...]
    lane = lax.broadcasted_iota(I32, (ROW_TILE, LANES), 1)
    first = (lane & 31) < 16

    def rope(t):
        rot = jnp.where(first, pltpu.roll(t, LANES - 16, 1), pltpu.roll(t, 16, 1))
        return t * cos + rot * sin

    map_of = jnp.where(lax.broadcasted_iota(I32, (LANES, LANES), 0) // HEAD_DIM
                       == lax.broadcasted_iota(I32, (LANES, LANES), 1), 1.0, 0.0).astype(BF16)
    for hh in range(DIFF_HEADS):
        lo = 2 * HEAD_DIM * hh
        q = (rope(qkv_scr[:, OFF_DFQ + lo:OFF_DFQ + lo + LANES]) * SCALE).astype(BF16)
        k = rope(qkv_scr[:, OFF_DFK + lo:OFF_DFK + lo + LANES])
        qf = q.astype(F32)
        qn = jnp.sqrt(_mm((qf * qf).astype(BF16), map_of) * (1.0 + 2.0 ** -7)) * (1.0 + 2.0 ** -7)
        for m in range(2):
            ext = jnp.where(lane64 == 0, -qn[:, m:m + 1], 0.0).astype(BF16)
            dfq_ref[0, hh, m] = jnp.concatenate([q[:, HEAD_DIM * m:HEAD_DIM * (m + 1)], ext], axis=1)
        dfk_ref[0, hh] = k.T.astype(BF16).reshape(2, HEAD_DIM, ROW_TILE)
        dfv_ref[0, hh] = qkv_scr[:, OFF_DFV + lo:OFF_DFV + lo + LANES].astype(BF16)


def _latent_qkv(x, mod, w_in, cos, sin, rpb_flat):
    nt = DEC_SEQ // ROW_TILE
    assert DEC_BATCH * nt == NA_HEADS
    tbl_shape = (NA_HEADS, 3, NA_QUAD * GRID_W, NA_WIN_ROWS * GRID_W)
    full = lambda shape: pl.BlockSpec(shape, lambda b, t: (0,) * len(shape))
    hs = lambda width: pl.BlockSpec((1, NA_HEADS, ROW_TILE, width), lambda b, t: (b, 0, t, 0))
    na_shape = lambda width: jax.ShapeDtypeStruct((DEC_BATCH, NA_HEADS, DEC_SEQ, width), BF16)
    return pl.pallas_call(
        _lat_qkv_kernel,
        grid=(DEC_BATCH, nt),
        in_specs=[pl.BlockSpec((1, ROW_TILE, D_MODEL), lambda b, t: (b, t, 0)),
                  full((8, 6 * D_MODEL)), full((D_MODEL, QKV_WIDTH)),
                  pl.BlockSpec((ROW_TILE, LANES), lambda b, t: (t, 0)),
                  pl.BlockSpec((ROW_TILE, LANES), lambda b, t: (t, 0)),
                  pl.BlockSpec(memory_space=pltpu.SMEM)],
        out_specs=[hs(2 * HEAD_DIM), hs(2 * HEAD_DIM), hs(HEAD_DIM),
                   pl.BlockSpec((1, DIFF_HEADS, 2, ROW_TILE, 2 * HEAD_DIM), lambda b, t: (b, 0, 0, t, 0)),
                   pl.BlockSpec((1, DIFF_HEADS, 2, HEAD_DIM, ROW_TILE), lambda b, t: (b, 0, 0, 0, t)),
                   pl.BlockSpec((1, DIFF_HEADS, ROW_TILE, 2 * HEAD_DIM), lambda b, t: (b, 0, t, 0)),
                   pl.BlockSpec((1,) + tbl_shape[1:], lambda b, t: (b * nt + t, 0, 0, 0)),
                   pl.BlockSpec((1, 8, LANES), lambda b, t: (b * nt + t, 0, 0))],
        out_shape=[na_shape(2 * HEAD_DIM), na_shape(2 * HEAD_DIM), na_shape(HEAD_DIM)]
        + [jax.ShapeDtypeStruct((DEC_BATCH, DIFF_HEADS, 2, DEC_SEQ, 2 * HEAD_DIM), BF16),
           jax.ShapeDtypeStruct((DEC_BATCH, DIFF_HEADS, 2, HEAD_DIM, DEC_SEQ), BF16),
           jax.ShapeDtypeStruct((DEC_BATCH, DIFF_HEADS, DEC_SEQ, 2 * HEAD_DIM), BF16),
           jax.ShapeDtypeStruct(tbl_shape, F32),
           jax.ShapeDtypeStruct((NA_HEADS, 8, LANES), F32)],
        scratch_shapes=[pltpu.VMEM((ROW_TILE, QKV_WIDTH), F32)],
        compiler_params=_params(("arbitrary", "arbitrary")),
        name="latent_qkv",
    )(x, mod, w_in, cos, sin, rpb_flat)


def _na_row_offset(kind, qr, kr):
    if kind == 0:
        return kr - qr + 7 if kr < NA_WIN_H else None
    if kind == 1:
        return kr - qr + 3 if qr <= kr < qr + NA_WIN_H else None
    return kr - qr - 1 if NA_WIN_ROWS - NA_WIN_H <= kr else None


def _write_bias_tables(rpb_ref, h, o_ref):
    base = h * (N_REL_H * N_REL_W)
    wq = lax.broadcasted_iota(I32, (GRID_W, LANES), 0)
    c = lax.broadcasted_iota(I32, (GRID_W, LANES), 1)
    wk = c & (GRID_W - 1)
    right = c >= GRID_W
    c0 = jnp.clip(wq - NA_WIN_W // 2, 0, GRID_W - NA_WIN_W)
    in_win = (wk >= c0) & (wk < c0 + NA_WIN_W)
    cache = {}
    strips = {}

    def strip(r):
        if r not in strips:
            c8 = lax.broadcasted_iota(I32, (8, LANES), 1)
            d = jnp.where(c8 < GRID_W, c8, c8 - LANES)
            co_of = jnp.clip(d, -(NA_WIN_W - 1), NA_WIN_W - 1) + (NA_WIN_W - 1)
            acc = jnp.zeros((8, LANES), F32)
            for co in range(N_REL_W):
                acc = jnp.where(co_of == co, rpb_ref[base + r * N_REL_W + co], acc)
            strips[r] = jnp.concatenate([acc] * (GRID_W // 8), axis=0)
        return strips[r]

    def tile(rl, rr):
        if (rl, rr) in cache:
            return cache[(rl, rr)]
        if rl is None and rr is None:
            t = jnp.full((GRID_W, LANES), NEG_INF, F32)
        else:
            zero = jnp.zeros((GRID_W, LANES), F32)
            tl = pltpu.roll(strip(rl), 0, 1, stride=1, stride_axis=0) if rl is not None else zero
            tr = pltpu.roll(strip(rr), GRID_W, 1, stride=1, stride_axis=0) if rr is not None else zero
            acc = jnp.where(right, tr, tl)
            ok = in_win
            if rl is None:
                ok = ok & right
            if rr is None:
                ok = ok & jnp.logical_not(right)
            t = jnp.where(ok, acc, NEG_INF)
        cache[(rl, rr)] = t
        return t

    for kind in range(3):
        for qr in range(NA_QUAD):
            row = [tile(_na_row_offset(kind, qr, 2 * p), _na_row_offset(kind, qr, 2 * p + 1))
                   for p in range(NA_WIN_ROWS // 2)]
            o_ref[0, kind, GRID_W * qr:GRID_W * (qr + 1), :] = jnp.concatenate(row, axis=1)
    top = functools.reduce(jnp.maximum, cache.values())
    return jnp.max(jnp.max(top, axis=-1, keepdims=True), axis=0, keepdims=True)


def _na_kernel(q_ref, k_ref, v_ref, ck_ref, cv_ref, tbl_ref, stat_ref, o_ref):
    b = pl.program_id(0)
    g = pl.program_id(2)
    u0 = jnp.clip(NA_QUAD * g - NA_QUAD, 0, GRID_ROWS - NA_WIN_ROWS)
    start = pl.multiple_of(u0 * GRID_W, GRID_W)
    nwin = NA_WIN_ROWS * GRID_W
    tiles = stat_ref.shape[0] // DEC_BATCH
    kn2 = functools.reduce(jnp.maximum, [stat_ref[b * tiles + t, 0:1, :] for t in range(tiles)])

    lane = lax.broadcasted_iota(I32, (1, 2 * HEAD_DIM), 1)
    sub = lax.broadcasted_iota(I32, (HEAD_DIM, PAST_LEN), 0)
    ones_rows = jnp.where(sub < 2, 1.0, 0.0).astype(BF16)

    def attend(exact_max):
        lmin = None
        for j in range(NA_GROUP):
            q = q_ref[0, j]
            kw = k_ref[0, j, pl.ds(start, nwin), :]
            vw = v_ref[0, j, pl.ds(start, nwin), :]
            kct = ck_ref[0, 0, j].astype(BF16)
            vct = cv_ref[0, 0, j].astype(BF16)
            kct1 = jnp.concatenate([kct, ones_rows], axis=0)
            if exact_max:
                q0 = (q.astype(F32) * jnp.where(lane < HEAD_DIM, 1.0, 0.0)).astype(BF16)
                sl = _nt(q0, kw) + tbl_ref[j, 0]
                sc = _mm(q[:, :HEAD_DIM], kct)
                m = jnp.maximum(jnp.max(sl, axis=-1, keepdims=True), jnp.max(sc, axis=-1, keepdims=True))
                el = jnp.exp(sl - m)
                ec = jnp.exp(sc - m)
            else:
                up = 1.0 + 2.0 ** -7
                knorm = jnp.maximum(jnp.sqrt(kn2[:, j:j + 1]) * BOUND_SLACK, _max_col_norm(kct)) * up
                btop = jnp.maximum(stat_ref[j, 1:2, 0:1], 0.0) * up
                scale = jnp.where(lane < HEAD_DIM, 1.0, jnp.where(lane == HEAD_DIM, knorm, btop))
                qs = (q.astype(F32) * scale).astype(BF16)
                el = jnp.exp(_nt(qs, kw) + tbl_ref[j, 0])
                ec = jnp.exp(_mm(qs, kct1))
            l = jnp.sum(el, axis=-1, keepdims=True) + jnp.sum(ec, axis=-1, keepdims=True)
            o = (_mm(el.astype(BF16), vw) + _nt(ec.astype(BF16), vct)) / l
            o_ref[0, :, HEAD_DIM * j:HEAD_DIM * (j + 1)] = o.astype(BF16)
            lmin = jnp.min(l) if lmin is None else jnp.minimum(lmin, jnp.min(l))
        return lmin

    lmin = attend(exact_max=False)

    @pl.when(jnp.logical_not(lmin >= SOFTMAX_FLOOR))
    def _():
        attend(exact_max=True)


def _latent_na(naq, nak, nav, cache_k, cache_v, tbl, stats):
    assert NA_GROUP == NA_HEADS
    nq = GRID_ROWS // NA_QUAD
    rows = NA_QUAD * GRID_W
    kind = lambda g: jnp.minimum(g, 1) + g // (nq - 1)
    return pl.pallas_call(
        _na_kernel,
        grid=(DEC_BATCH, NA_HEADS // NA_GROUP, nq),
        in_specs=[pl.BlockSpec((1, NA_GROUP, rows, 2 * HEAD_DIM), lambda b, p, g: (b, p, g, 0)),
                  pl.BlockSpec((1, NA_GROUP, DEC_SEQ, 2 * HEAD_DIM), lambda b, p, g: (b, p, 0, 0)),
                  pl.BlockSpec((1, NA_GROUP, DEC_SEQ, HEAD_DIM), lambda b, p, g: (b, p, 0, 0)),
                  pl.BlockSpec((1, 1, NA_GROUP, HEAD_DIM, PAST_LEN), lambda b, p, g: (b, 0, p, 0, 0)),
                  pl.BlockSpec((1, 1, NA_GROUP, HEAD_DIM, PAST_LEN), lambda b, p, g: (b, 0, p, 0, 0)),
                  pl.BlockSpec((NA_GROUP, 1, rows, NA_WIN_ROWS * GRID_W), lambda b, p, g: (p, kind(g), 0, 0)),
                  pl.BlockSpec(stats.shape, lambda b, p, g: (0, 0, 0))],
        out_specs=pl.BlockSpec((1, rows, NA_GROUP * HEAD_DIM), lambda b, p, g: (b, g, p)),
        out_shape=jax.ShapeDtypeStruct((DEC_BATCH, DEC_SEQ, NA_WIDTH), BF16),
        compiler_params=_params(("arbitrary",) * 3),
        name="latent_na",
    )(naq, nak, nav, cache_k, cache_v, tbl, stats)


def _diff_kernel(q_ref, kt_ref, v_ref, ck_ref, cv_ref, lam_ref, sub_ref, o_ref):
    lam = _lam(lam_ref)
    vc = cv_ref[0, 0, 0].astype(BF16)
    vl = v_ref[0, 0]
    kcs = [ck_ref[0, 0, 0, m].astype(BF16) for m in range(2)]
    kls = [kt_ref[0, 0, m] for m in range(2)]
    knorm = [jnp.maximum(_max_col_norm(kcs[m]), _max_col_norm(kls[m])) for m in range(2)]
    kcs1 = [_keys_with_ones(k, 0) for k in kcs]
    kls1 = [_keys_with_ones(k, 0) for k in kls]
    lane = lax.broadcasted_iota(I32, (1, 2 * HEAD_DIM), 1)
    scale = [jnp.where(lane < HEAD_DIM, 1.0, jnp.where(lane == HEAD_DIM, kn * (1.0 + 2.0 ** -7), 0.0))
             for kn in knorm]

    def attend(exact_max):
        lmin = None
        for r in range(DIFF_ROWS // Q_ROWS):
            rows = slice(Q_ROWS * r, Q_ROWS * (r + 1))
            parts = []
            for m in range(2):
                q = q_ref[0, 0, m, rows, :]
                if exact_max:
                    sc = _mm(q[:, :HEAD_DIM], kcs[m])
                    sl = _mm(q[:, :HEAD_DIM], kls[m])
                    mx = jnp.maximum(jnp.max(sc, axis=-1, keepdims=True), jnp.max(sl, axis=-1, keepdims=True))
                    ec = jnp.exp(sc - mx)
                    el = jnp.exp(sl - mx)
                else:
                    qs = (q.astype(F32) * scale[m]).astype(BF16)
                    ec = jnp.exp(_mm(qs, kcs1[m]))
                    el = jnp.exp(_mm(qs, kls1[m]))
                l = jnp.sum(ec, axis=-1, keepdims=True) + jnp.sum(el, axis=-1, keepdims=True)
                parts.append((_mm(ec.astype(BF16), vc) + _mm(el.astype(BF16), vl), l))
                lmin = jnp.min(l) if lmin is None else jnp.minimum(lmin, jnp.min(l))
            (a1, l1), (a2, l2) = parts
            o = a1 * (1.0 / l1) - a2 * (lam / l2)
            o_ref[0, rows, :] = _sub_norm(o, sub_ref[...]).astype(BF16)
        return lmin

    lmin = attend(exact_max=False)

    @pl.when(jnp.logical_not(lmin >= SOFTMAX_FLOOR))
    def _():
        attend(exact_max=True)


def _latent_diff(dfq, dfk, dfv, cache_k, cache_v, lamv, sub):
    nt = DEC_SEQ // DIFF_ROWS
    full = lambda shape: pl.BlockSpec(shape, lambda b, h, t: (0,) * len(shape))
    return pl.pallas_call(
        _diff_kernel,
        grid=(DEC_BATCH, DIFF_HEADS, nt),
        in_specs=[pl.BlockSpec((1, 1, 2, DIFF_ROWS, 2 * HEAD_DIM), lambda b, h, t: (b, h, 0, t, 0)),
                  pl.BlockSpec((1, 1, 2, HEAD_DIM, DEC_SEQ), lambda b, h, t: (b, h, 0, 0, 0)),
                  pl.BlockSpec((1, 1, DEC_SEQ, 2 * HEAD_DIM), lambda b, h, t: (b, h, 0, 0)),
                  pl.BlockSpec((1, 1, 1, 2, HEAD_DIM, PAST_LEN), lambda b, h, t: (b, 0, h, 0, 0, 0)),
                  pl.BlockSpec((1, 1, 1, PAST_LEN, 2 * HEAD_DIM), lambda b, h, t: (b, 0, h, 0, 0)),
                  full((4, HEAD_DIM)), full((1, 2 * HEAD_DIM))],
        out_specs=pl.BlockSpec((1, DIFF_ROWS, 2 * HEAD_DIM), lambda b, h, t: (b, t, h)),
        out_shape=jax.ShapeDtypeStruct((DEC_BATCH, DEC_SEQ, DIFF_WIDTH), BF16),
        compiler_params=_params(("arbitrary",) * 3),
        name="latent_diff",
    )(dfq, dfk, dfv, cache_k, cache_v, lamv, sub)


def _lat_out_kernel(x_ref, na_ref, df_ref, mod_ref, wout_ref, lng_ref, lnb_ref, wrt_ref,
                    x1_ref, h2_ref, aff_ref):
    b = pl.program_id(0)
    x = x_ref[0]
    _, _, g1, sh2, sc2, _ = _mod_rows(mod_ref, 1 + b)
    a = _mm(na_ref[0], wout_ref[0:NA_WIDTH, :]) + _mm(df_ref[0], wout_ref[NA_WIDTH:, :])
    x1 = _layer_norm(DEEPNORM_ALPHA * x + g1 * a, lng_ref[...], lnb_ref[...])
    x1_ref[0] = x1
    h2 = x1 * (1.0 + sc2) + sh2
    h2_ref[0] = h2.astype(BF16)
    aff_ref[0] = _router_aff_t(h2, wrt_ref[...])


def _latent_out(x, na_o, df_o, mod, w_out, ln_g, ln_b, wrt):
    nt = DEC_SEQ // ROW_TILE
    full = lambda shape: pl.BlockSpec(shape, lambda b, t: (0,) * len(shape))
    tok = lambda w: pl.BlockSpec((1, ROW_TILE, w), lambda b, t: (b, t, 0))
    return pl.pallas_call(
        _lat_out_kernel,
        grid=(DEC_BATCH, nt),
        in_specs=[tok(D_MODEL), tok(NA_WIDTH), tok(DIFF_WIDTH), full((8, 6 * D_MODEL)),
                  full((D_MODEL, D_MODEL)), full((1, D_MODEL)), full((1, D_MODEL)),
                  full((N_EXPERTS, D_MODEL))],
        out_specs=[tok(D_MODEL), tok(D_MODEL),
                   pl.BlockSpec((1, N_EXPERTS, ROW_TILE), lambda b, t: (b, 0, t))],
        out_shape=[jax.ShapeDtypeStruct((DEC_BATCH, DEC_SEQ, D_MODEL), F32),
                   jax.ShapeDtypeStruct((DEC_BATCH, DEC_SEQ, D_MODEL), BF16),
                   jax.ShapeDtypeStruct((DEC_BATCH, N_EXPERTS, DEC_SEQ), F32)],
        compiler_params=_params(("arbitrary", "arbitrary")),
        name="latent_out",
    )(x, na_o, df_o, mod, w_out, ln_g, ln_b, wrt)


def _ranks_first(va, ia, vb, ib):
    return (va > vb) | ((va == vb) & (ia < ib))


def _sort_rows_desc(aff):
    rows, n = aff.shape
    nb = n // LANES
    lane = lax.broadcasted_iota(I32, (rows, LANES), 1)
    v = [aff[:, LANES * b:LANES * (b + 1)] for b in range(nb)]
    ix = [lane + LANES * b for b in range(nb)]
    k = 2
    while k <= n:
        j = k // 2
        while j >= 1:
            if j >= LANES:
                sb = j // LANES
                for lo in range(nb):
                    if lo & sb:
                        continue
                    hi = lo | sb
                    f = _ranks_first(v[lo], ix[lo], v[hi], ix[hi])
                    if k < n and (LANES * lo) & k:
                        f = jnp.logical_not(f)
                    v[lo], v[hi] = jnp.where(f, v[lo], v[hi]), jnp.where(f, v[hi], v[lo])
                    ix[lo], ix[hi] = jnp.where(f, ix[lo], ix[hi]), jnp.where(f, ix[hi], ix[lo])
            else:
                upper = (lane & j) != 0
                if k < LANES:
                    flip_lanes = jnp.logical_xor(upper, (lane & k) != 0)
                for b in range(nb):
                    pv = jnp.where(upper, pltpu.roll(v[b], j, 1), pltpu.roll(v[b], LANES - j, 1))
                    pi = jnp.where(upper, pltpu.roll(ix[b], j, 1), pltpu.roll(ix[b], LANES - j, 1))
                    f = _ranks_first(v[b], ix[b], pv, pi)
                    if k < LANES:
                        flip = flip_lanes
                    elif k < n and (LANES * b) & k:
                        flip = jnp.logical_not(upper)
                    else:
                        flip = upper
                    keep = jnp.logical_xor(f, flip)
                    v[b] = jnp.where(keep, v[b], pv)
                    ix[b] = jnp.where(keep, ix[b], pi)
            j //= 2
        k *= 2
    return v, ix


def _route_kernel(affc_ref, affl_ref, idxc_ref, gatec_ref, idxl_ref, gatel_ref):
    for aff_ref, idx_ref, gate_ref, cap in ((affc_ref, idxc_ref, gatec_ref, CAP_CTX),
                                            (affl_ref, idxl_ref, gatel_ref, CAP_LAT)):
        v, ix = _sort_rows_desc(aff_ref[...])
        if cap < LANES:
            idx_ref[...] = ix[0][:, :cap]
            gate_ref[...] = v[0]
        else:
            idx_ref[...] = jnp.concatenate(ix[:cap // LANES], axis=1)
            gate_ref[...] = jnp.concatenate(v[:cap // LANES], axis=1)


def _route(affc_t, affl_t):
    out = lambda a, cap: [jax.ShapeDtypeStruct((a.shape[0], cap), I32),
                          jax.ShapeDtypeStruct((a.shape[0], max(cap, LANES)), F32)]
    return pl.pallas_call(
        _route_kernel,
        out_shape=out(affc_t, CAP_CTX) + out(affl_t, CAP_LAT),
        compiler_params=_params(None),
        name="route",
    )(affc_t, affl_t)


def _row_to_col(row):
    n = row.shape[1]
    eye = lax.broadcasted_iota(I32, (n, n), 0) == lax.broadcasted_iota(I32, (n, n), 1)
    return jnp.sum(jnp.where(eye, row, jnp.zeros_like(row)), axis=1, keepdims=True)


def _gather_ctx_kernel(idx_ref, h_ref, x_ref):
    nslot = N_EXPERTS * CAP_CTX
    for s in range(CTX_GROUP):
        hit = lax.broadcasted_iota(I32, (nslot, SEQ), 1) == _row_to_col(idx_ref[s])
        xs = _mm(jnp.where(hit, 1.0, 0.0).astype(BF16), h_ref[s])
        x_ref[:, CAP_CTX * s:CAP_CTX * (s + 1), :] = xs.astype(BF16).reshape(N_EXPERTS, CAP_CTX, D_MODEL)


def _gather_ctx(idx_row, h2):
    nslot = N_EXPERTS * CAP_CTX
    return pl.pallas_call(
        _gather_ctx_kernel,
        grid=(BATCH // CTX_GROUP,),
        in_specs=[pl.BlockSpec((CTX_GROUP, 1, nslot), lambda s: (s, 0, 0)),
                  pl.BlockSpec((CTX_GROUP, SEQ, D_MODEL), lambda s: (s, 0, 0))],
        out_specs=pl.BlockSpec((N_EXPERTS, CTX_GROUP * CAP_CTX, D_MODEL), lambda s: (0, s, 0)),
        out_shape=jax.ShapeDtypeStruct((N_EXPERTS, SLOTS_CTX, D_MODEL), BF16),
        compiler_params=_params(("arbitrary",)),
        name="gather_ctx",
    )(idx_row, h2)


def _gather_lat_kernel(idx_ref, h_ref, x_ref):
    first = GATHER_GROUP * pl.program_id(1)
    col = jnp.concatenate([_row_to_col(idx_ref[0, pl.ds(first + i, 1), :]) for i in range(GATHER_GROUP)],
                          axis=0)
    hit = lax.broadcasted_iota(I32, (GATHER_GROUP * CAP_LAT, DEC_SEQ), 1) == col
    xs = _mm(jnp.where(hit, 1.0, 0.0).astype(BF16), h_ref[0])
    x_ref[...] = xs.astype(BF16).reshape(GATHER_GROUP, CAP_LAT, D_MODEL)


def _gather_lat(idx, h2):
    return pl.pallas_call(
        _gather_lat_kernel,
        grid=(DEC_BATCH, N_EXPERTS // GATHER_GROUP),
        in_specs=[pl.BlockSpec((1, N_EXPERTS, CAP_LAT), lambda s, e: (s, 0, 0)),
                  pl.BlockSpec((1, DEC_SEQ, D_MODEL), lambda s, e: (s, 0, 0))],
        out_specs=pl.BlockSpec((GATHER_GROUP, CAP_LAT, D_MODEL), lambda s, e: (e, s, 0)),
        out_shape=jax.ShapeDtypeStruct((N_EXPERTS, SLOTS_LAT, D_MODEL), BF16),
        compiler_params=_params(("arbitrary", "arbitrary")),
        name="gather_lat",
    )(idx, h2)


def _ffn_kernel(xc_ref, xl_ref, gc_ref, gl_ref, wg_ref, wu_ref, wd_ref, yc_ref, yl_ref, h_scr, wl_scr):
    j = pl.program_id(1)
    tail = FF_HALF - FF_HALF // FF_TILE * FF_TILE

    def swiglu_cols(wg, wu, c0):
        wgb = wg.astype(BF16)
        wub = wu.astype(BF16)
        for i, x_ref in enumerate((xc_ref, xl_ref)):
            x = x_ref[0]
            a = _mm(x, wgb)
            u = _mm(x, wub)
            h = a * (1.0 / (1.0 + jnp.exp(-a))) * u
            h_scr[SLOTS_CTX * i:SLOTS_CTX * (i + 1), c0:c0 + FF_TILE] = h.astype(BF16)

    @pl.when(j == 0)
    def _():
        for c in range(FF_HALF // FF_TILE):
            lo = FF_TILE * c
            swiglu_cols(wg_ref[0, :, lo:lo + FF_TILE], wu_ref[0, :, lo:lo + FF_TILE], lo)
        wl_scr[0] = wg_ref[0, :, FF_HALF - tail:FF_HALF]
        wl_scr[1] = wu_ref[0, :, FF_HALF - tail:FF_HALF]

    @pl.when(j == 1)
    def _():
        head = FF_TILE - tail
        swiglu_cols(jnp.concatenate([wl_scr[0], wg_ref[0, :, 0:head]], axis=1),
                    jnp.concatenate([wl_scr[1], wu_ref[0, :, 0:head]], axis=1), FF_HALF - tail)
        for c in range(FF_HALF // FF_TILE):
            lo = head + FF_TILE * c
            swiglu_cols(wg_ref[0, :, lo:lo + FF_TILE], wu_ref[0, :, lo:lo + FF_TILE], FF_HALF + lo)

    @pl.when(j >= 2)
    def _():
        wdb = wd_ref[0].astype(BF16)
        e = pl.program_id(0)
        for i, (g_ref, y_ref) in enumerate(((gc_ref, yc_ref), (gl_ref, yl_ref))):
            nset = g_ref.shape[0] // N_EXPERTS
            gate = jnp.concatenate(
                [_row_to_col(g_ref[pl.ds(N_EXPERTS * s + e, 1), :][:, :SLOTS_CTX // nset]) for s in range(nset)],
                axis=0)
            y = _mm(h_scr[SLOTS_CTX * i:SLOTS_CTX * (i + 1), :], wdb)
            y_ref[0] = (y * gate).astype(BF16)


def _expert_ffn(xc, xl, gc, gl, w_gate, w_up, w_down):
    assert SLOTS_CTX == SLOTS_LAT and FF_HALF % LANES == 0 and D_MODEL % OUT_TILE == 0
    xs = lambda n: pl.BlockSpec((1, n, D_MODEL), lambda e, j: (e, 0, 0))
    gs = lambda g: pl.BlockSpec(g.shape, lambda e, j: (0, 0))
    last = N_EXPERTS - 1
    up = pl.BlockSpec((1, D_MODEL, FF_HALF),
                      lambda e, j: (jnp.where(j >= 2, jnp.minimum(e + 1, last), e), 0, jnp.where(j == 1, 1, 0)))
    down = pl.BlockSpec((1, D_FF, OUT_TILE), lambda e, j: (e, 0, jnp.maximum(j - 2, 0)))
    ys = lambda n: pl.BlockSpec((1, n, OUT_TILE), lambda e, j: (e, 0, jnp.maximum(j - 2, 0)))
    return pl.pallas_call(
        _ffn_kernel,
        grid=(N_EXPERTS, 2 + D_MODEL // OUT_TILE),
        in_specs=[xs(SLOTS_CTX), xs(SLOTS_LAT), gs(gc), gs(gl), up, up, down],
        out_specs=[ys(SLOTS_CTX), ys(SLOTS_LAT)],
        out_shape=[jax.ShapeDtypeStruct((N_EXPERTS, SLOTS_CTX, D_MODEL), BF16),
                   jax.ShapeDtypeStruct((N_EXPERTS, SLOTS_LAT, D_MODEL), BF16)],
        scratch_shapes=[pltpu.VMEM((SLOTS_CTX + SLOTS_LAT, D_FF), BF16),
                        pltpu.VMEM((2, D_MODEL, LANES), F32)],
        compiler_params=_params(("arbitrary", "arbitrary"), BIG_VMEM_LIMIT),
        name="expert_ffn",
    )(xc, xl, gc, gl, w_gate, w_up, w_down)


def _combine_ctx_kernel(y_ref, idx_ref, x1_ref, mod_ref, lng_ref, lnb_ref, o_ref):
    g2 = _mod_rows(mod_ref, 0)[5]
    nslot = N_EXPERTS * CAP_CTX
    for s in range(CTX_GROUP):
        hit = lax.broadcasted_iota(I32, (SEQ, nslot), 0) == idx_ref[s]
        y = y_ref[:, CAP_CTX * s:CAP_CTX * (s + 1), :].reshape(nslot, D_MODEL)
        f = _mm(jnp.where(hit, 1.0, 0.0).astype(BF16), y)
        o_ref[s] = _layer_norm(DEEPNORM_ALPHA * x1_ref[s] + g2 * f, lng_ref[...], lnb_ref[...])


def _combine_ctx(yc, idx_row, x1, mod, ln_g, ln_b):
    full = lambda shape: pl.BlockSpec(shape, lambda s: (0,) * len(shape))
    return pl.pallas_call(
        _combine_ctx_kernel,
        grid=(BATCH // CTX_GROUP,),
        in_specs=[pl.BlockSpec((N_EXPERTS, CTX_GROUP * CAP_CTX, D_MODEL), lambda s: (0, s, 0)),
                  pl.BlockSpec((CTX_GROUP, 1, N_EXPERTS * CAP_CTX), lambda s: (s, 0, 0)),
                  pl.BlockSpec((CTX_GROUP, SEQ, D_MODEL), lambda s: (s, 0, 0)),
                  full((8, 6 * D_MODEL)), full((1, D_MODEL)), full((1, D_MODEL))],
        out_specs=pl.BlockSpec((CTX_GROUP, SEQ, D_MODEL), lambda s: (s, 0, 0)),
        out_shape=jax.ShapeDtypeStruct((BATCH, SEQ, D_MODEL), F32),
        compiler_params=_params(("arbitrary",)),
        name="combine_ctx",
    )(yc, idx_row, x1, mod, ln_g, ln_b)


def _combine_lat_kernel(y_ref, idx_ref, x1_ref, mod_ref, lng_ref, lnb_ref, o_ref):
    s = pl.program_id(0)
    g2 = _mod_rows(mod_ref, 1 + s)[5]
    tok = lax.broadcasted_iota(I32, (ROW_TILE, CAP_LAT), 0) + ROW_TILE * pl.program_id(1)
    f = jnp.zeros((ROW_TILE, D_MODEL), F32)
    for e in range(N_EXPERTS):
        hit = tok == idx_ref[0, e:e + 1, :]
        f = f + _mm(jnp.where(hit, 1.0, 0.0).astype(BF16), y_ref[e])
    o_ref[0] = _layer_norm(DEEPNORM_ALPHA * x1_ref[0] + g2 * f, lng_ref[...], lnb_ref[...])


def _combine_lat(yl, idx, x1, mod, ln_g, ln_b):
    nt = DEC_SEQ // ROW_TILE
    full = lambda shape: pl.BlockSpec(shape, lambda s, t: (0,) * len(shape))
    return pl.pallas_call(
        _combine_lat_kernel,
        grid=(DEC_BATCH, nt),
        in_specs=[pl.BlockSpec((N_EXPERTS, CAP_LAT, D_MODEL), lambda s, t: (0, s, 0)),
                  pl.BlockSpec((1, N_EXPERTS, CAP_LAT), lambda s, t: (s, 0, 0)),
                  pl.BlockSpec((1, ROW_TILE, D_MODEL), lambda s, t: (s, t, 0)),
                  full((8, 6 * D_MODEL)), full((1, D_MODEL)), full((1, D_MODEL))],
        out_specs=pl.BlockSpec((1, ROW_TILE, D_MODEL), lambda s, t: (s, t, 0)),
        out_shape=jax.ShapeDtypeStruct((DEC_BATCH, DEC_SEQ, D_MODEL), F32),
        compiler_params=_params(("arbitrary", "arbitrary")),
        name="combine_lat",
    )(yl, idx, x1, mod, ln_g, ln_b)


def _rope_tables():
    t = np.arange(DEC_SEQ)
    row = (t // GRID_W).astype(np.float64)
    col = (t % GRID_W).astype(np.float64)
    half = HEAD_DIM // 2
    inv = ROPE_THETA ** (-np.arange(0, half, 2, dtype=np.float64) / half)
    ang_r = row[:, None] * inv[None, :]
    ang_c = col[:, None] * inv[None, :]
    ang = np.concatenate([ang_r, ang_r, ang_c, ang_c], axis=-1)
    cos = np.tile(np.cos(ang), (1, 2))
    sin = np.tile(np.sin(ang), (1, 2))
    first = (np.arange(LANES) % 32) < 16
    return jnp.asarray(cos, F32), jnp.asarray(np.where(first[None, :], -sin, sin), F32)


def kernel(x_prompt, x_sample, cache_na_k, cache_na_v, cache_diff_k, cache_diff_v, c, c_ctx, w_ada, b_ada, w_in, w_out, na_rel_bias, lambda_q1, lambda_k1, lambda_q2, lambda_k2, subln_g, ln1_g, ln1_b, ln2_g, ln2_b, w_router, w_gate, w_up, w_down):
    l = 0
    cvec = jnp.concatenate([c_ctx[None, :], c, jnp.zeros((8 - 1 - DEC_BATCH, D_MODEL), F32)], axis=0)
    mod = _modulation(cvec, w_ada[l], b_ada[l][None, :])
    lamv = jnp.stack([lambda_q1[l], lambda_k1[l], lambda_q2[l], lambda_k2[l]], axis=0)
    w_in_b = w_in[l].astype(BF16)
    w_out_b = w_out[l].astype(BF16)
    sub = subln_g[l][None, :]
    wrt = w_router[l].T

    tr = lambda a: jnp.swapaxes(a, -1, -2)

    x1c, h2c, affc, na_k_t, na_v_t, diff_k_t, new_diff_v = _context_block(
        x_prompt, mod, lamv, w_in_b, w_out_b, sub, ln1_g[l][None, :], ln1_b[l][None, :], wrt)

    cos, sin = _rope_tables()
    naq, nak, nav, dfq, dfk, dfv, tbl, stats = _latent_qkv(x_sample, mod, w_in_b, cos, sin,
                                                    na_rel_bias[l].reshape(-1))
    na_o = _latent_na(naq, nak, nav, tr(cache_na_k), tr(cache_na_v), tbl, stats)
    df_o = _latent_diff(dfq, dfk, dfv, tr(cache_diff_k), cache_diff_v, lamv, sub)
    x1l, h2l, affl = _latent_out(x_sample, na_o, df_o, mod, w_out_b,
                                 ln1_g[l][None, :], ln1_b[l][None, :], wrt)

    idxc, gatec, idxl, gatel = _route(affc.reshape(BATCH * N_EXPERTS, SEQ),
                                      affl.reshape(DEC_BATCH * N_EXPERTS, DEC_SEQ))
    idxc = idxc.reshape(BATCH, 1, N_EXPERTS * CAP_CTX)
    idxl = idxl.reshape(DEC_BATCH, N_EXPERTS, CAP_LAT)
    xc = _gather_ctx(idxc, h2c)
    xl = _gather_lat(idxl, h2l)
    yc, yl = _expert_ffn(xc, xl, gatec, gatel, w_gate[l], w_up[l], w_down[l])
    y_prompt = _combine_ctx(yc, idxc, x1c, mod, ln2_g[l][None, :], ln2_b[l][None, :])
    y_sample = _combine_lat(yl, idxl, x1l, mod, ln2_g[l][None, :], ln2_b[l][None, :])
    return (y_prompt, y_sample, tr(na_k_t), tr(na_v_t), tr(diff_k_t), new_diff_v)
```

```python
import functools
import math

import jax
import jax.numpy as jnp
import numpy as np
from jax import lax
from jax.experimental import pallas as pl
from jax.experimental.pallas import tpu as pltpu

F32 = jnp.float32
BF16 = jnp.bfloat16
I32 = jnp.int32

D_MODEL = 1024
BATCH = 16
SEQ = 256
DEC_BATCH = 2
DEC_SEQ = 2048
PAST_LEN = 256
GRID_W = 64
GRID_ROWS = DEC_SEQ // GRID_W
HEAD_DIM = 64
NA_HEADS = 8
DIFF_HEADS = 4
NA_WIDTH = NA_HEADS * HEAD_DIM
DIFF_WIDTH = DIFF_HEADS * 2 * HEAD_DIM
QKV_WIDTH = 3 * NA_WIDTH + 3 * DIFF_WIDTH
NA_WIN_H = 8
NA_WIN_W = 16
N_REL_H = 2 * NA_WIN_H - 1
N_REL_W = 2 * NA_WIN_W - 1
N_EXPERTS = 16
EC_CAPACITY_FACTOR = 2
D_FF = 2816
ROPE_THETA = 10000.0
NORM_EPS = 1e-5
NEG_INF = -1e30
DEPTH = 1
DEEPNORM_ALPHA = (2.0 * DEPTH) ** 0.25
LAMBDA_INIT = 0.8 - 0.6 * math.exp(-0.3 * 0)
SCALE = HEAD_DIM ** -0.5

OFF_NAQ = 0
OFF_NAK = NA_WIDTH
OFF_NAV = 2 * NA_WIDTH
OFF_DFQ = 3 * NA_WIDTH
OFF_DFK = 3 * NA_WIDTH + DIFF_WIDTH
OFF_DFV = 3 * NA_WIDTH + 2 * DIFF_WIDTH

CAP_CTX = EC_CAPACITY_FACTOR * SEQ // N_EXPERTS
CAP_LAT = EC_CAPACITY_FACTOR * DEC_SEQ // N_EXPERTS
SLOTS_CTX = BATCH * CAP_CTX
SLOTS_LAT = DEC_BATCH * CAP_LAT

ROW_TILE = 512
NA_QUAD = 4
NA_WIN_ROWS = 12
NA_GROUP = 8
FF_TILE = 256
FF_HALF = D_FF // 2
OUT_TILE = 512
LANES = 128
GATHER_GROUP = 2
CTX_GROUP = 4
CTX_REQ = 2
DIFF_ROWS = 1024
Q_ROWS = 256

BOUND_SLACK = 1.0 + 2.0 ** -10
ROUND_UP = 1.0 + 2.0 ** -7
SOFTMAX_FLOOR = 1e-25

SUBLANES = 8
VMEM_LIMIT = 48 * 1024 * 1024
BIG_VMEM_LIMIT = 58 * 1024 * 1024


def _params(sem, vmem=VMEM_LIMIT):
    return pltpu.CompilerParams(dimension_semantics=sem, vmem_limit_bytes=vmem)


def _mm(a, b):
    return jnp.dot(a, b, preferred_element_type=F32)


def _nt(a, b):
    return lax.dot_general(a, b, (((1,), (1,)), ((), ())), preferred_element_type=F32)


def _split(a):
    hi = a.astype(BF16)
    lo = (a - hi.astype(F32)).astype(BF16)
    return hi, lo


def _mm3(a, b):
    ah, al = _split(a)
    bh, bl = _split(b)
    return _mm(ah, bh) + _mm(al, bh) + _mm(ah, bl)


def _nt3(a, b):
    ah, al = _split(a)
    bh, bl = _split(b)
    return _nt(ah, bh) + _nt(al, bh) + _nt(ah, bl)


def _layer_norm(y, g, b):
    mu = jnp.mean(y, axis=-1, keepdims=True)
    d = y - mu
    var = jnp.mean(d * d, axis=-1, keepdims=True)
    return d * lax.rsqrt(var + NORM_EPS) * g + b


def _lam(l_ref):
    l = l_ref[...]
    a = jnp.sum(l[0:1] * l[1:2], axis=-1, keepdims=True)
    b = jnp.sum(l[2:3] * l[3:4], axis=-1, keepdims=True)
    return jnp.exp(a) - jnp.exp(b) + LAMBDA_INIT


def _max_col_norm(kt):
    k = kt.astype(F32)
    n2 = jnp.max(jnp.sum(k * k, axis=0, keepdims=True), axis=-1, keepdims=True)
    return jnp.sqrt(n2) * BOUND_SLACK


def _keys_with_ones(k, axis):
    first = lax.broadcasted_iota(I32, k.shape, axis) == 0
    return jnp.concatenate([k, jnp.where(first, 1.0, 0.0).astype(BF16)], axis=axis)


def _mod_rows(mod_ref, row):
    return [mod_ref[pl.ds(row, 1), i * D_MODEL:(i + 1) * D_MODEL] for i in range(6)]


def _sub_norm(o, sub):
    ms = jnp.mean(o * o, axis=-1, keepdims=True)
    return o * lax.rsqrt(ms + NORM_EPS) * sub * (1.0 - LAMBDA_INIT)


def _router_aff_t(h2, wrt):
    lg = _nt3(wrt, h2)
    m = jnp.max(lg, axis=0, keepdims=True)
    e = jnp.exp(lg - m)
    return e / jnp.sum(e, axis=0, keepdims=True)


def _mod_kernel(c_ref, wa_ref, wb_ref, b_ref, o_ref):
    c = c_ref[...]
    s = c * (1.0 / (1.0 + jnp.exp(-c)))
    half = D_MODEL // 2
    o_ref[...] = _mm3(s[:, :half], wa_ref[...]) + _mm3(s[:, half:], wb_ref[...]) + b_ref[...]


def _modulation(cvec, w_ada, b_ada):
    ncol = 6 * D_MODEL
    half = D_MODEL // 2
    return pl.pallas_call(
        _mod_kernel,
        grid=(6,),
        in_specs=[pl.BlockSpec((SUBLANES, D_MODEL), lambda j: (0, 0)),
                  pl.BlockSpec((half, D_MODEL), lambda j: (0, j)),
                  pl.BlockSpec((half, D_MODEL), lambda j: (1, j)),
                  pl.BlockSpec((1, D_MODEL), lambda j: (0, j))],
        out_specs=pl.BlockSpec((SUBLANES, D_MODEL), lambda j: (0, j)),
        out_shape=jax.ShapeDtypeStruct((SUBLANES, ncol), F32),
        compiler_params=_params(("arbitrary",)),
        name="modulation",
    )(cvec, w_ada, w_ada, b_ada)


def _ctx_kernel(x_ref, mod_ref, lam_ref, win_ref, wout_ref, sub_ref, lng_ref, lnb_ref, wrt_ref,
                x1_ref, h2_ref, aff_ref, nak_ref, nav_ref, dfk_ref, dfv_ref,
                qkv_scr, o_scr, s_scr, e_scr):
    rows = CTX_REQ * SEQ
    x = x_ref[...].reshape(rows, D_MODEL)
    sh1, sc1, g1, sh2, sc2, g2 = _mod_rows(mod_ref, 0)
    h = (x * (1.0 + sc1) + sh1).astype(BF16)
    qkv_scr[...] = _mm(h, win_ref[...])

    kt_na = qkv_scr[:, OFF_NAK:OFF_NAK + NA_WIDTH].T
    kt_df = qkv_scr[:, OFF_DFK:OFF_DFK + DIFF_WIDTH].T
    vt_na = qkv_scr[:, OFF_NAV:OFF_NAV + NA_WIDTH].T
    for r in range(CTX_REQ):
        tok = slice(SEQ * r, SEQ * (r + 1))
        nak_ref[r, 0] = kt_na[:, tok].reshape(NA_HEADS, HEAD_DIM, SEQ)
        nav_ref[r, 0] = vt_na[:, tok].reshape(NA_HEADS, HEAD_DIM, SEQ)
        dfk_ref[r, 0] = kt_df[:, tok].reshape(DIFF_HEADS, 2, HEAD_DIM, SEQ)
        for hh in range(DIFF_HEADS):
            c0 = OFF_DFV + 2 * HEAD_DIM * hh
            dfv_ref[r, 0, hh] = qkv_scr[tok, c0:c0 + 2 * HEAD_DIM]

    kt = jnp.concatenate([kt_na, kt_df], axis=0).astype(BF16)
    q_cols = [OFF_NAQ + HEAD_DIM * i for i in range(NA_HEADS)] + [
        OFF_DFQ + HEAD_DIM * i for i in range(2 * DIFF_HEADS)]
    nmap = len(q_cols)
    for r in range(CTX_REQ):
        tok = slice(SEQ * r, SEQ * (r + 1))
        for i, c in enumerate(q_cols):
            q = (qkv_scr[tok, c:c + HEAD_DIM] * SCALE).astype(BF16)
            s_scr[nmap * r + i] = _mm(q, kt[HEAD_DIM * i:HEAD_DIM * (i + 1), tok])
    s = s_scr[...]
    e = jnp.exp(s - jnp.max(s, axis=-1, keepdims=True))
    rl = 1.0 / jnp.sum(e, axis=-1, keepdims=True)
    e_scr[...] = e.astype(BF16)

    lam = _lam(lam_ref)
    sub = sub_ref[...]
    for r in range(CTX_REQ):
        tok = slice(SEQ * r, SEQ * (r + 1))
        for hh in range(NA_HEADS):
            i = nmap * r + hh
            v = qkv_scr[tok, OFF_NAV + HEAD_DIM * hh:OFF_NAV + HEAD_DIM * (hh + 1)].astype(BF16)
            o_scr[tok, HEAD_DIM * hh:HEAD_DIM * (hh + 1)] = (_mm(e_scr[i], v) * rl[i]).astype(BF16)
        for hh in range(DIFF_HEADS):
            i1 = nmap * r + NA_HEADS + 2 * hh
            c0 = OFF_DFV + 2 * HEAD_DIM * hh
            v = qkv_scr[tok, c0:c0 + 2 * HEAD_DIM].astype(BF16)
            o = _mm(e_scr[i1], v) * rl[i1] - _mm(e_scr[i1 + 1], v) * (lam * rl[i1 + 1])
            c1 = NA_WIDTH + 2 * HEAD_DIM * hh
            o_scr[tok, c1:c1 + 2 * HEAD_DIM] = _sub_norm(o, sub).astype(BF16)

    a = _mm(o_scr[...], wout_ref[...])
    x1 = _layer_norm(DEEPNORM_ALPHA * x + g1 * a, lng_ref[...], lnb_ref[...])
    x1_ref[...] = x1.reshape(CTX_REQ, SEQ, D_MODEL)
    h2 = x1 * (1.0 + sc2) + sh2
    h2_ref[...] = h2.astype(BF16).reshape(CTX_REQ, SEQ, D_MODEL)
    aff = _router_aff_t(h2, wrt_ref[...])
    for r in range(CTX_REQ):
        aff_ref[r] = aff[:, SEQ * r:SEQ * (r + 1)]


def _context_block(x, mod, lamv, w_in, w_out, sub, ln_g, ln_b, wrt):
    full = lambda shape: pl.BlockSpec(shape, lambda b: (0,) * len(shape))
    once = lambda shape: pl.BlockSpec(shape, lambda b: (0,) * len(shape), pipeline_mode=pl.Buffered(1))
    req = lambda *tail: pl.BlockSpec((CTX_REQ,) + tail, lambda b: (b,) + (0,) * len(tail))
    nmaps = CTX_REQ * (NA_HEADS + 2 * DIFF_HEADS)
    return pl.pallas_call(
        _ctx_kernel,
        grid=(BATCH // CTX_REQ,),
        in_specs=[req(SEQ, D_MODEL),
                  full((SUBLANES, 6 * D_MODEL)), full((4, HEAD_DIM)),
                  once((D_MODEL, QKV_WIDTH)), once((D_MODEL, D_MODEL)),
                  full((1, 2 * HEAD_DIM)), full((1, D_MODEL)), full((1, D_MODEL)),
                  full((N_EXPERTS, D_MODEL))],
        out_specs=[req(SEQ, D_MODEL), req(SEQ, D_MODEL), req(N_EXPERTS, SEQ),
                   req(1, NA_HEADS, HEAD_DIM, SEQ), req(1, NA_HEADS, HEAD_DIM, SEQ),
                   req(1, DIFF_HEADS, 2, HEAD_DIM, SEQ), req(1, DIFF_HEADS, SEQ, 2 * HEAD_DIM)],
        out_shape=[jax.ShapeDtypeStruct((BATCH, SEQ, D_MODEL), F32),
                   jax.ShapeDtypeStruct((BATCH, SEQ, D_MODEL), BF16),
                   jax.ShapeDtypeStruct((BATCH, N_EXPERTS, SEQ), F32),
                   jax.ShapeDtypeStruct((BATCH, 1, NA_HEADS, HEAD_DIM, SEQ), F32),
                   jax.ShapeDtypeStruct((BATCH, 1, NA_HEADS, HEAD_DIM, SEQ), F32),
                   jax.ShapeDtypeStruct((BATCH, 1, DIFF_HEADS, 2, HEAD_DIM, SEQ), F32),
                   jax.ShapeDtypeStruct((BATCH, 1, DIFF_HEADS, SEQ, 2 * HEAD_DIM), F32)],
        scratch_shapes=[pltpu.VMEM((CTX_REQ * SEQ, QKV_WIDTH), F32), pltpu.VMEM((CTX_REQ * SEQ, D_MODEL), BF16),
                        pltpu.VMEM((nmaps, SEQ, SEQ), F32), pltpu.VMEM((nmaps, SEQ, SEQ), BF16)],
        compiler_params=_params(("arbitrary",)),
        name="context_block",
    )(x, mod, lamv, w_in, w_out, sub, ln_g, ln_b, wrt)


def _lat_qkv_kernel(x_ref, mod_ref, win_ref, cos_ref, sin_ref, rpb_ref,
                    naq_ref, nak_ref, nav_ref, dfq_ref, dfk_ref, dfv_ref, tbl_ref, stat_ref, qkv_scr):
    b = pl.program_id(0)
    bias_top = _write_bias_tables(rpb_ref, b * pl.num_programs(1) + pl.program_id(1), tbl_ref)
    x = x_ref[0]
    sh1, sc1 = _mod_rows(mod_ref, 1 + b)[:2]
    h = (x * (1.0 + sc1) + sh1).astype(BF16)
    qkv_scr[...] = _mm(h, win_ref[...])

    kb = qkv_scr[:, OFF_NAK:OFF_NAK + NA_WIDTH].astype(BF16).astype(F32)
    col = lax.broadcasted_iota(I32, (NA_WIDTH, LANES), 0) // HEAD_DIM
    head_of = jnp.where(col == lax.broadcasted_iota(I32, (NA_WIDTH, LANES), 1), 1.0, 0.0).astype(BF16)
    kn2 = jnp.max(_mm((kb * kb).astype(BF16), head_of), axis=0, keepdims=True) * ROUND_UP
    stat_ref[0] = jnp.concatenate([kn2, jnp.broadcast_to(bias_top, (1, LANES)),
                                   jnp.zeros((SUBLANES - 2, LANES), F32)], axis=0)

    qb = (qkv_scr[:, OFF_NAQ:OFF_NAQ + NA_WIDTH] * SCALE).astype(BF16)
    qf = qb.astype(F32)
    qn = jnp.sqrt(_mm((qf * qf).astype(BF16), head_of) * ROUND_UP) * ROUND_UP
    lane64 = lax.broadcasted_iota(I32, (ROW_TILE, HEAD_DIM), 1)
    key_ext = jnp.where(lane64 < 2, 1.0, 0.0).astype(BF16)
    for hh in range(NA_HEADS):
        lo, hi = HEAD_DIM * hh, HEAD_DIM * (hh + 1)
        ext = jnp.where(lane64 == 0, -qn[:, hh:hh + 1], jnp.where(lane64 == 1, -1.0, 0.0)).astype(BF16)
        naq_ref[0, hh] = jnp.concatenate([qb[:, lo:hi], ext], axis=1)
        nak_ref[0, hh] = jnp.concatenate([qkv_scr[:, OFF_NAK + lo:OFF_NAK + hi].astype(BF16), key_ext], axis=1)
        nav_ref[0, hh] = qkv_scr[:, OFF_NAV + lo:OFF_NAV + hi].astype(BF16)

    cos = cos_ref[...]
    sin = sin_ref[...]
    lane = lax.broadcasted_iota(I32, (ROW_TILE, LANES), 1)
    first = (lane & 31) < 16

    def rope(t):
        rot = jnp.where(first, pltpu.roll(t, LANES - 16, 1), pltpu.roll(t, 16, 1))
        return t * cos + rot * sin

    map_of = jnp.where(lax.broadcasted_iota(I32, (LANES, LANES), 0) // HEAD_DIM
                       == lax.broadcasted_iota(I32, (LANES, LANES), 1), 1.0, 0.0).astype(BF16)
    for hh in range(DIFF_HEADS):
        lo = 2 * HEAD_DIM * hh
        q = (rope(qkv_scr[:, OFF_DFQ + lo:OFF_DFQ + lo + LANES]) * SCALE).astype(BF16)
        k = rope(qkv_scr[:, OFF_DFK + lo:OFF_DFK + lo + LANES])
        qf = q.astype(F32)
        qn = jnp.sqrt(_mm((qf * qf).astype(BF16), map_of) * ROUND_UP) * ROUND_UP
        for m in range(2):
            ext = jnp.where(lane64 == 0, -qn[:, m:m + 1], 0.0).astype(BF16)
            dfq_ref[0, hh, m] = jnp.concatenate([q[:, HEAD_DIM * m:HEAD_DIM * (m + 1)], ext], axis=1)
        dfk_ref[0, hh] = k.T.astype(BF16).reshape(2, HEAD_DIM, ROW_TILE)
        dfv_ref[0, hh] = qkv_scr[:, OFF_DFV + lo:OFF_DFV + lo + LANES].astype(BF16)


def _latent_qkv(x, mod, w_in, cos, sin, rpb_flat):
    nt = DEC_SEQ // ROW_TILE
    assert DEC_BATCH * nt == NA_HEADS
    tbl_shape = (NA_HEADS, 3, NA_QUAD * GRID_W, NA_WIN_ROWS * GRID_W)
    full = lambda shape: pl.BlockSpec(shape, lambda b, t: (0,) * len(shape))
    hs = lambda width: pl.BlockSpec((1, NA_HEADS, ROW_TILE, width), lambda b, t: (b, 0, t, 0))
    na_shape = lambda width: jax.ShapeDtypeStruct((DEC_BATCH, NA_HEADS, DEC_SEQ, width), BF16)
    return pl.pallas_call(
        _lat_qkv_kernel,
        grid=(DEC_BATCH, nt),
        in_specs=[pl.BlockSpec((1, ROW_TILE, D_MODEL), lambda b, t: (b, t, 0)),
                  full((SUBLANES, 6 * D_MODEL)), full((D_MODEL, QKV_WIDTH)),
                  pl.BlockSpec((ROW_TILE, LANES), lambda b, t: (t, 0)),
                  pl.BlockSpec((ROW_TILE, LANES), lambda b, t: (t, 0)),
                  pl.BlockSpec(memory_space=pltpu.SMEM)],
        out_specs=[hs(2 * HEAD_DIM), hs(2 * HEAD_DIM), hs(HEAD_DIM),
                   pl.BlockSpec((1, DIFF_HEADS, 2, ROW_TILE, 2 * HEAD_DIM), lambda b, t: (b, 0, 0, t, 0)),
                   pl.BlockSpec((1, DIFF_HEADS, 2, HEAD_DIM, ROW_TILE), lambda b, t: (b, 0, 0, 0, t)),
                   pl.BlockSpec((1, DIFF_HEADS, ROW_TILE, 2 * HEAD_DIM), lambda b, t: (b, 0, t, 0)),
                   pl.BlockSpec((1,) + tbl_shape[1:], lambda b, t: (b * nt + t, 0, 0, 0)),
                   pl.BlockSpec((1, SUBLANES, LANES), lambda b, t: (b * nt + t, 0, 0))],
        out_shape=[na_shape(2 * HEAD_DIM), na_shape(2 * HEAD_DIM), na_shape(HEAD_DIM)]
        + [jax.ShapeDtypeStruct((DEC_BATCH, DIFF_HEADS, 2, DEC_SEQ, 2 * HEAD_DIM), BF16),
           jax.ShapeDtypeStruct((DEC_BATCH, DIFF_HEADS, 2, HEAD_DIM, DEC_SEQ), BF16),
           jax.ShapeDtypeStruct((DEC_BATCH, DIFF_HEADS, DEC_SEQ, 2 * HEAD_DIM), BF16),
           jax.ShapeDtypeStruct(tbl_shape, F32),
           jax.ShapeDtypeStruct((NA_HEADS, SUBLANES, LANES), F32)],
        scratch_shapes=[pltpu.VMEM((ROW_TILE, QKV_WIDTH), F32)],
        compiler_params=_params(("arbitrary", "arbitrary")),
        name="latent_qkv",
    )(x, mod, w_in, cos, sin, rpb_flat)


def _na_row_offset(kind, qr, kr):
    if kind == 0:
        return kr - qr + 7 if kr < NA_WIN_H else None
    if kind == 1:
        return kr - qr + 3 if qr <= kr < qr + NA_WIN_H else None
    return kr - qr - 1 if NA_WIN_ROWS - NA_WIN_H <= kr else None


def _write_bias_tables(rpb_ref, h, o_ref):
    base = h * (N_REL_H * N_REL_W)
    wq = lax.broadcasted_iota(I32, (GRID_W, LANES), 0)
    c = lax.broadcasted_iota(I32, (GRID_W, LANES), 1)
    wk = c & (GRID_W - 1)
    right = c >= GRID_W
    c0 = jnp.clip(wq - NA_WIN_W // 2, 0, GRID_W - NA_WIN_W)
    in_win = (wk >= c0) & (wk < c0 + NA_WIN_W)
    cache = {}
    strips = {}

    def strip(r):
        if r not in strips:
            c8 = lax.broadcasted_iota(I32, (SUBLANES, LANES), 1)
            d = jnp.where(c8 < GRID_W, c8, c8 - LANES)
            co_of = jnp.clip(d, -(NA_WIN_W - 1), NA_WIN_W - 1) + (NA_WIN_W - 1)
            acc = jnp.zeros((SUBLANES, LANES), F32)
            for co in range(N_REL_W):
                acc = jnp.where(co_of == co, rpb_ref[base + r * N_REL_W + co], acc)
            strips[r] = jnp.concatenate([acc] * (GRID_W // SUBLANES), axis=0)
        return strips[r]

    def tile(rl, rr):
        if (rl, rr) in cache:
            return cache[(rl, rr)]
        if rl is None and rr is None:
            t = jnp.full((GRID_W, LANES), NEG_INF, F32)
        else:
            zero = jnp.zeros((GRID_W, LANES), F32)
            tl = pltpu.roll(strip(rl), 0, 1, stride=1, stride_axis=0) if rl is not None else zero
            tr = pltpu.roll(strip(rr), GRID_W, 1, stride=1, stride_axis=0) if rr is not None else zero
            acc = jnp.where(right, tr, tl)
            ok = in_win
            if rl is None:
                ok = ok & right
            if rr is None:
                ok = ok & jnp.logical_not(right)
            t = jnp.where(ok, acc, NEG_INF)
        cache[(rl, rr)] = t
        return t

    for kind in range(3):
        for qr in range(NA_QUAD):
            row = [tile(_na_row_offset(kind, qr, 2 * p), _na_row_offset(kind, qr, 2 * p + 1))
                   for p in range(NA_WIN_ROWS // 2)]
            o_ref[0, kind, GRID_W * qr:GRID_W * (qr + 1), :] = jnp.concatenate(row, axis=1)
    top = functools.reduce(jnp.maximum, cache.values())
    return jnp.max(jnp.max(top, axis=-1, keepdims=True), axis=0, keepdims=True)


def _na_kernel(q_ref, k_ref, v_ref, ck_ref, cv_ref, tbl_ref, stat_ref, o_ref):
    b = pl.program_id(0)
    g = pl.program_id(2)
    u0 = jnp.clip(NA_QUAD * g - NA_QUAD, 0, GRID_ROWS - NA_WIN_ROWS)
    start = pl.multiple_of(u0 * GRID_W, GRID_W)
    nwin = NA_WIN_ROWS * GRID_W
    tiles = stat_ref.shape[0] // DEC_BATCH
    kn2 = functools.reduce(jnp.maximum, [stat_ref[b * tiles + t, 0:1, :] for t in range(tiles)])

    lane = lax.broadcasted_iota(I32, (1, 2 * HEAD_DIM), 1)
    sub = lax.broadcasted_iota(I32, (HEAD_DIM, PAST_LEN), 0)
    ones_rows = jnp.where(sub < 2, 1.0, 0.0).astype(BF16)

    def attend(exact_max):
        lmin = None
        for j in range(NA_GROUP):
            q = q_ref[0, j]
            kw = k_ref[0, j, pl.ds(start, nwin), :]
            vw = v_ref[0, j, pl.ds(start, nwin), :]
            kct = ck_ref[0, 0, j].astype(BF16)
            vct = cv_ref[0, 0, j].astype(BF16)
            kct1 = jnp.concatenate([kct, ones_rows], axis=0)
            if exact_max:
                q0 = (q.astype(F32) * jnp.where(lane < HEAD_DIM, 1.0, 0.0)).astype(BF16)
                sl = _nt(q0, kw) + tbl_ref[j, 0]
                sc = _mm(q[:, :HEAD_DIM], kct)
                m = jnp.maximum(jnp.max(sl, axis=-1, keepdims=True), jnp.max(sc, axis=-1, keepdims=True))
                el = jnp.exp(sl - m)
                ec = jnp.exp(sc - m)
            else:
                knorm = jnp.maximum(jnp.sqrt(kn2[:, j:j + 1]) * BOUND_SLACK, _max_col_norm(kct)) * ROUND_UP
                btop = jnp.maximum(stat_ref[j, 1:2, 0:1], 0.0) * ROUND_UP
                scale = jnp.where(lane < HEAD_DIM, 1.0, jnp.where(lane == HEAD_DIM, knorm, btop))
                qs = (q.astype(F32) * scale).astype(BF16)
                el = jnp.exp(_nt(qs, kw) + tbl_ref[j, 0])
                ec = jnp.exp(_mm(qs, kct1))
            l = jnp.sum(el, axis=-1, keepdims=True) + jnp.sum(ec, axis=-1, keepdims=True)
            o = (_mm(el.astype(BF16), vw) + _nt(ec.astype(BF16), vct)) / l
            o_ref[0, :, HEAD_DIM * j:HEAD_DIM * (j + 1)] = o.astype(BF16)
            lmin = jnp.min(l) if lmin is None else jnp.minimum(lmin, jnp.min(l))
        return lmin

    lmin = attend(exact_max=False)

    @pl.when(jnp.logical_not(lmin >= SOFTMAX_FLOOR))
    def _():
        attend(exact_max=True)


def _latent_na(naq, nak, nav, cache_k, cache_v, tbl, stats):
    assert NA_GROUP == NA_HEADS
    nq = GRID_ROWS // NA_QUAD
    rows = NA_QUAD * GRID_W
    kind = lambda g: jnp.minimum(g, 1) + g // (nq - 1)
    return pl.pallas_call(
        _na_kernel,
        grid=(DEC_BATCH, NA_HEADS // NA_GROUP, nq),
        in_specs=[pl.BlockSpec((1, NA_GROUP, rows, 2 * HEAD_DIM), lambda b, p, g: (b, p, g, 0)),
                  pl.BlockSpec((1, NA_GROUP, DEC_SEQ, 2 * HEAD_DIM), lambda b, p, g: (b, p, 0, 0)),
                  pl.BlockSpec((1, NA_GROUP, DEC_SEQ, HEAD_DIM), lambda b, p, g: (b, p, 0, 0)),
                  pl.BlockSpec((1, 1, NA_GROUP, HEAD_DIM, PAST_LEN), lambda b, p, g: (b, 0, p, 0, 0)),
                  pl.BlockSpec((1, 1, NA_GROUP, HEAD_DIM, PAST_LEN), lambda b, p, g: (b, 0, p, 0, 0)),
                  pl.BlockSpec((NA_GROUP, 1, rows, NA_WIN_ROWS * GRID_W), lambda b, p, g: (p, kind(g), 0, 0)),
                  pl.BlockSpec(stats.shape, lambda b, p, g: (0, 0, 0))],
        out_specs=pl.BlockSpec((1, rows, NA_GROUP * HEAD_DIM), lambda b, p, g: (b, g, p)),
        out_shape=jax.ShapeDtypeStruct((DEC_BATCH, DEC_SEQ, NA_WIDTH), BF16),
        compiler_params=_params(("arbitrary",) * 3),
        name="latent_na",
    )(naq, nak, nav, cache_k, cache_v, tbl, stats)


def _diff_kernel(q_ref, kt_ref, v_ref, ck_ref, cv_ref, lam_ref, sub_ref, o_ref):
    lam = _lam(lam_ref)
    vc = cv_ref[0, 0, 0].astype(BF16)
    vl = v_ref[0, 0]
    kcs = [ck_ref[0, 0, 0, m].astype(BF16) for m in range(2)]
    kls = [kt_ref[0, 0, m] for m in range(2)]
    knorm = [jnp.maximum(_max_col_norm(kcs[m]), _max_col_norm(kls[m])) for m in range(2)]
    kcs1 = [_keys_with_ones(k, 0) for k in kcs]
    kls1 = [_keys_with_ones(k, 0) for k in kls]
    lane = lax.broadcasted_iota(I32, (1, 2 * HEAD_DIM), 1)
    scale = [jnp.where(lane < HEAD_DIM, 1.0, jnp.where(lane == HEAD_DIM, kn * ROUND_UP, 0.0))
             for kn in knorm]

    def attend(exact_max):
        lmin = None
        for r in range(DIFF_ROWS // Q_ROWS):
            rows = slice(Q_ROWS * r, Q_ROWS * (r + 1))
            parts = []
            for m in range(2):
                q = q_ref[0, 0, m, rows, :]
                if exact_max:
                    sc = _mm(q[:, :HEAD_DIM], kcs[m])
                    sl = _mm(q[:, :HEAD_DIM], kls[m])
                    mx = jnp.maximum(jnp.max(sc, axis=-1, keepdims=True), jnp.max(sl, axis=-1, keepdims=True))
                    ec = jnp.exp(sc - mx)
                    el = jnp.exp(sl - mx)
                else:
                    qs = (q.astype(F32) * scale[m]).astype(BF16)
                    ec = jnp.exp(_mm(qs, kcs1[m]))
                    el = jnp.exp(_mm(qs, kls1[m]))
                l = jnp.sum(ec, axis=-1, keepdims=True) + jnp.sum(el, axis=-1, keepdims=True)
                parts.append((_mm(ec.astype(BF16), vc) + _mm(el.astype(BF16), vl), l))
                lmin = jnp.min(l) if lmin is None else jnp.minimum(lmin, jnp.min(l))
            (a1, l1), (a2, l2) = parts
            o = a1 * (1.0 / l1) - a2 * (lam / l2)
            o_ref[0, rows, :] = _sub_norm(o, sub_ref[...]).astype(BF16)
        return lmin

    lmin = attend(exact_max=False)

    @pl.when(jnp.logical_not(lmin >= SOFTMAX_FLOOR))
    def _():
        attend(exact_max=True)


def _latent_diff(dfq, dfk, dfv, cache_k, cache_v, lamv, sub):
    nt = DEC_SEQ // DIFF_ROWS
    full = lambda shape: pl.BlockSpec(shape, lambda b, h, t: (0,) * len(shape))
    return pl.pallas_call(
        _diff_kernel,
        grid=(DEC_BATCH, DIFF_HEADS, nt),
        in_specs=[pl.BlockSpec((1, 1, 2, DIFF_ROWS, 2 * HEAD_DIM), lambda b, h, t: (b, h, 0, t, 0)),
                  pl.BlockSpec((1, 1, 2, HEAD_DIM, DEC_SEQ), lambda b, h, t: (b, h, 0, 0, 0)),
                  pl.BlockSpec((1, 1, DEC_SEQ, 2 * HEAD_DIM), lambda b, h, t: (b, h, 0, 0)),
                  pl.BlockSpec((1, 1, 1, 2, HEAD_DIM, PAST_LEN), lambda b, h, t: (b, 0, h, 0, 0, 0)),
                  pl.BlockSpec((1, 1, 1, PAST_LEN, 2 * HEAD_DIM), lambda b, h, t: (b, 0, h, 0, 0)),
                  full((4, HEAD_DIM)), full((1, 2 * HEAD_DIM))],
        out_specs=pl.BlockSpec((1, DIFF_ROWS, 2 * HEAD_DIM), lambda b, h, t: (b, t, h)),
        out_shape=jax.ShapeDtypeStruct((DEC_BATCH, DEC_SEQ, DIFF_WIDTH), BF16),
        compiler_params=_params(("arbitrary",) * 3),
        name="latent_diff",
    )(dfq, dfk, dfv, cache_k, cache_v, lamv, sub)


def _lat_out_kernel(x_ref, na_ref, df_ref, mod_ref, wout_ref, lng_ref, lnb_ref, wrt_ref,
                    x1_ref, h2_ref, aff_ref):
    b = pl.program_id(0)
    x = x_ref[0]
    _, _, g1, sh2, sc2, _ = _mod_rows(mod_ref, 1 + b)
    a = _mm(na_ref[0], wout_ref[0:NA_WIDTH, :]) + _mm(df_ref[0], wout_ref[NA_WIDTH:, :])
    x1 = _layer_norm(DEEPNORM_ALPHA * x + g1 * a, lng_ref[...], lnb_ref[...])
    x1_ref[0] = x1
    h2 = x1 * (1.0 + sc2) + sh2
    h2_ref[0] = h2.astype(BF16)
    aff_ref[0] = _router_aff_t(h2, wrt_ref[...])


def _latent_out(x, na_o, df_o, mod, w_out, ln_g, ln_b, wrt):
    nt = DEC_SEQ // ROW_TILE
    full = lambda shape: pl.BlockSpec(shape, lambda b, t: (0,) * len(shape))
    tok = lambda w: pl.BlockSpec((1, ROW_TILE, w), lambda b, t: (b, t, 0))
    return pl.pallas_call(
        _lat_out_kernel,
        grid=(DEC_BATCH, nt),
        in_specs=[tok(D_MODEL), tok(NA_WIDTH), tok(DIFF_WIDTH), full((SUBLANES, 6 * D_MODEL)),
                  full((D_MODEL, D_MODEL)), full((1, D_MODEL)), full((1, D_MODEL)),
                  full((N_EXPERTS, D_MODEL))],
        out_specs=[tok(D_MODEL), tok(D_MODEL),
                   pl.BlockSpec((1, N_EXPERTS, ROW_TILE), lambda b, t: (b, 0, t))],
        out_shape=[jax.ShapeDtypeStruct((DEC_BATCH, DEC_SEQ, D_MODEL), F32),
                   jax.ShapeDtypeStruct((DEC_BATCH, DEC_SEQ, D_MODEL), BF16),
                   jax.ShapeDtypeStruct((DEC_BATCH, N_EXPERTS, DEC_SEQ), F32)],
        compiler_params=_params(("arbitrary", "arbitrary")),
        name="latent_out",
    )(x, na_o, df_o, mod, w_out, ln_g, ln_b, wrt)


def _ranks_first(va, ia, vb, ib):
    return (va > vb) | ((va == vb) & (ia < ib))


def _sort_rows_desc(aff):
    rows, n = aff.shape
    nb = n // LANES
    lane = lax.broadcasted_iota(I32, (rows, LANES), 1)
    v = [aff[:, LANES * b:LANES * (b + 1)] for b in range(nb)]
    ix = [lane + LANES * b for b in range(nb)]
    k = 2
    while k <= n:
        j = k // 2
        while j >= 1:
            if j >= LANES:
                sb = j // LANES
                for lo in range(nb):
                    if lo & sb:
                        continue
                    hi = lo | sb
                    f = _ranks_first(v[lo], ix[lo], v[hi], ix[hi])
                    if k < n and (LANES * lo) & k:
                        f = jnp.logical_not(f)
                    v[lo], v[hi] = jnp.where(f, v[lo], v[hi]), jnp.where(f, v[hi], v[lo])
                    ix[lo], ix[hi] = jnp.where(f, ix[lo], ix[hi]), jnp.where(f, ix[hi], ix[lo])
            else:
                upper = (lane & j) != 0
                if k < LANES:
                    flip_lanes = jnp.logical_xor(upper, (lane & k) != 0)
                for b in range(nb):
                    pv = jnp.where(upper, pltpu.roll(v[b], j, 1), pltpu.roll(v[b], LANES - j, 1))
                    pi = jnp.where(upper, pltpu.roll(ix[b], j, 1), pltpu.roll(ix[b], LANES - j, 1))
                    f = _ranks_first(v[b], ix[b], pv, pi)
                    if k < LANES:
                        flip = flip_lanes
                    elif k < n and (LANES * b) & k:
                        flip = jnp.logical_not(upper)
                    else:
                        flip = upper
                    keep = jnp.logical_xor(f, flip)
                    v[b] = jnp.where(keep, v[b], pv)
                    ix[b] = jnp.where(keep, ix[b], pi)
            j //= 2
        k *= 2
    return v, ix


def _route_kernel(affc_ref, affl_ref, idxc_ref, gatec_ref, idxl_ref, gatel_ref):
    for aff_ref, idx_ref, gate_ref, cap in ((affc_ref, idxc_ref, gatec_ref, CAP_CTX),
                                            (affl_ref, idxl_ref, gatel_ref, CAP_LAT)):
        v, ix = _sort_rows_desc(aff_ref[...])
        if cap < LANES:
            idx_ref[...] = ix[0][:, :cap]
            gate_ref[...] = v[0]
        else:
            idx_ref[...] = jnp.concatenate(ix[:cap // LANES], axis=1)
            gate_ref[...] = jnp.concatenate(v[:cap // LANES], axis=1)


def _route(affc_t, affl_t):
    out = lambda a, cap: [jax.ShapeDtypeStruct((a.shape[0], cap), I32),
                          jax.ShapeDtypeStruct((a.shape[0], max(cap, LANES)), F32)]
    return pl.pallas_call(
        _route_kernel,
        out_shape=out(affc_t, CAP_CTX) + out(affl_t, CAP_LAT),
        compiler_params=_params(None),
        name="route",
    )(affc_t, affl_t)


def _row_to_col(row):
    n = row.shape[1]
    eye = lax.broadcasted_iota(I32, (n, n), 0) == lax.broadcasted_iota(I32, (n, n), 1)
    return jnp.sum(jnp.where(eye, row, jnp.zeros_like(row)), axis=1, keepdims=True)


def _gather_ctx_kernel(idx_ref, h_ref, x_ref):
    nslot = N_EXPERTS * CAP_CTX
    for s in range(CTX_GROUP):
        hit = lax.broadcasted_iota(I32, (nslot, SEQ), 1) == _row_to_col(idx_ref[s])
        xs = _mm(jnp.where(hit, 1.0, 0.0).astype(BF16), h_ref[s])
        x_ref[:, CAP_CTX * s:CAP_CTX * (s + 1), :] = xs.astype(BF16).reshape(N_EXPERTS, CAP_CTX, D_MODEL)


def _gather_ctx(idx_row, h2):
    nslot = N_EXPERTS * CAP_CTX
    return pl.pallas_call(
        _gather_ctx_kernel,
        grid=(BATCH // CTX_GROUP,),
        in_specs=[pl.BlockSpec((CTX_GROUP, 1, nslot), lambda s: (s, 0, 0)),
                  pl.BlockSpec((CTX_GROUP, SEQ, D_MODEL), lambda s: (s, 0, 0))],
        out_specs=pl.BlockSpec((N_EXPERTS, CTX_GROUP * CAP_CTX, D_MODEL), lambda s: (0, s, 0)),
        out_shape=jax.ShapeDtypeStruct((N_EXPERTS, SLOTS_CTX, D_MODEL), BF16),
        compiler_params=_params(("arbitrary",)),
        name="gather_ctx",
    )(idx_row, h2)


def _gather_lat_kernel(idx_ref, h_ref, x_ref):
    first = GATHER_GROUP * pl.program_id(1)
    col = jnp.concatenate([_row_to_col(idx_ref[0, pl.ds(first + i, 1), :]) for i in range(GATHER_GROUP)],
                          axis=0)
    hit = lax.broadcasted_iota(I32, (GATHER_GROUP * CAP_LAT, DEC_SEQ), 1) == col
    xs = _mm(jnp.where(hit, 1.0, 0.0).astype(BF16), h_ref[0])
    x_ref[...] = xs.astype(BF16).reshape(GATHER_GROUP, CAP_LAT, D_MODEL)


def _gather_lat(idx, h2):
    return pl.pallas_call(
        _gather_lat_kernel,
        grid=(DEC_BATCH, N_EXPERTS // GATHER_GROUP),
        in_specs=[pl.BlockSpec((1, N_EXPERTS, CAP_LAT), lambda s, e: (s, 0, 0)),
                  pl.BlockSpec((1, DEC_SEQ, D_MODEL), lambda s, e: (s, 0, 0))],
        out_specs=pl.BlockSpec((GATHER_GROUP, CAP_LAT, D_MODEL), lambda s, e: (e, s, 0)),
        out_shape=jax.ShapeDtypeStruct((N_EXPERTS, SLOTS_LAT, D_MODEL), BF16),
        compiler_params=_params(("arbitrary", "arbitrary")),
        name="gather_lat",
    )(idx, h2)


def _ffn_kernel(xc_ref, xl_ref, gc_ref, gl_ref, wg_ref, wu_ref, wd_ref, yc_ref, yl_ref, h_scr, wl_scr):
    j = pl.program_id(1)
    tail = FF_HALF - FF_HALF // FF_TILE * FF_TILE

    def swiglu_cols(wg, wu, c0):
        wgb = wg.astype(BF16)
        wub = wu.astype(BF16)
        for i, x_ref in enumerate((xc_ref, xl_ref)):
            x = x_ref[0]
            a = _mm(x, wgb)
            u = _mm(x, wub)
            h = a * (1.0 / (1.0 + jnp.exp(-a))) * u
            h_scr[SLOTS_CTX * i:SLOTS_CTX * (i + 1), c0:c0 + FF_TILE] = h.astype(BF16)

    @pl.when(j == 0)
    def _():
        for c in range(FF_HALF // FF_TILE):
            lo = FF_TILE * c
            swiglu_cols(wg_ref[0, :, lo:lo + FF_TILE], wu_ref[0, :, lo:lo + FF_TILE], lo)
        wl_scr[0] = wg_ref[0, :, FF_HALF - tail:FF_HALF]
        wl_scr[1] = wu_ref[0, :, FF_HALF - tail:FF_HALF]

    @pl.when(j == 1)
    def _():
        head = FF_TILE - tail
        swiglu_cols(jnp.concatenate([wl_scr[0], wg_ref[0, :, 0:head]], axis=1),
                    jnp.concatenate([wl_scr[1], wu_ref[0, :, 0:head]], axis=1), FF_HALF - tail)
        for c in range(FF_HALF // FF_TILE):
            lo = head + FF_TILE * c
            swiglu_cols(wg_ref[0, :, lo:lo + FF_TILE], wu_ref[0, :, lo:lo + FF_TILE], FF_HALF + lo)

    @pl.when(j >= 2)
    def _():
        wdb = wd_ref[0].astype(BF16)
        e = pl.program_id(0)
        for i, (g_ref, y_ref) in enumerate(((gc_ref, yc_ref), (gl_ref, yl_ref))):
            nset = g_ref.shape[0] // N_EXPERTS
            gate = jnp.concatenate(
                [_row_to_col(g_ref[pl.ds(N_EXPERTS * s + e, 1), :][:, :SLOTS_CTX // nset]) for s in range(nset)],
                axis=0)
            y = _mm(h_scr[SLOTS_CTX * i:SLOTS_CTX * (i + 1), :], wdb)
            y_ref[0] = (y * gate).astype(BF16)


def _expert_ffn(xc, xl, gc, gl, w_gate, w_up, w_down):
    assert SLOTS_CTX == SLOTS_LAT and FF_HALF % LANES == 0 and D_MODEL % OUT_TILE == 0
    xs = lambda n: pl.BlockSpec((1, n, D_MODEL), lambda e, j: (e, 0, 0))
    gs = lambda g: pl.BlockSpec(g.shape, lambda e, j: (0, 0))
    last = N_EXPERTS - 1
    up = pl.BlockSpec((1, D_MODEL, FF_HALF),
                      lambda e, j: (jnp.where(j >= 2, jnp.minimum(e + 1, last), e), 0, jnp.where(j == 1, 1, 0)))
    down = pl.BlockSpec((1, D_FF, OUT_TILE), lambda e, j: (e, 0, jnp.maximum(j - 2, 0)))
    ys = lambda n: pl.BlockSpec((1, n, OUT_TILE), lambda e, j: (e, 0, jnp.maximum(j - 2, 0)))
    return pl.pallas_call(
        _ffn_kernel,
        grid=(N_EXPERTS, 2 + D_MODEL // OUT_TILE),
        in_specs=[xs(SLOTS_CTX), xs(SLOTS_LAT), gs(gc), gs(gl), up, up, down],
        out_specs=[ys(SLOTS_CTX), ys(SLOTS_LAT)],
        out_shape=[jax.ShapeDtypeStruct((N_EXPERTS, SLOTS_CTX, D_MODEL), BF16),
                   jax.ShapeDtypeStruct((N_EXPERTS, SLOTS_LAT, D_MODEL), BF16)],
        scratch_shapes=[pltpu.VMEM((SLOTS_CTX + SLOTS_LAT, D_FF), BF16),
                        pltpu.VMEM((2, D_MODEL, LANES), F32)],
        compiler_params=_params(("arbitrary", "arbitrary"), BIG_VMEM_LIMIT),
        name="expert_ffn",
    )(xc, xl, gc, gl, w_gate, w_up, w_down)


def _combine_ctx_kernel(y_ref, idx_ref, x1_ref, mod_ref, lng_ref, lnb_ref, o_ref):
    g2 = _mod_rows(mod_ref, 0)[5]
    nslot = N_EXPERTS * CAP_CTX
    for s in range(CTX_GROUP):
        hit = lax.broadcasted_iota(I32, (SEQ, nslot), 0) == idx_ref[s]
        y = y_ref[:, CAP_CTX * s:CAP_CTX * (s + 1), :].reshape(nslot, D_MODEL)
        f = _mm(jnp.where(hit, 1.0, 0.0).astype(BF16), y)
        o_ref[s] = _layer_norm(DEEPNORM_ALPHA * x1_ref[s] + g2 * f, lng_ref[...], lnb_ref[...])


def _combine_ctx(yc, idx_row, x1, mod, ln_g, ln_b):
    full = lambda shape: pl.BlockSpec(shape, lambda s: (0,) * len(shape))
    return pl.pallas_call(
        _combine_ctx_kernel,
        grid=(BATCH // CTX_GROUP,),
        in_specs=[pl.BlockSpec((N_EXPERTS, CTX_GROUP * CAP_CTX, D_MODEL), lambda s: (0, s, 0)),
                  pl.BlockSpec((CTX_GROUP, 1, N_EXPERTS * CAP_CTX), lambda s: (s, 0, 0)),
                  pl.BlockSpec((CTX_GROUP, SEQ, D_MODEL), lambda s: (s, 0, 0)),
                  full((SUBLANES, 6 * D_MODEL)), full((1, D_MODEL)), full((1, D_MODEL))],
        out_specs=pl.BlockSpec((CTX_GROUP, SEQ, D_MODEL), lambda s: (s, 0, 0)),
        out_shape=jax.ShapeDtypeStruct((BATCH, SEQ, D_MODEL), F32),
        compiler_params=_params(("arbitrary",)),
        name="combine_ctx",
    )(yc, idx_row, x1, mod, ln_g, ln_b)


def _combine_lat_kernel(y_ref, idx_ref, x1_ref, mod_ref, lng_ref, lnb_ref, o_ref):
    s = pl.program_id(0)
    g2 = _mod_rows(mod_ref, 1 + s)[5]
    tok = lax.broadcasted_iota(I32, (ROW_TILE, CAP_LAT), 0) + ROW_TILE * pl.program_id(1)
    f = jnp.zeros((ROW_TILE, D_MODEL), F32)
    for e in range(N_EXPERTS):
        hit = tok == idx_ref[0, e:e + 1, :]
        f = f + _mm(jnp.where(hit, 1.0, 0.0).astype(BF16), y_ref[e])
    o_ref[0] = _layer_norm(DEEPNORM_ALPHA * x1_ref[0] + g2 * f, lng_ref[...], lnb_ref[...])


def _combine_lat(yl, idx, x1, mod, ln_g, ln_b):
    nt = DEC_SEQ // ROW_TILE
    full = lambda shape: pl.BlockSpec(shape, lambda s, t: (0,) * len(shape))
    return pl.pallas_call(
        _combine_lat_kernel,
        grid=(DEC_BATCH, nt),
        in_specs=[pl.BlockSpec((N_EXPERTS, CAP_LAT, D_MODEL), lambda s, t: (0, s, 0)),
                  pl.BlockSpec((1, N_EXPERTS, CAP_LAT), lambda s, t: (s, 0, 0)),
                  pl.BlockSpec((1, ROW_TILE, D_MODEL), lambda s, t: (s, t, 0)),
                  full((SUBLANES, 6 * D_MODEL)), full((1, D_MODEL)), full((1, D_MODEL))],
        out_specs=pl.BlockSpec((1, ROW_TILE, D_MODEL), lambda s, t: (s, t, 0)),
        out_shape=jax.ShapeDtypeStruct((DEC_BATCH, DEC_SEQ, D_MODEL), F32),
        compiler_params=_params(("arbitrary", "arbitrary")),
        name="combine_lat",
    )(yl, idx, x1, mod, ln_g, ln_b)


def _rope_tables():
    t = np.arange(DEC_SEQ)
    row = (t // GRID_W).astype(np.float64)
    col = (t % GRID_W).astype(np.float64)
    half = HEAD_DIM // 2
    inv = ROPE_THETA ** (-np.arange(0, half, 2, dtype=np.float64) / half)
    ang_r = row[:, None] * inv[None, :]
    ang_c = col[:, None] * inv[None, :]
    ang = np.concatenate([ang_r, ang_r, ang_c, ang_c], axis=-1)
    cos = np.tile(np.cos(ang), (1, 2))
    sin = np.tile(np.sin(ang), (1, 2))
    first = (np.arange(LANES) % 32) < 16
    return jnp.asarray(cos, F32), jnp.asarray(np.where(first[None, :], -sin, sin), F32)


def kernel(x_prompt, x_sample, cache_na_k, cache_na_v, cache_diff_k, cache_diff_v, c, c_ctx, w_ada, b_ada, w_in, w_out, na_rel_bias, lambda_q1, lambda_k1, lambda_q2, lambda_k2, subln_g, ln1_g, ln1_b, ln2_g, ln2_b, w_router, w_gate, w_up, w_down):
    l = 0
    cvec = jnp.concatenate([c_ctx[None, :], c, jnp.zeros((SUBLANES - 1 - DEC_BATCH, D_MODEL), F32)], axis=0)
    mod = _modulation(cvec, w_ada[l], b_ada[l][None, :])
    lamv = jnp.stack([lambda_q1[l], lambda_k1[l], lambda_q2[l], lambda_k2[l]], axis=0)
    w_in_b = w_in[l].astype(BF16)
    w_out_b = w_out[l].astype(BF16)
    sub = subln_g[l][None, :]
    wrt = w_router[l].T

    tr = lambda a: jnp.swapaxes(a, -1, -2)

    x1c, h2c, affc, na_k_t, na_v_t, diff_k_t, new_diff_v = _context_block(
        x_prompt, mod, lamv, w_in_b, w_out_b, sub, ln1_g[l][None, :], ln1_b[l][None, :], wrt)

    cos, sin = _rope_tables()
    naq, nak, nav, dfq, dfk, dfv, tbl, stats = _latent_qkv(x_sample, mod, w_in_b, cos, sin,
                                                    na_rel_bias[l].reshape(-1))
    na_o = _latent_na(naq, nak, nav, tr(cache_na_k), tr(cache_na_v), tbl, stats)
    df_o = _latent_diff(dfq, dfk, dfv, tr(cache_diff_k), cache_diff_v, lamv, sub)
    x1l, h2l, affl = _latent_out(x_sample, na_o, df_o, mod, w_out_b,
                                 ln1_g[l][None, :], ln1_b[l][None, :], wrt)

    idxc, gatec, idxl, gatel = _route(affc.reshape(BATCH * N_EXPERTS, SEQ),
                                      affl.reshape(DEC_BATCH * N_EXPERTS, DEC_SEQ))
    idxc = idxc.reshape(BATCH, 1, N_EXPERTS * CAP_CTX)
    idxl = idxl.reshape(DEC_BATCH, N_EXPERTS, CAP_LAT)
    xc = _gather_ctx(idxc, h2c)
    xl = _gather_lat(idxl, h2l)
    yc, yl = _expert_ffn(xc, xl, gatec, gatel, w_gate[l], w_up[l], w_down[l])
    y_prompt = _combine_ctx(yc, idxc, x1c, mod, ln2_g[l][None, :], ln2_b[l][None, :])
    y_sample = _combine_lat(yl, idxl, x1l, mod, ln2_g[l][None, :], ln2_b[l][None, :])
    return (y_prompt, y_sample, tr(na_k_t), tr(na_v_t), tr(diff_k_t), new_diff_v)
```

```python
import functools
import math

import jax
import jax.numpy as jnp
import numpy as np
from jax import lax
from jax.experimental import pallas as pl
from jax.experimental.pallas import tpu as pltpu

F32 = jnp.float32
BF16 = jnp.bfloat16
I32 = jnp.int32

D_MODEL = 1024
BATCH = 16
SEQ = 256
DEC_BATCH = 2
DEC_SEQ = 2048
PAST_LEN = 256
GRID_W = 64
GRID_ROWS = DEC_SEQ // GRID_W
HEAD_DIM = 64
NA_HEADS = 8
DIFF_HEADS = 4
NA_WIDTH = NA_HEADS * HEAD_DIM
DIFF_WIDTH = DIFF_HEADS * 2 * HEAD_DIM
QKV_WIDTH = 3 * NA_WIDTH + 3 * DIFF_WIDTH
NA_WIN_H = 8
NA_WIN_W = 16
N_REL_H = 2 * NA_WIN_H - 1
N_REL_W = 2 * NA_WIN_W - 1
N_EXPERTS = 16
EC_CAPACITY_FACTOR = 2
D_FF = 2816
ROPE_THETA = 10000.0
NORM_EPS = 1e-5
NEG_INF = -1e30
DEPTH = 1
DEEPNORM_ALPHA = (2.0 * DEPTH) ** 0.25
LAMBDA_INIT = 0.8 - 0.6 * math.exp(-0.3 * 0)
SCALE = HEAD_DIM ** -0.5

OFF_NAQ = 0
OFF_NAK = NA_WIDTH
OFF_NAV = 2 * NA_WIDTH
OFF_DFQ = 3 * NA_WIDTH
OFF_DFK = 3 * NA_WIDTH + DIFF_WIDTH
OFF_DFV = 3 * NA_WIDTH + 2 * DIFF_WIDTH

CAP_CTX = EC_CAPACITY_FACTOR * SEQ // N_EXPERTS
CAP_LAT = EC_CAPACITY_FACTOR * DEC_SEQ // N_EXPERTS
SLOTS_CTX = BATCH * CAP_CTX
SLOTS_LAT = DEC_BATCH * CAP_LAT

ROW_TILE = 512
NA_QUAD = 4
NA_WIN_ROWS = 12
NA_GROUP = 8
FF_TILE = 256
FF_HALF = D_FF // 2
OUT_TILE = 512
LANES = 128
GATHER_GROUP = 4
CTX_GROUP = 4
CTX_REQ = 2
DIFF_ROWS = 1024
Q_ROWS = 256

BOUND_SLACK = 1.0 + 2.0 ** -10
ROUND_UP = 1.0 + 2.0 ** -7
SOFTMAX_FLOOR = 1e-25

SUBLANES = 8
VMEM_LIMIT = 48 * 1024 * 1024
BIG_VMEM_LIMIT = 58 * 1024 * 1024


def _params(sem, vmem=VMEM_LIMIT):
    return pltpu.CompilerParams(dimension_semantics=sem, vmem_limit_bytes=vmem)


def _mm(a, b):
    return jnp.dot(a, b, preferred_element_type=F32)


def _nt(a, b):
    return lax.dot_general(a, b, (((1,), (1,)), ((), ())), preferred_element_type=F32)


def _split(a):
    hi = a.astype(BF16)
    lo = (a - hi.astype(F32)).astype(BF16)
    return hi, lo


def _mm3(a, b):
    ah, al = _split(a)
    bh, bl = _split(b)
    return _mm(ah, bh) + _mm(al, bh) + _mm(ah, bl)


def _nt3(a, b):
    ah, al = _split(a)
    bh, bl = _split(b)
    return _nt(ah, bh) + _nt(al, bh) + _nt(ah, bl)


def _layer_norm(y, g, b):
    mu = jnp.mean(y, axis=-1, keepdims=True)
    d = y - mu
    var = jnp.mean(d * d, axis=-1, keepdims=True)
    return d * lax.rsqrt(var + NORM_EPS) * g + b


def _lam(l_ref):
    l = l_ref[...]
    a = jnp.sum(l[0:1] * l[1:2], axis=-1, keepdims=True)
    b = jnp.sum(l[2:3] * l[3:4], axis=-1, keepdims=True)
    return jnp.exp(a) - jnp.exp(b) + LAMBDA_INIT


def _max_col_norm(kt):
    k = kt.astype(F32)
    n2 = jnp.max(jnp.sum(k * k, axis=0, keepdims=True), axis=-1, keepdims=True)
    return jnp.sqrt(n2) * BOUND_SLACK


def _keys_with_ones(k, axis):
    first = lax.broadcasted_iota(I32, k.shape, axis) == 0
    return jnp.concatenate([k, jnp.where(first, 1.0, 0.0).astype(BF16)], axis=axis)


def _mod_rows(mod_ref, row):
    return [mod_ref[pl.ds(row, 1), i * D_MODEL:(i + 1) * D_MODEL] for i in range(6)]


def _sub_norm(o, sub):
    ms = jnp.mean(o * o, axis=-1, keepdims=True)
    return o * lax.rsqrt(ms + NORM_EPS) * sub * (1.0 - LAMBDA_INIT)


def _router_aff_t(h2, wrt):
    lg = _nt3(wrt, h2)
    m = jnp.max(lg, axis=0, keepdims=True)
    e = jnp.exp(lg - m)
    return e / jnp.sum(e, axis=0, keepdims=True)


def _mod_kernel(c_ref, wa_ref, wb_ref, b_ref, o_ref):
    c = c_ref[...]
    s = c * (1.0 / (1.0 + jnp.exp(-c)))
    half = D_MODEL // 2
    o_ref[...] = _mm3(s[:, :half], wa_ref[...]) + _mm3(s[:, half:], wb_ref[...]) + b_ref[...]


def _modulation(cvec, w_ada, b_ada):
    ncol = 6 * D_MODEL
    half = D_MODEL // 2
    return pl.pallas_call(
        _mod_kernel,
        grid=(6,),
        in_specs=[pl.BlockSpec((SUBLANES, D_MODEL), lambda j: (0, 0)),
                  pl.BlockSpec((half, D_MODEL), lambda j: (0, j)),
                  pl.BlockSpec((half, D_MODEL), lambda j: (1, j)),
                  pl.BlockSpec((1, D_MODEL), lambda j: (0, j))],
        out_specs=pl.BlockSpec((SUBLANES, D_MODEL), lambda j: (0, j)),
        out_shape=jax.ShapeDtypeStruct((SUBLANES, ncol), F32),
        compiler_params=_params(("arbitrary",)),
        name="modulation",
    )(cvec, w_ada, w_ada, b_ada)


def _ctx_kernel(x_ref, mod_ref, lam_ref, win_ref, wout_ref, sub_ref, lng_ref, lnb_ref, wrt_ref,
                x1_ref, h2_ref, aff_ref, nak_ref, nav_ref, dfk_ref, dfv_ref,
                qkv_scr, o_scr, s_scr, e_scr):
    rows = CTX_REQ * SEQ
    x = x_ref[...].reshape(rows, D_MODEL)
    sh1, sc1, g1, sh2, sc2, g2 = _mod_rows(mod_ref, 0)
    h = (x * (1.0 + sc1) + sh1).astype(BF16)
    qkv_scr[...] = _mm(h, win_ref[...])

    kt_na = qkv_scr[:, OFF_NAK:OFF_NAK + NA_WIDTH].T
    kt_df = qkv_scr[:, OFF_DFK:OFF_DFK + DIFF_WIDTH].T
    vt_na = qkv_scr[:, OFF_NAV:OFF_NAV + NA_WIDTH].T
    for r in range(CTX_REQ):
        tok = slice(SEQ * r, SEQ * (r + 1))
        nak_ref[r, 0] = kt_na[:, tok].reshape(NA_HEADS, HEAD_DIM, SEQ)
        nav_ref[r, 0] = vt_na[:, tok].reshape(NA_HEADS, HEAD_DIM, SEQ)
        dfk_ref[r, 0] = kt_df[:, tok].reshape(DIFF_HEADS, 2, HEAD_DIM, SEQ)
        for hh in range(DIFF_HEADS):
            c0 = OFF_DFV + 2 * HEAD_DIM * hh
            dfv_ref[r, 0, hh] = qkv_scr[tok, c0:c0 + 2 * HEAD_DIM]

    kt = jnp.concatenate([kt_na, kt_df], axis=0).astype(BF16)
    q_cols = [OFF_NAQ + HEAD_DIM * i for i in range(NA_HEADS)] + [
        OFF_DFQ + HEAD_DIM * i for i in range(2 * DIFF_HEADS)]
    nmap = len(q_cols)
    for r in range(CTX_REQ):
        tok = slice(SEQ * r, SEQ * (r + 1))
        for i, c in enumerate(q_cols):
            q = (qkv_scr[tok, c:c + HEAD_DIM] * SCALE).astype(BF16)
            s_scr[nmap * r + i] = _mm(q, kt[HEAD_DIM * i:HEAD_DIM * (i + 1), tok])
    s = s_scr[...]
    e = jnp.exp(s - jnp.max(s, axis=-1, keepdims=True))
    rl = 1.0 / jnp.sum(e, axis=-1, keepdims=True)
    e_scr[...] = e.astype(BF16)

    lam = _lam(lam_ref)
    sub = sub_ref[...]
    for r in range(CTX_REQ):
        tok = slice(SEQ * r, SEQ * (r + 1))
        for hh in range(NA_HEADS):
            i = nmap * r + hh
            v = qkv_scr[tok, OFF_NAV + HEAD_DIM * hh:OFF_NAV + HEAD_DIM * (hh + 1)].astype(BF16)
            o_scr[tok, HEAD_DIM * hh:HEAD_DIM * (hh + 1)] = (_mm(e_scr[i], v) * rl[i]).astype(BF16)
        for hh in range(DIFF_HEADS):
            i1 = nmap * r + NA_HEADS + 2 * hh
            c0 = OFF_DFV + 2 * HEAD_DIM * hh
            v = qkv_scr[tok, c0:c0 + 2 * HEAD_DIM].astype(BF16)
            o = _mm(e_scr[i1], v) * rl[i1] - _mm(e_scr[i1 + 1], v) * (lam * rl[i1 + 1])
            c1 = NA_WIDTH + 2 * HEAD_DIM * hh
            o_scr[tok, c1:c1 + 2 * HEAD_DIM] = _sub_norm(o, sub).astype(BF16)

    a = _mm(o_scr[...], wout_ref[...])
    x1 = _layer_norm(DEEPNORM_ALPHA * x + g1 * a, lng_ref[...], lnb_ref[...])
    x1_ref[...] = x1.reshape(CTX_REQ, SEQ, D_MODEL)
    h2 = x1 * (1.0 + sc2) + sh2
    h2_ref[...] = h2.astype(BF16).reshape(CTX_REQ, SEQ, D_MODEL)
    aff = _router_aff_t(h2, wrt_ref[...])
    for r in range(CTX_REQ):
        aff_ref[r] = aff[:, SEQ * r:SEQ * (r + 1)]


def _context_block(x, mod, lamv, w_in, w_out, sub, ln_g, ln_b, wrt):
    full = lambda shape: pl.BlockSpec(shape, lambda b: (0,) * len(shape))
    once = lambda shape: pl.BlockSpec(shape, lambda b: (0,) * len(shape), pipeline_mode=pl.Buffered(1))
    req = lambda *tail: pl.BlockSpec((CTX_REQ,) + tail, lambda b: (b,) + (0,) * len(tail))
    nmaps = CTX_REQ * (NA_HEADS + 2 * DIFF_HEADS)
    return pl.pallas_call(
        _ctx_kernel,
        grid=(BATCH // CTX_REQ,),
        in_specs=[req(SEQ, D_MODEL),
                  full((SUBLANES, 6 * D_MODEL)), full((4, HEAD_DIM)),
                  once((D_MODEL, QKV_WIDTH)), once((D_MODEL, D_MODEL)),
                  full((1, 2 * HEAD_DIM)), full((1, D_MODEL)), full((1, D_MODEL)),
                  full((N_EXPERTS, D_MODEL))],
        out_specs=[req(SEQ, D_MODEL), req(SEQ, D_MODEL), req(N_EXPERTS, SEQ),
                   req(1, NA_HEADS, HEAD_DIM, SEQ), req(1, NA_HEADS, HEAD_DIM, SEQ),
                   req(1, DIFF_HEADS, 2, HEAD_DIM, SEQ), req(1, DIFF_HEADS, SEQ, 2 * HEAD_DIM)],
        out_shape=[jax.ShapeDtypeStruct((BATCH, SEQ, D_MODEL), F32),
                   jax.ShapeDtypeStruct((BATCH, SEQ, D_MODEL), BF16),
                   jax.ShapeDtypeStruct((BATCH, N_EXPERTS, SEQ), F32),
                   jax.ShapeDtypeStruct((BATCH, 1, NA_HEADS, HEAD_DIM, SEQ), F32),
                   jax.ShapeDtypeStruct((BATCH, 1, NA_HEADS, HEAD_DIM, SEQ), F32),
                   jax.ShapeDtypeStruct((BATCH, 1, DIFF_HEADS, 2, HEAD_DIM, SEQ), F32),
                   jax.ShapeDtypeStruct((BATCH, 1, DIFF_HEADS, SEQ, 2 * HEAD_DIM), F32)],
        scratch_shapes=[pltpu.VMEM((CTX_REQ * SEQ, QKV_WIDTH), F32), pltpu.VMEM((CTX_REQ * SEQ, D_MODEL), BF16),
                        pltpu.VMEM((nmaps, SEQ, SEQ), F32), pltpu.VMEM((nmaps, SEQ, SEQ), BF16)],
        compiler_params=_params(("arbitrary",)),
        name="context_block",
    )(x, mod, lamv, w_in, w_out, sub, ln_g, ln_b, wrt)


def _lat_qkv_kernel(x_ref, mod_ref, win_ref, cos_ref, sin_ref, rpb_ref,
                    naq_ref, nak_ref, nav_ref, dfq_ref, dfk_ref, dfv_ref, tbl_ref, stat_ref, qkv_scr):
    b = pl.program_id(0)
    bias_top = _write_bias_tables(rpb_ref, b * pl.num_programs(1) + pl.program_id(1), tbl_ref)
    x = x_ref[0]
    sh1, sc1 = _mod_rows(mod_ref, 1 + b)[:2]
    h = (x * (1.0 + sc1) + sh1).astype(BF16)
    qkv_scr[...] = _mm(h, win_ref[...])

    kb = qkv_scr[:, OFF_NAK:OFF_NAK + NA_WIDTH].astype(BF16).astype(F32)
    col = lax.broadcasted_iota(I32, (NA_WIDTH, LANES), 0) // HEAD_DIM
    head_of = jnp.where(col == lax.broadcasted_iota(I32, (NA_WIDTH, LANES), 1), 1.0, 0.0).astype(BF16)
    kn2 = jnp.max(_mm((kb * kb).astype(BF16), head_of), axis=0, keepdims=True) * ROUND_UP
    stat_ref[0] = jnp.concatenate([kn2, jnp.broadcast_to(bias_top, (1, LANES)),
                                   jnp.zeros((SUBLANES - 2, LANES), F32)], axis=0)

    qb = (qkv_scr[:, OFF_NAQ:OFF_NAQ + NA_WIDTH] * SCALE).astype(BF16)
    qf = qb.astype(F32)
    qn = jnp.sqrt(_mm((qf * qf).astype(BF16), head_of) * ROUND_UP) * ROUND_UP
    lane64 = lax.broadcasted_iota(I32, (ROW_TILE, HEAD_DIM), 1)
    key_ext = jnp.where(lane64 < 2, 1.0, 0.0).astype(BF16)
    for hh in range(NA_HEADS):
        lo, hi = HEAD_DIM * hh, HEAD_DIM * (hh + 1)
        ext = jnp.where(lane64 == 0, -qn[:, hh:hh + 1], jnp.where(lane64 == 1, -1.0, 0.0)).astype(BF16)
        naq_ref[0, hh] = jnp.concatenate([qb[:, lo:hi], ext], axis=1)
        nak_ref[0, hh] = jnp.concatenate([qkv_scr[:, OFF_NAK + lo:OFF_NAK + hi].astype(BF16), key_ext], axis=1)
        nav_ref[0, hh] = qkv_scr[:, OFF_NAV + lo:OFF_NAV + hi].astype(BF16)

    cos = cos_ref[...]
    sin = sin_ref[...]
    lane = lax.broadcasted_iota(I32, (ROW_TILE, LANES), 1)
    first = (lane & 31) < 16

    def rope(t):
        rot = jnp.where(first, pltpu.roll(t, LANES - 16, 1), pltpu.roll(t, 16, 1))
        return t * cos + rot * sin

    map_of = jnp.where(lax.broadcasted_iota(I32, (LANES, LANES), 0) // HEAD_DIM
                       == lax.broadcasted_iota(I32, (LANES, LANES), 1), 1.0, 0.0).astype(BF16)
    for hh in range(DIFF_HEADS):
        lo = 2 * HEAD_DIM * hh
        q = (rope(qkv_scr[:, OFF_DFQ + lo:OFF_DFQ + lo + LANES]) * SCALE).astype(BF16)
        k = rope(qkv_scr[:, OFF_DFK + lo:OFF_DFK + lo + LANES])
        qf = q.astype(F32)
        qn = jnp.sqrt(_mm((qf * qf).astype(BF16), map_of) * ROUND_UP) * ROUND_UP
        for m in range(2):
            ext = jnp.where(lane64 == 0, -qn[:, m:m + 1], 0.0).astype(BF16)
            dfq_ref[0, hh, m] = jnp.concatenate([q[:, HEAD_DIM * m:HEAD_DIM * (m + 1)], ext], axis=1)
        dfk_ref[0, hh] = k.T.astype(BF16).reshape(2, HEAD_DIM, ROW_TILE)
        dfv_ref[0, hh] = qkv_scr[:, OFF_DFV + lo:OFF_DFV + lo + LANES].astype(BF16)


def _latent_qkv(x, mod, w_in, cos, sin, rpb_flat):
    nt = DEC_SEQ // ROW_TILE
    assert DEC_BATCH * nt == NA_HEADS
    tbl_shape = (NA_HEADS, 3, NA_QUAD * GRID_W, NA_WIN_ROWS * GRID_W)
    full = lambda shape: pl.BlockSpec(shape, lambda b, t: (0,) * len(shape))
    hs = lambda width: pl.BlockSpec((1, NA_HEADS, ROW_TILE, width), lambda b, t: (b, 0, t, 0))
    na_shape = lambda width: jax.ShapeDtypeStruct((DEC_BATCH, NA_HEADS, DEC_SEQ, width), BF16)
    return pl.pallas_call(
        _lat_qkv_kernel,
        grid=(DEC_BATCH, nt),
        in_specs=[pl.BlockSpec((1, ROW_TILE, D_MODEL), lambda b, t: (b, t, 0)),
                  full((SUBLANES, 6 * D_MODEL)), full((D_MODEL, QKV_WIDTH)),
                  pl.BlockSpec((ROW_TILE, LANES), lambda b, t: (t, 0)),
                  pl.BlockSpec((ROW_TILE, LANES), lambda b, t: (t, 0)),
                  pl.BlockSpec(memory_space=pltpu.SMEM)],
        out_specs=[hs(2 * HEAD_DIM), hs(2 * HEAD_DIM), hs(HEAD_DIM),
                   pl.BlockSpec((1, DIFF_HEADS, 2, ROW_TILE, 2 * HEAD_DIM), lambda b, t: (b, 0, 0, t, 0)),
                   pl.BlockSpec((1, DIFF_HEADS, 2, HEAD_DIM, ROW_TILE), lambda b, t: (b, 0, 0, 0, t)),
                   pl.BlockSpec((1, DIFF_HEADS, ROW_TILE, 2 * HEAD_DIM), lambda b, t: (b, 0, t, 0)),
                   pl.BlockSpec((1,) + tbl_shape[1:], lambda b, t: (b * nt + t, 0, 0, 0)),
                   pl.BlockSpec((1, SUBLANES, LANES), lambda b, t: (b * nt + t, 0, 0))],
        out_shape=[na_shape(2 * HEAD_DIM), na_shape(2 * HEAD_DIM), na_shape(HEAD_DIM)]
        + [jax.ShapeDtypeStruct((DEC_BATCH, DIFF_HEADS, 2, DEC_SEQ, 2 * HEAD_DIM), BF16),
           jax.ShapeDtypeStruct((DEC_BATCH, DIFF_HEADS, 2, HEAD_DIM, DEC_SEQ), BF16),
           jax.ShapeDtypeStruct((DEC_BATCH, DIFF_HEADS, DEC_SEQ, 2 * HEAD_DIM), BF16),
           jax.ShapeDtypeStruct(tbl_shape, F32),
           jax.ShapeDtypeStruct((NA_HEADS, SUBLANES, LANES), F32)],
        scratch_shapes=[pltpu.VMEM((ROW_TILE, QKV_WIDTH), F32)],
        compiler_params=_params(("arbitrary", "arbitrary")),
        name="latent_qkv",
    )(x, mod, w_in, cos, sin, rpb_flat)


def _na_row_offset(kind, qr, kr):
    if kind == 0:
        return kr - qr + 7 if kr < NA_WIN_H else None
    if kind == 1:
        return kr - qr + 3 if qr <= kr < qr + NA_WIN_H else None
    return kr - qr - 1 if NA_WIN_ROWS - NA_WIN_H <= kr else None


def _write_bias_tables(rpb_ref, h, o_ref):
    base = h * (N_REL_H * N_REL_W)
    wq = lax.broadcasted_iota(I32, (GRID_W, LANES), 0)
    c = lax.broadcasted_iota(I32, (GRID_W, LANES), 1)
    wk = c & (GRID_W - 1)
    right = c >= GRID_W
    c0 = jnp.clip(wq - NA_WIN_W // 2, 0, GRID_W - NA_WIN_W)
    in_win = (wk >= c0) & (wk < c0 + NA_WIN_W)
    cache = {}
    strips = {}

    def strip(r):
        if r not in strips:
            c8 = lax.broadcasted_iota(I32, (SUBLANES, LANES), 1)
            d = jnp.where(c8 < GRID_W, c8, c8 - LANES)
            co_of = jnp.clip(d, -(NA_WIN_W - 1), NA_WIN_W - 1) + (NA_WIN_W - 1)
            acc = jnp.zeros((SUBLANES, LANES), F32)
            for co in range(N_REL_W):
                acc = jnp.where(co_of == co, rpb_ref[base + r * N_REL_W + co], acc)
            strips[r] = jnp.concatenate([acc] * (GRID_W // SUBLANES), axis=0)
        return strips[r]

    def tile(rl, rr):
        if (rl, rr) in cache:
            return cache[(rl, rr)]
        if rl is None and rr is None:
            t = jnp.full((GRID_W, LANES), NEG_INF, F32)
        else:
            zero = jnp.zeros((GRID_W, LANES), F32)
            tl = pltpu.roll(strip(rl), 0, 1, stride=1, stride_axis=0) if rl is not None else zero
            tr = pltpu.roll(strip(rr), GRID_W, 1, stride=1, stride_axis=0) if rr is not None else zero
            acc = jnp.where(right, tr, tl)
            ok = in_win
            if rl is None:
                ok = ok & right
            if rr is None:
                ok = ok & jnp.logical_not(right)
            t = jnp.where(ok, acc, NEG_INF)
        cache[(rl, rr)] = t
        return t

    for kind in range(3):
        for qr in range(NA_QUAD):
            row = [tile(_na_row_offset(kind, qr, 2 * p), _na_row_offset(kind, qr, 2 * p + 1))
                   for p in range(NA_WIN_ROWS // 2)]
            o_ref[0, kind, GRID_W * qr:GRID_W * (qr + 1), :] = jnp.concatenate(row, axis=1)
    top = functools.reduce(jnp.maximum, cache.values())
    return jnp.max(jnp.max(top, axis=-1, keepdims=True), axis=0, keepdims=True)


def _na_kernel(q_ref, k_ref, v_ref, ck_ref, cv_ref, tbl_ref, stat_ref, o_ref):
    b = pl.program_id(0)
    g = pl.program_id(2)
    u0 = jnp.clip(NA_QUAD * g - NA_QUAD, 0, GRID_ROWS - NA_WIN_ROWS)
    start = pl.multiple_of(u0 * GRID_W, GRID_W)
    nwin = NA_WIN_ROWS * GRID_W
    tiles = stat_ref.shape[0] // DEC_BATCH
    kn2 = functools.reduce(jnp.maximum, [stat_ref[b * tiles + t, 0:1, :] for t in range(tiles)])

    lane = lax.broadcasted_iota(I32, (1, 2 * HEAD_DIM), 1)
    sub = lax.broadcasted_iota(I32, (HEAD_DIM, PAST_LEN), 0)
    ones_rows = jnp.where(sub < 2, 1.0, 0.0).astype(BF16)

    def attend(exact_max):
        lmin = None
        for j in range(NA_GROUP):
            q = q_ref[0, j]
            kw = k_ref[0, j, pl.ds(start, nwin), :]
            vw = v_ref[0, j, pl.ds(start, nwin), :]
            kct = ck_ref[0, 0, j].astype(BF16)
            vct = cv_ref[0, 0, j].astype(BF16)
            kct1 = jnp.concatenate([kct, ones_rows], axis=0)
            if exact_max:
                q0 = (q.astype(F32) * jnp.where(lane < HEAD_DIM, 1.0, 0.0)).astype(BF16)
                sl = _nt(q0, kw) + tbl_ref[j, 0]
                sc = _mm(q[:, :HEAD_DIM], kct)
                m = jnp.maximum(jnp.max(sl, axis=-1, keepdims=True), jnp.max(sc, axis=-1, keepdims=True))
                el = jnp.exp(sl - m)
                ec = jnp.exp(sc - m)
            else:
                knorm = jnp.maximum(jnp.sqrt(kn2[:, j:j + 1]) * BOUND_SLACK, _max_col_norm(kct)) * ROUND_UP
                btop = jnp.maximum(stat_ref[j, 1:2, 0:1], 0.0) * ROUND_UP
                scale = jnp.where(lane < HEAD_DIM, 1.0, jnp.where(lane == HEAD_DIM, knorm, btop))
                qs = (q.astype(F32) * scale).astype(BF16)
                el = jnp.exp(_nt(qs, kw) + tbl_ref[j, 0])
                ec = jnp.exp(_mm(qs, kct1))
            l = jnp.sum(el, axis=-1, keepdims=True) + jnp.sum(ec, axis=-1, keepdims=True)
            o = (_mm(el.astype(BF16), vw) + _nt(ec.astype(BF16), vct)) / l
            o_ref[0, :, HEAD_DIM * j:HEAD_DIM * (j + 1)] = o.astype(BF16)
            lmin = jnp.min(l) if lmin is None else jnp.minimum(lmin, jnp.min(l))
        return lmin

    lmin = attend(exact_max=False)

    @pl.when(jnp.logical_not(lmin >= SOFTMAX_FLOOR))
    def _():
        attend(exact_max=True)


def _latent_na(naq, nak, nav, cache_k, cache_v, tbl, stats):
    assert NA_GROUP == NA_HEADS
    nq = GRID_ROWS // NA_QUAD
    rows = NA_QUAD * GRID_W
    kind = lambda g: jnp.minimum(g, 1) + g // (nq - 1)
    return pl.pallas_call(
        _na_kernel,
        grid=(DEC_BATCH, NA_HEADS // NA_GROUP, nq),
        in_specs=[pl.BlockSpec((1, NA_GROUP, rows, 2 * HEAD_DIM), lambda b, p, g: (b, p, g, 0)),
                  pl.BlockSpec((1, NA_GROUP, DEC_SEQ, 2 * HEAD_DIM), lambda b, p, g: (b, p, 0, 0)),
                  pl.BlockSpec((1, NA_GROUP, DEC_SEQ, HEAD_DIM), lambda b, p, g: (b, p, 0, 0)),
                  pl.BlockSpec((1, 1, NA_GROUP, HEAD_DIM, PAST_LEN), lambda b, p, g: (b, 0, p, 0, 0)),
                  pl.BlockSpec((1, 1, NA_GROUP, HEAD_DIM, PAST_LEN), lambda b, p, g: (b, 0, p, 0, 0)),
                  pl.BlockSpec((NA_GROUP, 1, rows, NA_WIN_ROWS * GRID_W), lambda b, p, g: (p, kind(g), 0, 0)),
                  pl.BlockSpec(stats.shape, lambda b, p, g: (0, 0, 0))],
        out_specs=pl.BlockSpec((1, rows, NA_GROUP * HEAD_DIM), lambda b, p, g: (b, g, p)),
        out_shape=jax.ShapeDtypeStruct((DEC_BATCH, DEC_SEQ, NA_WIDTH), BF16),
        compiler_params=_params(("arbitrary",) * 3),
        name="latent_na",
    )(naq, nak, nav, cache_k, cache_v, tbl, stats)


def _diff_kernel(q_ref, kt_ref, v_ref, ck_ref, cv_ref, lam_ref, sub_ref, o_ref):
    lam = _lam(lam_ref)
    vc = cv_ref[0, 0, 0].astype(BF16)
    vl = v_ref[0, 0]
    kcs = [ck_ref[0, 0, 0, m].astype(BF16) for m in range(2)]
    kls = [kt_ref[0, 0, m] for m in range(2)]
    knorm = [jnp.maximum(_max_col_norm(kcs[m]), _max_col_norm(kls[m])) for m in range(2)]
    kcs1 = [_keys_with_ones(k, 0) for k in kcs]
    kls1 = [_keys_with_ones(k, 0) for k in kls]
    lane = lax.broadcasted_iota(I32, (1, 2 * HEAD_DIM), 1)
    scale = [jnp.where(lane < HEAD_DIM, 1.0, jnp.where(lane == HEAD_DIM, kn * ROUND_UP, 0.0))
             for kn in knorm]

    def attend(exact_max):
        lmin = None
        for r in range(DIFF_ROWS // Q_ROWS):
            rows = slice(Q_ROWS * r, Q_ROWS * (r + 1))
            parts = []
            for m in range(2):
                q = q_ref[0, 0, m, rows, :]
                if exact_max:
                    sc = _mm(q[:, :HEAD_DIM], kcs[m])
                    sl = _mm(q[:, :HEAD_DIM], kls[m])
                    mx = jnp.maximum(jnp.max(sc, axis=-1, keepdims=True), jnp.max(sl, axis=-1, keepdims=True))
                    ec = jnp.exp(sc - mx)
                    el = jnp.exp(sl - mx)
                else:
                    qs = (q.astype(F32) * scale[m]).astype(BF16)
                    ec = jnp.exp(_mm(qs, kcs1[m]))
                    el = jnp.exp(_mm(qs, kls1[m]))
                l = jnp.sum(ec, axis=-1, keepdims=True) + jnp.sum(el, axis=-1, keepdims=True)
                parts.append((_mm(ec.astype(BF16), vc) + _mm(el.astype(BF16), vl), l))
                lmin = jnp.min(l) if lmin is None else jnp.minimum(lmin, jnp.min(l))
            (a1, l1), (a2, l2) = parts
            o = a1 * (1.0 / l1) - a2 * (lam / l2)
            o_ref[0, rows, :] = _sub_norm(o, sub_ref[...]).astype(BF16)
        return lmin

    lmin = attend(exact_max=False)

    @pl.when(jnp.logical_not(lmin >= SOFTMAX_FLOOR))
    def _():
        attend(exact_max=True)


def _latent_diff(dfq, dfk, dfv, cache_k, cache_v, lamv, sub):
    nt = DEC_SEQ // DIFF_ROWS
    full = lambda shape: pl.BlockSpec(shape, lambda b, h, t: (0,) * len(shape))
    return pl.pallas_call(
        _diff_kernel,
        grid=(DEC_BATCH, DIFF_HEADS, nt),
        in_specs=[pl.BlockSpec((1, 1, 2, DIFF_ROWS, 2 * HEAD_DIM), lambda b, h, t: (b, h, 0, t, 0)),
                  pl.BlockSpec((1, 1, 2, HEAD_DIM, DEC_SEQ), lambda b, h, t: (b, h, 0, 0, 0)),
                  pl.BlockSpec((1, 1, DEC_SEQ, 2 * HEAD_DIM), lambda b, h, t: (b, h, 0, 0)),
                  pl.BlockSpec((1, 1, 1, 2, HEAD_DIM, PAST_LEN), lambda b, h, t: (b, 0, h, 0, 0, 0)),
                  pl.BlockSpec((1, 1, 1, PAST_LEN, 2 * HEAD_DIM), lambda b, h, t: (b, 0, h, 0, 0)),
                  full((4, HEAD_DIM)), full((1, 2 * HEAD_DIM))],
        out_specs=pl.BlockSpec((1, DIFF_ROWS, 2 * HEAD_DIM), lambda b, h, t: (b, t, h)),
        out_shape=jax.ShapeDtypeStruct((DEC_BATCH, DEC_SEQ, DIFF_WIDTH), BF16),
        compiler_params=_params(("arbitrary",) * 3),
        name="latent_diff",
    )(dfq, dfk, dfv, cache_k, cache_v, lamv, sub)


def _lat_out_kernel(x_ref, na_ref, df_ref, mod_ref, wout_ref, lng_ref, lnb_ref, wrt_ref,
                    x1_ref, h2_ref, aff_ref):
    b = pl.program_id(0)
    x = x_ref[0]
    _, _, g1, sh2, sc2, _ = _mod_rows(mod_ref, 1 + b)
    a = _mm(na_ref[0], wout_ref[0:NA_WIDTH, :]) + _mm(df_ref[0], wout_ref[NA_WIDTH:, :])
    x1 = _layer_norm(DEEPNORM_ALPHA * x + g1 * a, lng_ref[...], lnb_ref[...])
    x1_ref[0] = x1
    h2 = x1 * (1.0 + sc2) + sh2
    h2_ref[0] = h2.astype(BF16)
    aff_ref[0] = _router_aff_t(h2, wrt_ref[...])


def _latent_out(x, na_o, df_o, mod, w_out, ln_g, ln_b, wrt):
    nt = DEC_SEQ // ROW_TILE
    full = lambda shape: pl.BlockSpec(shape, lambda b, t: (0,) * len(shape))
    tok = lambda w: pl.BlockSpec((1, ROW_TILE, w), lambda b, t: (b, t, 0))
    return pl.pallas_call(
        _lat_out_kernel,
        grid=(DEC_BATCH, nt),
        in_specs=[tok(D_MODEL), tok(NA_WIDTH), tok(DIFF_WIDTH), full((SUBLANES, 6 * D_MODEL)),
                  full((D_MODEL, D_MODEL)), full((1, D_MODEL)), full((1, D_MODEL)),
                  full((N_EXPERTS, D_MODEL))],
        out_specs=[tok(D_MODEL), tok(D_MODEL),
                   pl.BlockSpec((1, N_EXPERTS, ROW_TILE), lambda b, t: (b, 0, t))],
        out_shape=[jax.ShapeDtypeStruct((DEC_BATCH, DEC_SEQ, D_MODEL), F32),
                   jax.ShapeDtypeStruct((DEC_BATCH, DEC_SEQ, D_MODEL), BF16),
                   jax.ShapeDtypeStruct((DEC_BATCH, N_EXPERTS, DEC_SEQ), F32)],
        compiler_params=_params(("arbitrary", "arbitrary")),
        name="latent_out",
    )(x, na_o, df_o, mod, w_out, ln_g, ln_b, wrt)


def _ranks_first(va, ia, vb, ib):
    return (va > vb) | ((va == vb) & (ia < ib))


def _sort_rows_desc(aff):
    rows, n = aff.shape
    nb = n // LANES
    lane = lax.broadcasted_iota(I32, (rows, LANES), 1)
    v = [aff[:, LANES * b:LANES * (b + 1)] for b in range(nb)]
    ix = [lane + LANES * b for b in range(nb)]
    k = 2
    while k <= n:
        j = k // 2
        while j >= 1:
            if j >= LANES:
                sb = j // LANES
                for lo in range(nb):
                    if lo & sb:
                        continue
                    hi = lo | sb
                    f = _ranks_first(v[lo], ix[lo], v[hi], ix[hi])
                    if k < n and (LANES * lo) & k:
                        f = jnp.logical_not(f)
                    v[lo], v[hi] = jnp.where(f, v[lo], v[hi]), jnp.where(f, v[hi], v[lo])
                    ix[lo], ix[hi] = jnp.where(f, ix[lo], ix[hi]), jnp.where(f, ix[hi], ix[lo])
            else:
                upper = (lane & j) != 0
                if k < LANES:
                    flip_lanes = jnp.logical_xor(upper, (lane & k) != 0)
                for b in range(nb):
                    pv = jnp.where(upper, pltpu.roll(v[b], j, 1), pltpu.roll(v[b], LANES - j, 1))
                    pi = jnp.where(upper, pltpu.roll(ix[b], j, 1), pltpu.roll(ix[b], LANES - j, 1))
                    f = _ranks_first(v[b], ix[b], pv, pi)
                    if k < LANES:
                        flip = flip_lanes
                    elif k < n and (LANES * b) & k:
                        flip = jnp.logical_not(upper)
                    else:
                        flip = upper
                    keep = jnp.logical_xor(f, flip)
                    v[b] = jnp.where(keep, v[b], pv)
                    ix[b] = jnp.where(keep, ix[b], pi)
            j //= 2
        k *= 2
    return v, ix


def _route_kernel(affc_ref, affl_ref, idxc_ref, gatec_ref, idxl_ref, gatel_ref):
    for aff_ref, idx_ref, gate_ref, cap in ((affc_ref, idxc_ref, gatec_ref, CAP_CTX),
                                            (affl_ref, idxl_ref, gatel_ref, CAP_LAT)):
        v, ix = _sort_rows_desc(aff_ref[...])
        if cap < LANES:
            idx_ref[...] = ix[0][:, :cap]
            gate_ref[...] = v[0]
        else:
            idx_ref[...] = jnp.concatenate(ix[:cap // LANES], axis=1)
            gate_ref[...] = jnp.concatenate(v[:cap // LANES], axis=1)


def _route(affc_t, affl_t):
    out = lambda a, cap: [jax.ShapeDtypeStruct((a.shape[0], cap), I32),
                          jax.ShapeDtypeStruct((a.shape[0], max(cap, LANES)), F32)]
    return pl.pallas_call(
        _route_kernel,
        out_shape=out(affc_t, CAP_CTX) + out(affl_t, CAP_LAT),
        compiler_params=_params(None),
        name="route",
    )(affc_t, affl_t)


def _row_to_col(row):
    n = row.shape[1]
    eye = lax.broadcasted_iota(I32, (n, n), 0) == lax.broadcasted_iota(I32, (n, n), 1)
    return jnp.sum(jnp.where(eye, row, jnp.zeros_like(row)), axis=1, keepdims=True)


def _gather_ctx_kernel(idx_ref, h_ref, x_ref):
    nslot = N_EXPERTS * CAP_CTX
    for s in range(CTX_GROUP):
        hit = lax.broadcasted_iota(I32, (nslot, SEQ), 1) == _row_to_col(idx_ref[s])
        xs = _mm(jnp.where(hit, 1.0, 0.0).astype(BF16), h_ref[s])
        x_ref[:, CAP_CTX * s:CAP_CTX * (s + 1), :] = xs.astype(BF16).reshape(N_EXPERTS, CAP_CTX, D_MODEL)


def _gather_ctx(idx_row, h2):
    nslot = N_EXPERTS * CAP_CTX
    return pl.pallas_call(
        _gather_ctx_kernel,
        grid=(BATCH // CTX_GROUP,),
        in_specs=[pl.BlockSpec((CTX_GROUP, 1, nslot), lambda s: (s, 0, 0)),
                  pl.BlockSpec((CTX_GROUP, SEQ, D_MODEL), lambda s: (s, 0, 0))],
        out_specs=pl.BlockSpec((N_EXPERTS, CTX_GROUP * CAP_CTX, D_MODEL), lambda s: (0, s, 0)),
        out_shape=jax.ShapeDtypeStruct((N_EXPERTS, SLOTS_CTX, D_MODEL), BF16),
        compiler_params=_params(("arbitrary",)),
        name="gather_ctx",
    )(idx_row, h2)


def _gather_lat_kernel(idx_ref, h_ref, x_ref):
    first = GATHER_GROUP * pl.program_id(1)
    col = jnp.concatenate([_row_to_col(idx_ref[0, pl.ds(first + i, 1), :]) for i in range(GATHER_GROUP)],
                          axis=0)
    hit = lax.broadcasted_iota(I32, (GATHER_GROUP * CAP_LAT, DEC_SEQ), 1) == col
    xs = _mm(jnp.where(hit, 1.0, 0.0).astype(BF16), h_ref[0])
    x_ref[...] = xs.astype(BF16).reshape(GATHER_GROUP, CAP_LAT, D_MODEL)


def _gather_lat(idx, h2):
    return pl.pallas_call(
        _gather_lat_kernel,
        grid=(DEC_BATCH, N_EXPERTS // GATHER_GROUP),
        in_specs=[pl.BlockSpec((1, N_EXPERTS, CAP_LAT), lambda s, e: (s, 0, 0)),
                  pl.BlockSpec((1, DEC_SEQ, D_MODEL), lambda s, e: (s, 0, 0))],
        out_specs=pl.BlockSpec((GATHER_GROUP, CAP_LAT, D_MODEL), lambda s, e: (e, s, 0)),
        out_shape=jax.ShapeDtypeStruct((N_EXPERTS, SLOTS_LAT, D_MODEL), BF16),
        compiler_params=_params(("arbitrary", "arbitrary")),
        name="gather_lat",
    )(idx, h2)


def _ffn_kernel(xc_ref, xl_ref, gc_ref, gl_ref, wg_ref, wu_ref, wd_ref, yc_ref, yl_ref, h_scr, wl_scr):
    j = pl.program_id(1)
    tail = FF_HALF - FF_HALF // FF_TILE * FF_TILE

    def swiglu_cols(wg, wu, c0):
        wgb = wg.astype(BF16)
        wub = wu.astype(BF16)
        for i, x_ref in enumerate((xc_ref, xl_ref)):
            x = x_ref[0]
            a = _mm(x, wgb)
            u = _mm(x, wub)
            h = a * (1.0 / (1.0 + jnp.exp(-a))) * u
            h_scr[SLOTS_CTX * i:SLOTS_CTX * (i + 1), c0:c0 + FF_TILE] = h.astype(BF16)

    @pl.when(j == 0)
    def _():
        for c in range(FF_HALF // FF_TILE):
            lo = FF_TILE * c
            swiglu_cols(wg_ref[0, :, lo:lo + FF_TILE], wu_ref[0, :, lo:lo + FF_TILE], lo)
        wl_scr[0] = wg_ref[0, :, FF_HALF - tail:FF_HALF]
        wl_scr[1] = wu_ref[0, :, FF_HALF - tail:FF_HALF]

    @pl.when(j == 1)
    def _():
        head = FF_TILE - tail
        swiglu_cols(jnp.concatenate([wl_scr[0], wg_ref[0, :, 0:head]], axis=1),
                    jnp.concatenate([wl_scr[1], wu_ref[0, :, 0:head]], axis=1), FF_HALF - tail)
        for c in range(FF_HALF // FF_TILE):
            lo = head + FF_TILE * c
            swiglu_cols(wg_ref[0, :, lo:lo + FF_TILE], wu_ref[0, :, lo:lo + FF_TILE], FF_HALF + lo)

    @pl.when(j >= 2)
    def _():
        wdb = wd_ref[0].astype(BF16)
        e = pl.program_id(0)
        for i, (g_ref, y_ref) in enumerate(((gc_ref, yc_ref), (gl_ref, yl_ref))):
            nset = g_ref.shape[0] // N_EXPERTS
            gate = jnp.concatenate(
                [_row_to_col(g_ref[pl.ds(N_EXPERTS * s + e, 1), :][:, :SLOTS_CTX // nset]) for s in range(nset)],
                axis=0)
            y = _mm(h_scr[SLOTS_CTX * i:SLOTS_CTX * (i + 1), :], wdb)
            y_ref[0] = (y * gate).astype(BF16)


def _expert_ffn(xc, xl, gc, gl, w_gate, w_up, w_down):
    assert SLOTS_CTX == SLOTS_LAT and FF_HALF % LANES == 0 and D_MODEL % OUT_TILE == 0
    xs = lambda n: pl.BlockSpec((1, n, D_MODEL), lambda e, j: (e, 0, 0))
    gs = lambda g: pl.BlockSpec(g.shape, lambda e, j: (0, 0))
    last = N_EXPERTS - 1
    up = pl.BlockSpec((1, D_MODEL, FF_HALF),
                      lambda e, j: (jnp.where(j >= 2, jnp.minimum(e + 1, last), e), 0, jnp.where(j == 1, 1, 0)))
    down = pl.BlockSpec((1, D_FF, OUT_TILE), lambda e, j: (e, 0, jnp.maximum(j - 2, 0)))
    ys = lambda n: pl.BlockSpec((1, n, OUT_TILE), lambda e, j: (e, 0, jnp.maximum(j - 2, 0)))
    return pl.pallas_call(
        _ffn_kernel,
        grid=(N_EXPERTS, 2 + D_MODEL // OUT_TILE),
        in_specs=[xs(SLOTS_CTX), xs(SLOTS_LAT), gs(gc), gs(gl), up, up, down],
        out_specs=[ys(SLOTS_CTX), ys(SLOTS_LAT)],
        out_shape=[jax.ShapeDtypeStruct((N_EXPERTS, SLOTS_CTX, D_MODEL), BF16),
                   jax.ShapeDtypeStruct((N_EXPERTS, SLOTS_LAT, D_MODEL), BF16)],
        scratch_shapes=[pltpu.VMEM((SLOTS_CTX + SLOTS_LAT, D_FF), BF16),
                        pltpu.VMEM((2, D_MODEL, LANES), F32)],
        compiler_params=_params(("arbitrary", "arbitrary"), BIG_VMEM_LIMIT),
        name="expert_ffn",
    )(xc, xl, gc, gl, w_gate, w_up, w_down)


def _combine_ctx_kernel(y_ref, idx_ref, x1_ref, mod_ref, lng_ref, lnb_ref, o_ref):
    g2 = _mod_rows(mod_ref, 0)[5]
    nslot = N_EXPERTS * CAP_CTX
    for s in range(CTX_GROUP):
        hit = lax.broadcasted_iota(I32, (SEQ, nslot), 0) == idx_ref[s]
        y = y_ref[:, CAP_CTX * s:CAP_CTX * (s + 1), :].reshape(nslot, D_MODEL)
        f = _mm(jnp.where(hit, 1.0, 0.0).astype(BF16), y)
        o_ref[s] = _layer_norm(DEEPNORM_ALPHA * x1_ref[s] + g2 * f, lng_ref[...], lnb_ref[...])


def _combine_ctx(yc, idx_row, x1, mod, ln_g, ln_b):
    full = lambda shape: pl.BlockSpec(shape, lambda s: (0,) * len(shape))
    return pl.pallas_call(
        _combine_ctx_kernel,
        grid=(BATCH // CTX_GROUP,),
        in_specs=[pl.BlockSpec((N_EXPERTS, CTX_GROUP * CAP_CTX, D_MODEL), lambda s: (0, s, 0)),
                  pl.BlockSpec((CTX_GROUP, 1, N_EXPERTS * CAP_CTX), lambda s: (s, 0, 0)),
                  pl.BlockSpec((CTX_GROUP, SEQ, D_MODEL), lambda s: (s, 0, 0)),
                  full((SUBLANES, 6 * D_MODEL)), full((1, D_MODEL)), full((1, D_MODEL))],
        out_specs=pl.BlockSpec((CTX_GROUP, SEQ, D_MODEL), lambda s: (s, 0, 0)),
        out_shape=jax.ShapeDtypeStruct((BATCH, SEQ, D_MODEL), F32),
        compiler_params=_params(("arbitrary",)),
        name="combine_ctx",
    )(yc, idx_row, x1, mod, ln_g, ln_b)


def _combine_lat_kernel(y_ref, idx_ref, x1_ref, mod_ref, lng_ref, lnb_ref, o_ref):
    s = pl.program_id(0)
    g2 = _mod_rows(mod_ref, 1 + s)[5]
    tok = lax.broadcasted_iota(I32, (ROW_TILE, CAP_LAT), 0) + ROW_TILE * pl.program_id(1)
    f = jnp.zeros((ROW_TILE, D_MODEL), F32)
    for e in range(N_EXPERTS):
        hit = tok == idx_ref[0, e:e + 1, :]
        f = f + _mm(jnp.where(hit, 1.0, 0.0).astype(BF16), y_ref[e])
    o_ref[0] = _layer_norm(DEEPNORM_ALPHA * x1_ref[0] + g2 * f, lng_ref[...], lnb_ref[...])


def _combine_lat(yl, idx, x1, mod, ln_g, ln_b):
    nt = DEC_SEQ // ROW_TILE
    full = lambda shape: pl.BlockSpec(shape, lambda s, t: (0,) * len(shape))
    return pl.pallas_call(
        _combine_lat_kernel,
        grid=(DEC_BATCH, nt),
        in_specs=[pl.BlockSpec((N_EXPERTS, CAP_LAT, D_MODEL), lambda s, t: (0, s, 0)),
                  pl.BlockSpec((1, N_EXPERTS, CAP_LAT), lambda s, t: (s, 0, 0)),
                  pl.BlockSpec((1, ROW_TILE, D_MODEL), lambda s, t: (s, t, 0)),
                  full((SUBLANES, 6 * D_MODEL)), full((1, D_MODEL)), full((1, D_MODEL))],
        out_specs=pl.BlockSpec((1, ROW_TILE, D_MODEL), lambda s, t: (s, t, 0)),
        out_shape=jax.ShapeDtypeStruct((DEC_BATCH, DEC_SEQ, D_MODEL), F32),
        compiler_params=_params(("arbitrary", "arbitrary")),
        name="combine_lat",
    )(yl, idx, x1, mod, ln_g, ln_b)


def _rope_tables():
    t = np.arange(DEC_SEQ)
    row = (t // GRID_W).astype(np.float64)
    col = (t % GRID_W).astype(np.float64)
    half = HEAD_DIM // 2
    inv = ROPE_THETA ** (-np.arange(0, half, 2, dtype=np.float64) / half)
    ang_r = row[:, None] * inv[None, :]
    ang_c = col[:, None] * inv[None, :]
    ang = np.concatenate([ang_r, ang_r, ang_c, ang_c], axis=-1)
    cos = np.tile(np.cos(ang), (1, 2))
    sin = np.tile(np.sin(ang), (1, 2))
    first = (np.arange(LANES) % 32) < 16
    return jnp.asarray(cos, F32), jnp.asarray(np.where(first[None, :], -sin, sin), F32)


def kernel(x_prompt, x_sample, cache_na_k, cache_na_v, cache_diff_k, cache_diff_v, c, c_ctx, w_ada, b_ada, w_in, w_out, na_rel_bias, lambda_q1, lambda_k1, lambda_q2, lambda_k2, subln_g, ln1_g, ln1_b, ln2_g, ln2_b, w_router, w_gate, w_up, w_down):
    l = 0
    cvec = jnp.concatenate([c_ctx[None, :], c, jnp.zeros((SUBLANES - 1 - DEC_BATCH, D_MODEL), F32)], axis=0)
    mod = _modulation(cvec, w_ada[l], b_ada[l][None, :])
    lamv = jnp.stack([lambda_q1[l], lambda_k1[l], lambda_q2[l], lambda_k2[l]], axis=0)
    w_in_b = w_in[l].astype(BF16)
    w_out_b = w_out[l].astype(BF16)
    sub = subln_g[l][None, :]
    wrt = w_router[l].T

    tr = lambda a: jnp.swapaxes(a, -1, -2)

    x1c, h2c, affc, na_k_t, na_v_t, diff_k_t, new_diff_v = _context_block(
        x_prompt, mod, lamv, w_in_b, w_out_b, sub, ln1_g[l][None, :], ln1_b[l][None, :], wrt)

    cos, sin = _rope_tables()
    naq, nak, nav, dfq, dfk, dfv, tbl, stats = _latent_qkv(x_sample, mod, w_in_b, cos, sin,
                                                    na_rel_bias[l].reshape(-1))
    na_o = _latent_na(naq, nak, nav, tr(cache_na_k), tr(cache_na_v), tbl, stats)
    df_o = _latent_diff(dfq, dfk, dfv, tr(cache_diff_k), cache_diff_v, lamv, sub)
    x1l, h2l, affl = _latent_out(x_sample, na_o, df_o, mod, w_out_b,
                                 ln1_g[l][None, :], ln1_b[l][None, :], wrt)

    idxc, gatec, idxl, gatel = _route(affc.reshape(BATCH * N_EXPERTS, SEQ),
                                      affl.reshape(DEC_BATCH * N_EXPERTS, DEC_SEQ))
    idxc = idxc.reshape(BATCH, 1, N_EXPERTS * CAP_CTX)
    idxl = idxl.reshape(DEC_BATCH, N_EXPERTS, CAP_LAT)
    xc = _gather_ctx(idxc, h2c)
    xl = _gather_lat(idxl, h2l)
    yc, yl = _expert_ffn(xc, xl, gatec, gatel, w_gate[l], w_up[l], w_down[l])
    y_prompt = _combine_ctx(yc, idxc, x1c, mod, ln2_g[l][None, :], ln2_b[l][None, :])
    y_sample = _combine_lat(yl, idxl, x1l, mod, ln2_g[l][None, :], ln2_b[l][None, :])
    return (y_prompt, y_sample, tr(na_k_t), tr(na_v_t), tr(diff_k_t), new_diff_v)
```

```python
import functools
import math

import jax
import jax.numpy as jnp
import numpy as np
from jax import lax
from jax.experimental import pallas as pl
from jax.experimental.pallas import tpu as pltpu

F32 = jnp.float32
BF16 = jnp.bfloat16
I32 = jnp.int32

D_MODEL = 1024
BATCH = 16
SEQ = 256
DEC_BATCH = 2
DEC_SEQ = 2048
PAST_LEN = 256
GRID_W = 64
GRID_ROWS = DEC_SEQ // GRID_W
HEAD_DIM = 64
NA_HEADS = 8
DIFF_HEADS = 4
NA_WIDTH = NA_HEADS * HEAD_DIM
DIFF_WIDTH = DIFF_HEADS * 2 * HEAD_DIM
QKV_WIDTH = 3 * NA_WIDTH + 3 * DIFF_WIDTH
NA_WIN_H = 8
NA_WIN_W = 16
N_REL_H = 2 * NA_WIN_H - 1
N_REL_W = 2 * NA_WIN_W - 1
N_EXPERTS = 16
EC_CAPACITY_FACTOR = 2
D_FF = 2816
ROPE_THETA = 10000.0
NORM_EPS = 1e-5
NEG_INF = -1e30
DEPTH = 1
DEEPNORM_ALPHA = (2.0 * DEPTH) ** 0.25
LAMBDA_INIT = 0.8 - 0.6 * math.exp(-0.3 * 0)
SCALE = HEAD_DIM ** -0.5

OFF_NAQ = 0
OFF_NAK = NA_WIDTH
OFF_NAV = 2 * NA_WIDTH
OFF_DFQ = 3 * NA_WIDTH
OFF_DFK = 3 * NA_WIDTH + DIFF_WIDTH
OFF_DFV = 3 * NA_WIDTH + 2 * DIFF_WIDTH

CAP_CTX = EC_CAPACITY_FACTOR * SEQ // N_EXPERTS
CAP_LAT = EC_CAPACITY_FACTOR * DEC_SEQ // N_EXPERTS
SLOTS_CTX = BATCH * CAP_CTX
SLOTS_LAT = DEC_BATCH * CAP_LAT

ROW_TILE = 512
OUT_ROWS = 1024
NA_QUAD = 4
NA_WIN_ROWS = 12
NA_GROUP = 8
FF_TILE = 256
FF_HALF = D_FF // 2
OUT_TILE = 512
LANES = 128
GATHER_GROUP = 4
CTX_GROUP = 4
CTX_REQ = 2
DIFF_ROWS = 1024
Q_ROWS = 256

BOUND_SLACK = 1.0 + 2.0 ** -10
ROUND_UP = 1.0 + 2.0 ** -7
SOFTMAX_FLOOR = 1e-25

SUBLANES = 8
VMEM_LIMIT = 48 * 1024 * 1024
BIG_VMEM_LIMIT = 58 * 1024 * 1024


def _params(sem, vmem=VMEM_LIMIT):
    return pltpu.CompilerParams(dimension_semantics=sem, vmem_limit_bytes=vmem)


def _mm(a, b):
    return jnp.dot(a, b, preferred_element_type=F32)


def _nt(a, b):
    return lax.dot_general(a, b, (((1,), (1,)), ((), ())), preferred_element_type=F32)


def _split(a):
    hi = a.astype(BF16)
    lo = (a - hi.astype(F32)).astype(BF16)
    return hi, lo


def _mm3(a, b):
    ah, al = _split(a)
    bh, bl = _split(b)
    return _mm(ah, bh) + _mm(al, bh) + _mm(ah, bl)


def _nt3(a, b):
    ah, al = _split(a)
    bh, bl = _split(b)
    return _nt(ah, bh) + _nt(al, bh) + _nt(ah, bl)


def _layer_norm(y, g, b):
    mu = jnp.mean(y, axis=-1, keepdims=True)
    d = y - mu
    var = jnp.mean(d * d, axis=-1, keepdims=True)
    return d * lax.rsqrt(var + NORM_EPS) * g + b


def _lam(l_ref):
    l = l_ref[...]
    a = jnp.sum(l[0:1] * l[1:2], axis=-1, keepdims=True)
    b = jnp.sum(l[2:3] * l[3:4], axis=-1, keepdims=True)
    return jnp.exp(a) - jnp.exp(b) + LAMBDA_INIT


def _max_col_norm(kt):
    k = kt.astype(F32)
    n2 = jnp.max(jnp.sum(k * k, axis=0, keepdims=True), axis=-1, keepdims=True)
    return jnp.sqrt(n2) * BOUND_SLACK


def _keys_with_ones(k, axis):
    first = lax.broadcasted_iota(I32, k.shape, axis) == 0
    return jnp.concatenate([k, jnp.where(first, 1.0, 0.0).astype(BF16)], axis=axis)


def _mod_rows(mod_ref, row):
    return [mod_ref[pl.ds(row, 1), i * D_MODEL:(i + 1) * D_MODEL] for i in range(6)]


def _sub_norm(o, sub):
    ms = jnp.mean(o * o, axis=-1, keepdims=True)
    return o * lax.rsqrt(ms + NORM_EPS) * sub * (1.0 - LAMBDA_INIT)


def _router_aff_t(h2, wrt):
    lg = _nt3(wrt, h2)
    m = jnp.max(lg, axis=0, keepdims=True)
    e = jnp.exp(lg - m)
    return e / jnp.sum(e, axis=0, keepdims=True)


def _mod_kernel(c_ref, wa_ref, wb_ref, b_ref, o_ref):
    c = c_ref[...]
    s = c * (1.0 / (1.0 + jnp.exp(-c)))
    half = D_MODEL // 2
    o_ref[...] = _mm3(s[:, :half], wa_ref[...]) + _mm3(s[:, half:], wb_ref[...]) + b_ref[...]


def _modulation(cvec, w_ada, b_ada):
    ncol = 6 * D_MODEL
    half = D_MODEL // 2
    return pl.pallas_call(
        _mod_kernel,
        grid=(6,),
        in_specs=[pl.BlockSpec((SUBLANES, D_MODEL), lambda j: (0, 0)),
                  pl.BlockSpec((half, D_MODEL), lambda j: (0, j)),
                  pl.BlockSpec((half, D_MODEL), lambda j: (1, j)),
                  pl.BlockSpec((1, D_MODEL), lambda j: (0, j))],
        out_specs=pl.BlockSpec((SUBLANES, D_MODEL), lambda j: (0, j)),
        out_shape=jax.ShapeDtypeStruct((SUBLANES, ncol), F32),
        compiler_params=_params(("arbitrary",)),
        name="modulation",
    )(cvec, w_ada, w_ada, b_ada)


def _ctx_kernel(x_ref, mod_ref, lam_ref, win_ref, wout_ref, sub_ref, lng_ref, lnb_ref, wrt_ref,
                x1_ref, h2_ref, aff_ref, nak_ref, nav_ref, dfk_ref, dfv_ref,
                qkv_scr, o_scr, s_scr, e_scr):
    rows = CTX_REQ * SEQ
    x = x_ref[...].reshape(rows, D_MODEL)
    sh1, sc1, g1, sh2, sc2, g2 = _mod_rows(mod_ref, 0)
    h = (x * (1.0 + sc1) + sh1).astype(BF16)
    qkv_scr[...] = _mm(h, win_ref[...])

    kt_na = qkv_scr[:, OFF_NAK:OFF_NAK + NA_WIDTH].T
    kt_df = qkv_scr[:, OFF_DFK:OFF_DFK + DIFF_WIDTH].T
    vt_na = qkv_scr[:, OFF_NAV:OFF_NAV + NA_WIDTH].T
    for r in range(CTX_REQ):
        tok = slice(SEQ * r, SEQ * (r + 1))
        nak_ref[r, 0] = kt_na[:, tok].reshape(NA_HEADS, HEAD_DIM, SEQ)
        nav_ref[r, 0] = vt_na[:, tok].reshape(NA_HEADS, HEAD_DIM, SEQ)
        dfk_ref[r, 0] = kt_df[:, tok].reshape(DIFF_HEADS, 2, HEAD_DIM, SEQ)
        for hh in range(DIFF_HEADS):
            c0 = OFF_DFV + 2 * HEAD_DIM * hh
            dfv_ref[r, 0, hh] = qkv_scr[tok, c0:c0 + 2 * HEAD_DIM]

    kt = jnp.concatenate([kt_na, kt_df], axis=0).astype(BF16)
    q_cols = [OFF_NAQ + HEAD_DIM * i for i in range(NA_HEADS)] + [
        OFF_DFQ + HEAD_DIM * i for i in range(2 * DIFF_HEADS)]
    nmap = len(q_cols)
    for r in range(CTX_REQ):
        tok = slice(SEQ * r, SEQ * (r + 1))
        for i, c in enumerate(q_cols):
            q = (qkv_scr[tok, c:c + HEAD_DIM] * SCALE).astype(BF16)
            s_scr[nmap * r + i] = _mm(q, kt[HEAD_DIM * i:HEAD_DIM * (i + 1), tok])
    s = s_scr[...]
    e = jnp.exp(s - jnp.max(s, axis=-1, keepdims=True))
    rl = 1.0 / jnp.sum(e, axis=-1, keepdims=True)
    e_scr[...] = e.astype(BF16)

    lam = _lam(lam_ref)
    sub = sub_ref[...]
    for r in range(CTX_REQ):
        tok = slice(SEQ * r, SEQ * (r + 1))
        for hh in range(NA_HEADS):
            i = nmap * r + hh
            v = qkv_scr[tok, OFF_NAV + HEAD_DIM * hh:OFF_NAV + HEAD_DIM * (hh + 1)].astype(BF16)
            o_scr[tok, HEAD_DIM * hh:HEAD_DIM * (hh + 1)] = (_mm(e_scr[i], v) * rl[i]).astype(BF16)
        for hh in range(DIFF_HEADS):
            i1 = nmap * r + NA_HEADS + 2 * hh
            c0 = OFF_DFV + 2 * HEAD_DIM * hh
            v = qkv_scr[tok, c0:c0 + 2 * HEAD_DIM].astype(BF16)
            o = _mm(e_scr[i1], v) * rl[i1] - _mm(e_scr[i1 + 1], v) * (lam * rl[i1 + 1])
            c1 = NA_WIDTH + 2 * HEAD_DIM * hh
            o_scr[tok, c1:c1 + 2 * HEAD_DIM] = _sub_norm(o, sub).astype(BF16)

    a = _mm(o_scr[...], wout_ref[...])
    x1 = _layer_norm(DEEPNORM_ALPHA * x + g1 * a, lng_ref[...], lnb_ref[...])
    x1_ref[...] = x1.reshape(CTX_REQ, SEQ, D_MODEL)
    h2 = x1 * (1.0 + sc2) + sh2
    h2_ref[...] = h2.astype(BF16).reshape(CTX_REQ, SEQ, D_MODEL)
    aff = _router_aff_t(h2, wrt_ref[...])
    for r in range(CTX_REQ):
        aff_ref[r] = aff[:, SEQ * r:SEQ * (r + 1)]


def _context_block(x, mod, lamv, w_in, w_out, sub, ln_g, ln_b, wrt):
    full = lambda shape: pl.BlockSpec(shape, lambda b: (0,) * len(shape))
    once = lambda shape: pl.BlockSpec(shape, lambda b: (0,) * len(shape), pipeline_mode=pl.Buffered(1))
    req = lambda *tail: pl.BlockSpec((CTX_REQ,) + tail, lambda b: (b,) + (0,) * len(tail))
    nmaps = CTX_REQ * (NA_HEADS + 2 * DIFF_HEADS)
    return pl.pallas_call(
        _ctx_kernel,
        grid=(BATCH // CTX_REQ,),
        in_specs=[req(SEQ, D_MODEL),
                  full((SUBLANES, 6 * D_MODEL)), full((4, HEAD_DIM)),
                  once((D_MODEL, QKV_WIDTH)), once((D_MODEL, D_MODEL)),
                  full((1, 2 * HEAD_DIM)), full((1, D_MODEL)), full((1, D_MODEL)),
                  full((N_EXPERTS, D_MODEL))],
        out_specs=[req(SEQ, D_MODEL), req(SEQ, D_MODEL), req(N_EXPERTS, SEQ),
                   req(1, NA_HEADS, HEAD_DIM, SEQ), req(1, NA_HEADS, HEAD_DIM, SEQ),
                   req(1, DIFF_HEADS, 2, HEAD_DIM, SEQ), req(1, DIFF_HEADS, SEQ, 2 * HEAD_DIM)],
        out_shape=[jax.ShapeDtypeStruct((BATCH, SEQ, D_MODEL), F32),
                   jax.ShapeDtypeStruct((BATCH, SEQ, D_MODEL), BF16),
                   jax.ShapeDtypeStruct((BATCH, N_EXPERTS, SEQ), F32),
                   jax.ShapeDtypeStruct((BATCH, 1, NA_HEADS, HEAD_DIM, SEQ), F32),
                   jax.ShapeDtypeStruct((BATCH, 1, NA_HEADS, HEAD_DIM, SEQ), F32),
                   jax.ShapeDtypeStruct((BATCH, 1, DIFF_HEADS, 2, HEAD_DIM, SEQ), F32),
                   jax.ShapeDtypeStruct((BATCH, 1, DIFF_HEADS, SEQ, 2 * HEAD_DIM), F32)],
        scratch_shapes=[pltpu.VMEM((CTX_REQ * SEQ, QKV_WIDTH), F32), pltpu.VMEM((CTX_REQ * SEQ, D_MODEL), BF16),
                        pltpu.VMEM((nmaps, SEQ, SEQ), F32), pltpu.VMEM((nmaps, SEQ, SEQ), BF16)],
        compiler_params=_params(("arbitrary",)),
        name="context_block",
    )(x, mod, lamv, w_in, w_out, sub, ln_g, ln_b, wrt)


def _lat_qkv_kernel(x_ref, mod_ref, win_ref, cos_ref, sin_ref, rpb_ref,
                    naq_ref, nak_ref, nav_ref, dfq_ref, dfk_ref, dfv_ref, tbl_ref, stat_ref, qkv_scr):
    b = pl.program_id(0)
    bias_top = _write_bias_tables(rpb_ref, b * pl.num_programs(1) + pl.program_id(1), tbl_ref)
    x = x_ref[0]
    sh1, sc1 = _mod_rows(mod_ref, 1 + b)[:2]
    h = (x * (1.0 + sc1) + sh1).astype(BF16)
    qkv_scr[...] = _mm(h, win_ref[...])

    kb = qkv_scr[:, OFF_NAK:OFF_NAK + NA_WIDTH].astype(BF16).astype(F32)
    col = lax.broadcasted_iota(I32, (NA_WIDTH, LANES), 0) // HEAD_DIM
    head_of = jnp.where(col == lax.broadcasted_iota(I32, (NA_WIDTH, LANES), 1), 1.0, 0.0).astype(BF16)
    kn2 = jnp.max(_mm((kb * kb).astype(BF16), head_of), axis=0, keepdims=True) * ROUND_UP
    stat_ref[0] = jnp.concatenate([kn2, jnp.broadcast_to(bias_top, (1, LANES)),
                                   jnp.zeros((SUBLANES - 2, LANES), F32)], axis=0)

    qb = (qkv_scr[:, OFF_NAQ:OFF_NAQ + NA_WIDTH] * SCALE).astype(BF16)
    qf = qb.astype(F32)
    qn = jnp.sqrt(_mm((qf * qf).astype(BF16), head_of) * ROUND_UP) * ROUND_UP
    lane64 = lax.broadcasted_iota(I32, (ROW_TILE, HEAD_DIM), 1)
    key_ext = jnp.where(lane64 < 2, 1.0, 0.0).astype(BF16)
    for hh in range(NA_HEADS):
        lo, hi = HEAD_DIM * hh, HEAD_DIM * (hh + 1)
        ext = jnp.where(lane64 == 0, -qn[:, hh:hh + 1], jnp.where(lane64 == 1, -1.0, 0.0)).astype(BF16)
        naq_ref[0, hh] = jnp.concatenate([qb[:, lo:hi], ext], axis=1)
        nak_ref[0, hh] = jnp.concatenate([qkv_scr[:, OFF_NAK + lo:OFF_NAK + hi].astype(BF16), key_ext], axis=1)
        nav_ref[0, hh] = qkv_scr[:, OFF_NAV + lo:OFF_NAV + hi].astype(BF16)

    cos = cos_ref[...]
    sin = sin_ref[...]
    lane = lax.broadcasted_iota(I32, (ROW_TILE, LANES), 1)
    first = (lane & 31) < 16

    def rope(t):
        rot = jnp.where(first, pltpu.roll(t, LANES - 16, 1), pltpu.roll(t, 16, 1))
        return t * cos + rot * sin

    map_of = jnp.where(lax.broadcasted_iota(I32, (LANES, LANES), 0) // HEAD_DIM
                       == lax.broadcasted_iota(I32, (LANES, LANES), 1), 1.0, 0.0).astype(BF16)
    for hh in range(DIFF_HEADS):
        lo = 2 * HEAD_DIM * hh
        q = (rope(qkv_scr[:, OFF_DFQ + lo:OFF_DFQ + lo + LANES]) * SCALE).astype(BF16)
        k = rope(qkv_scr[:, OFF_DFK + lo:OFF_DFK + lo + LANES])
        qf = q.astype(F32)
        qn = jnp.sqrt(_mm((qf * qf).astype(BF16), map_of) * ROUND_UP) * ROUND_UP
        for m in range(2):
            ext = jnp.where(lane64 == 0, -qn[:, m:m + 1], 0.0).astype(BF16)
            dfq_ref[0, hh, m] = jnp.concatenate([q[:, HEAD_DIM * m:HEAD_DIM * (m + 1)], ext], axis=1)
        dfk_ref[0, hh] = k.T.astype(BF16).reshape(2, HEAD_DIM, ROW_TILE)
        dfv_ref[0, hh] = qkv_scr[:, OFF_DFV + lo:OFF_DFV + lo + LANES].astype(BF16)


def _latent_qkv(x, mod, w_in, cos, sin, rpb_flat):
    nt = DEC_SEQ // ROW_TILE
    assert DEC_BATCH * nt == NA_HEADS
    tbl_shape = (NA_HEADS, 3, NA_QUAD * GRID_W, NA_WIN_ROWS * GRID_W)
    full = lambda shape: pl.BlockSpec(shape, lambda b, t: (0,) * len(shape))
    hs = lambda width: pl.BlockSpec((1, NA_HEADS, ROW_TILE, width), lambda b, t: (b, 0, t, 0))
    na_shape = lambda width: jax.ShapeDtypeStruct((DEC_BATCH, NA_HEADS, DEC_SEQ, width), BF16)
    return pl.pallas_call(
        _lat_qkv_kernel,
        grid=(DEC_BATCH, nt),
        in_specs=[pl.BlockSpec((1, ROW_TILE, D_MODEL), lambda b, t: (b, t, 0)),
                  full((SUBLANES, 6 * D_MODEL)), full((D_MODEL, QKV_WIDTH)),
                  pl.BlockSpec((ROW_TILE, LANES), lambda b, t: (t, 0)),
                  pl.BlockSpec((ROW_TILE, LANES), lambda b, t: (t, 0)),
                  pl.BlockSpec(memory_space=pltpu.SMEM)],
        out_specs=[hs(2 * HEAD_DIM), hs(2 * HEAD_DIM), hs(HEAD_DIM),
                   pl.BlockSpec((1, DIFF_HEADS, 2, ROW_TILE, 2 * HEAD_DIM), lambda b, t: (b, 0, 0, t, 0)),
                   pl.BlockSpec((1, DIFF_HEADS, 2, HEAD_DIM, ROW_TILE), lambda b, t: (b, 0, 0, 0, t)),
                   pl.BlockSpec((1, DIFF_HEADS, ROW_TILE, 2 * HEAD_DIM), lambda b, t: (b, 0, t, 0)),
                   pl.BlockSpec((1,) + tbl_shape[1:], lambda b, t: (b * nt + t, 0, 0, 0)),
                   pl.BlockSpec((1, SUBLANES, LANES), lambda b, t: (b * nt + t, 0, 0))],
        out_shape=[na_shape(2 * HEAD_DIM), na_shape(2 * HEAD_DIM), na_shape(HEAD_DIM)]
        + [jax.ShapeDtypeStruct((DEC_BATCH, DIFF_HEADS, 2, DEC_SEQ, 2 * HEAD_DIM), BF16),
           jax.ShapeDtypeStruct((DEC_BATCH, DIFF_HEADS, 2, HEAD_DIM, DEC_SEQ), BF16),
           jax.ShapeDtypeStruct((DEC_BATCH, DIFF_HEADS, DEC_SEQ, 2 * HEAD_DIM), BF16),
           jax.ShapeDtypeStruct(tbl_shape, F32),
           jax.ShapeDtypeStruct((NA_HEADS, SUBLANES, LANES), F32)],
        scratch_shapes=[pltpu.VMEM((ROW_TILE, QKV_WIDTH), F32)],
        compiler_params=_params(("arbitrary", "arbitrary")),
        name="latent_qkv",
    )(x, mod, w_in, cos, sin, rpb_flat)


def _na_row_offset(kind, qr, kr):
    if kind == 0:
        return kr - qr + 7 if kr < NA_WIN_H else None
    if kind == 1:
        return kr - qr + 3 if qr <= kr < qr + NA_WIN_H else None
    return kr - qr - 1 if NA_WIN_ROWS - NA_WIN_H <= kr else None


def _write_bias_tables(rpb_ref, h, o_ref):
    base = h * (N_REL_H * N_REL_W)
    wq = lax.broadcasted_iota(I32, (GRID_W, LANES), 0)
    c = lax.broadcasted_iota(I32, (GRID_W, LANES), 1)
    wk = c & (GRID_W - 1)
    right = c >= GRID_W
    c0 = jnp.clip(wq - NA_WIN_W // 2, 0, GRID_W - NA_WIN_W)
    in_win = (wk >= c0) & (wk < c0 + NA_WIN_W)
    cache = {}
    strips = {}

    def strip(r):
        if r not in strips:
            c8 = lax.broadcasted_iota(I32, (SUBLANES, LANES), 1)
            d = jnp.where(c8 < GRID_W, c8, c8 - LANES)
            co_of = jnp.clip(d, -(NA_WIN_W - 1), NA_WIN_W - 1) + (NA_WIN_W - 1)
            acc = jnp.zeros((SUBLANES, LANES), F32)
            for co in range(N_REL_W):
                acc = jnp.where(co_of == co, rpb_ref[base + r * N_REL_W + co], acc)
            strips[r] = jnp.concatenate([acc] * (GRID_W // SUBLANES), axis=0)
        return strips[r]

    def tile(rl, rr):
        if (rl, rr) in cache:
            return cache[(rl, rr)]
        if rl is None and rr is None:
            t = jnp.full((GRID_W, LANES), NEG_INF, F32)
        else:
            zero = jnp.zeros((GRID_W, LANES), F32)
            tl = pltpu.roll(strip(rl), 0, 1, stride=1, stride_axis=0) if rl is not None else zero
            tr = pltpu.roll(strip(rr), GRID_W, 1, stride=1, stride_axis=0) if rr is not None else zero
            acc = jnp.where(right, tr, tl)
            ok = in_win
            if rl is None:
                ok = ok & right
            if rr is None:
                ok = ok & jnp.logical_not(right)
            t = jnp.where(ok, acc, NEG_INF)
        cache[(rl, rr)] = t
        return t

    for kind in range(3):
        for qr in range(NA_QUAD):
            row = [tile(_na_row_offset(kind, qr, 2 * p), _na_row_offset(kind, qr, 2 * p + 1))
                   for p in range(NA_WIN_ROWS // 2)]
            o_ref[0, kind, GRID_W * qr:GRID_W * (qr + 1), :] = jnp.concatenate(row, axis=1)
    top = functools.reduce(jnp.maximum, cache.values())
    return jnp.max(jnp.max(top, axis=-1, keepdims=True), axis=0, keepdims=True)


def _na_kernel(q_ref, k_ref, v_ref, ck_ref, cv_ref, tbl_ref, stat_ref, o_ref):
    b = pl.program_id(0)
    g = pl.program_id(2)
    u0 = jnp.clip(NA_QUAD * g - NA_QUAD, 0, GRID_ROWS - NA_WIN_ROWS)
    start = pl.multiple_of(u0 * GRID_W, GRID_W)
    nwin = NA_WIN_ROWS * GRID_W
    tiles = stat_ref.shape[0] // DEC_BATCH
    kn2 = functools.reduce(jnp.maximum, [stat_ref[b * tiles + t, 0:1, :] for t in range(tiles)])

    lane = lax.broadcasted_iota(I32, (1, 2 * HEAD_DIM), 1)
    sub = lax.broadcasted_iota(I32, (HEAD_DIM, PAST_LEN), 0)
    ones_rows = jnp.where(sub < 2, 1.0, 0.0).astype(BF16)

    def attend(exact_max):
        lmin = None
        for j in range(NA_GROUP):
            q = q_ref[0, j]
            kw = k_ref[0, j, pl.ds(start, nwin), :]
            vw = v_ref[0, j, pl.ds(start, nwin), :]
            kct = ck_ref[0, 0, j].astype(BF16)
            vct = cv_ref[0, 0, j].astype(BF16)
            kct1 = jnp.concatenate([kct, ones_rows], axis=0)
            if exact_max:
                q0 = (q.astype(F32) * jnp.where(lane < HEAD_DIM, 1.0, 0.0)).astype(BF16)
                sl = _nt(q0, kw) + tbl_ref[j, 0]
                sc = _mm(q[:, :HEAD_DIM], kct)
                m = jnp.maximum(jnp.max(sl, axis=-1, keepdims=True), jnp.max(sc, axis=-1, keepdims=True))
                el = jnp.exp(sl - m)
                ec = jnp.exp(sc - m)
            else:
                knorm = jnp.maximum(jnp.sqrt(kn2[:, j:j + 1]) * BOUND_SLACK, _max_col_norm(kct)) * ROUND_UP
                btop = jnp.maximum(stat_ref[j, 1:2, 0:1], 0.0) * ROUND_UP
                scale = jnp.where(lane < HEAD_DIM, 1.0, jnp.where(lane == HEAD_DIM, knorm, btop))
                qs = (q.astype(F32) * scale).astype(BF16)
                el = jnp.exp(_nt(qs, kw) + tbl_ref[j, 0])
                ec = jnp.exp(_mm(qs, kct1))
            l = jnp.sum(el, axis=-1, keepdims=True) + jnp.sum(ec, axis=-1, keepdims=True)
            o = (_mm(el.astype(BF16), vw) + _nt(ec.astype(BF16), vct)) / l
            o_ref[0, :, HEAD_DIM * j:HEAD_DIM * (j + 1)] = o.astype(BF16)
            lmin = jnp.min(l) if lmin is None else jnp.minimum(lmin, jnp.min(l))
        return lmin

    lmin = attend(exact_max=False)

    @pl.when(jnp.logical_not(lmin >= SOFTMAX_FLOOR))
    def _():
        attend(exact_max=True)


def _latent_na(naq, nak, nav, cache_k, cache_v, tbl, stats):
    assert NA_GROUP == NA_HEADS
    nq = GRID_ROWS // NA_QUAD
    rows = NA_QUAD * GRID_W
    kind = lambda g: jnp.minimum(g, 1) + g // (nq - 1)
    return pl.pallas_call(
        _na_kernel,
        grid=(DEC_BATCH, NA_HEADS // NA_GROUP, nq),
        in_specs=[pl.BlockSpec((1, NA_GROUP, rows, 2 * HEAD_DIM), lambda b, p, g: (b, p, g, 0)),
                  pl.BlockSpec((1, NA_GROUP, DEC_SEQ, 2 * HEAD_DIM), lambda b, p, g: (b, p, 0, 0)),
                  pl.BlockSpec((1, NA_GROUP, DEC_SEQ, HEAD_DIM), lambda b, p, g: (b, p, 0, 0)),
                  pl.BlockSpec((1, 1, NA_GROUP, HEAD_DIM, PAST_LEN), lambda b, p, g: (b, 0, p, 0, 0)),
                  pl.BlockSpec((1, 1, NA_GROUP, HEAD_DIM, PAST_LEN), lambda b, p, g: (b, 0, p, 0, 0)),
                  pl.BlockSpec((NA_GROUP, 1, rows, NA_WIN_ROWS * GRID_W), lambda b, p, g: (p, kind(g), 0, 0)),
                  pl.BlockSpec(stats.shape, lambda b, p, g: (0, 0, 0))],
        out_specs=pl.BlockSpec((1, rows, NA_GROUP * HEAD_DIM), lambda b, p, g: (b, g, p)),
        out_shape=jax.ShapeDtypeStruct((DEC_BATCH, DEC_SEQ, NA_WIDTH), BF16),
        compiler_params=_params(("arbitrary",) * 3),
        name="latent_na",
    )(naq, nak, nav, cache_k, cache_v, tbl, stats)


def _diff_kernel(q_ref, kt_ref, v_ref, ck_ref, cv_ref, lam_ref, sub_ref, o_ref):
    lam = _lam(lam_ref)
    vc = cv_ref[0, 0, 0].astype(BF16)
    vl = v_ref[0, 0]
    kcs = [ck_ref[0, 0, 0, m].astype(BF16) for m in range(2)]
    kls = [kt_ref[0, 0, m] for m in range(2)]
    knorm = [jnp.maximum(_max_col_norm(kcs[m]), _max_col_norm(kls[m])) for m in range(2)]
    kcs1 = [_keys_with_ones(k, 0) for k in kcs]
    kls1 = [_keys_with_ones(k, 0) for k in kls]
    lane = lax.broadcasted_iota(I32, (1, 2 * HEAD_DIM), 1)
    scale = [jnp.where(lane < HEAD_DIM, 1.0, jnp.where(lane == HEAD_DIM, kn * ROUND_UP, 0.0))
             for kn in knorm]

    def attend(exact_max):
        lmin = None
        for r in range(DIFF_ROWS // Q_ROWS):
            rows = slice(Q_ROWS * r, Q_ROWS * (r + 1))
            parts = []
            for m in range(2):
                q = q_ref[0, 0, m, rows, :]
                if exact_max:
                    sc = _mm(q[:, :HEAD_DIM], kcs[m])
                    sl = _mm(q[:, :HEAD_DIM], kls[m])
                    mx = jnp.maximum(jnp.max(sc, axis=-1, keepdims=True), jnp.max(sl, axis=-1, keepdims=True))
                    ec = jnp.exp(sc - mx)
                    el = jnp.exp(sl - mx)
                else:
                    qs = (q.astype(F32) * scale[m]).astype(BF16)
                    ec = jnp.exp(_mm(qs, kcs1[m]))
                    el = jnp.exp(_mm(qs, kls1[m]))
                l = jnp.sum(ec, axis=-1, keepdims=True) + jnp.sum(el, axis=-1, keepdims=True)
                parts.append((_mm(ec.astype(BF16), vc) + _mm(el.astype(BF16), vl), l))
                lmin = jnp.min(l) if lmin is None else jnp.minimum(lmin, jnp.min(l))
            (a1, l1), (a2, l2) = parts
            o = a1 * (1.0 / l1) - a2 * (lam / l2)
            o_ref[0, rows, :] = _sub_norm(o, sub_ref[...]).astype(BF16)
        return lmin

    lmin = attend(exact_max=False)

    @pl.when(jnp.logical_not(lmin >= SOFTMAX_FLOOR))
    def _():
        attend(exact_max=True)


def _latent_diff(dfq, dfk, dfv, cache_k, cache_v, lamv, sub):
    nt = DEC_SEQ // DIFF_ROWS
    full = lambda shape: pl.BlockSpec(shape, lambda b, h, t: (0,) * len(shape))
    return pl.pallas_call(
        _diff_kernel,
        grid=(DEC_BATCH, DIFF_HEADS, nt),
        in_specs=[pl.BlockSpec((1, 1, 2, DIFF_ROWS, 2 * HEAD_DIM), lambda b, h, t: (b, h, 0, t, 0)),
                  pl.BlockSpec((1, 1, 2, HEAD_DIM, DEC_SEQ), lambda b, h, t: (b, h, 0, 0, 0)),
                  pl.BlockSpec((1, 1, DEC_SEQ, 2 * HEAD_DIM), lambda b, h, t: (b, h, 0, 0)),
                  pl.BlockSpec((1, 1, 1, 2, HEAD_DIM, PAST_LEN), lambda b, h, t: (b, 0, h, 0, 0, 0)),
                  pl.BlockSpec((1, 1, 1, PAST_LEN, 2 * HEAD_DIM), lambda b, h, t: (b, 0, h, 0, 0)),
                  full((4, HEAD_DIM)), full((1, 2 * HEAD_DIM))],
        out_specs=pl.BlockSpec((1, DIFF_ROWS, 2 * HEAD_DIM), lambda b, h, t: (b, t, h)),
        out_shape=jax.ShapeDtypeStruct((DEC_BATCH, DEC_SEQ, DIFF_WIDTH), BF16),
        compiler_params=_params(("arbitrary",) * 3),
        name="latent_diff",
    )(dfq, dfk, dfv, cache_k, cache_v, lamv, sub)


def _lat_out_kernel(x_ref, na_ref, df_ref, mod_ref, wout_ref, lng_ref, lnb_ref, wrt_ref,
                    x1_ref, h2_ref, aff_ref):
    b = pl.program_id(0)
    x = x_ref[0]
    _, _, g1, sh2, sc2, _ = _mod_rows(mod_ref, 1 + b)
    a = _mm(na_ref[0], wout_ref[0:NA_WIDTH, :]) + _mm(df_ref[0], wout_ref[NA_WIDTH:, :])
    x1 = _layer_norm(DEEPNORM_ALPHA * x + g1 * a, lng_ref[...], lnb_ref[...])
    x1_ref[0] = x1
    h2 = x1 * (1.0 + sc2) + sh2
    h2_ref[0] = h2.astype(BF16)
    aff_ref[0] = _router_aff_t(h2, wrt_ref[...])


def _latent_out(x, na_o, df_o, mod, w_out, ln_g, ln_b, wrt):
    nt = DEC_SEQ // OUT_ROWS
    full = lambda shape: pl.BlockSpec(shape, lambda b, t: (0,) * len(shape))
    tok = lambda w: pl.BlockSpec((1, OUT_ROWS, w), lambda b, t: (b, t, 0))
    return pl.pallas_call(
        _lat_out_kernel,
        grid=(DEC_BATCH, nt),
        in_specs=[tok(D_MODEL), tok(NA_WIDTH), tok(DIFF_WIDTH), full((SUBLANES, 6 * D_MODEL)),
                  full((D_MODEL, D_MODEL)), full((1, D_MODEL)), full((1, D_MODEL)),
                  full((N_EXPERTS, D_MODEL))],
        out_specs=[tok(D_MODEL), tok(D_MODEL),
                   pl.BlockSpec((1, N_EXPERTS, OUT_ROWS), lambda b, t: (b, 0, t))],
        out_shape=[jax.ShapeDtypeStruct((DEC_BATCH, DEC_SEQ, D_MODEL), F32),
                   jax.ShapeDtypeStruct((DEC_BATCH, DEC_SEQ, D_MODEL), BF16),
                   jax.ShapeDtypeStruct((DEC_BATCH, N_EXPERTS, DEC_SEQ), F32)],
        compiler_params=_params(("arbitrary", "arbitrary")),
        name="latent_out",
    )(x, na_o, df_o, mod, w_out, ln_g, ln_b, wrt)


def _ranks_first(va, ia, vb, ib):
    return (va > vb) | ((va == vb) & (ia < ib))


def _sort_rows_desc(aff):
    rows, n = aff.shape
    nb = n // LANES
    lane = lax.broadcasted_iota(I32, (rows, LANES), 1)
    v = [aff[:, LANES * b:LANES * (b + 1)] for b in range(nb)]
    ix = [lane + LANES * b for b in range(nb)]
    k = 2
    while k <= n:
        j = k // 2
        while j >= 1:
            if j >= LANES:
                sb = j // LANES
                for lo in range(nb):
                    if lo & sb:
                        continue
                    hi = lo | sb
                    f = _ranks_first(v[lo], ix[lo], v[hi], ix[hi])
                    if k < n and (LANES * lo) & k:
                        f = jnp.logical_not(f)
                    v[lo], v[hi] = jnp.where(f, v[lo], v[hi]), jnp.where(f, v[hi], v[lo])
                    ix[lo], ix[hi] = jnp.where(f, ix[lo], ix[hi]), jnp.where(f, ix[hi], ix[lo])
            else:
                upper = (lane & j) != 0
                if k < LANES:
                    flip_lanes = jnp.logical_xor(upper, (lane & k) != 0)
                for b in range(nb):
                    pv = jnp.where(upper, pltpu.roll(v[b], j, 1), pltpu.roll(v[b], LANES - j, 1))
                    pi = jnp.where(upper, pltpu.roll(ix[b], j, 1), pltpu.roll(ix[b], LANES - j, 1))
                    f = _ranks_first(v[b], ix[b], pv, pi)
                    if k < LANES:
                        flip = flip_lanes
                    elif k < n and (LANES * b) & k:
                        flip = jnp.logical_not(upper)
                    else:
                        flip = upper
                    keep = jnp.logical_xor(f, flip)
                    v[b] = jnp.where(keep, v[b], pv)
                    ix[b] = jnp.where(keep, ix[b], pi)
            j //= 2
        k *= 2
    return v, ix


def _route_kernel(affc_ref, affl_ref, idxc_ref, gatec_ref, idxl_ref, gatel_ref):
    for aff_ref, idx_ref, gate_ref, cap in ((affc_ref, idxc_ref, gatec_ref, CAP_CTX),
                                            (affl_ref, idxl_ref, gatel_ref, CAP_LAT)):
        v, ix = _sort_rows_desc(aff_ref[...])
        if cap < LANES:
            idx_ref[...] = ix[0][:, :cap]
            gate_ref[...] = v[0]
        else:
            idx_ref[...] = jnp.concatenate(ix[:cap // LANES], axis=1)
            gate_ref[...] = jnp.concatenate(v[:cap // LANES], axis=1)


def _route(affc_t, affl_t):
    out = lambda a, cap: [jax.ShapeDtypeStruct((a.shape[0], cap), I32),
                          jax.ShapeDtypeStruct((a.shape[0], max(cap, LANES)), F32)]
    return pl.pallas_call(
        _route_kernel,
        out_shape=out(affc_t, CAP_CTX) + out(affl_t, CAP_LAT),
        compiler_params=_params(None),
        name="route",
    )(affc_t, affl_t)


def _row_to_col(row):
    n = row.shape[1]
    eye = lax.broadcasted_iota(I32, (n, n), 0) == lax.broadcasted_iota(I32, (n, n), 1)
    return jnp.sum(jnp.where(eye, row, jnp.zeros_like(row)), axis=1, keepdims=True)


def _gather_ctx_kernel(idx_ref, h_ref, x_ref):
    nslot = N_EXPERTS * CAP_CTX
    for s in range(CTX_GROUP):
        hit = lax.broadcasted_iota(I32, (nslot, SEQ), 1) == _row_to_col(idx_ref[s])
        xs = _mm(jnp.where(hit, 1.0, 0.0).astype(BF16), h_ref[s])
        x_ref[:, CAP_CTX * s:CAP_CTX * (s + 1), :] = xs.astype(BF16).reshape(N_EXPERTS, CAP_CTX, D_MODEL)


def _gather_ctx(idx_row, h2):
    nslot = N_EXPERTS * CAP_CTX
    return pl.pallas_call(
        _gather_ctx_kernel,
        grid=(BATCH // CTX_GROUP,),
        in_specs=[pl.BlockSpec((CTX_GROUP, 1, nslot), lambda s: (s, 0, 0)),
                  pl.BlockSpec((CTX_GROUP, SEQ, D_MODEL), lambda s: (s, 0, 0))],
        out_specs=pl.BlockSpec((N_EXPERTS, CTX_GROUP * CAP_CTX, D_MODEL), lambda s: (0, s, 0)),
        out_shape=jax.ShapeDtypeStruct((N_EXPERTS, SLOTS_CTX, D_MODEL), BF16),
        compiler_params=_params(("arbitrary",)),
        name="gather_ctx",
    )(idx_row, h2)


def _gather_lat_kernel(idx_ref, h_ref, x_ref):
    first = GATHER_GROUP * pl.program_id(1)
    col = jnp.concatenate([_row_to_col(idx_ref[0, pl.ds(first + i, 1), :]) for i in range(GATHER_GROUP)],
                          axis=0)
    hit = lax.broadcasted_iota(I32, (GATHER_GROUP * CAP_LAT, DEC_SEQ), 1) == col
    xs = _mm(jnp.where(hit, 1.0, 0.0).astype(BF16), h_ref[0])
    x_ref[...] = xs.astype(BF16).reshape(GATHER_GROUP, CAP_LAT, D_MODEL)


def _gather_lat(idx, h2):
    return pl.pallas_call(
        _gather_lat_kernel,
        grid=(DEC_BATCH, N_EXPERTS // GATHER_GROUP),
        in_specs=[pl.BlockSpec((1, N_EXPERTS, CAP_LAT), lambda s, e: (s, 0, 0)),
                  pl.BlockSpec((1, DEC_SEQ, D_MODEL), lambda s, e: (s, 0, 0))],
        out_specs=pl.BlockSpec((GATHER_GROUP, CAP_LAT, D_MODEL), lambda s, e: (e, s, 0)),
        out_shape=jax.ShapeDtypeStruct((N_EXPERTS, SLOTS_LAT, D_MODEL), BF16),
        compiler_params=_params(("arbitrary", "arbitrary")),
        name="gather_lat",
    )(idx, h2)


def _ffn_kernel(xc_ref, xl_ref, gc_ref, gl_ref, wg_ref, wu_ref, wd_ref, yc_ref, yl_ref, h_scr, wl_scr):
    j = pl.program_id(1)
    tail = FF_HALF - FF_HALF // FF_TILE * FF_TILE

    def swiglu_cols(wg, wu, c0):
        wgb = wg.astype(BF16)
        wub = wu.astype(BF16)
        for i, x_ref in enumerate((xc_ref, xl_ref)):
            x = x_ref[0]
            a = _mm(x, wgb)
            u = _mm(x, wub)
            h = a * (1.0 / (1.0 + jnp.exp(-a))) * u
            h_scr[SLOTS_CTX * i:SLOTS_CTX * (i + 1), c0:c0 + FF_TILE] = h.astype(BF16)

    @pl.when(j == 0)
    def _():
        for c in range(FF_HALF // FF_TILE):
            lo = FF_TILE * c
            swiglu_cols(wg_ref[0, :, lo:lo + FF_TILE], wu_ref[0, :, lo:lo + FF_TILE], lo)
        wl_scr[0] = wg_ref[0, :, FF_HALF - tail:FF_HALF]
        wl_scr[1] = wu_ref[0, :, FF_HALF - tail:FF_HALF]

    @pl.when(j == 1)
    def _():
        head = FF_TILE - tail
        swiglu_cols(jnp.concatenate([wl_scr[0], wg_ref[0, :, 0:head]], axis=1),
                    jnp.concatenate([wl_scr[1], wu_ref[0, :, 0:head]], axis=1), FF_HALF - tail)
        for c in range(FF_HALF // FF_TILE):
            lo = head + FF_TILE * c
            swiglu_cols(wg_ref[0, :, lo:lo + FF_TILE], wu_ref[0, :, lo:lo + FF_TILE], FF_HALF + lo)

    @pl.when(j >= 2)
    def _():
        wdb = wd_ref[0].astype(BF16)
        e = pl.program_id(0)
        for i, (g_ref, y_ref) in enumerate(((gc_ref, yc_ref), (gl_ref, yl_ref))):
            nset = g_ref.shape[0] // N_EXPERTS
            gate = jnp.concatenate(
                [_row_to_col(g_ref[pl.ds(N_EXPERTS * s + e, 1), :][:, :SLOTS_CTX // nset]) for s in range(nset)],
                axis=0)
            y = _mm(h_scr[SLOTS_CTX * i:SLOTS_CTX * (i + 1), :], wdb)
            y_ref[0] = (y * gate).astype(BF16)


def _expert_ffn(xc, xl, gc, gl, w_gate, w_up, w_down):
    assert SLOTS_CTX == SLOTS_LAT and FF_HALF % LANES == 0 and D_MODEL % OUT_TILE == 0
    xs = lambda n: pl.BlockSpec((1, n, D_MODEL), lambda e, j: (e, 0, 0))
    gs = lambda g: pl.BlockSpec(g.shape, lambda e, j: (0, 0))
    last = N_EXPERTS - 1
    up = pl.BlockSpec((1, D_MODEL, FF_HALF),
                      lambda e, j: (jnp.where(j >= 2, jnp.minimum(e + 1, last), e), 0, jnp.where(j == 1, 1, 0)))
    down = pl.BlockSpec((1, D_FF, OUT_TILE), lambda e, j: (e, 0, jnp.maximum(j - 2, 0)))
    ys = lambda n: pl.BlockSpec((1, n, OUT_TILE), lambda e, j: (e, 0, jnp.maximum(j - 2, 0)))
    return pl.pallas_call(
        _ffn_kernel,
        grid=(N_EXPERTS, 2 + D_MODEL // OUT_TILE),
        in_specs=[xs(SLOTS_CTX), xs(SLOTS_LAT), gs(gc), gs(gl), up, up, down],
        out_specs=[ys(SLOTS_CTX), ys(SLOTS_LAT)],
        out_shape=[jax.ShapeDtypeStruct((N_EXPERTS, SLOTS_CTX, D_MODEL), BF16),
                   jax.ShapeDtypeStruct((N_EXPERTS, SLOTS_LAT, D_MODEL), BF16)],
        scratch_shapes=[pltpu.VMEM((SLOTS_CTX + SLOTS_LAT, D_FF), BF16),
                        pltpu.VMEM((2, D_MODEL, LANES), F32)],
        compiler_params=_params(("arbitrary", "arbitrary"), BIG_VMEM_LIMIT),
        name="expert_ffn",
    )(xc, xl, gc, gl, w_gate, w_up, w_down)


def _combine_ctx_kernel(y_ref, idx_ref, x1_ref, mod_ref, lng_ref, lnb_ref, o_ref):
    g2 = _mod_rows(mod_ref, 0)[5]
    nslot = N_EXPERTS * CAP_CTX
    for s in range(CTX_GROUP):
        hit = lax.broadcasted_iota(I32, (SEQ, nslot), 0) == idx_ref[s]
        y = y_ref[:, CAP_CTX * s:CAP_CTX * (s + 1), :].reshape(nslot, D_MODEL)
        f = _mm(jnp.where(hit, 1.0, 0.0).astype(BF16), y)
        o_ref[s] = _layer_norm(DEEPNORM_ALPHA * x1_ref[s] + g2 * f, lng_ref[...], lnb_ref[...])


def _combine_ctx(yc, idx_row, x1, mod, ln_g, ln_b):
    full = lambda shape: pl.BlockSpec(shape, lambda s: (0,) * len(shape))
    return pl.pallas_call(
        _combine_ctx_kernel,
        grid=(BATCH // CTX_GROUP,),
        in_specs=[pl.BlockSpec((N_EXPERTS, CTX_GROUP * CAP_CTX, D_MODEL), lambda s: (0, s, 0)),
                  pl.BlockSpec((CTX_GROUP, 1, N_EXPERTS * CAP_CTX), lambda s: (s, 0, 0)),
                  pl.BlockSpec((CTX_GROUP, SEQ, D_MODEL), lambda s: (s, 0, 0)),
                  full((SUBLANES, 6 * D_MODEL)), full((1, D_MODEL)), full((1, D_MODEL))],
        out_specs=pl.BlockSpec((CTX_GROUP, SEQ, D_MODEL), lambda s: (s, 0, 0)),
        out_shape=jax.ShapeDtypeStruct((BATCH, SEQ, D_MODEL), F32),
        compiler_params=_params(("arbitrary",)),
        name="combine_ctx",
    )(yc, idx_row, x1, mod, ln_g, ln_b)


def _combine_lat_kernel(y_ref, idx_ref, x1_ref, mod_ref, lng_ref, lnb_ref, o_ref):
    s = pl.program_id(0)
    g2 = _mod_rows(mod_ref, 1 + s)[5]
    tok = lax.broadcasted_iota(I32, (ROW_TILE, CAP_LAT), 0) + ROW_TILE * pl.program_id(1)
    f = jnp.zeros((ROW_TILE, D_MODEL), F32)
    for e in range(N_EXPERTS):
        hit = tok == idx_ref[0, e:e + 1, :]
        f = f + _mm(jnp.where(hit, 1.0, 0.0).astype(BF16), y_ref[e])
    o_ref[0] = _layer_norm(DEEPNORM_ALPHA * x1_ref[0] + g2 * f, lng_ref[...], lnb_ref[...])


def _combine_lat(yl, idx, x1, mod, ln_g, ln_b):
    nt = DEC_SEQ // ROW_TILE
    full = lambda shape: pl.BlockSpec(shape, lambda s, t: (0,) * len(shape))
    return pl.pallas_call(
        _combine_lat_kernel,
        grid=(DEC_BATCH, nt),
        in_specs=[pl.BlockSpec((N_EXPERTS, CAP_LAT, D_MODEL), lambda s, t: (0, s, 0)),
                  pl.BlockSpec((1, N_EXPERTS, CAP_LAT), lambda s, t: (s, 0, 0)),
                  pl.BlockSpec((1, ROW_TILE, D_MODEL), lambda s, t: (s, t, 0)),
                  full((SUBLANES, 6 * D_MODEL)), full((1, D_MODEL)), full((1, D_MODEL))],
        out_specs=pl.BlockSpec((1, ROW_TILE, D_MODEL), lambda s, t: (s, t, 0)),
        out_shape=jax.ShapeDtypeStruct((DEC_BATCH, DEC_SEQ, D_MODEL), F32),
        compiler_params=_params(("arbitrary", "arbitrary")),
        name="combine_lat",
    )(yl, idx, x1, mod, ln_g, ln_b)


def _rope_tables():
    t = np.arange(DEC_SEQ)
    row = (t // GRID_W).astype(np.float64)
    col = (t % GRID_W).astype(np.float64)
    half = HEAD_DIM // 2
    inv = ROPE_THETA ** (-np.arange(0, half, 2, dtype=np.float64) / half)
    ang_r = row[:, None] * inv[None, :]
    ang_c = col[:, None] * inv[None, :]
    ang = np.concatenate([ang_r, ang_r, ang_c, ang_c], axis=-1)
    cos = np.tile(np.cos(ang), (1, 2))
    sin = np.tile(np.sin(ang), (1, 2))
    first = (np.arange(LANES) % 32) < 16
    return jnp.asarray(cos, F32), jnp.asarray(np.where(first[None, :], -sin, sin), F32)


def kernel(x_prompt, x_sample, cache_na_k, cache_na_v, cache_diff_k, cache_diff_v, c, c_ctx, w_ada, b_ada, w_in, w_out, na_rel_bias, lambda_q1, lambda_k1, lambda_q2, lambda_k2, subln_g, ln1_g, ln1_b, ln2_g, ln2_b, w_router, w_gate, w_up, w_down):
    l = 0
    cvec = jnp.concatenate([c_ctx[None, :], c, jnp.zeros((SUBLANES - 1 - DEC_BATCH, D_MODEL), F32)], axis=0)
    mod = _modulation(cvec, w_ada[l], b_ada[l][None, :])
    lamv = jnp.stack([lambda_q1[l], lambda_k1[l], lambda_q2[l], lambda_k2[l]], axis=0)
    w_in_b = w_in[l].astype(BF16)
    w_out_b = w_out[l].astype(BF16)
    sub = subln_g[l][None, :]
    wrt = w_router[l].T

    tr = lambda a: jnp.swapaxes(a, -1, -2)

    x1c, h2c, affc, na_k_t, na_v_t, diff_k_t, new_diff_v = _context_block(
        x_prompt, mod, lamv, w_in_b, w_out_b, sub, ln1_g[l][None, :], ln1_b[l][None, :], wrt)

    cos, sin = _rope_tables()
    naq, nak, nav, dfq, dfk, dfv, tbl, stats = _latent_qkv(x_sample, mod, w_in_b, cos, sin,
                                                    na_rel_bias[l].reshape(-1))
    na_o = _latent_na(naq, nak, nav, tr(cache_na_k), tr(cache_na_v), tbl, stats)
    df_o = _latent_diff(dfq, dfk, dfv, tr(cache_diff_k), cache_diff_v, lamv, sub)
    x1l, h2l, affl = _latent_out(x_sample, na_o, df_o, mod, w_out_b,
                                 ln1_g[l][None, :], ln1_b[l][None, :], wrt)

    idxc, gatec, idxl, gatel = _route(affc.reshape(BATCH * N_EXPERTS, SEQ),
                                      affl.reshape(DEC_BATCH * N_EXPERTS, DEC_SEQ))
    idxc = idxc.reshape(BATCH, 1, N_EXPERTS * CAP_CTX)
    idxl = idxl.reshape(DEC_BATCH, N_EXPERTS, CAP_LAT)
    xc = _gather_ctx(idxc, h2c)
    xl = _gather_lat(idxl, h2l)
    yc, yl = _expert_ffn(xc, xl, gatec, gatel, w_gate[l], w_up[l], w_down[l])
    y_prompt = _combine_ctx(yc, idxc, x1c, mod, ln2_g[l][None, :], ln2_b[l][None, :])
    y_sample = _combine_lat(yl, idxl, x1l, mod, ln2_g[l][None, :], ln2_b[l][None, :])
    return (y_prompt, y_sample, tr(na_k_t), tr(na_v_t), tr(diff_k_t), new_diff_v)
```
